```python
import math
import jax, jax.numpy as jnp
from jax import lax
import numpy as np

D_MODEL = 1024
BATCH = 2
SEQ = 16384
DEPTH = 2

CTX_LEN = 256
GRID_W = 64
HEAD_DIM = 64
FOURIER_GROUPS = 6
FOURIER_WIDTH = FOURIER_GROUPS * HEAD_DIM
SWA_Q_HEADS = 8
SWA_KV_HEADS = 2
SWA_WIDTH = SWA_Q_HEADS * HEAD_DIM
WINDOW = 128
BLOCK = 128
CONV_WIDTH = 384
CONV_K = 3
DIFF_HEADS = 4
DIFF_V_DIM = 2 * HEAD_DIM
DIFF_WIDTH = DIFF_HEADS * DIFF_V_DIM
N_BRANCHES = 4
ROPE_BASE = 10000.0
NORM_EPS = 1e-6
NEG_INF = -1e30

IN_NAMES = ("a_x", "b_q", "b_k", "b_v", "c_x", "c_b", "c_c", "d_q", "d_k", "d_v",
            "z_a", "z_b", "z_c", "z_d", "m_g")
IN_SPLITS = (
    FOURIER_WIDTH,
    SWA_WIDTH, SWA_KV_HEADS * HEAD_DIM, SWA_KV_HEADS * HEAD_DIM,
    CONV_WIDTH, CONV_WIDTH, CONV_WIDTH,
    DIFF_HEADS * 2 * HEAD_DIM, DIFF_HEADS * 2 * HEAD_DIM, DIFF_WIDTH,
    FOURIER_WIDTH, SWA_WIDTH, CONV_WIDTH, DIFF_WIDTH,
    N_BRANCHES * D_MODEL,
)
IN_WIDTH = sum(IN_SPLITS)

kernel_name = "hybrid_fourier_swa_conv_diffattn_dit"


def rms_norm(t, gain):
    tf = t.astype(jnp.float32)
    y = tf * lax.rsqrt(jnp.mean(tf * tf, axis=-1, keepdims=True) + NORM_EPS)
    return (y * gain.astype(jnp.float32)).astype(t.dtype)


def rope_tables(n):
    rows = n // GRID_W
    row = jnp.broadcast_to(jnp.arange(rows, dtype=jnp.float32)[:, None], (rows, GRID_W)).reshape(-1)
    col = jnp.broadcast_to(jnp.arange(GRID_W, dtype=jnp.float32)[None, :], (rows, GRID_W)).reshape(-1)
    nf = HEAD_DIM // 4
    inv = ROPE_BASE ** (-jnp.arange(nf, dtype=jnp.float32) / nf)
    ar = row[:, None] * inv[None, :]
    ac = col[:, None] * inv[None, :]
    cos = jnp.concatenate([jnp.cos(ar), jnp.cos(ar), jnp.cos(ac), jnp.cos(ac)], axis=-1)
    sin = jnp.concatenate([jnp.sin(ar), jnp.sin(ar), jnp.sin(ac), jnp.sin(ac)], axis=-1)
    return cos, sin


def apply_rope(t, cos, sin):
    shape = (1, t.shape[1]) + (1,) * (t.ndim - 3) + (HEAD_DIM,)
    cs = cos.reshape(shape).astype(t.dtype)
    sn = sin.reshape(shape).astype(t.dtype)
    t1, t2, t3, t4 = jnp.split(t, 4, axis=-1)
    rot = jnp.concatenate([-t2, t1, -t4, t3], axis=-1)
    return t * cs + rot * sn


def in_project(h, w):
    idx = np.cumsum(np.array(IN_SPLITS))[:-1].tolist()
    parts = jnp.split(h @ w, idx, axis=-1)
    return dict(zip(IN_NAMES, parts))


def fourier_mix(u):
    b, n, _ = u.shape
    ug = u.astype(jnp.float32).reshape(b, n, FOURIER_GROUPS, HEAD_DIM)
    f = jnp.fft.fft2(ug, axes=(1, 3), norm="ortho").real
    return f.reshape(b, n, FOURIER_WIDTH).astype(u.dtype)


def short_conv(u_x, u_b, u_c, w):
    u = u_c * u_x
    n = u.shape[1]
    half = CONV_K // 2
    up = jnp.pad(u, ((0, 0), (half, half), (0, 0)))
    conv = sum(up[:, j:j + n] * w[j] for j in range(CONV_K))
    return u_b * conv


def window_attn_latent(q, k, v, kc, vc, sink):
    b, n, hq, d = q.shape
    hkv = k.shape[2]
    g = hq // hkv
    nb = n // BLOCK
    qb = q.reshape(b, nb, BLOCK, hkv, g, d)

    def band(t):
        tp = jnp.pad(t, ((0, 0), (BLOCK, BLOCK), (0, 0), (0, 0))).reshape(b, nb + 2, BLOCK, hkv, d)
        return jnp.concatenate([tp[:, :-2], tp[:, 1:-1], tp[:, 2:]], axis=2)

    kw, vw = band(k), band(v)
    qi = jnp.arange(BLOCK)[:, None]
    kj = jnp.arange(3 * BLOCK)[None, :]
    in_band = jnp.abs(kj - BLOCK - qi) <= WINDOW
    kpos = jnp.arange(nb)[:, None] * BLOCK + jnp.arange(3 * BLOCK)[None, :] - BLOCK
    valid = (kpos >= 0) & (kpos < n)
    mask = in_band[None] & valid[:, None, :]
    scale = HEAD_DIM ** -0.5
    s_loc = jnp.einsum("bnqhgd,bnkhd->bnhgqk", qb, kw).astype(jnp.float32) * scale
    s_loc = jnp.where(mask[None, :, None, None], s_loc, NEG_INF)
    s_ctx = jnp.einsum("bnqhgd,blhd->bnhgql", qb, kc).astype(jnp.float32) * scale
    sk = sink.astype(jnp.float32).reshape(1, 1, hkv, g, 1, 1)
    m = jnp.maximum(jnp.maximum(s_loc.max(-1, keepdims=True), s_ctx.max(-1, keepdims=True)), sk)
    p_loc = jnp.exp(s_loc - m)
    p_ctx = jnp.exp(s_ctx - m)
    inv = 1.0 / (p_loc.sum(-1, keepdims=True) + p_ctx.sum(-1, keepdims=True) + jnp.exp(sk - m))
    o = (jnp.einsum("bnhgqk,bnkhd->bnqhgd", (p_loc * inv).astype(v.dtype), vw)
         + jnp.einsum("bnhgql,blhd->bnqhgd", (p_ctx * inv).astype(v.dtype), vc))
    return o.reshape(b, n, hq * d)


def ctx_sink_attn(q, k, v, sink):
    b, l, hq, d = q.shape
    hkv = k.shape[2]
    g = hq // hkv
    qg = q.reshape(b, l, hkv, g, d)
    s = jnp.einsum("bqhgd,bkhd->bhgqk", qg, k).astype(jnp.float32) * (HEAD_DIM ** -0.5)
    sk = sink.astype(jnp.float32).reshape(1, hkv, g, 1, 1)
    m = jnp.maximum(s.max(-1, keepdims=True), sk)
    p = jnp.exp(s - m)
    p = p / (p.sum(-1, keepdims=True) + jnp.exp(sk - m))
    o = jnp.einsum("bhgqk,bkhd->bqhgd", p.astype(v.dtype), v)
    return o.reshape(b, l, hq * d)


def diff_attn_block(q, k, v, lam):
    s = jnp.einsum("bqhcd,bkhcd->bhcqk", q, k).astype(jnp.float32) * (HEAD_DIM ** -0.5)
    p = jax.nn.softmax(s, axis=-1)
    a = p[:, :, 0] - lam * p[:, :, 1]
    return jnp.einsum("bhqk,bkhe->bqhe", a.astype(v.dtype), v)


def diff_attn_latent(q, k, v, kc, vc, lam):
    b, n, h, _, d = q.shape
    nb = n // BLOCK
    k_all = jnp.concatenate([kc, k], axis=1)
    v_all = jnp.concatenate([vc, v], axis=1)
    qb = jnp.moveaxis(q.reshape(b, nb, BLOCK, h, 2, d), 1, 0)
    o = lax.map(lambda qq: diff_attn_block(qq, k_all, v_all, lam), qb)
    return jnp.moveaxis(o, 0, 1).reshape(b, n, h, DIFF_V_DIM)


def diff_out(o, subln_g, lam_init):
    b, n, h, e = o.shape
    return (rms_norm(o, subln_g) * (1.0 - lam_init)).reshape(b, n, h * e)


def merge(ys, zs, m_g, w_os, w_out_l):
    gates = jnp.split(jax.nn.sigmoid(m_g.astype(jnp.float32)).astype(m_g.dtype), N_BRANCHES, axis=-1)
    mixed = sum(gt * ((y * jax.nn.silu(z)) @ w) for gt, y, z, w in zip(gates, ys, zs, w_os))
    return mixed @ w_out_l


def setup_inputs(seed: int = 0) -> dict:
    key = jax.random.key(seed)
    ks = jax.random.split(key, 24)
    d = D_MODEL

    def nrm(k, shape, s):
        return jax.random.normal(k, shape, jnp.float32) * s

    return {
        "x": nrm(ks[0], (BATCH, SEQ, d), 1.0),
        "c": nrm(ks[1], (BATCH, d), 1.0),
        "ctx": nrm(ks[2], (BATCH, CTX_LEN, d), 1.0),
        "c_ctx": nrm(ks[3], (d,), 1.0),
        "norm_g": 1.0 + nrm(ks[4], (DEPTH, d), 0.02),
        "w_mod": nrm(ks[5], (DEPTH, d, 3 * d), 0.5 * d ** -0.5),
        "b_mod": nrm(ks[6], (DEPTH, 3 * d), 0.02),
        "w_in": nrm(ks[7], (DEPTH, d, IN_WIDTH), d ** -0.5),
        "q_norm_b": 1.0 + nrm(ks[8], (DEPTH, HEAD_DIM), 0.02),
        "k_norm_b": 1.0 + nrm(ks[9], (DEPTH, HEAD_DIM), 0.02),
        "sink_b": nrm(ks[10], (DEPTH, SWA_Q_HEADS), 0.5),
        "conv_w": nrm(ks[11], (DEPTH, CONV_K, CONV_WIDTH), CONV_K ** -0.5),
        "q_norm_d": 1.0 + nrm(ks[12], (DEPTH, HEAD_DIM), 0.02),
        "k_norm_d": 1.0 + nrm(ks[13], (DEPTH, HEAD_DIM), 0.02),
        "lam_q1": nrm(ks[14], (DEPTH, HEAD_DIM), 0.1),
        "lam_k1": nrm(ks[15], (DEPTH, HEAD_DIM), 0.1),
        "lam_q2": nrm(ks[16], (DEPTH, HEAD_DIM), 0.1),
        "lam_k2": nrm(ks[17], (DEPTH, HEAD_DIM), 0.1),
        "subln_d": 1.0 + nrm(ks[18], (DEPTH, DIFF_V_DIM), 0.02),
        "w_o_a": nrm(ks[19], (DEPTH, FOURIER_WIDTH, d), FOURIER_WIDTH ** -0.5),
        "w_o_b": nrm(ks[20], (DEPTH, SWA_WIDTH, d), SWA_WIDTH ** -0.5),
        "w_o_c": nrm(ks[21], (DEPTH, CONV_WIDTH, d), CONV_WIDTH ** -0.5),
        "w_o_d": nrm(ks[22], (DEPTH, DIFF_WIDTH, d), DIFF_WIDTH ** -0.5),
        "w_out": nrm(ks[23], (DEPTH, d, d), d ** -0.5),
    }


def reference(x, c, ctx, c_ctx, norm_g, w_mod, b_mod, w_in, q_norm_b, k_norm_b, sink_b, conv_w,
              q_norm_d, k_norm_d, lam_q1, lam_k1, lam_q2, lam_k2, subln_d,
              w_o_a, w_o_b, w_o_c, w_o_d, w_out):
    b, n, _ = x.shape
    lc = ctx.shape[1]
    cos, sin = rope_tables(n)
    for l in range(DEPTH):
        last = l == DEPTH - 1
        shift, scale, gate = jnp.split(jax.nn.silu(c) @ w_mod[l] + b_mod[l], 3, axis=-1)
        shift_c, scale_c, gate_c = jnp.split(jax.nn.silu(c_ctx) @ w_mod[l] + b_mod[l], 3, axis=-1)
        h = rms_norm(x, norm_g[l]) * (1.0 + scale[:, None]) + shift[:, None]
        hc = rms_norm(ctx, norm_g[l]) * (1.0 + scale_c) + shift_c
        p = in_project(h, w_in[l])
        pc = in_project(hc, w_in[l])

        lam_init = 0.8 - 0.6 * math.exp(-0.3 * l)
        lam = (jnp.exp(jnp.sum(lam_q1[l].astype(jnp.float32) * lam_k1[l].astype(jnp.float32)))
               - jnp.exp(jnp.sum(lam_q2[l].astype(jnp.float32) * lam_k2[l].astype(jnp.float32)))
               + lam_init)

        kc_b = rms_norm(pc["b_k"].reshape(b, lc, SWA_KV_HEADS, HEAD_DIM), k_norm_b[l])
        vc_b = pc["b_v"].reshape(b, lc, SWA_KV_HEADS, HEAD_DIM)
        kc_d = rms_norm(pc["d_k"].reshape(b, lc, DIFF_HEADS, 2, HEAD_DIM), k_norm_d[l])
        vc_d = pc["d_v"].reshape(b, lc, DIFF_HEADS, DIFF_V_DIM)

        y_a = fourier_mix(p["a_x"])
        q_b = apply_rope(rms_norm(p["b_q"].reshape(b, n, SWA_Q_HEADS, HEAD_DIM), q_norm_b[l]), cos, sin)
        k_b = apply_rope(rms_norm(p["b_k"].reshape(b, n, SWA_KV_HEADS, HEAD_DIM), k_norm_b[l]), cos, sin)
        v_b = p["b_v"].reshape(b, n, SWA_KV_HEADS, HEAD_DIM)
        y_b = window_attn_latent(q_b, k_b, v_b, kc_b, vc_b, sink_b[l])
        y_c = short_conv(p["c_x"], p["c_b"], p["c_c"], conv_w[l])
        q_d = apply_rope(rms_norm(p["d_q"].reshape(b, n, DIFF_HEADS, 2, HEAD_DIM), q_norm_d[l]), cos, sin)
        k_d = apply_rope(rms_norm(p["d_k"].reshape(b, n, DIFF_HEADS, 2, HEAD_DIM), k_norm_d[l]), cos, sin)
        v_d = p["d_v"].reshape(b, n, DIFF_HEADS, DIFF_V_DIM)
        y_d = diff_out(diff_attn_latent(q_d, k_d, v_d, kc_d, vc_d, lam), subln_d[l], lam_init)
        out = merge([y_a, y_b, y_c, y_d], [p["z_a"], p["z_b"], p["z_c"], p["z_d"]], p["m_g"],
                    [w_o_a[l], w_o_b[l], w_o_c[l], w_o_d[l]], w_out[l])
        x_next = x + gate[:, None] * out

        if not last:
            yc_a = fourier_mix(pc["a_x"])
            qc_b = rms_norm(pc["b_q"].reshape(b, lc, SWA_Q_HEADS, HEAD_DIM), q_norm_b[l])
            yc_b = ctx_sink_attn(qc_b, kc_b, vc_b, sink_b[l])
            yc_c = short_conv(pc["c_x"], pc["c_b"], pc["c_c"], conv_w[l])
            qc_d = rms_norm(pc["d_q"].reshape(b, lc, DIFF_HEADS, 2, HEAD_DIM), q_norm_d[l])
            yc_d = diff_out(diff_attn_block(qc_d, kc_d, vc_d, lam), subln_d[l], lam_init)
            out_c = merge([yc_a, yc_b, yc_c, yc_d], [pc["z_a"], pc["z_b"], pc["z_c"], pc["z_d"]], pc["m_g"],
                          [w_o_a[l], w_o_b[l], w_o_c[l], w_o_d[l]], w_out[l])
            ctx = ctx + gate_c * out_c
        x = x_next
    return x
```

```python
import functools
import math

import jax
import jax.numpy as jnp
import numpy as np
from jax import lax
from jax.experimental import pallas as pl
from jax.experimental.pallas import tpu as pltpu

HEAD_DIM = 64
LANES = 128
FOURIER_W = 384
SWA_HEADS = 8
SWA_KV = 2
SWA_W = SWA_HEADS * HEAD_DIM
WINDOW = 128
BLOCK = 128
CONV_W = 384
CONV_K = 3
DIFF_HEADS = 4
DIFF_W = DIFF_HEADS * 2 * HEAD_DIM
N_BRANCH = 4
ROPE_BASE = 10000.0
NORM_EPS = 1e-6
NEG_INF = -1e30
GRID_W = 64
DFT_MINOR = 128
VMEM_LIMIT = 56 * 1024 * 1024

_SRC = dict(a_x=(0, 384), b_q=(384, 896), b_k=(896, 1024), b_v=(1024, 1152), c_x=(1152, 1536),
            c_b=(1536, 1920), c_c=(1920, 2304), d_q=(2304, 2816), d_k=(2816, 3328), d_v=(3328, 3840),
            z=(3840, 5632), m_g=(5632, 9728))


def _bf(x):
    return x.astype(jnp.bfloat16)


def _dot(a, b):
    return jnp.dot(a, b, preferred_element_type=jnp.float32)


def _dot_nt(a, b):
    return lax.dot_general(a, b, (((1,), (1,)), ((), ())), preferred_element_type=jnp.float32)


def _params(n_axes):
    return pltpu.CompilerParams(dimension_semantics=("arbitrary",) * n_axes, vmem_limit_bytes=VMEM_LIMIT)


def _const_spec(shape):
    nd = len(shape)
    return pl.BlockSpec(shape, lambda *_: (0,) * nd, pipeline_mode=pl.Buffered(1))


def _mod_kernel(c_ref, w_ref, b_ref, lam_ref, mod_ref, lamo_ref, *, lam_init):
    c = c_ref[...]
    s = c * jax.nn.sigmoid(c)
    mod_ref[...] = _dot(_bf(s), w_ref[...]) + b_ref[...]
    lv = lam_ref[...]
    a1 = jnp.sum(lv[0:1, :] * lv[1:2, :], axis=-1, keepdims=True)
    a2 = jnp.sum(lv[2:3, :] * lv[3:4, :], axis=-1, keepdims=True)
    lam = jnp.exp(a1) - jnp.exp(a2) + lam_init
    lamo_ref[...] = jnp.broadcast_to(lam, lamo_ref.shape)


def _modulation(c_rows, w_mod, b_mod, lam_vecs, lam_init):
    r, d = c_rows.shape
    return pl.pallas_call(
        functools.partial(_mod_kernel, lam_init=lam_init),
        out_shape=(jax.ShapeDtypeStruct((r, 3 * d), jnp.float32),
                   jax.ShapeDtypeStruct((8, LANES), jnp.float32)),
        name="mod",
    )(c_rows, _bf(w_mod), b_mod.reshape(1, 3 * d), lam_vecs)


def _head_norm_rope(t, gain, mg, cos, sin_s, lane_lo, scale):
    ms = _dot(_bf(t * t), mg)
    y = t * lax.rsqrt(ms + NORM_EPS) * gain
    if cos is not None:
        rot = jnp.where(lane_lo, pltpu.roll(y, LANES - 16, 1), pltpu.roll(y, 16, 1))
        y = y * cos + rot * sin_s
    if scale != 1.0:
        y = y * scale
    return y


def _proj_kernel(*refs, use_rope, tm):
    if use_rope:
        (x_ref, sc_ref, sh_ref, g_ref, w_ref, cs_ref, mg_ref, hg_ref, cos_ref, sin_ref), outs = refs[:10], refs[10:]
    else:
        (x_ref, sc_ref, sh_ref, g_ref, w_ref, cs_ref, mg_ref, hg_ref), outs = refs[:8], refs[8:]
        cos_ref = sin_ref = None
    (zr_ref, zi_ref, qb_ref, kb_ref, vb_ref, uc_ref, cb_ref, qd_ref, kd_ref, vd_ref,
     sza_ref, szb_ref, szc_ref, szd_ref, gate_ref) = outs

    x = x_ref[...]
    y = x * lax.rsqrt(jnp.mean(x * x, axis=-1, keepdims=True) + NORM_EPS) * g_ref[...]
    h = _bf(y * (1.0 + sc_ref[...]) + sh_ref[...])

    def proj(c0, width):
        return _dot(h, w_ref[:, c0:c0 + width])

    col = 0
    a = _bf(proj(col, FOURIER_W))
    zz = _dot(a, cs_ref[...])
    zr_ref[...] = _bf(zz[:, :FOURIER_W])
    zi_ref[...] = _bf(zz[:, FOURIER_W:])
    col += FOURIER_W

    mg = mg_ref[...]
    if use_rope:
        cos = cos_ref[...]
        sin_s = sin_ref[...]
    else:
        cos = sin_s = None
    lane = lax.broadcasted_iota(jnp.int32, (tm, LANES), 1)
    lane_lo = (lane & 31) < 16
    qscale = HEAD_DIM ** -0.5

    def normed(c0, width, gain_row, out_ref, scale):
        t = proj(c0, width)
        gain = hg_ref[gain_row:gain_row + 1, :]
        for s in range(width // LANES):
            ts = t[:, s * LANES:(s + 1) * LANES]
            out_ref[:, s * LANES:(s + 1) * LANES] = _bf(
                _head_norm_rope(ts, gain, mg, cos, sin_s, lane_lo, scale))

    normed(col, SWA_W, 0, qb_ref, qscale); col += SWA_W
    normed(col, 2 * LANES, 1, kb_ref, 1.0); col += 2 * LANES
    vb_ref[...] = _bf(proj(col, 2 * LANES)); col += 2 * LANES
    c3 = proj(col, 3 * CONV_W); col += 3 * CONV_W
    uc_ref[...] = _bf(c3[:, 2 * CONV_W:] * c3[:, :CONV_W])
    cb_ref[...] = _bf(c3[:, CONV_W:2 * CONV_W])
    normed(col, DIFF_W, 2, qd_ref, qscale); col += DIFF_W
    normed(col, DIFF_W, 3, kd_ref, 1.0); col += DIFF_W
    vd_ref[...] = _bf(proj(col, DIFF_W)); col += DIFF_W
    for ref, width in ((sza_ref, FOURIER_W), (szb_ref, SWA_W), (szc_ref, CONV_W), (szd_ref, DIFF_W)):
        z = proj(col, width)
        ref[...] = _bf(z * jax.nn.sigmoid(z))
        col += width
    d = x.shape[-1]
    for j in range(N_BRANCH):
        gate_ref[:, j * d:(j + 1) * d] = _bf(jax.nn.sigmoid(proj(col, d)))
        col += d


_PROJ_OUT_W = (FOURIER_W, FOURIER_W, SWA_W, 2 * LANES, 2 * LANES, CONV_W, CONV_W, DIFF_W, DIFF_W, DIFF_W,
               FOURIER_W, SWA_W, CONV_W, DIFF_W)


def _project(x, scale, shift, norm_g, w_ext, cs_bd, mg, head_gains, rope, tm):
    b, n, d = x.shape
    use_rope = rope is not None
    tok = lambda w: pl.BlockSpec((None, tm, w), lambda bi, i: (bi, i, 0))
    per_b = pl.BlockSpec((None, 1, d), lambda bi, i: (bi, 0, 0))
    in_specs = [tok(d), per_b, per_b, _const_spec((1, d)), _const_spec(w_ext.shape), _const_spec(cs_bd.shape),
                _const_spec(mg.shape), _const_spec(head_gains.shape)]
    args = [x, scale, shift, norm_g.reshape(1, d), w_ext, cs_bd, mg, head_gains]
    if use_rope:
        tab = pl.BlockSpec((tm, LANES), lambda bi, i: (i, 0))
        in_specs += [tab, tab]
        args += list(rope)
    widths = _PROJ_OUT_W + (N_BRANCH * d,)
    return pl.pallas_call(
        functools.partial(_proj_kernel, use_rope=use_rope, tm=tm),
        grid=(b, n // tm),
        in_specs=in_specs,
        out_specs=[tok(w) for w in widths],
        out_shape=[jax.ShapeDtypeStruct((b, n, w), jnp.bfloat16) for w in widths],
        compiler_params=_params(2),
        name="proj",
    )(*args)


def _dft1_kernel(zr_ref, zi_ref, f_ref, tc_ref, ts_ref, o_ref, *, r, tn2):
    z = jnp.concatenate([zr_ref[...], zi_ref[...]], axis=0)
    a = _dot(f_ref[...], z)
    for j in range(tn2):
        ar = a[:r, j * FOURIER_W:(j + 1) * FOURIER_W]
        ai = a[r:, j * FOURIER_W:(j + 1) * FOURIER_W]
        tc = jnp.concatenate([tc_ref[j]] * (FOURIER_W // LANES), axis=-1)
        ts = jnp.concatenate([ts_ref[j]] * (FOURIER_W // LANES), axis=-1)
        o_ref[0, j] = _bf(ar * tc + ai * ts)
        o_ref[1, j] = _bf(ai * tc - ar * ts)


def _left_matmul_kernel(m_ref, x_ref, o_ref):
    o_ref[...] = _bf(_dot(m_ref[...], x_ref[...]))


def _left_matmul(mat, x, tc):
    b, k, c = x.shape
    rows = mat.shape[0]
    return pl.pallas_call(
        _left_matmul_kernel,
        grid=(b, c // tc),
        in_specs=[_const_spec(mat.shape), pl.BlockSpec((None, k, tc), lambda bi, i: (bi, 0, i))],
        out_specs=pl.BlockSpec((None, rows, tc), lambda bi, i: (bi, 0, i)),
        out_shape=jax.ShapeDtypeStruct((b, rows, c), jnp.bfloat16),
        compiler_params=_params(2),
        name="dft2",
    )(mat, x)


def _dft_tables(n):
    inv = 1.0 / math.sqrt(n)
    if n <= 2 * DFT_MINOR:
        k = np.arange(n)
        ang = 2.0 * np.pi * ((k[:, None] * k[None, :]) % n) / n
        return dict(direct=jnp.asarray(np.concatenate([np.cos(ang), np.sin(ang)], axis=1) * inv, jnp.bfloat16))
    r = n // DFT_MINOR
    k1 = np.arange(r)
    a1 = 2.0 * np.pi * ((k1[:, None] * k1[None, :]) % r) / r
    c1, s1 = np.cos(a1), np.sin(a1)
    f1 = np.block([[c1, s1], [-s1, c1]])
    n2 = np.arange(DFT_MINOR)
    at = 2.0 * np.pi * (n2[:, None] * k1[None, :]) / n
    tw_c = np.repeat(np.cos(at)[:, :, None], LANES, axis=2)
    tw_s = np.repeat(np.sin(at)[:, :, None], LANES, axis=2)
    a2 = 2.0 * np.pi * ((n2[:, None] * n2[None, :]) % DFT_MINOR) / DFT_MINOR
    f2 = np.concatenate([np.cos(a2), np.sin(a2)], axis=1) * inv
    return dict(f1=jnp.asarray(f1, jnp.bfloat16), tw_c=jnp.asarray(tw_c, jnp.float32),
                tw_s=jnp.asarray(tw_s, jnp.float32), f2=jnp.asarray(f2, jnp.bfloat16))


def _fourier_positions(zr, zi, tabs):
    b, n, w = zr.shape
    if "direct" in tabs:
        return _left_matmul(tabs["direct"], jnp.concatenate([zr, zi], axis=1), w)
    r = n // DFT_MINOR
    tn2 = 8
    zr2 = zr.reshape(b, r, DFT_MINOR * w)
    zi2 = zi.reshape(b, r, DFT_MINOR * w)
    zin = pl.BlockSpec((None, r, tn2 * w), lambda bi, i: (bi, 0, i))
    tw = pl.BlockSpec((tn2, r, LANES), lambda bi, i: (i, 0, 0))
    g = pl.pallas_call(
        functools.partial(_dft1_kernel, r=r, tn2=tn2),
        grid=(b, DFT_MINOR // tn2),
        in_specs=[zin, zin, _const_spec(tabs["f1"].shape), tw, tw],
        out_specs=pl.BlockSpec((None, 2, tn2, r, w), lambda bi, i: (bi, 0, i, 0, 0)),
        out_shape=jax.ShapeDtypeStruct((b, 2, DFT_MINOR, r, w), jnp.bfloat16),
        compiler_params=_params(2),
        name="dft1",
    )(zr2, zi2, tabs["f1"], tabs["tw_c"], tabs["tw_s"])
    y = _left_matmul(tabs["f2"], g.reshape(b, 2 * DFT_MINOR, r * w), 4 * w)
    return y.reshape(b, n, w)


def _win_kernel(*refs, has_local, nb):
    if has_local:
        sink_ref, q_ref, kp_ref, kc_ref, kn_ref, vp_ref, vc_ref, vn_ref, kx_ref, vx_ref, o_ref = refs
    else:
        sink_ref, q_ref, kx_ref, vx_ref, o_ref = refs
    i = pl.program_id(1)
    tq = q_ref.shape[0]
    lx = kx_ref.shape[0]
    lane = lax.broadcasted_iota(jnp.int32, (1, LANES), 1)
    halves = (lane < HEAD_DIM, lane >= HEAD_DIM)
    if has_local:
        r = lax.broadcasted_iota(jnp.int32, (tq, 3 * BLOCK), 0)
        c = lax.broadcasted_iota(jnp.int32, (tq, 3 * BLOCK), 1)
        kpos = (i - 1) * BLOCK + c
        mask = (jnp.abs(c - BLOCK - r) <= WINDOW) & (kpos >= 0) & (kpos < nb * BLOCK)
    for j in range(SWA_KV):
        sl = slice(j * LANES, (j + 1) * LANES)
        if has_local:
            kd = jnp.concatenate([kp_ref[:, sl], kc_ref[:, sl], kn_ref[:, sl], kx_ref[:, sl]], axis=0)
            vd = jnp.concatenate([vp_ref[:, sl], vc_ref[:, sl], vn_ref[:, sl], vx_ref[:, sl]], axis=0)
        else:
            kd = kx_ref[:, sl]
            vd = vx_ref[:, sl]
        for t in range(SWA_HEADS // SWA_KV // 2):
            slab = j * (SWA_HEADS // SWA_KV // 2) + t
            q2 = q_ref[:, slab * LANES:(slab + 1) * LANES]
            acc = jnp.zeros((tq, LANES), jnp.float32)
            for e in range(2):
                sk = sink_ref[0, 2 * slab + e]
                s = _dot_nt(jnp.where(halves[e], q2, jnp.zeros_like(q2)), kd)
                if has_local:
                    s = jnp.concatenate([jnp.where(mask, s[:, :3 * BLOCK], NEG_INF), s[:, 3 * BLOCK:]], axis=1)
                m = jnp.maximum(jnp.max(s, axis=-1, keepdims=True), sk)
                p = jnp.exp(s - m)
                inv = 1.0 / (jnp.sum(p, axis=-1, keepdims=True) + jnp.exp(sk - m))
                acc = acc + _dot(_bf(p * inv), jnp.where(halves[e], vd, jnp.zeros_like(vd)))
            o_ref[:, slab * LANES:(slab + 1) * LANES] = _bf(acc)
    del lx


def _window_attention(qb, kb, vb, kx, vx, sink, has_local):
    b, n, _ = qb.shape
    nb = n // BLOCK
    lx = kx.shape[1]
    smem = pl.BlockSpec(memory_space=pltpu.SMEM)
    qspec = pl.BlockSpec((None, BLOCK, SWA_W), lambda bi, i: (bi, i, 0))
    ctx = pl.BlockSpec((None, lx, 2 * LANES), lambda bi, i: (bi, 0, 0))
    args = [sink.reshape(1, SWA_HEADS), qb]
    specs = [smem, qspec]
    if has_local:
        prv = pl.BlockSpec((None, BLOCK, 2 * LANES), lambda bi, i: (bi, jnp.maximum(i - 1, 0), 0))
        cur = pl.BlockSpec((None, BLOCK, 2 * LANES), lambda bi, i: (bi, i, 0))
        nxt = pl.BlockSpec((None, BLOCK, 2 * LANES), lambda bi, i: (bi, jnp.minimum(i + 1, nb - 1), 0))
        args += [kb, kb, kb, vb, vb, vb]
        specs += [prv, cur, nxt, prv, cur, nxt]
    args += [kx, vx]
    specs += [ctx, ctx]
    return pl.pallas_call(
        functools.partial(_win_kernel, has_local=has_local, nb=nb),
        grid=(b, nb),
        in_specs=specs,
        out_specs=qspec,
        out_shape=jax.ShapeDtypeStruct((b, n, SWA_W), jnp.bfloat16),
        compiler_params=_params(2),
        name="win",
    )(*args)


def _diff_kernel(lam_ref, q_ref, k_ref, v_ref, g_ref, o_ref, vt_ref, acc_ref, *, tk, out_scale):
    qi = pl.program_id(2)
    tq = q_ref.shape[0]
    n_chunks = vt_ref.shape[0]

    @pl.when(qi == 0)
    def _():
        def body(ci, carry):
            c0 = pl.multiple_of(ci * tk, tk)
            vt_ref[ci] = _bf(v_ref[pl.ds(c0, tk), :].astype(jnp.float32).T)
            return carry
        lax.fori_loop(0, n_chunks, body, 0)

    q = q_ref[...]
    lane = lax.broadcasted_iota(jnp.int32, (1, LANES), 1)
    qm = (jnp.where(lane < HEAD_DIM, q, jnp.zeros_like(q)), jnp.where(lane >= HEAD_DIM, q, jnp.zeros_like(q)))
    acc_ref[...] = jnp.zeros_like(acc_ref)

    def chunk(ci, carry):
        c0 = pl.multiple_of(ci * tk, tk)
        k = k_ref[pl.ds(c0, tk), :]
        vt = vt_ref[ci]
        out = []
        for c in range(2):
            m_old, l_old = carry[2 * c], carry[2 * c + 1]
            st = _dot_nt(k, qm[c])
            m_new = jnp.maximum(m_old, jnp.max(st, axis=0, keepdims=True))
            alpha = jnp.exp(m_old - m_new)
            p = jnp.exp(st - m_new)
            out += [m_new, alpha * l_old + jnp.sum(p, axis=0, keepdims=True)]
            acc_ref[c] = alpha * acc_ref[c] + _dot(vt, _bf(p))
        return tuple(out)

    neg = jnp.full((1, tq), NEG_INF, jnp.float32)
    zero = jnp.zeros((1, tq), jnp.float32)
    _, l1, _, l2 = lax.fori_loop(0, n_chunks, chunk, (neg, zero, neg, zero))
    lam = lam_ref[0, 0]
    o = acc_ref[0] * (1.0 / l1) - lam * (acc_ref[1] * (1.0 / l2))
    y = o * lax.rsqrt(jnp.mean(o * o, axis=0, keepdims=True) + NORM_EPS) * g_ref[...] * out_scale
    o_ref[...] = _bf(y.T)


def _pick_tk(nk, cap):
    best = LANES
    for t in range(LANES, min(cap, nk) + 1, LANES):
        if nk % t == 0:
            best = t
    return best


def _diff_attention(qd, k_all, v_all, lam, subln_g, lam_init, tq, tk_cap):
    b, n, _ = qd.shape
    nk = k_all.shape[1]
    tk = _pick_tk(nk, tk_cap)
    smem = pl.BlockSpec(memory_space=pltpu.SMEM)
    qspec = pl.BlockSpec((None, tq, LANES), lambda bi, h, i: (bi, i, h))
    kvspec = pl.BlockSpec((None, nk, LANES), lambda bi, h, i: (bi, 0, h))
    gain = jnp.broadcast_to(subln_g.astype(jnp.float32)[:, None], (LANES, tq))
    return pl.pallas_call(
        functools.partial(_diff_kernel, tk=tk, out_scale=1.0 - lam_init),
        grid=(b, DIFF_HEADS, n // tq),
        in_specs=[smem, qspec, kvspec, kvspec, _const_spec((LANES, tq))],
        out_specs=qspec,
        out_shape=jax.ShapeDtypeStruct((b, n, DIFF_W), jnp.bfloat16),
        scratch_shapes=[pltpu.VMEM((nk // tk, LANES, tk), jnp.bfloat16), pltpu.VMEM((2, LANES, tq), jnp.float32)],
        compiler_params=_params(3),
        name="diff",
    )(lam, qd, k_all, v_all, gain)


def _merge_kernel(x_ref, gt_ref, ya_ref, sza_ref, yb_ref, szb_ref, uc_ref, up_ref, un_ref, cb_ref, szc_ref,
                  yd_ref, szd_ref, g_ref, cw_ref, wa_ref, wb_ref, wc_ref, wd_ref, wo_ref, o_ref, *, tm, nt):
    i = pl.program_id(1)
    f32 = jnp.float32
    u = uc_ref[...].astype(f32)
    row = lax.broadcasted_iota(jnp.int32, (tm, 1), 0)
    prev_row = jnp.where(i > 0, up_ref[7:8, :].astype(f32), 0.0)
    next_row = jnp.where(i < nt - 1, un_ref[0:1, :].astype(f32), 0.0)
    u_prev = jnp.where(row == 0, prev_row, pltpu.roll(u, 1, 0))
    u_next = jnp.where(row == tm - 1, next_row, pltpu.roll(u, tm - 1, 0))
    cw = cw_ref[...]
    conv = u_prev * cw[0:1, :] + u * cw[1:2, :] + u_next * cw[2:3, :]
    yc = cb_ref[...].astype(f32) * conv
    d = x_ref.shape[-1]
    branches = ((ya_ref[...].astype(f32), sza_ref, wa_ref), (yb_ref[...].astype(f32), szb_ref, wb_ref),
                (yc, szc_ref, wc_ref), (yd_ref[...].astype(f32), szd_ref, wd_ref))
    mixed = jnp.zeros((tm, d), f32)
    for j, (y, sz_ref, w_ref) in enumerate(branches):
        t = _dot(_bf(y * sz_ref[...].astype(f32)), w_ref[...])
        mixed = mixed + g_ref[:, j * d:(j + 1) * d].astype(f32) * t
    out = _dot(_bf(mixed), wo_ref[...])
    o_ref[...] = x_ref[...] + gt_ref[...] * out


def _merge(x, gate, ya, sza, yb, szb, uc, cb, szc, yd, szd, g, conv_w, w_a, w_b, w_c, w_d, w_out, tm):
    b, n, d = x.shape
    nt = n // tm
    tok = lambda w: pl.BlockSpec((None, tm, w), lambda bi, i: (bi, i, 0))
    hb = tm // 8
    halo_p = pl.BlockSpec((None, 8, CONV_W), lambda bi, i: (bi, jnp.maximum(i * hb - 1, 0), 0))
    halo_n = pl.BlockSpec((None, 8, CONV_W), lambda bi, i: (bi, jnp.minimum((i + 1) * hb, n // 8 - 1), 0))
    per_b = pl.BlockSpec((None, 1, d), lambda bi, i: (bi, 0, 0))
    ws = [_bf(w_a), _bf(w_b), _bf(w_c), _bf(w_d), _bf(w_out)]
    return pl.pallas_call(
        functools.partial(_merge_kernel, tm=tm, nt=nt),
        grid=(b, nt),
        in_specs=[tok(d), per_b, tok(FOURIER_W), tok(FOURIER_W), tok(SWA_W), tok(SWA_W), tok(CONV_W), halo_p, halo_n,
                  tok(CONV_W), tok(CONV_W), tok(DIFF_W), tok(DIFF_W), tok(N_BRANCH * d), _const_spec(conv_w.shape)]
                 + [_const_spec(w.shape) for w in ws],
        out_specs=tok(d),
        out_shape=jax.ShapeDtypeStruct((b, n, d), jnp.float32),
        compiler_params=_params(2),
        name="merge",
    )(x, gate, ya, sza, yb, szb, uc, uc, uc, cb, szc, yd, szd, g, conv_w.astype(jnp.float32), *ws)


def _rope_tables(n):
    rows = n // GRID_W
    row = jnp.broadcast_to(jnp.arange(rows, dtype=jnp.float32)[:, None], (rows, GRID_W)).reshape(-1)
    col = jnp.broadcast_to(jnp.arange(GRID_W, dtype=jnp.float32)[None, :], (rows, GRID_W)).reshape(-1)
    nf = HEAD_DIM // 4
    inv = ROPE_BASE ** (-jnp.arange(nf, dtype=jnp.float32) / nf)
    ar = row[:, None] * inv[None, :]
    ac = col[:, None] * inv[None, :]
    cos = jnp.concatenate([jnp.cos(ar), jnp.cos(ar), jnp.cos(ac), jnp.cos(ac)], axis=-1)
    sin = jnp.concatenate([jnp.sin(ar), jnp.sin(ar), jnp.sin(ac), jnp.sin(ac)], axis=-1)
    sign = jnp.where((jnp.arange(HEAD_DIM) % 32) < 16, -1.0, 1.0)
    return jnp.tile(cos, (1, 2)), jnp.tile(sin * sign, (1, 2))


def _channel_dft():
    k = np.arange(HEAD_DIM)
    ang = 2.0 * np.pi * ((k[:, None] * k[None, :]) % HEAD_DIM) / HEAD_DIM
    eye = np.eye(FOURIER_W // HEAD_DIM)
    inv = 1.0 / math.sqrt(HEAD_DIM)
    return jnp.asarray(np.concatenate([np.kron(eye, np.cos(ang)), -np.kron(eye, np.sin(ang))], axis=1) * inv,
                       jnp.bfloat16)


def _extend_w_in(w):
    s = _SRC
    k0, k1 = s["b_k"][0], s["b_k"][0] + HEAD_DIM
    v0, v1 = s["b_v"][0], s["b_v"][0] + HEAD_DIM
    pieces = [w[:, s["a_x"][0]:s["b_q"][1]],
              w[:, k0:k1], w[:, k0:k1], w[:, k1:k1 + HEAD_DIM], w[:, k1:k1 + HEAD_DIM],
              w[:, v0:v1], w[:, v0:v1], w[:, v1:v1 + HEAD_DIM], w[:, v1:v1 + HEAD_DIM],
              w[:, s["c_x"][0]:]]
    return _bf(jnp.concatenate(pieces, axis=1))


def _mix(stream, scale, shift, gate, ctx_kv, lp, consts, lam, lam_init, rope, dft, tm, tq, tk, has_local):
    parts = _project(stream, scale, shift, lp["norm_g"], lp["w_ext"], consts["cs_bd"], consts["mg"],
                     lp["head_gains"], rope, tm)
    zr, zi, qb, kb, vb, uc, cb, qd, kd, vd, sza, szb, szc, szd, g = parts
    if ctx_kv is None:
        kx_b, vx_b, k_all, v_all = kb, vb, kd, vd
    else:
        kx_b, vx_b, kx_d, vx_d = ctx_kv
        k_all = jnp.concatenate([kx_d, kd], axis=1)
        v_all = jnp.concatenate([vx_d, vd], axis=1)
    ya = _fourier_positions(zr, zi, dft)
    yb = _window_attention(qb, kb, vb, kx_b, vx_b, lp["sink"], has_local)
    yd = _diff_attention(qd, k_all, v_all, lam, lp["subln"], lam_init, tq, tk)
    new = _merge(stream, gate, ya, sza, yb, szb, uc, cb, szc, yd, szd, g, lp["conv_w"],
                 lp["w_o_a"], lp["w_o_b"], lp["w_o_c"], lp["w_o_d"], lp["w_out"], tm)
    return new, (kb, vb, kd, vd)


def kernel(x, c, ctx, c_ctx, norm_g, w_mod, b_mod, w_in, q_norm_b, k_norm_b, sink_b, conv_w, q_norm_d, k_norm_d,
           lam_q1, lam_k1, lam_q2, lam_k2, subln_d, w_o_a, w_o_b, w_o_c, w_o_d, w_out):
    b, n, d = x.shape
    lc = ctx.shape[1]
    depth = w_in.shape[0]
    rope = _rope_tables(n)
    consts = dict(cs_bd=_channel_dft(),
                  mg=jnp.asarray(np.kron(np.eye(2), np.full((HEAD_DIM, HEAD_DIM), 1.0 / HEAD_DIM)), jnp.bfloat16))
    dft_x = _dft_tables(n)
    dft_c = _dft_tables(lc)
    c_rows = jnp.zeros((8, d), jnp.float32).at[:b].set(c).at[b].set(c_ctx)
    tm = min(256, n)
    tq = min(256, n)
    tk = 1280
    for l in range(depth):
        last = l == depth - 1
        lam_init = 0.8 - 0.6 * math.exp(-0.3 * l)
        lam_vecs = jnp.stack([lam_q1[l], lam_k1[l], lam_q2[l], lam_k2[l]]).astype(jnp.float32)
        mod, lam_o = _modulation(c_rows, w_mod[l], b_mod[l], lam_vecs, lam_init)
        lam = lam_o[0:1, 0:1]
        shift, scale, gate = (mod[:b, None, j * d:(j + 1) * d] for j in range(3))
        shift_c, scale_c, gate_c = (jnp.broadcast_to(mod[b:b + 1, None, j * d:(j + 1) * d], (b, 1, d))
                                    for j in range(3))
        tile2 = lambda v: jnp.tile(v.astype(jnp.float32), 2)
        lp = dict(norm_g=norm_g[l], w_ext=_extend_w_in(w_in[l]), sink=sink_b[l].astype(jnp.float32),
                  head_gains=jnp.stack([tile2(q_norm_b[l]), tile2(k_norm_b[l]), tile2(q_norm_d[l]), tile2(k_norm_d[l])]),
                  subln=subln_d[l], conv_w=conv_w[l], w_o_a=w_o_a[l], w_o_b=w_o_b[l], w_o_c=w_o_c[l],
                  w_o_d=w_o_d[l], w_out=w_out[l])
        if last:
            parts = _project(ctx, scale_c, shift_c, lp["norm_g"], lp["w_ext"], consts["cs_bd"], consts["mg"],
                             lp["head_gains"], None, min(256, lc))
            ctx_kv = (parts[3], parts[4], parts[8], parts[9])
        else:
            ctx, ctx_kv = _mix(ctx, scale_c, shift_c, gate_c, None, lp, consts, lam, lam_init, None, dft_c,
                               min(256, lc), min(256, lc), tk, False)
        x, _ = _mix(x, scale, shift, gate, ctx_kv, lp, consts, lam, lam_init, rope, dft_x, tm, tq, tk, True)
    return x
```

```python
import functools
import math

import jax
import jax.numpy as jnp
import numpy as np
from jax import lax
from jax.experimental import pallas as pl
from jax.experimental.pallas import tpu as pltpu

HEAD_DIM = 64
LANES = 128
FOURIER_W = 384
SWA_HEADS = 8
SWA_KV = 2
SWA_W = SWA_HEADS * HEAD_DIM
WINDOW = 128
BLOCK = 128
CONV_W = 384
CONV_K = 3
DIFF_HEADS = 4
DIFF_W = DIFF_HEADS * 2 * HEAD_DIM
N_BRANCH = 4
ROPE_BASE = 10000.0
NORM_EPS = 1e-6
NEG_INF = -1e30
GRID_W = 64
DFT_MINOR = 128
VMEM_LIMIT = 56 * 1024 * 1024

_SRC = dict(a_x=(0, 384), b_q=(384, 896), b_k=(896, 1024), b_v=(1024, 1152), c_x=(1152, 1536),
            c_b=(1536, 1920), c_c=(1920, 2304), d_q=(2304, 2816), d_k=(2816, 3328), d_v=(3328, 3840),
            z=(3840, 5632), m_g=(5632, 9728))


def _bf(x):
    return x.astype(jnp.bfloat16)


def _dot(a, b):
    return jnp.dot(a, b, preferred_element_type=jnp.float32)


def _dot_nt(a, b):
    return lax.dot_general(a, b, (((1,), (1,)), ((), ())), preferred_element_type=jnp.float32)


def _params(n_axes):
    return pltpu.CompilerParams(dimension_semantics=("arbitrary",) * n_axes, vmem_limit_bytes=VMEM_LIMIT)


def _const_spec(shape):
    nd = len(shape)
    return pl.BlockSpec(shape, lambda *_: (0,) * nd, pipeline_mode=pl.Buffered(1))


def _mod_kernel(c_ref, w_ref, b_ref, lam_ref, mod_ref, lamo_ref, *, lam_init):
    c = c_ref[...]
    s = c * jax.nn.sigmoid(c)
    mod_ref[...] = _dot(_bf(s), w_ref[...]) + b_ref[...]
    lv = lam_ref[...]
    a1 = jnp.sum(lv[0:1, :] * lv[1:2, :], axis=-1, keepdims=True)
    a2 = jnp.sum(lv[2:3, :] * lv[3:4, :], axis=-1, keepdims=True)
    lam = jnp.exp(a1) - jnp.exp(a2) + lam_init
    lamo_ref[...] = jnp.broadcast_to(lam, lamo_ref.shape)


def _modulation(c_rows, w_mod, b_mod, lam_vecs, lam_init):
    r, d = c_rows.shape
    return pl.pallas_call(
        functools.partial(_mod_kernel, lam_init=lam_init),
        out_shape=(jax.ShapeDtypeStruct((r, 3 * d), jnp.float32),
                   jax.ShapeDtypeStruct((8, LANES), jnp.float32)),
        name="mod",
    )(c_rows, _bf(w_mod), b_mod.reshape(1, 3 * d), lam_vecs)


def _head_norm_rope(t, gain, mg, cos, sin_s, lane_lo, scale):
    ms = _dot(_bf(t * t), mg)
    y = t * lax.rsqrt(ms + NORM_EPS) * gain
    if cos is not None:
        rot = jnp.where(lane_lo, pltpu.roll(y, LANES - 16, 1), pltpu.roll(y, 16, 1))
        y = y * cos + rot * sin_s
    if scale != 1.0:
        y = y * scale
    return y


def _proj_kernel(*refs, use_rope, tm):
    if use_rope:
        (x_ref, sc_ref, sh_ref, g_ref, w_ref, cs_ref, mg_ref, hg_ref, cos_ref, sin_ref), outs = refs[:10], refs[10:]
    else:
        (x_ref, sc_ref, sh_ref, g_ref, w_ref, cs_ref, mg_ref, hg_ref), outs = refs[:8], refs[8:]
        cos_ref = sin_ref = None
    (zr_ref, zi_ref, qb_ref, kb_ref, vb_ref, uc_ref, cb_ref, qd_ref, kd_ref, vd_ref,
     sza_ref, szb_ref, szc_ref, szd_ref, gate_ref) = outs

    x = x_ref[...]
    y = x * lax.rsqrt(jnp.mean(x * x, axis=-1, keepdims=True) + NORM_EPS) * g_ref[...]
    h = _bf(y * (1.0 + sc_ref[...]) + sh_ref[...])

    def proj(c0, width):
        return _dot(h, w_ref[:, c0:c0 + width])

    col = 0
    a = _bf(proj(col, FOURIER_W))
    zz = _dot(a, cs_ref[...])
    zr_ref[...] = _bf(zz[:, :FOURIER_W])
    zi_ref[...] = _bf(zz[:, FOURIER_W:])
    col += FOURIER_W

    mg = mg_ref[...]
    if use_rope:
        cos = cos_ref[...]
        sin_s = sin_ref[...]
    else:
        cos = sin_s = None
    lane = lax.broadcasted_iota(jnp.int32, (tm, LANES), 1)
    lane_lo = (lane & 31) < 16
    qscale = HEAD_DIM ** -0.5

    def normed(c0, width, gain_row, out_ref, scale):
        t = proj(c0, width)
        gain = hg_ref[gain_row:gain_row + 1, :]
        for s in range(width // LANES):
            ts = t[:, s * LANES:(s + 1) * LANES]
            out_ref[:, s * LANES:(s + 1) * LANES] = _bf(
                _head_norm_rope(ts, gain, mg, cos, sin_s, lane_lo, scale))

    normed(col, SWA_W, 0, qb_ref, qscale); col += SWA_W
    normed(col, 2 * LANES, 1, kb_ref, 1.0); col += 2 * LANES
    vb_ref[...] = _bf(proj(col, 2 * LANES)); col += 2 * LANES
    c3 = proj(col, 3 * CONV_W); col += 3 * CONV_W
    uc_ref[...] = _bf(c3[:, 2 * CONV_W:] * c3[:, :CONV_W])
    cb_ref[...] = _bf(c3[:, CONV_W:2 * CONV_W])
    normed(col, DIFF_W, 2, qd_ref, qscale); col += DIFF_W
    normed(col, DIFF_W, 3, kd_ref, 1.0); col += DIFF_W
    vd_ref[...] = _bf(proj(col, DIFF_W)); col += DIFF_W
    for ref, width in ((sza_ref, FOURIER_W), (szb_ref, SWA_W), (szc_ref, CONV_W), (szd_ref, DIFF_W)):
        z = proj(col, width)
        ref[...] = _bf(z * jax.nn.sigmoid(z))
        col += width
    d = x.shape[-1]
    for j in range(N_BRANCH):
        gate_ref[:, j * d:(j + 1) * d] = _bf(jax.nn.sigmoid(proj(col, d)))
        col += d


_PROJ_OUT_W = (FOURIER_W, FOURIER_W, SWA_W, 2 * LANES, 2 * LANES, CONV_W, CONV_W, DIFF_W, DIFF_W, DIFF_W,
               FOURIER_W, SWA_W, CONV_W, DIFF_W)


def _project(x, scale, shift, norm_g, w_ext, cs_bd, mg, head_gains, rope, tm):
    b, n, d = x.shape
    use_rope = rope is not None
    tok = lambda w: pl.BlockSpec((None, tm, w), lambda bi, i: (bi, i, 0))
    per_b = pl.BlockSpec((None, 1, d), lambda bi, i: (bi, 0, 0))
    in_specs = [tok(d), per_b, per_b, _const_spec((1, d)), _const_spec(w_ext.shape), _const_spec(cs_bd.shape),
                _const_spec(mg.shape), _const_spec(head_gains.shape)]
    args = [x, scale, shift, norm_g.reshape(1, d), w_ext, cs_bd, mg, head_gains]
    if use_rope:
        tab = pl.BlockSpec((tm, LANES), lambda bi, i: (i, 0))
        in_specs += [tab, tab]
        args += list(rope)
    widths = _PROJ_OUT_W + (N_BRANCH * d,)
    return pl.pallas_call(
        functools.partial(_proj_kernel, use_rope=use_rope, tm=tm),
        grid=(b, n // tm),
        in_specs=in_specs,
        out_specs=[tok(w) for w in widths],
        out_shape=[jax.ShapeDtypeStruct((b, n, w), jnp.bfloat16) for w in widths],
        compiler_params=_params(2),
        name="proj",
    )(*args)


def _dft1_kernel(zr_ref, zi_ref, f_ref, tc_ref, ts_ref, o_ref, *, r, tn2):
    z = jnp.concatenate([zr_ref[...], zi_ref[...]], axis=0)
    a = _dot(f_ref[...], z)
    for j in range(tn2):
        ar = a[:r, j * FOURIER_W:(j + 1) * FOURIER_W]
        ai = a[r:, j * FOURIER_W:(j + 1) * FOURIER_W]
        tc = jnp.concatenate([tc_ref[j]] * (FOURIER_W // LANES), axis=-1)
        ts = jnp.concatenate([ts_ref[j]] * (FOURIER_W // LANES), axis=-1)
        o_ref[0, j] = _bf(ar * tc + ai * ts)
        o_ref[1, j] = _bf(ai * tc - ar * ts)


def _left_matmul_kernel(m_ref, x_ref, o_ref):
    o_ref[...] = _bf(_dot(m_ref[...], x_ref[...]))


def _left_matmul(mat, x, tc):
    b, k, c = x.shape
    rows = mat.shape[0]
    return pl.pallas_call(
        _left_matmul_kernel,
        grid=(b, c // tc),
        in_specs=[_const_spec(mat.shape), pl.BlockSpec((None, k, tc), lambda bi, i: (bi, 0, i))],
        out_specs=pl.BlockSpec((None, rows, tc), lambda bi, i: (bi, 0, i)),
        out_shape=jax.ShapeDtypeStruct((b, rows, c), jnp.bfloat16),
        compiler_params=_params(2),
        name="dft2",
    )(mat, x)


def _dft_tables(n):
    inv = 1.0 / math.sqrt(n)
    if n <= 2 * DFT_MINOR:
        k = np.arange(n)
        ang = 2.0 * np.pi * ((k[:, None] * k[None, :]) % n) / n
        return dict(direct=jnp.asarray(np.concatenate([np.cos(ang), np.sin(ang)], axis=1) * inv, jnp.bfloat16))
    r = n // DFT_MINOR
    k1 = np.arange(r)
    a1 = 2.0 * np.pi * ((k1[:, None] * k1[None, :]) % r) / r
    c1, s1 = np.cos(a1), np.sin(a1)
    f1 = np.block([[c1, s1], [-s1, c1]])
    n2 = np.arange(DFT_MINOR)
    at = 2.0 * np.pi * (n2[:, None] * k1[None, :]) / n
    tw_c = np.repeat(np.cos(at)[:, :, None], LANES, axis=2)
    tw_s = np.repeat(np.sin(at)[:, :, None], LANES, axis=2)
    a2 = 2.0 * np.pi * ((n2[:, None] * n2[None, :]) % DFT_MINOR) / DFT_MINOR
    f2 = np.concatenate([np.cos(a2), np.sin(a2)], axis=1) * inv
    return dict(f1=jnp.asarray(f1, jnp.bfloat16), tw_c=jnp.asarray(tw_c, jnp.float32),
                tw_s=jnp.asarray(tw_s, jnp.float32), f2=jnp.asarray(f2, jnp.bfloat16))


def _fourier_positions(zr, zi, tabs):
    b, n, w = zr.shape
    if "direct" in tabs:
        return _left_matmul(tabs["direct"], jnp.concatenate([zr, zi], axis=1), w)
    r = n // DFT_MINOR
    tn2 = 8
    zr2 = zr.reshape(b, r, DFT_MINOR * w)
    zi2 = zi.reshape(b, r, DFT_MINOR * w)
    zin = pl.BlockSpec((None, r, tn2 * w), lambda bi, i: (bi, 0, i))
    tw = pl.BlockSpec((tn2, r, LANES), lambda bi, i: (i, 0, 0))
    g = pl.pallas_call(
        functools.partial(_dft1_kernel, r=r, tn2=tn2),
        grid=(b, DFT_MINOR // tn2),
        in_specs=[zin, zin, _const_spec(tabs["f1"].shape), tw, tw],
        out_specs=pl.BlockSpec((None, 2, tn2, r, w), lambda bi, i: (bi, 0, i, 0, 0)),
        out_shape=jax.ShapeDtypeStruct((b, 2, DFT_MINOR, r, w), jnp.bfloat16),
        compiler_params=_params(2),
        name="dft1",
    )(zr2, zi2, tabs["f1"], tabs["tw_c"], tabs["tw_s"])
    y = _left_matmul(tabs["f2"], g.reshape(b, 2 * DFT_MINOR, r * w), 4 * w)
    return y.reshape(b, n, w)


def _win_kernel(*refs, has_local, nb):
    if has_local:
        sink_ref, q_ref, kp_ref, kc_ref, kn_ref, vp_ref, vc_ref, vn_ref, kx_ref, vx_ref, o_ref = refs
    else:
        sink_ref, q_ref, kx_ref, vx_ref, o_ref = refs
    i = pl.program_id(1)
    tq = q_ref.shape[0]
    lx = kx_ref.shape[0]
    lane = lax.broadcasted_iota(jnp.int32, (1, LANES), 1)
    halves = (lane < HEAD_DIM, lane >= HEAD_DIM)
    if has_local:
        r = lax.broadcasted_iota(jnp.int32, (tq, 3 * BLOCK), 0)
        c = lax.broadcasted_iota(jnp.int32, (tq, 3 * BLOCK), 1)
        kpos = (i - 1) * BLOCK + c
        mask = (jnp.abs(c - BLOCK - r) <= WINDOW) & (kpos >= 0) & (kpos < nb * BLOCK)
    for j in range(SWA_KV):
        sl = slice(j * LANES, (j + 1) * LANES)
        if has_local:
            kd = jnp.concatenate([kp_ref[:, sl], kc_ref[:, sl], kn_ref[:, sl], kx_ref[:, sl]], axis=0)
            vd = jnp.concatenate([vp_ref[:, sl], vc_ref[:, sl], vn_ref[:, sl], vx_ref[:, sl]], axis=0)
        else:
            kd = kx_ref[:, sl]
            vd = vx_ref[:, sl]
        for t in range(SWA_HEADS // SWA_KV // 2):
            slab = j * (SWA_HEADS // SWA_KV // 2) + t
            q2 = q_ref[:, slab * LANES:(slab + 1) * LANES]
            acc = jnp.zeros((tq, LANES), jnp.float32)
            for e in range(2):
                sk = sink_ref[0, 2 * slab + e]
                s = _dot_nt(jnp.where(halves[e], q2, jnp.zeros_like(q2)), kd)
                if has_local:
                    s = jnp.concatenate([jnp.where(mask, s[:, :3 * BLOCK], NEG_INF), s[:, 3 * BLOCK:]], axis=1)
                m = jnp.maximum(jnp.max(s, axis=-1, keepdims=True), sk)
                p = jnp.exp(s - m)
                inv = 1.0 / (jnp.sum(p, axis=-1, keepdims=True) + jnp.exp(sk - m))
                acc = acc + _dot(_bf(p * inv), jnp.where(halves[e], vd, jnp.zeros_like(vd)))
            o_ref[:, slab * LANES:(slab + 1) * LANES] = _bf(acc)
    del lx


def _window_attention(qb, kb, vb, kx, vx, sink, has_local):
    b, n, _ = qb.shape
    nb = n // BLOCK
    lx = kx.shape[1]
    smem = pl.BlockSpec(memory_space=pltpu.SMEM)
    qspec = pl.BlockSpec((None, BLOCK, SWA_W), lambda bi, i: (bi, i, 0))
    ctx = pl.BlockSpec((None, lx, 2 * LANES), lambda bi, i: (bi, 0, 0))
    args = [sink.reshape(1, SWA_HEADS), qb]
    specs = [smem, qspec]
    if has_local:
        prv = pl.BlockSpec((None, BLOCK, 2 * LANES), lambda bi, i: (bi, jnp.maximum(i - 1, 0), 0))
        cur = pl.BlockSpec((None, BLOCK, 2 * LANES), lambda bi, i: (bi, i, 0))
        nxt = pl.BlockSpec((None, BLOCK, 2 * LANES), lambda bi, i: (bi, jnp.minimum(i + 1, nb - 1), 0))
        args += [kb, kb, kb, vb, vb, vb]
        specs += [prv, cur, nxt, prv, cur, nxt]
    args += [kx, vx]
    specs += [ctx, ctx]
    return pl.pallas_call(
        functools.partial(_win_kernel, has_local=has_local, nb=nb),
        grid=(b, nb),
        in_specs=specs,
        out_specs=qspec,
        out_shape=jax.ShapeDtypeStruct((b, n, SWA_W), jnp.bfloat16),
        compiler_params=_params(2),
        name="win",
    )(*args)


def _diff_kernel(lam_ref, q_ref, k_ref, v_ref, g_ref, o_ref, vt_ref, acc_ref, s0_ref, s1_ref, *, tk, out_scale):
    qi = pl.program_id(2)
    tq = q_ref.shape[0]
    n_chunks = vt_ref.shape[0]

    @pl.when(qi == 0)
    def _():
        def body(ci, carry):
            c0 = pl.multiple_of(ci * tk, tk)
            vt_ref[ci] = _bf(v_ref[pl.ds(c0, tk), :].astype(jnp.float32).T)
            return carry
        lax.fori_loop(0, n_chunks, body, 0)

    q = q_ref[...]
    lane = lax.broadcasted_iota(jnp.int32, (1, LANES), 1)
    qm = (jnp.where(lane < HEAD_DIM, q, jnp.zeros_like(q)), jnp.where(lane >= HEAD_DIM, q, jnp.zeros_like(q)))
    acc_ref[...] = jnp.zeros_like(acc_ref)
    s_bufs = (s0_ref, s1_ref)

    def scores(ci, s_ref):
        c0 = pl.multiple_of(ci * tk, tk)
        k = k_ref[pl.ds(c0, tk), :]
        mx = []
        for c in range(2):
            st = _dot_nt(k, qm[c])
            s_ref[c] = st
            mx.append(jnp.max(st, axis=0, keepdims=True))
        return tuple(mx)

    def softmax_pv(ci, s_ref, mx, stats):
        vt = vt_ref[ci]
        out = []
        for c in range(2):
            m_old, l_old = stats[2 * c], stats[2 * c + 1]
            m_new = jnp.maximum(m_old, mx[c])
            alpha = jnp.exp(m_old - m_new)
            p = jnp.exp(s_ref[c] - m_new)
            out += [m_new, alpha * l_old + jnp.sum(p, axis=0, keepdims=True)]
            acc_ref[c] = alpha * acc_ref[c] + _dot(vt, _bf(p))
        return tuple(out)

    neg = jnp.full((1, tq), NEG_INF, jnp.float32)
    zero = jnp.zeros((1, tq), jnp.float32)
    n_pairs = (n_chunks - 1) // 2

    def pair(j, carry):
        stats, mx = carry[:4], carry[4:]
        mx_odd = scores(2 * j + 1, s_bufs[1])
        stats = softmax_pv(2 * j, s_bufs[0], mx, stats)
        mx_even = scores(2 * j + 2, s_bufs[0])
        stats = softmax_pv(2 * j + 1, s_bufs[1], mx_odd, stats)
        return stats + mx_even

    carry = lax.fori_loop(0, n_pairs, pair, (neg, zero, neg, zero) + scores(0, s_bufs[0]))
    stats, mx = carry[:4], carry[4:]
    if n_chunks % 2 == 0:
        mx_last = scores(n_chunks - 1, s_bufs[1])
        stats = softmax_pv(n_chunks - 2, s_bufs[0], mx, stats)
        stats = softmax_pv(n_chunks - 1, s_bufs[1], mx_last, stats)
    else:
        stats = softmax_pv(n_chunks - 1, s_bufs[0], mx, stats)
    l1, l2 = stats[1], stats[3]
    lam = lam_ref[0, 0]
    o = acc_ref[0] * (1.0 / l1) - lam * (acc_ref[1] * (1.0 / l2))
    y = o * lax.rsqrt(jnp.mean(o * o, axis=0, keepdims=True) + NORM_EPS) * g_ref[...] * out_scale
    o_ref[...] = _bf(y.T)


def _pick_tk(nk, cap):
    best = LANES
    for t in range(LANES, min(cap, nk) + 1, LANES):
        if nk % t == 0:
            best = t
    return best


def _diff_attention(qd, k_all, v_all, lam, subln_g, lam_init, tq, tk_cap):
    b, n, _ = qd.shape
    nk = k_all.shape[1]
    tk = _pick_tk(nk, tk_cap)
    smem = pl.BlockSpec(memory_space=pltpu.SMEM)
    qspec = pl.BlockSpec((None, tq, LANES), lambda bi, h, i: (bi, i, h))
    kvspec = pl.BlockSpec((None, nk, LANES), lambda bi, h, i: (bi, 0, h))
    gain = jnp.broadcast_to(subln_g.astype(jnp.float32)[:, None], (LANES, tq))
    return pl.pallas_call(
        functools.partial(_diff_kernel, tk=tk, out_scale=1.0 - lam_init),
        grid=(b, DIFF_HEADS, n // tq),
        in_specs=[smem, qspec, kvspec, kvspec, _const_spec((LANES, tq))],
        out_specs=qspec,
        out_shape=jax.ShapeDtypeStruct((b, n, DIFF_W), jnp.bfloat16),
        scratch_shapes=[pltpu.VMEM((nk // tk, LANES, tk), jnp.bfloat16), pltpu.VMEM((2, LANES, tq), jnp.float32),
                        pltpu.VMEM((2, tk, tq), jnp.float32), pltpu.VMEM((2, tk, tq), jnp.float32)],
        compiler_params=_params(3),
        name="diff",
    )(lam, qd, k_all, v_all, gain)


def _merge_kernel(x_ref, gt_ref, ya_ref, sza_ref, yb_ref, szb_ref, uc_ref, up_ref, un_ref, cb_ref, szc_ref,
                  yd_ref, szd_ref, g_ref, cw_ref, wa_ref, wb_ref, wc_ref, wd_ref, wo_ref, o_ref, *, tm, nt):
    i = pl.program_id(1)
    f32 = jnp.float32
    u = uc_ref[...].astype(f32)
    row = lax.broadcasted_iota(jnp.int32, (tm, 1), 0)
    prev_row = jnp.where(i > 0, up_ref[7:8, :].astype(f32), 0.0)
    next_row = jnp.where(i < nt - 1, un_ref[0:1, :].astype(f32), 0.0)
    u_prev = jnp.where(row == 0, prev_row, pltpu.roll(u, 1, 0))
    u_next = jnp.where(row == tm - 1, next_row, pltpu.roll(u, tm - 1, 0))
    cw = cw_ref[...]
    conv = u_prev * cw[0:1, :] + u * cw[1:2, :] + u_next * cw[2:3, :]
    yc = cb_ref[...].astype(f32) * conv
    d = x_ref.shape[-1]
    branches = ((ya_ref[...].astype(f32), sza_ref, wa_ref), (yb_ref[...].astype(f32), szb_ref, wb_ref),
                (yc, szc_ref, wc_ref), (yd_ref[...].astype(f32), szd_ref, wd_ref))
    mixed = jnp.zeros((tm, d), f32)
    for j, (y, sz_ref, w_ref) in enumerate(branches):
        t = _dot(_bf(y * sz_ref[...].astype(f32)), w_ref[...])
        mixed = mixed + g_ref[:, j * d:(j + 1) * d].astype(f32) * t
    out = _dot(_bf(mixed), wo_ref[...])
    o_ref[...] = x_ref[...] + gt_ref[...] * out


def _merge(x, gate, ya, sza, yb, szb, uc, cb, szc, yd, szd, g, conv_w, w_a, w_b, w_c, w_d, w_out, tm):
    b, n, d = x.shape
    nt = n // tm
    tok = lambda w: pl.BlockSpec((None, tm, w), lambda bi, i: (bi, i, 0))
    hb = tm // 8
    halo_p = pl.BlockSpec((None, 8, CONV_W), lambda bi, i: (bi, jnp.maximum(i * hb - 1, 0), 0))
    halo_n = pl.BlockSpec((None, 8, CONV_W), lambda bi, i: (bi, jnp.minimum((i + 1) * hb, n // 8 - 1), 0))
    per_b = pl.BlockSpec((None, 1, d), lambda bi, i: (bi, 0, 0))
    ws = [_bf(w_a), _bf(w_b), _bf(w_c), _bf(w_d), _bf(w_out)]
    return pl.pallas_call(
        functools.partial(_merge_kernel, tm=tm, nt=nt),
        grid=(b, nt),
        in_specs=[tok(d), per_b, tok(FOURIER_W), tok(FOURIER_W), tok(SWA_W), tok(SWA_W), tok(CONV_W), halo_p, halo_n,
                  tok(CONV_W), tok(CONV_W), tok(DIFF_W), tok(DIFF_W), tok(N_BRANCH * d), _const_spec(conv_w.shape)]
                 + [_const_spec(w.shape) for w in ws],
        out_specs=tok(d),
        out_shape=jax.ShapeDtypeStruct((b, n, d), jnp.float32),
        compiler_params=_params(2),
        name="merge",
    )(x, gate, ya, sza, yb, szb, uc, uc, uc, cb, szc, yd, szd, g, conv_w.astype(jnp.float32), *ws)


def _rope_tables(n):
    rows = n // GRID_W
    row = jnp.broadcast_to(jnp.arange(rows, dtype=jnp.float32)[:, None], (rows, GRID_W)).reshape(-1)
    col = jnp.broadcast_to(jnp.arange(GRID_W, dtype=jnp.float32)[None, :], (rows, GRID_W)).reshape(-1)
    nf = HEAD_DIM // 4
    inv = ROPE_BASE ** (-jnp.arange(nf, dtype=jnp.float32) / nf)
    ar = row[:, None] * inv[None, :]
    ac = col[:, None] * inv[None, :]
    cos = jnp.concatenate([jnp.cos(ar), jnp.cos(ar), jnp.cos(ac), jnp.cos(ac)], axis=-1)
    sin = jnp.concatenate([jnp.sin(ar), jnp.sin(ar), jnp.sin(ac), jnp.sin(ac)], axis=-1)
    sign = jnp.where((jnp.arange(HEAD_DIM) % 32) < 16, -1.0, 1.0)
    return jnp.tile(cos, (1, 2)), jnp.tile(sin * sign, (1, 2))


def _channel_dft():
    k = np.arange(HEAD_DIM)
    ang = 2.0 * np.pi * ((k[:, None] * k[None, :]) % HEAD_DIM) / HEAD_DIM
    eye = np.eye(FOURIER_W // HEAD_DIM)
    inv = 1.0 / math.sqrt(HEAD_DIM)
    return jnp.asarray(np.concatenate([np.kron(eye, np.cos(ang)), -np.kron(eye, np.sin(ang))], axis=1) * inv,
                       jnp.bfloat16)


def _extend_w_in(w):
    s = _SRC
    k0, k1 = s["b_k"][0], s["b_k"][0] + HEAD_DIM
    v0, v1 = s["b_v"][0], s["b_v"][0] + HEAD_DIM
    pieces = [w[:, s["a_x"][0]:s["b_q"][1]],
              w[:, k0:k1], w[:, k0:k1], w[:, k1:k1 + HEAD_DIM], w[:, k1:k1 + HEAD_DIM],
              w[:, v0:v1], w[:, v0:v1], w[:, v1:v1 + HEAD_DIM], w[:, v1:v1 + HEAD_DIM],
              w[:, s["c_x"][0]:]]
    return _bf(jnp.concatenate(pieces, axis=1))


def _mix(stream, scale, shift, gate, ctx_kv, lp, consts, lam, lam_init, rope, dft, tm, tq, tk, has_local):
    parts = _project(stream, scale, shift, lp["norm_g"], lp["w_ext"], consts["cs_bd"], consts["mg"],
                     lp["head_gains"], rope, tm)
    zr, zi, qb, kb, vb, uc, cb, qd, kd, vd, sza, szb, szc, szd, g = parts
    if ctx_kv is None:
        kx_b, vx_b, k_all, v_all = kb, vb, kd, vd
    else:
        kx_b, vx_b, kx_d, vx_d = ctx_kv
        k_all = jnp.concatenate([kx_d, kd], axis=1)
        v_all = jnp.concatenate([vx_d, vd], axis=1)
    ya = _fourier_positions(zr, zi, dft)
    yb = _window_attention(qb, kb, vb, kx_b, vx_b, lp["sink"], has_local)
    yd = _diff_attention(qd, k_all, v_all, lam, lp["subln"], lam_init, tq, tk)
    new = _merge(stream, gate, ya, sza, yb, szb, uc, cb, szc, yd, szd, g, lp["conv_w"],
                 lp["w_o_a"], lp["w_o_b"], lp["w_o_c"], lp["w_o_d"], lp["w_out"], tm)
    return new, (kb, vb, kd, vd)


def kernel(x, c, ctx, c_ctx, norm_g, w_mod, b_mod, w_in, q_norm_b, k_norm_b, sink_b, conv_w, q_norm_d, k_norm_d,
           lam_q1, lam_k1, lam_q2, lam_k2, subln_d, w_o_a, w_o_b, w_o_c, w_o_d, w_out):
    b, n, d = x.shape
    lc = ctx.shape[1]
    depth = w_in.shape[0]
    rope = _rope_tables(n)
    consts = dict(cs_bd=_channel_dft(),
                  mg=jnp.asarray(np.kron(np.eye(2), np.full((HEAD_DIM, HEAD_DIM), 1.0 / HEAD_DIM)), jnp.bfloat16))
    dft_x = _dft_tables(n)
    dft_c = _dft_tables(lc)
    c_rows = jnp.zeros((8, d), jnp.float32).at[:b].set(c).at[b].set(c_ctx)
    tm = min(256, n)
    tq = min(256, n)
    tk = 1280
    for l in range(depth):
        last = l == depth - 1
        lam_init = 0.8 - 0.6 * math.exp(-0.3 * l)
        lam_vecs = jnp.stack([lam_q1[l], lam_k1[l], lam_q2[l], lam_k2[l]]).astype(jnp.float32)
        mod, lam_o = _modulation(c_rows, w_mod[l], b_mod[l], lam_vecs, lam_init)
        lam = lam_o[0:1, 0:1]
        shift, scale, gate = (mod[:b, None, j * d:(j + 1) * d] for j in range(3))
        shift_c, scale_c, gate_c = (jnp.broadcast_to(mod[b:b + 1, None, j * d:(j + 1) * d], (b, 1, d))
                                    for j in range(3))
        tile2 = lambda v: jnp.tile(v.astype(jnp.float32), 2)
        lp = dict(norm_g=norm_g[l], w_ext=_extend_w_in(w_in[l]), sink=sink_b[l].astype(jnp.float32),
                  head_gains=jnp.stack([tile2(q_norm_b[l]), tile2(k_norm_b[l]), tile2(q_norm_d[l]), tile2(k_norm_d[l])]),
                  subln=subln_d[l], conv_w=conv_w[l], w_o_a=w_o_a[l], w_o_b=w_o_b[l], w_o_c=w_o_c[l],
                  w_o_d=w_o_d[l], w_out=w_out[l])
        if last:
            parts = _project(ctx, scale_c, shift_c, lp["norm_g"], lp["w_ext"], consts["cs_bd"], consts["mg"],
                             lp["head_gains"], None, min(256, lc))
            ctx_kv = (parts[3], parts[4], parts[8], parts[9])
        else:
            ctx, ctx_kv = _mix(ctx, scale_c, shift_c, gate_c, None, lp, consts, lam, lam_init, None, dft_c,
                               min(256, lc), min(256, lc), tk, False)
        x, _ = _mix(x, scale, shift, gate, ctx_kv, lp, consts, lam, lam_init, rope, dft_x, tm, tq, tk, True)
    return x
```

```python
import functools
import math

import jax
import jax.numpy as jnp
import numpy as np
from jax import lax
from jax.experimental import pallas as pl
from jax.experimental.pallas import tpu as pltpu

HEAD_DIM = 64
LANES = 128
FOURIER_W = 384
SWA_HEADS = 8
SWA_KV = 2
SWA_W = SWA_HEADS * HEAD_DIM
WINDOW = 128
BLOCK = 128
CONV_W = 384
CONV_K = 3
DIFF_HEADS = 4
DIFF_W = DIFF_HEADS * 2 * HEAD_DIM
N_BRANCH = 4
ROPE_BASE = 10000.0
NORM_EPS = 1e-6
NEG_INF = -1e30
GRID_W = 64
DFT_MINOR = 128
SUM_ROWS = 16
LOG2E = math.log2(math.e)
VMEM_LIMIT = 56 * 1024 * 1024

_SRC = dict(a_x=(0, 384), b_q=(384, 896), b_k=(896, 1024), b_v=(1024, 1152), c_x=(1152, 1536),
            c_b=(1536, 1920), c_c=(1920, 2304), d_q=(2304, 2816), d_k=(2816, 3328), d_v=(3328, 3840),
            z=(3840, 5632), m_g=(5632, 9728))


def _bf(x):
    return x.astype(jnp.bfloat16)


def _dot(a, b):
    return jnp.dot(a, b, preferred_element_type=jnp.float32)


def _dot_nt(a, b):
    return lax.dot_general(a, b, (((1,), (1,)), ((), ())), preferred_element_type=jnp.float32)


def _params(n_axes, flags=None):
    return pltpu.CompilerParams(dimension_semantics=("arbitrary",) * n_axes, vmem_limit_bytes=VMEM_LIMIT,
                                flags=flags)


def _const_spec(shape):
    nd = len(shape)
    return pl.BlockSpec(shape, lambda *_: (0,) * nd, pipeline_mode=pl.Buffered(1))


def _mod_kernel(c_ref, w_ref, b_ref, lam_ref, mod_ref, lamo_ref, *, lam_init):
    c = c_ref[...]
    s = c * jax.nn.sigmoid(c)
    mod_ref[...] = _dot(_bf(s), w_ref[...]) + b_ref[...]
    lv = lam_ref[...]
    a1 = jnp.sum(lv[0:1, :] * lv[1:2, :], axis=-1, keepdims=True)
    a2 = jnp.sum(lv[2:3, :] * lv[3:4, :], axis=-1, keepdims=True)
    lam = jnp.exp(a1) - jnp.exp(a2) + lam_init
    lamo_ref[...] = jnp.broadcast_to(lam, lamo_ref.shape)


def _modulation(c_rows, w_mod, b_mod, lam_vecs, lam_init):
    r, d = c_rows.shape
    return pl.pallas_call(
        functools.partial(_mod_kernel, lam_init=lam_init),
        out_shape=(jax.ShapeDtypeStruct((r, 3 * d), jnp.float32),
                   jax.ShapeDtypeStruct((8, LANES), jnp.float32)),
        name="mod",
    )(c_rows, _bf(w_mod), b_mod.reshape(1, 3 * d), lam_vecs)


def _head_norm_rope(t, gain, mg, cos, sin_s, lane_lo, scale):
    ms = _dot(_bf(t * t), mg)
    y = t * lax.rsqrt(ms + NORM_EPS) * gain
    if cos is not None:
        rot = jnp.where(lane_lo, pltpu.roll(y, LANES - 16, 1), pltpu.roll(y, 16, 1))
        y = y * cos + rot * sin_s
    if scale != 1.0:
        y = y * scale
    return y


def _proj_kernel(*refs, use_rope, tm):
    if use_rope:
        (x_ref, sc_ref, sh_ref, g_ref, w_ref, cs_ref, mg_ref, hg_ref, cos_ref, sin_ref), outs = refs[:10], refs[10:]
    else:
        (x_ref, sc_ref, sh_ref, g_ref, w_ref, cs_ref, mg_ref, hg_ref), outs = refs[:8], refs[8:]
        cos_ref = sin_ref = None
    (zr_ref, zi_ref, qb_ref, kb_ref, vb_ref, uc_ref, cb_ref, qd_ref, kd_ref, vd_ref,
     sza_ref, szb_ref, szc_ref, szd_ref, gate_ref) = outs

    x = x_ref[...]
    y = x * lax.rsqrt(jnp.mean(x * x, axis=-1, keepdims=True) + NORM_EPS) * g_ref[...]
    h = _bf(y * (1.0 + sc_ref[...]) + sh_ref[...])

    def proj(c0, width):
        return _dot(h, w_ref[:, c0:c0 + width])

    col = 0
    a = _bf(proj(col, FOURIER_W))
    zz = _dot(a, cs_ref[...])
    zr_ref[...] = _bf(zz[:, :FOURIER_W])
    zi_ref[...] = _bf(zz[:, FOURIER_W:])
    col += FOURIER_W

    mg = mg_ref[...]
    if use_rope:
        cos = cos_ref[...]
        sin_s = sin_ref[...]
    else:
        cos = sin_s = None
    lane = lax.broadcasted_iota(jnp.int32, (tm, LANES), 1)
    lane_lo = (lane & 31) < 16
    qscale = HEAD_DIM ** -0.5

    def normed(c0, width, gain_row, out_ref, scale):
        t = proj(c0, width)
        gain = hg_ref[gain_row:gain_row + 1, :]
        for s in range(width // LANES):
            ts = t[:, s * LANES:(s + 1) * LANES]
            out_ref[:, s * LANES:(s + 1) * LANES] = _bf(
                _head_norm_rope(ts, gain, mg, cos, sin_s, lane_lo, scale))

    normed(col, SWA_W, 0, qb_ref, qscale); col += SWA_W
    normed(col, 2 * LANES, 1, kb_ref, 1.0); col += 2 * LANES
    vb_ref[...] = _bf(proj(col, 2 * LANES)); col += 2 * LANES
    c3 = proj(col, 3 * CONV_W); col += 3 * CONV_W
    uc_ref[...] = _bf(c3[:, 2 * CONV_W:] * c3[:, :CONV_W])
    cb_ref[...] = _bf(c3[:, CONV_W:2 * CONV_W])
    normed(col, DIFF_W, 2, qd_ref, qscale * LOG2E); col += DIFF_W
    normed(col, DIFF_W, 3, kd_ref, 1.0); col += DIFF_W
    vd_ref[...] = _bf(proj(col, DIFF_W)); col += DIFF_W
    for ref, width in ((sza_ref, FOURIER_W), (szb_ref, SWA_W), (szc_ref, CONV_W), (szd_ref, DIFF_W)):
        z = proj(col, width)
        ref[...] = _bf(z * jax.nn.sigmoid(z))
        col += width
    d = x.shape[-1]
    for j in range(N_BRANCH):
        gate_ref[:, j * d:(j + 1) * d] = _bf(jax.nn.sigmoid(proj(col, d)))
        col += d


_PROJ_OUT_W = (FOURIER_W, FOURIER_W, SWA_W, 2 * LANES, 2 * LANES, CONV_W, CONV_W, DIFF_W, DIFF_W, DIFF_W,
               FOURIER_W, SWA_W, CONV_W, DIFF_W)


def _project(x, scale, shift, norm_g, w_ext, cs_bd, mg, head_gains, rope, tm):
    b, n, d = x.shape
    use_rope = rope is not None
    tok = lambda w: pl.BlockSpec((None, tm, w), lambda bi, i: (bi, i, 0))
    per_b = pl.BlockSpec((None, 1, d), lambda bi, i: (bi, 0, 0))
    in_specs = [tok(d), per_b, per_b, _const_spec((1, d)), _const_spec(w_ext.shape), _const_spec(cs_bd.shape),
                _const_spec(mg.shape), _const_spec(head_gains.shape)]
    args = [x, scale, shift, norm_g.reshape(1, d), w_ext, cs_bd, mg, head_gains]
    if use_rope:
        tab = pl.BlockSpec((tm, LANES), lambda bi, i: (i, 0))
        in_specs += [tab, tab]
        args += list(rope)
    widths = _PROJ_OUT_W + (N_BRANCH * d,)
    return pl.pallas_call(
        functools.partial(_proj_kernel, use_rope=use_rope, tm=tm),
        grid=(b, n // tm),
        in_specs=in_specs,
        out_specs=[tok(w) for w in widths],
        out_shape=[jax.ShapeDtypeStruct((b, n, w), jnp.bfloat16) for w in widths],
        compiler_params=_params(2),
        name="proj",
    )(*args)


def _dft1_kernel(zr_ref, zi_ref, f_ref, tc_ref, ts_ref, o_ref, *, r, tn2):
    z = jnp.concatenate([zr_ref[...], zi_ref[...]], axis=0)
    a = _dot(f_ref[...], z)
    for j in range(tn2):
        ar = a[:r, j * FOURIER_W:(j + 1) * FOURIER_W]
        ai = a[r:, j * FOURIER_W:(j + 1) * FOURIER_W]
        tc = jnp.concatenate([tc_ref[j]] * (FOURIER_W // LANES), axis=-1)
        ts = jnp.concatenate([ts_ref[j]] * (FOURIER_W // LANES), axis=-1)
        o_ref[0, j] = _bf(ar * tc + ai * ts)
        o_ref[1, j] = _bf(ai * tc - ar * ts)


def _left_matmul_kernel(m_ref, x_ref, o_ref):
    o_ref[...] = _bf(_dot(m_ref[...], x_ref[...]))


def _left_matmul(mat, x, tc):
    b, k, c = x.shape
    rows = mat.shape[0]
    return pl.pallas_call(
        _left_matmul_kernel,
        grid=(b, c // tc),
        in_specs=[_const_spec(mat.shape), pl.BlockSpec((None, k, tc), lambda bi, i: (bi, 0, i))],
        out_specs=pl.BlockSpec((None, rows, tc), lambda bi, i: (bi, 0, i)),
        out_shape=jax.ShapeDtypeStruct((b, rows, c), jnp.bfloat16),
        compiler_params=_params(2),
        name="dft2",
    )(mat, x)


def _dft_tables(n):
    inv = 1.0 / math.sqrt(n)
    if n <= 2 * DFT_MINOR:
        k = np.arange(n)
        ang = 2.0 * np.pi * ((k[:, None] * k[None, :]) % n) / n
        return dict(direct=jnp.asarray(np.concatenate([np.cos(ang), np.sin(ang)], axis=1) * inv, jnp.bfloat16))
    r = n // DFT_MINOR
    k1 = np.arange(r)
    a1 = 2.0 * np.pi * ((k1[:, None] * k1[None, :]) % r) / r
    c1, s1 = np.cos(a1), np.sin(a1)
    f1 = np.block([[c1, s1], [-s1, c1]])
    n2 = np.arange(DFT_MINOR)
    at = 2.0 * np.pi * (n2[:, None] * k1[None, :]) / n
    tw_c = np.repeat(np.cos(at)[:, :, None], LANES, axis=2)
    tw_s = np.repeat(np.sin(at)[:, :, None], LANES, axis=2)
    a2 = 2.0 * np.pi * ((n2[:, None] * n2[None, :]) % DFT_MINOR) / DFT_MINOR
    f2 = np.concatenate([np.cos(a2), np.sin(a2)], axis=1) * inv
    return dict(f1=jnp.asarray(f1, jnp.bfloat16), tw_c=jnp.asarray(tw_c, jnp.float32),
                tw_s=jnp.asarray(tw_s, jnp.float32), f2=jnp.asarray(f2, jnp.bfloat16))


def _fourier_positions(zr, zi, tabs):
    b, n, w = zr.shape
    if "direct" in tabs:
        return _left_matmul(tabs["direct"], jnp.concatenate([zr, zi], axis=1), w)
    r = n // DFT_MINOR
    tn2 = 8
    zr2 = zr.reshape(b, r, DFT_MINOR * w)
    zi2 = zi.reshape(b, r, DFT_MINOR * w)
    zin = pl.BlockSpec((None, r, tn2 * w), lambda bi, i: (bi, 0, i))
    tw = pl.BlockSpec((tn2, r, LANES), lambda bi, i: (i, 0, 0))
    g = pl.pallas_call(
        functools.partial(_dft1_kernel, r=r, tn2=tn2),
        grid=(b, DFT_MINOR // tn2),
        in_specs=[zin, zin, _const_spec(tabs["f1"].shape), tw, tw],
        out_specs=pl.BlockSpec((None, 2, tn2, r, w), lambda bi, i: (bi, 0, i, 0, 0)),
        out_shape=jax.ShapeDtypeStruct((b, 2, DFT_MINOR, r, w), jnp.bfloat16),
        compiler_params=_params(2),
        name="dft1",
    )(zr2, zi2, tabs["f1"], tabs["tw_c"], tabs["tw_s"])
    y = _left_matmul(tabs["f2"], g.reshape(b, 2 * DFT_MINOR, r * w), 4 * w)
    return y.reshape(b, n, w)


def _win_kernel(*refs, has_local, nb):
    if has_local:
        sink_ref, q_ref, kp_ref, kc_ref, kn_ref, vp_ref, vc_ref, vn_ref, kx_ref, vx_ref, o_ref = refs
    else:
        sink_ref, q_ref, kx_ref, vx_ref, o_ref = refs
    i = pl.program_id(1)
    tq = q_ref.shape[0]
    lx = kx_ref.shape[0]
    lane = lax.broadcasted_iota(jnp.int32, (1, LANES), 1)
    halves = (lane < HEAD_DIM, lane >= HEAD_DIM)
    if has_local:
        r = lax.broadcasted_iota(jnp.int32, (tq, 3 * BLOCK), 0)
        c = lax.broadcasted_iota(jnp.int32, (tq, 3 * BLOCK), 1)
        kpos = (i - 1) * BLOCK + c
        mask = (jnp.abs(c - BLOCK - r) <= WINDOW) & (kpos >= 0) & (kpos < nb * BLOCK)
    for j in range(SWA_KV):
        sl = slice(j * LANES, (j + 1) * LANES)
        if has_local:
            kd = jnp.concatenate([kp_ref[:, sl], kc_ref[:, sl], kn_ref[:, sl], kx_ref[:, sl]], axis=0)
            vd = jnp.concatenate([vp_ref[:, sl], vc_ref[:, sl], vn_ref[:, sl], vx_ref[:, sl]], axis=0)
        else:
            kd = kx_ref[:, sl]
            vd = vx_ref[:, sl]
        for t in range(SWA_HEADS // SWA_KV // 2):
            slab = j * (SWA_HEADS // SWA_KV // 2) + t
            q2 = q_ref[:, slab * LANES:(slab + 1) * LANES]
            acc = jnp.zeros((tq, LANES), jnp.float32)
            for e in range(2):
                sk = sink_ref[0, 2 * slab + e]
                s = _dot_nt(jnp.where(halves[e], q2, jnp.zeros_like(q2)), kd)
                if has_local:
                    s = jnp.concatenate([jnp.where(mask, s[:, :3 * BLOCK], NEG_INF), s[:, 3 * BLOCK:]], axis=1)
                m = jnp.maximum(jnp.max(s, axis=-1, keepdims=True), sk)
                p = jnp.exp(s - m)
                inv = 1.0 / (jnp.sum(p, axis=-1, keepdims=True) + jnp.exp(sk - m))
                acc = acc + _dot(_bf(p * inv), jnp.where(halves[e], vd, jnp.zeros_like(vd)))
            o_ref[:, slab * LANES:(slab + 1) * LANES] = _bf(acc)
    del lx


def _window_attention(qb, kb, vb, kx, vx, sink, has_local):
    b, n, _ = qb.shape
    nb = n // BLOCK
    lx = kx.shape[1]
    smem = pl.BlockSpec(memory_space=pltpu.SMEM)
    qspec = pl.BlockSpec((None, BLOCK, SWA_W), lambda bi, i: (bi, i, 0))
    ctx = pl.BlockSpec((None, lx, 2 * LANES), lambda bi, i: (bi, 0, 0))
    args = [sink.reshape(1, SWA_HEADS), qb]
    specs = [smem, qspec]
    if has_local:
        prv = pl.BlockSpec((None, BLOCK, 2 * LANES), lambda bi, i: (bi, jnp.maximum(i - 1, 0), 0))
        cur = pl.BlockSpec((None, BLOCK, 2 * LANES), lambda bi, i: (bi, i, 0))
        nxt = pl.BlockSpec((None, BLOCK, 2 * LANES), lambda bi, i: (bi, jnp.minimum(i + 1, nb - 1), 0))
        args += [kb, kb, kb, vb, vb, vb]
        specs += [prv, cur, nxt, prv, cur, nxt]
    args += [kx, vx]
    specs += [ctx, ctx]
    return pl.pallas_call(
        functools.partial(_win_kernel, has_local=has_local, nb=nb),
        grid=(b, nb),
        in_specs=specs,
        out_specs=qspec,
        out_shape=jax.ShapeDtypeStruct((b, n, SWA_W), jnp.bfloat16),
        compiler_params=_params(2),
        name="win",
    )(*args)


def _diff_kernel(lam_ref, q_ref, k_ref, v_ref, g_ref, o_ref, vt_ref, acc_ref, s0_ref, s1_ref, p0_ref, p1_ref,
                 *, tk, out_scale):
    qi = pl.program_id(2)
    tq = q_ref.shape[0]
    n_chunks = vt_ref.shape[0]

    @pl.when(qi == 0)
    def _():
        def body(ci, carry):
            c0 = pl.multiple_of(ci * tk, tk)
            vt_ref[ci, :LANES, :] = _bf(v_ref[pl.ds(c0, tk), :].astype(jnp.float32).T)
            vt_ref[ci, LANES:, :] = jnp.ones((SUM_ROWS, tk), jnp.bfloat16)
            return carry
        lax.fori_loop(0, n_chunks, body, 0)

    q = q_ref[...]
    lane = lax.broadcasted_iota(jnp.int32, (1, LANES), 1)
    qm = (jnp.where(lane < HEAD_DIM, q, jnp.zeros_like(q)), jnp.where(lane >= HEAD_DIM, q, jnp.zeros_like(q)))
    acc_ref[...] = jnp.zeros_like(acc_ref)
    s_bufs = (s0_ref, s1_ref)
    p_bufs = (p0_ref, p1_ref)

    def scores(ci, slot):
        c0 = ci * tk if isinstance(ci, int) else pl.multiple_of(ci * tk, tk)
        k = k_ref[pl.ds(c0, tk), :]
        mx = []
        for c in range(2):
            st = _dot_nt(k, qm[c])
            s_bufs[slot][c] = st
            mx.append(jnp.max(st, axis=0, keepdims=True))
        return tuple(mx)

    def probs(slot, mx, m_run):
        m_new, alpha = [], []
        for c in range(2):
            m = jnp.maximum(m_run[c], mx[c])
            alpha.append(jnp.exp2(m_run[c] - m))
            p_bufs[slot][c] = _bf(jnp.exp2(s_bufs[slot][c] - m))
            m_new.append(m)
        return tuple(m_new), tuple(alpha)

    def accumulate(ci, slot, alpha):
        vt = vt_ref[ci]
        for c in range(2):
            acc_ref[c] = alpha[c] * acc_ref[c] + _dot(vt, p_bufs[slot][c])

    def step(t, par, do_scores, do_probs, do_acc, state):
        m_run, mx, alpha = state
        mx_next = scores(t + 2, par) if do_scores else mx
        if do_probs:
            m_run, alpha_next = probs(1 - par, mx, m_run)
        else:
            alpha_next = alpha
        if do_acc:
            accumulate(t, par, alpha)
        return m_run, mx_next, alpha_next

    neg = jnp.full((1, tq), NEG_INF, jnp.float32)
    one = jnp.ones((1, tq), jnp.float32)
    state = ((neg, neg), (neg, neg), (one, one))
    for t in (-2, -1):
        state = step(t, t % 2, t + 2 < n_chunks, 0 <= t + 1 < n_chunks, False, state)
    n_steady = max(n_chunks - 2, 0)

    def pair(j, state):
        state = step(2 * j, 0, True, True, True, state)
        return step(2 * j + 1, 1, True, True, True, state)

    state = lax.fori_loop(0, n_steady // 2, pair, state)
    if n_steady % 2:
        state = step(n_steady - 1, 0, True, True, True, state)
    for t in range(n_steady, n_chunks):
        state = step(t, t % 2, False, t + 1 < n_chunks, True, state)
    lam = lam_ref[0, 0]
    a1 = acc_ref[0]
    a2 = acc_ref[1]
    o = a1[:LANES] * (1.0 / a1[LANES:LANES + 1]) - lam * (a2[:LANES] * (1.0 / a2[LANES:LANES + 1]))
    y = o * lax.rsqrt(jnp.mean(o * o, axis=0, keepdims=True) + NORM_EPS) * g_ref[...] * out_scale
    o_ref[...] = _bf(y.T)


def _pick_tk(nk, cap):
    best = LANES
    for t in range(LANES, min(cap, nk) + 1, LANES):
        if nk % t == 0:
            best = t
    return best


def _diff_attention(qd, k_all, v_all, lam, subln_g, lam_init, tq, tk_cap):
    b, n, _ = qd.shape
    nk = k_all.shape[1]
    tk = _pick_tk(nk, tk_cap)
    smem = pl.BlockSpec(memory_space=pltpu.SMEM)
    qspec = pl.BlockSpec((None, tq, LANES), lambda bi, h, i: (bi, i, h))
    kvspec = pl.BlockSpec((None, nk, LANES), lambda bi, h, i: (bi, 0, h))
    gain = jnp.broadcast_to(subln_g.astype(jnp.float32)[:, None], (LANES, tq))
    return pl.pallas_call(
        functools.partial(_diff_kernel, tk=tk, out_scale=1.0 - lam_init),
        grid=(b, DIFF_HEADS, n // tq),
        in_specs=[smem, qspec, kvspec, kvspec, _const_spec((LANES, tq))],
        out_specs=qspec,
        out_shape=jax.ShapeDtypeStruct((b, n, DIFF_W), jnp.bfloat16),
        scratch_shapes=[pltpu.VMEM((nk // tk, LANES + SUM_ROWS, tk), jnp.bfloat16),
                        pltpu.VMEM((2, LANES + SUM_ROWS, tq), jnp.float32),
                        pltpu.VMEM((2, tk, tq), jnp.float32), pltpu.VMEM((2, tk, tq), jnp.float32),
                        pltpu.VMEM((2, tk, tq), jnp.bfloat16), pltpu.VMEM((2, tk, tq), jnp.bfloat16)],
        compiler_params=_params(3),
        name="diff",
    )(lam, qd, k_all, v_all, gain)


def _merge_kernel(x_ref, gt_ref, ya_ref, sza_ref, yb_ref, szb_ref, uc_ref, up_ref, un_ref, cb_ref, szc_ref,
                  yd_ref, szd_ref, g_ref, cw_ref, wa_ref, wb_ref, wc_ref, wd_ref, wo_ref, o_ref, *, tm, nt):
    i = pl.program_id(1)
    f32 = jnp.float32
    u = uc_ref[...].astype(f32)
    row = lax.broadcasted_iota(jnp.int32, (tm, 1), 0)
    prev_row = jnp.where(i > 0, up_ref[7:8, :].astype(f32), 0.0)
    next_row = jnp.where(i < nt - 1, un_ref[0:1, :].astype(f32), 0.0)
    u_prev = jnp.where(row == 0, prev_row, pltpu.roll(u, 1, 0))
    u_next = jnp.where(row == tm - 1, next_row, pltpu.roll(u, tm - 1, 0))
    cw = cw_ref[...]
    conv = u_prev * cw[0:1, :] + u * cw[1:2, :] + u_next * cw[2:3, :]
    yc = cb_ref[...].astype(f32) * conv
    d = x_ref.shape[-1]
    branches = ((ya_ref[...].astype(f32), sza_ref, wa_ref), (yb_ref[...].astype(f32), szb_ref, wb_ref),
                (yc, szc_ref, wc_ref), (yd_ref[...].astype(f32), szd_ref, wd_ref))
    mixed = jnp.zeros((tm, d), f32)
    for j, (y, sz_ref, w_ref) in enumerate(branches):
        t = _dot(_bf(y * sz_ref[...].astype(f32)), w_ref[...])
        mixed = mixed + g_ref[:, j * d:(j + 1) * d].astype(f32) * t
    out = _dot(_bf(mixed), wo_ref[...])
    o_ref[...] = x_ref[...] + gt_ref[...] * out


def _merge(x, gate, ya, sza, yb, szb, uc, cb, szc, yd, szd, g, conv_w, w_a, w_b, w_c, w_d, w_out, tm):
    b, n, d = x.shape
    nt = n // tm
    tok = lambda w: pl.BlockSpec((None, tm, w), lambda bi, i: (bi, i, 0))
    hb = tm // 8
    halo_p = pl.BlockSpec((None, 8, CONV_W), lambda bi, i: (bi, jnp.maximum(i * hb - 1, 0), 0))
    halo_n = pl.BlockSpec((None, 8, CONV_W), lambda bi, i: (bi, jnp.minimum((i + 1) * hb, n // 8 - 1), 0))
    per_b = pl.BlockSpec((None, 1, d), lambda bi, i: (bi, 0, 0))
    ws = [_bf(w_a), _bf(w_b), _bf(w_c), _bf(w_d), _bf(w_out)]
    return pl.pallas_call(
        functools.partial(_merge_kernel, tm=tm, nt=nt),
        grid=(b, nt),
        in_specs=[tok(d), per_b, tok(FOURIER_W), tok(FOURIER_W), tok(SWA_W), tok(SWA_W), tok(CONV_W), halo_p, halo_n,
                  tok(CONV_W), tok(CONV_W), tok(DIFF_W), tok(DIFF_W), tok(N_BRANCH * d), _const_spec(conv_w.shape)]
                 + [_const_spec(w.shape) for w in ws],
        out_specs=tok(d),
        out_shape=jax.ShapeDtypeStruct((b, n, d), jnp.float32),
        compiler_params=_params(2),
        name="merge",
    )(x, gate, ya, sza, yb, szb, uc, uc, uc, cb, szc, yd, szd, g, conv_w.astype(jnp.float32), *ws)


def _rope_tables(n):
    rows = n // GRID_W
    row = jnp.broadcast_to(jnp.arange(rows, dtype=jnp.float32)[:, None], (rows, GRID_W)).reshape(-1)
    col = jnp.broadcast_to(jnp.arange(GRID_W, dtype=jnp.float32)[None, :], (rows, GRID_W)).reshape(-1)
    nf = HEAD_DIM // 4
    inv = ROPE_BASE ** (-jnp.arange(nf, dtype=jnp.float32) / nf)
    ar = row[:, None] * inv[None, :]
    ac = col[:, None] * inv[None, :]
    cos = jnp.concatenate([jnp.cos(ar), jnp.cos(ar), jnp.cos(ac), jnp.cos(ac)], axis=-1)
    sin = jnp.concatenate([jnp.sin(ar), jnp.sin(ar), jnp.sin(ac), jnp.sin(ac)], axis=-1)
    sign = jnp.where((jnp.arange(HEAD_DIM) % 32) < 16, -1.0, 1.0)
    return jnp.tile(cos, (1, 2)), jnp.tile(sin * sign, (1, 2))


def _channel_dft():
    k = np.arange(HEAD_DIM)
    ang = 2.0 * np.pi * ((k[:, None] * k[None, :]) % HEAD_DIM) / HEAD_DIM
    eye = np.eye(FOURIER_W // HEAD_DIM)
    inv = 1.0 / math.sqrt(HEAD_DIM)
    return jnp.asarray(np.concatenate([np.kron(eye, np.cos(ang)), -np.kron(eye, np.sin(ang))], axis=1) * inv,
                       jnp.bfloat16)


def _extend_w_in(w):
    s = _SRC
    k0, k1 = s["b_k"][0], s["b_k"][0] + HEAD_DIM
    v0, v1 = s["b_v"][0], s["b_v"][0] + HEAD_DIM
    pieces = [w[:, s["a_x"][0]:s["b_q"][1]],
              w[:, k0:k1], w[:, k0:k1], w[:, k1:k1 + HEAD_DIM], w[:, k1:k1 + HEAD_DIM],
              w[:, v0:v1], w[:, v0:v1], w[:, v1:v1 + HEAD_DIM], w[:, v1:v1 + HEAD_DIM],
              w[:, s["c_x"][0]:]]
    return _bf(jnp.concatenate(pieces, axis=1))


def _mix(stream, scale, shift, gate, ctx_kv, lp, consts, lam, lam_init, rope, dft, tm, tq, tk, has_local):
    parts = _project(stream, scale, shift, lp["norm_g"], lp["w_ext"], consts["cs_bd"], consts["mg"],
                     lp["head_gains"], rope, tm)
    zr, zi, qb, kb, vb, uc, cb, qd, kd, vd, sza, szb, szc, szd, g = parts
    if ctx_kv is None:
        kx_b, vx_b, k_all, v_all = kb, vb, kd, vd
    else:
        kx_b, vx_b, kx_d, vx_d = ctx_kv
        k_all = jnp.concatenate([kx_d, kd], axis=1)
        v_all = jnp.concatenate([vx_d, vd], axis=1)
    ya = _fourier_positions(zr, zi, dft)
    yb = _window_attention(qb, kb, vb, kx_b, vx_b, lp["sink"], has_local)
    yd = _diff_attention(qd, k_all, v_all, lam, lp["subln"], lam_init, tq, tk)
    new = _merge(stream, gate, ya, sza, yb, szb, uc, cb, szc, yd, szd, g, lp["conv_w"],
                 lp["w_o_a"], lp["w_o_b"], lp["w_o_c"], lp["w_o_d"], lp["w_out"], tm)
    return new, (kb, vb, kd, vd)


def kernel(x, c, ctx, c_ctx, norm_g, w_mod, b_mod, w_in, q_norm_b, k_norm_b, sink_b, conv_w, q_norm_d, k_norm_d,
           lam_q1, lam_k1, lam_q2, lam_k2, subln_d, w_o_a, w_o_b, w_o_c, w_o_d, w_out):
    b, n, d = x.shape
    lc = ctx.shape[1]
    depth = w_in.shape[0]
    rope = _rope_tables(n)
    consts = dict(cs_bd=_channel_dft(),
                  mg=jnp.asarray(np.kron(np.eye(2), np.full((HEAD_DIM, HEAD_DIM), 1.0 / HEAD_DIM)), jnp.bfloat16))
    dft_x = _dft_tables(n)
    dft_c = _dft_tables(lc)
    c_rows = jnp.zeros((8, d), jnp.float32).at[:b].set(c).at[b].set(c_ctx)
    tm = min(256, n)
    tq = min(256, n)
    tk = 1280
    for l in range(depth):
        last = l == depth - 1
        lam_init = 0.8 - 0.6 * math.exp(-0.3 * l)
        lam_vecs = jnp.stack([lam_q1[l], lam_k1[l], lam_q2[l], lam_k2[l]]).astype(jnp.float32)
        mod, lam_o = _modulation(c_rows, w_mod[l], b_mod[l], lam_vecs, lam_init)
        lam = lam_o[0:1, 0:1]
        shift, scale, gate = (mod[:b, None, j * d:(j + 1) * d] for j in range(3))
        shift_c, scale_c, gate_c = (jnp.broadcast_to(mod[b:b + 1, None, j * d:(j + 1) * d], (b, 1, d))
                                    for j in range(3))
        tile2 = lambda v: jnp.tile(v.astype(jnp.float32), 2)
        lp = dict(norm_g=norm_g[l], w_ext=_extend_w_in(w_in[l]), sink=sink_b[l].astype(jnp.float32),
                  head_gains=jnp.stack([tile2(q_norm_b[l]), tile2(k_norm_b[l]), tile2(q_norm_d[l]), tile2(k_norm_d[l])]),
                  subln=subln_d[l], conv_w=conv_w[l], w_o_a=w_o_a[l], w_o_b=w_o_b[l], w_o_c=w_o_c[l],
                  w_o_d=w_o_d[l], w_out=w_out[l])
        if last:
            parts = _project(ctx, scale_c, shift_c, lp["norm_g"], lp["w_ext"], consts["cs_bd"], consts["mg"],
                             lp["head_gains"], None, min(256, lc))
            ctx_kv = (parts[3], parts[4], parts[8], parts[9])
        else:
            ctx, ctx_kv = _mix(ctx, scale_c, shift_c, gate_c, None, lp, consts, lam, lam_init, None, dft_c,
                               min(256, lc), min(256, lc), tk, False)
        x, _ = _mix(x, scale, shift, gate, ctx_kv, lp, consts, lam, lam_init, rope, dft_x, tm, tq, tk, True)
    return x
```

```python
import functools
import math

import jax
import jax.numpy as jnp
import numpy as np
from jax import lax
from jax.experimental import pallas as pl
from jax.experimental.pallas import tpu as pltpu

HEAD_DIM = 64
LANES = 128
FOURIER_W = 384
SWA_HEADS = 8
SWA_KV = 2
SWA_W = SWA_HEADS * HEAD_DIM
WINDOW = 128
BLOCK = 128
CONV_W = 384
CONV_K = 3
DIFF_HEADS = 4
DIFF_W = DIFF_HEADS * 2 * HEAD_DIM
N_BRANCH = 4
ROPE_BASE = 10000.0
NORM_EPS = 1e-6
NEG_INF = -1e30
GRID_W = 64
DFT_MINOR = 128
SUM_ROWS = 16
LOG2E = math.log2(math.e)
SAFE_SHIFT = 60.0
VMEM_LIMIT = 56 * 1024 * 1024

_SRC = dict(a_x=(0, 384), b_q=(384, 896), b_k=(896, 1024), b_v=(1024, 1152), c_x=(1152, 1536),
            c_b=(1536, 1920), c_c=(1920, 2304), d_q=(2304, 2816), d_k=(2816, 3328), d_v=(3328, 3840),
            z=(3840, 5632), m_g=(5632, 9728))


def _bf(x):
    return x.astype(jnp.bfloat16)


def _dot(a, b):
    return jnp.dot(a, b, preferred_element_type=jnp.float32)


def _dot_nt(a, b):
    return lax.dot_general(a, b, (((1,), (1,)), ((), ())), preferred_element_type=jnp.float32)


def _params(n_axes, flags=None):
    return pltpu.CompilerParams(dimension_semantics=("arbitrary",) * n_axes, vmem_limit_bytes=VMEM_LIMIT,
                                flags=flags)


def _const_spec(shape):
    nd = len(shape)
    return pl.BlockSpec(shape, lambda *_: (0,) * nd, pipeline_mode=pl.Buffered(1))


def _mod_kernel(c_ref, w_ref, b_ref, lam_ref, mod_ref, lamo_ref, *, lam_init):
    c = c_ref[...]
    s = c * jax.nn.sigmoid(c)
    mod_ref[...] = _dot(_bf(s), w_ref[...]) + b_ref[...]
    lv = lam_ref[...]
    a1 = jnp.sum(lv[0:1, :] * lv[1:2, :], axis=-1, keepdims=True)
    a2 = jnp.sum(lv[2:3, :] * lv[3:4, :], axis=-1, keepdims=True)
    lam = jnp.exp(a1) - jnp.exp(a2) + lam_init
    lamo_ref[...] = jnp.broadcast_to(lam, lamo_ref.shape)


def _modulation(c_rows, w_mod, b_mod, lam_vecs, lam_init):
    r, d = c_rows.shape
    return pl.pallas_call(
        functools.partial(_mod_kernel, lam_init=lam_init),
        out_shape=(jax.ShapeDtypeStruct((r, 3 * d), jnp.float32),
                   jax.ShapeDtypeStruct((8, LANES), jnp.float32)),
        name="mod",
    )(c_rows, _bf(w_mod), b_mod.reshape(1, 3 * d), lam_vecs)


def _head_norm_rope(t, gain, mg, cos, sin_s, lane_lo, scale):
    ms = _dot(_bf(t * t), mg)
    y = t * lax.rsqrt(ms + NORM_EPS) * gain
    if cos is not None:
        rot = jnp.where(lane_lo, pltpu.roll(y, LANES - 16, 1), pltpu.roll(y, 16, 1))
        y = y * cos + rot * sin_s
    if scale != 1.0:
        y = y * scale
    return y


def _proj_kernel(*refs, use_rope, tm):
    if use_rope:
        (x_ref, sc_ref, sh_ref, g_ref, w_ref, cs_ref, mg_ref, hg_ref, cos_ref, sin_ref), outs = refs[:10], refs[10:]
    else:
        (x_ref, sc_ref, sh_ref, g_ref, w_ref, cs_ref, mg_ref, hg_ref), outs = refs[:8], refs[8:]
        cos_ref = sin_ref = None
    (zr_ref, zi_ref, qb_ref, kb_ref, vb_ref, uc_ref, cb_ref, qd_ref, kd_ref, vd_ref,
     sza_ref, szb_ref, szc_ref, szd_ref, gate_ref) = outs

    x = x_ref[...]
    y = x * lax.rsqrt(jnp.mean(x * x, axis=-1, keepdims=True) + NORM_EPS) * g_ref[...]
    h = _bf(y * (1.0 + sc_ref[...]) + sh_ref[...])

    def proj(c0, width):
        return _dot(h, w_ref[:, c0:c0 + width])

    col = 0
    a = _bf(proj(col, FOURIER_W))
    zz = _dot(a, cs_ref[...])
    zr_ref[...] = _bf(zz[:, :FOURIER_W])
    zi_ref[...] = _bf(zz[:, FOURIER_W:])
    col += FOURIER_W

    mg = mg_ref[...]
    if use_rope:
        cos = cos_ref[...]
        sin_s = sin_ref[...]
    else:
        cos = sin_s = None
    lane = lax.broadcasted_iota(jnp.int32, (tm, LANES), 1)
    lane_lo = (lane & 31) < 16
    qscale = HEAD_DIM ** -0.5

    def normed(c0, width, gain_row, out_ref, scale):
        t = proj(c0, width)
        gain = hg_ref[gain_row:gain_row + 1, :]
        for s in range(width // LANES):
            ts = t[:, s * LANES:(s + 1) * LANES]
            out_ref[:, s * LANES:(s + 1) * LANES] = _bf(
                _head_norm_rope(ts, gain, mg, cos, sin_s, lane_lo, scale))

    normed(col, SWA_W, 0, qb_ref, qscale); col += SWA_W
    normed(col, 2 * LANES, 1, kb_ref, 1.0); col += 2 * LANES
    vb_ref[...] = _bf(proj(col, 2 * LANES)); col += 2 * LANES
    c3 = proj(col, 3 * CONV_W); col += 3 * CONV_W
    uc_ref[...] = _bf(c3[:, 2 * CONV_W:] * c3[:, :CONV_W])
    cb_ref[...] = _bf(c3[:, CONV_W:2 * CONV_W])
    normed(col, DIFF_W, 2, qd_ref, qscale * LOG2E); col += DIFF_W
    normed(col, DIFF_W, 3, kd_ref, 1.0); col += DIFF_W
    vd_ref[...] = _bf(proj(col, DIFF_W)); col += DIFF_W
    for ref, width in ((sza_ref, FOURIER_W), (szb_ref, SWA_W), (szc_ref, CONV_W), (szd_ref, DIFF_W)):
        z = proj(col, width)
        ref[...] = _bf(z * jax.nn.sigmoid(z))
        col += width
    d = x.shape[-1]
    for j in range(N_BRANCH):
        gate_ref[:, j * d:(j + 1) * d] = _bf(jax.nn.sigmoid(proj(col, d)))
        col += d


_PROJ_OUT_W = (FOURIER_W, FOURIER_W, SWA_W, 2 * LANES, 2 * LANES, CONV_W, CONV_W, DIFF_W, DIFF_W, DIFF_W,
               FOURIER_W, SWA_W, CONV_W, DIFF_W)


def _project(x, scale, shift, norm_g, w_ext, cs_bd, mg, head_gains, rope, tm):
    b, n, d = x.shape
    use_rope = rope is not None
    tok = lambda w: pl.BlockSpec((None, tm, w), lambda bi, i: (bi, i, 0))
    per_b = pl.BlockSpec((None, 1, d), lambda bi, i: (bi, 0, 0))
    in_specs = [tok(d), per_b, per_b, _const_spec((1, d)), _const_spec(w_ext.shape), _const_spec(cs_bd.shape),
                _const_spec(mg.shape), _const_spec(head_gains.shape)]
    args = [x, scale, shift, norm_g.reshape(1, d), w_ext, cs_bd, mg, head_gains]
    if use_rope:
        tab = pl.BlockSpec((tm, LANES), lambda bi, i: (i, 0))
        in_specs += [tab, tab]
        args += list(rope)
    widths = _PROJ_OUT_W + (N_BRANCH * d,)
    return pl.pallas_call(
        functools.partial(_proj_kernel, use_rope=use_rope, tm=tm),
        grid=(b, n // tm),
        in_specs=in_specs,
        out_specs=[tok(w) for w in widths],
        out_shape=[jax.ShapeDtypeStruct((b, n, w), jnp.bfloat16) for w in widths],
        compiler_params=_params(2),
        name="proj",
    )(*args)


def _dft1_kernel(zr_ref, zi_ref, f_ref, tc_ref, ts_ref, o_ref, *, r, tn2):
    z = jnp.concatenate([zr_ref[...], zi_ref[...]], axis=0)
    a = _dot(f_ref[...], z)
    for j in range(tn2):
        ar = a[:r, j * FOURIER_W:(j + 1) * FOURIER_W]
        ai = a[r:, j * FOURIER_W:(j + 1) * FOURIER_W]
        tc = jnp.concatenate([tc_ref[j]] * (FOURIER_W // LANES), axis=-1)
        ts = jnp.concatenate([ts_ref[j]] * (FOURIER_W // LANES), axis=-1)
        o_ref[0, j] = _bf(ar * tc + ai * ts)
        o_ref[1, j] = _bf(ai * tc - ar * ts)


def _left_matmul_kernel(m_ref, x_ref, o_ref):
    o_ref[...] = _bf(_dot(m_ref[...], x_ref[...]))


def _left_matmul(mat, x, tc):
    b, k, c = x.shape
    rows = mat.shape[0]
    return pl.pallas_call(
        _left_matmul_kernel,
        grid=(b, c // tc),
        in_specs=[_const_spec(mat.shape), pl.BlockSpec((None, k, tc), lambda bi, i: (bi, 0, i))],
        out_specs=pl.BlockSpec((None, rows, tc), lambda bi, i: (bi, 0, i)),
        out_shape=jax.ShapeDtypeStruct((b, rows, c), jnp.bfloat16),
        compiler_params=_params(2),
        name="dft2",
    )(mat, x)


def _dft_tables(n):
    inv = 1.0 / math.sqrt(n)
    if n <= 2 * DFT_MINOR:
        k = np.arange(n)
        ang = 2.0 * np.pi * ((k[:, None] * k[None, :]) % n) / n
        return dict(direct=jnp.asarray(np.concatenate([np.cos(ang), np.sin(ang)], axis=1) * inv, jnp.bfloat16))
    r = n // DFT_MINOR
    k1 = np.arange(r)
    a1 = 2.0 * np.pi * ((k1[:, None] * k1[None, :]) % r) / r
    c1, s1 = np.cos(a1), np.sin(a1)
    f1 = np.block([[c1, s1], [-s1, c1]])
    n2 = np.arange(DFT_MINOR)
    at = 2.0 * np.pi * (n2[:, None] * k1[None, :]) / n
    tw_c = np.repeat(np.cos(at)[:, :, None], LANES, axis=2)
    tw_s = np.repeat(np.sin(at)[:, :, None], LANES, axis=2)
    a2 = 2.0 * np.pi * ((n2[:, None] * n2[None, :]) % DFT_MINOR) / DFT_MINOR
    f2 = np.concatenate([np.cos(a2), np.sin(a2)], axis=1) * inv
    return dict(f1=jnp.asarray(f1, jnp.bfloat16), tw_c=jnp.asarray(tw_c, jnp.float32),
                tw_s=jnp.asarray(tw_s, jnp.float32), f2=jnp.asarray(f2, jnp.bfloat16))


def _fourier_positions(zr, zi, tabs):
    b, n, w = zr.shape
    if "direct" in tabs:
        return _left_matmul(tabs["direct"], jnp.concatenate([zr, zi], axis=1), w)
    r = n // DFT_MINOR
    tn2 = 8
    zr2 = zr.reshape(b, r, DFT_MINOR * w)
    zi2 = zi.reshape(b, r, DFT_MINOR * w)
    zin = pl.BlockSpec((None, r, tn2 * w), lambda bi, i: (bi, 0, i))
    tw = pl.BlockSpec((tn2, r, LANES), lambda bi, i: (i, 0, 0))
    g = pl.pallas_call(
        functools.partial(_dft1_kernel, r=r, tn2=tn2),
        grid=(b, DFT_MINOR // tn2),
        in_specs=[zin, zin, _const_spec(tabs["f1"].shape), tw, tw],
        out_specs=pl.BlockSpec((None, 2, tn2, r, w), lambda bi, i: (bi, 0, i, 0, 0)),
        out_shape=jax.ShapeDtypeStruct((b, 2, DFT_MINOR, r, w), jnp.bfloat16),
        compiler_params=_params(2),
        name="dft1",
    )(zr2, zi2, tabs["f1"], tabs["tw_c"], tabs["tw_s"])
    y = _left_matmul(tabs["f2"], g.reshape(b, 2 * DFT_MINOR, r * w), 4 * w)
    return y.reshape(b, n, w)


def _win_kernel(*refs, has_local, nb):
    if has_local:
        sink_ref, q_ref, kp_ref, kc_ref, kn_ref, vp_ref, vc_ref, vn_ref, kx_ref, vx_ref, o_ref = refs
    else:
        sink_ref, q_ref, kx_ref, vx_ref, o_ref = refs
    i = pl.program_id(1)
    tq = q_ref.shape[0]
    lx = kx_ref.shape[0]
    lane = lax.broadcasted_iota(jnp.int32, (1, LANES), 1)
    halves = (lane < HEAD_DIM, lane >= HEAD_DIM)
    if has_local:
        r = lax.broadcasted_iota(jnp.int32, (tq, 3 * BLOCK), 0)
        c = lax.broadcasted_iota(jnp.int32, (tq, 3 * BLOCK), 1)
        kpos = (i - 1) * BLOCK + c
        mask = (jnp.abs(c - BLOCK - r) <= WINDOW) & (kpos >= 0) & (kpos < nb * BLOCK)
    for j in range(SWA_KV):
        sl = slice(j * LANES, (j + 1) * LANES)
        if has_local:
            kd = jnp.concatenate([kp_ref[:, sl], kc_ref[:, sl], kn_ref[:, sl], kx_ref[:, sl]], axis=0)
            vd = jnp.concatenate([vp_ref[:, sl], vc_ref[:, sl], vn_ref[:, sl], vx_ref[:, sl]], axis=0)
        else:
            kd = kx_ref[:, sl]
            vd = vx_ref[:, sl]
        for t in range(SWA_HEADS // SWA_KV // 2):
            slab = j * (SWA_HEADS // SWA_KV // 2) + t
            q2 = q_ref[:, slab * LANES:(slab + 1) * LANES]
            acc = jnp.zeros((tq, LANES), jnp.float32)
            for e in range(2):
                sk = sink_ref[0, 2 * slab + e]
                s = _dot_nt(jnp.where(halves[e], q2, jnp.zeros_like(q2)), kd)
                if has_local:
                    s = jnp.concatenate([jnp.where(mask, s[:, :3 * BLOCK], NEG_INF), s[:, 3 * BLOCK:]], axis=1)
                m = jnp.maximum(jnp.max(s, axis=-1, keepdims=True), sk)
                p = jnp.exp(s - m)
                inv = 1.0 / (jnp.sum(p, axis=-1, keepdims=True) + jnp.exp(sk - m))
                acc = acc + _dot(_bf(p * inv), jnp.where(halves[e], vd, jnp.zeros_like(vd)))
            o_ref[:, slab * LANES:(slab + 1) * LANES] = _bf(acc)
    del lx


def _window_attention(qb, kb, vb, kx, vx, sink, has_local):
    b, n, _ = qb.shape
    nb = n // BLOCK
    lx = kx.shape[1]
    smem = pl.BlockSpec(memory_space=pltpu.SMEM)
    qspec = pl.BlockSpec((None, BLOCK, SWA_W), lambda bi, i: (bi, i, 0))
    ctx = pl.BlockSpec((None, lx, 2 * LANES), lambda bi, i: (bi, 0, 0))
    args = [sink.reshape(1, SWA_HEADS), qb]
    specs = [smem, qspec]
    if has_local:
        prv = pl.BlockSpec((None, BLOCK, 2 * LANES), lambda bi, i: (bi, jnp.maximum(i - 1, 0), 0))
        cur = pl.BlockSpec((None, BLOCK, 2 * LANES), lambda bi, i: (bi, i, 0))
        nxt = pl.BlockSpec((None, BLOCK, 2 * LANES), lambda bi, i: (bi, jnp.minimum(i + 1, nb - 1), 0))
        args += [kb, kb, kb, vb, vb, vb]
        specs += [prv, cur, nxt, prv, cur, nxt]
    args += [kx, vx]
    specs += [ctx, ctx]
    return pl.pallas_call(
        functools.partial(_win_kernel, has_local=has_local, nb=nb),
        grid=(b, nb),
        in_specs=specs,
        out_specs=qspec,
        out_shape=jax.ShapeDtypeStruct((b, n, SWA_W), jnp.bfloat16),
        compiler_params=_params(2),
        name="win",
    )(*args)


def _diff_kernel(lam_ref, q_ref, k_ref, v_ref, g_ref, o_ref, vt_ref, kmax_ref, acc_ref, s0_ref, s1_ref, p0_ref,
                 p1_ref, *, tk, out_scale):
    qi = pl.program_id(2)
    tq = q_ref.shape[0]
    n_chunks = vt_ref.shape[0]

    sel_r = lax.broadcasted_iota(jnp.int32, (8, LANES), 0)
    sel_l = lax.broadcasted_iota(jnp.int32, (8, LANES), 1)
    sel = _bf(jnp.where((sel_l >= sel_r * HEAD_DIM) & (sel_l < (sel_r + 1) * HEAD_DIM), 1.0, 0.0))

    def sq_norms(t):
        tf = t.astype(jnp.float32)
        return _dot_nt(sel, _bf(tf * tf))

    @pl.when(qi == 0)
    def _():
        def body(ci, kmax):
            c0 = pl.multiple_of(ci * tk, tk)
            vt_ref[ci, :LANES, :] = _bf(v_ref[pl.ds(c0, tk), :].astype(jnp.float32).T)
            vt_ref[ci, LANES:, :] = jnp.ones((SUM_ROWS, tk), jnp.bfloat16)
            return jnp.maximum(kmax, jnp.max(sq_norms(k_ref[pl.ds(c0, tk), :]), axis=1, keepdims=True))
        kmax = lax.fori_loop(0, n_chunks, body, jnp.zeros((8, 1), jnp.float32))
        kmax_ref[...] = jnp.broadcast_to(kmax, kmax_ref.shape)

    q = q_ref[...]
    lane = lax.broadcasted_iota(jnp.int32, (1, LANES), 1)
    qm = (jnp.where(lane < HEAD_DIM, q, jnp.zeros_like(q)), jnp.where(lane >= HEAD_DIM, q, jnp.zeros_like(q)))
    acc_ref[...] = jnp.zeros_like(acc_ref)
    s_bufs = (s0_ref, s1_ref)
    p_bufs = (p0_ref, p1_ref)

    bound = jnp.sqrt(sq_norms(q) * kmax_ref[:, 0:1])
    shift_safe = jnp.max(bound[0:2, :]) <= SAFE_SHIFT

    @pl.when(shift_safe)
    def _():
        q_both = jnp.concatenate(qm, axis=0)
        b_both = jnp.concatenate([bound[0:1, :], bound[1:2, :]], axis=1)

        def chunk(ci, carry):
            c0 = pl.multiple_of(ci * tk, tk)
            p = _bf(jnp.exp2(_dot_nt(k_ref[pl.ds(c0, tk), :], q_both) - b_both))
            acc_ref[...] += _dot(vt_ref[ci], p)
            return carry
        lax.fori_loop(0, n_chunks, chunk, 0, unroll=True)

    @pl.when(jnp.logical_not(shift_safe))
    def _():
        _diff_online(qm, k_ref, vt_ref, acc_ref, s_bufs, p_bufs, tk=tk, tq=tq, n_chunks=n_chunks)

    lam = lam_ref[0, 0]
    a1 = acc_ref[:, :tq]
    a2 = acc_ref[:, tq:]
    o = a1[:LANES] * (1.0 / a1[LANES:LANES + 1]) - lam * (a2[:LANES] * (1.0 / a2[LANES:LANES + 1]))
    y = o * lax.rsqrt(jnp.mean(o * o, axis=0, keepdims=True) + NORM_EPS) * g_ref[...] * out_scale
    o_ref[...] = _bf(y.T)


def _diff_online(qm, k_ref, vt_ref, acc_ref, s_bufs, p_bufs, *, tk, tq, n_chunks):
    def scores(ci, slot):
        c0 = ci * tk if isinstance(ci, int) else pl.multiple_of(ci * tk, tk)
        k = k_ref[pl.ds(c0, tk), :]
        mx = []
        for c in range(2):
            st = _dot_nt(k, qm[c])
            s_bufs[slot][c] = st
            mx.append(jnp.max(st, axis=0, keepdims=True))
        return tuple(mx)

    def probs(slot, mx, m_run):
        m_new, alpha = [], []
        for c in range(2):
            m = jnp.maximum(m_run[c], mx[c])
            alpha.append(jnp.exp2(m_run[c] - m))
            p_bufs[slot][c] = _bf(jnp.exp2(s_bufs[slot][c] - m))
            m_new.append(m)
        return tuple(m_new), tuple(alpha)

    def accumulate(ci, slot, alpha):
        vt = vt_ref[ci]
        for c in range(2):
            cols = slice(c * tq, (c + 1) * tq)
            acc_ref[:, cols] = alpha[c] * acc_ref[:, cols] + _dot(vt, p_bufs[slot][c])

    def step(t, par, do_scores, do_probs, do_acc, state):
        m_run, mx, alpha = state
        mx_next = scores(t + 2, par) if do_scores else mx
        if do_probs:
            m_run, alpha_next = probs(1 - par, mx, m_run)
        else:
            alpha_next = alpha
        if do_acc:
            accumulate(t, par, alpha)
        return m_run, mx_next, alpha_next

    neg = jnp.full((1, tq), NEG_INF, jnp.float32)
    one = jnp.ones((1, tq), jnp.float32)
    state = ((neg, neg), (neg, neg), (one, one))
    for t in (-2, -1):
        state = step(t, t % 2, t + 2 < n_chunks, 0 <= t + 1 < n_chunks, False, state)
    n_steady = max(n_chunks - 2, 0)

    def pair(j, state):
        state = step(2 * j, 0, True, True, True, state)
        return step(2 * j + 1, 1, True, True, True, state)

    state = lax.fori_loop(0, n_steady // 2, pair, state)
    if n_steady % 2:
        state = step(n_steady - 1, 0, True, True, True, state)
    for t in range(n_steady, n_chunks):
        state = step(t, t % 2, False, t + 1 < n_chunks, True, state)


def _pick_tk(nk, cap):
    best = LANES
    for t in range(LANES, min(cap, nk) + 1, LANES):
        if nk % t == 0:
            best = t
    return best


def _diff_attention(qd, k_all, v_all, lam, subln_g, lam_init, tq, tk_cap):
    b, n, _ = qd.shape
    nk = k_all.shape[1]
    tk = _pick_tk(nk, tk_cap)
    smem = pl.BlockSpec(memory_space=pltpu.SMEM)
    qspec = pl.BlockSpec((None, tq, LANES), lambda bi, h, i: (bi, i, h))
    kvspec = pl.BlockSpec((None, nk, LANES), lambda bi, h, i: (bi, 0, h))
    gain = jnp.broadcast_to(subln_g.astype(jnp.float32)[:, None], (LANES, tq))
    return pl.pallas_call(
        functools.partial(_diff_kernel, tk=tk, out_scale=1.0 - lam_init),
        grid=(b, DIFF_HEADS, n // tq),
        in_specs=[smem, qspec, kvspec, kvspec, _const_spec((LANES, tq))],
        out_specs=qspec,
        out_shape=jax.ShapeDtypeStruct((b, n, DIFF_W), jnp.bfloat16),
        scratch_shapes=[pltpu.VMEM((nk // tk, LANES + SUM_ROWS, tk), jnp.bfloat16),
                        pltpu.VMEM((8, LANES), jnp.float32),
                        pltpu.VMEM((LANES + SUM_ROWS, 2 * tq), jnp.float32),
                        pltpu.VMEM((2, tk, tq), jnp.float32), pltpu.VMEM((2, tk, tq), jnp.float32),
                        pltpu.VMEM((2, tk, tq), jnp.bfloat16), pltpu.VMEM((2, tk, tq), jnp.bfloat16)],
        compiler_params=_params(3),
        name="diff",
    )(lam, qd, k_all, v_all, gain)


def _merge_kernel(x_ref, gt_ref, ya_ref, sza_ref, yb_ref, szb_ref, uc_ref, up_ref, un_ref, cb_ref, szc_ref,
                  yd_ref, szd_ref, g_ref, cw_ref, wa_ref, wb_ref, wc_ref, wd_ref, wo_ref, o_ref, *, tm, nt):
    i = pl.program_id(1)
    f32 = jnp.float32
    u = uc_ref[...].astype(f32)
    row = lax.broadcasted_iota(jnp.int32, (tm, 1), 0)
    prev_row = jnp.where(i > 0, up_ref[7:8, :].astype(f32), 0.0)
    next_row = jnp.where(i < nt - 1, un_ref[0:1, :].astype(f32), 0.0)
    u_prev = jnp.where(row == 0, prev_row, pltpu.roll(u, 1, 0))
    u_next = jnp.where(row == tm - 1, next_row, pltpu.roll(u, tm - 1, 0))
    cw = cw_ref[...]
    conv = u_prev * cw[0:1, :] + u * cw[1:2, :] + u_next * cw[2:3, :]
    yc = cb_ref[...].astype(f32) * conv
    d = x_ref.shape[-1]
    branches = ((ya_ref[...].astype(f32), sza_ref, wa_ref), (yb_ref[...].astype(f32), szb_ref, wb_ref),
                (yc, szc_ref, wc_ref), (yd_ref[...].astype(f32), szd_ref, wd_ref))
    mixed = jnp.zeros((tm, d), f32)
    for j, (y, sz_ref, w_ref) in enumerate(branches):
        t = _dot(_bf(y * sz_ref[...].astype(f32)), w_ref[...])
        mixed = mixed + g_ref[:, j * d:(j + 1) * d].astype(f32) * t
    out = _dot(_bf(mixed), wo_ref[...])
    o_ref[...] = x_ref[...] + gt_ref[...] * out


def _merge(x, gate, ya, sza, yb, szb, uc, cb, szc, yd, szd, g, conv_w, w_a, w_b, w_c, w_d, w_out, tm):
    b, n, d = x.shape
    nt = n // tm
    tok = lambda w: pl.BlockSpec((None, tm, w), lambda bi, i: (bi, i, 0))
    hb = tm // 8
    halo_p = pl.BlockSpec((None, 8, CONV_W), lambda bi, i: (bi, jnp.maximum(i * hb - 1, 0), 0))
    halo_n = pl.BlockSpec((None, 8, CONV_W), lambda bi, i: (bi, jnp.minimum((i + 1) * hb, n // 8 - 1), 0))
    per_b = pl.BlockSpec((None, 1, d), lambda bi, i: (bi, 0, 0))
    ws = [_bf(w_a), _bf(w_b), _bf(w_c), _bf(w_d), _bf(w_out)]
    return pl.pallas_call(
        functools.partial(_merge_kernel, tm=tm, nt=nt),
        grid=(b, nt),
        in_specs=[tok(d), per_b, tok(FOURIER_W), tok(FOURIER_W), tok(SWA_W), tok(SWA_W), tok(CONV_W), halo_p, halo_n,
                  tok(CONV_W), tok(CONV_W), tok(DIFF_W), tok(DIFF_W), tok(N_BRANCH * d), _const_spec(conv_w.shape)]
                 + [_const_spec(w.shape) for w in ws],
        out_specs=tok(d),
        out_shape=jax.ShapeDtypeStruct((b, n, d), jnp.float32),
        compiler_params=_params(2),
        name="merge",
    )(x, gate, ya, sza, yb, szb, uc, uc, uc, cb, szc, yd, szd, g, conv_w.astype(jnp.float32), *ws)


def _rope_tables(n):
    rows = n // GRID_W
    row = jnp.broadcast_to(jnp.arange(rows, dtype=jnp.float32)[:, None], (rows, GRID_W)).reshape(-1)
    col = jnp.broadcast_to(jnp.arange(GRID_W, dtype=jnp.float32)[None, :], (rows, GRID_W)).reshape(-1)
    nf = HEAD_DIM // 4
    inv = ROPE_BASE ** (-jnp.arange(nf, dtype=jnp.float32) / nf)
    ar = row[:, None] * inv[None, :]
    ac = col[:, None] * inv[None, :]
    cos = jnp.concatenate([jnp.cos(ar), jnp.cos(ar), jnp.cos(ac), jnp.cos(ac)], axis=-1)
    sin = jnp.concatenate([jnp.sin(ar), jnp.sin(ar), jnp.sin(ac), jnp.sin(ac)], axis=-1)
    sign = jnp.where((jnp.arange(HEAD_DIM) % 32) < 16, -1.0, 1.0)
    return jnp.tile(cos, (1, 2)), jnp.tile(sin * sign, (1, 2))


def _channel_dft():
    k = np.arange(HEAD_DIM)
    ang = 2.0 * np.pi * ((k[:, None] * k[None, :]) % HEAD_DIM) / HEAD_DIM
    eye = np.eye(FOURIER_W // HEAD_DIM)
    inv = 1.0 / math.sqrt(HEAD_DIM)
    return jnp.asarray(np.concatenate([np.kron(eye, np.cos(ang)), -np.kron(eye, np.sin(ang))], axis=1) * inv,
                       jnp.bfloat16)


def _extend_w_in(w):
    s = _SRC
    k0, k1 = s["b_k"][0], s["b_k"][0] + HEAD_DIM
    v0, v1 = s["b_v"][0], s["b_v"][0] + HEAD_DIM
    pieces = [w[:, s["a_x"][0]:s["b_q"][1]],
              w[:, k0:k1], w[:, k0:k1], w[:, k1:k1 + HEAD_DIM], w[:, k1:k1 + HEAD_DIM],
              w[:, v0:v1], w[:, v0:v1], w[:, v1:v1 + HEAD_DIM], w[:, v1:v1 + HEAD_DIM],
              w[:, s["c_x"][0]:]]
    return _bf(jnp.concatenate(pieces, axis=1))


def _mix(stream, scale, shift, gate, ctx_kv, lp, consts, lam, lam_init, rope, dft, tm, tq, tk, has_local):
    parts = _project(stream, scale, shift, lp["norm_g"], lp["w_ext"], consts["cs_bd"], consts["mg"],
                     lp["head_gains"], rope, tm)
    zr, zi, qb, kb, vb, uc, cb, qd, kd, vd, sza, szb, szc, szd, g = parts
    if ctx_kv is None:
        kx_b, vx_b, k_all, v_all = kb, vb, kd, vd
    else:
        kx_b, vx_b, kx_d, vx_d = ctx_kv
        k_all = jnp.concatenate([kx_d, kd], axis=1)
        v_all = jnp.concatenate([vx_d, vd], axis=1)
    ya = _fourier_positions(zr, zi, dft)
    yb = _window_attention(qb, kb, vb, kx_b, vx_b, lp["sink"], has_local)
    yd = _diff_attention(qd, k_all, v_all, lam, lp["subln"], lam_init, tq, tk)
    new = _merge(stream, gate, ya, sza, yb, szb, uc, cb, szc, yd, szd, g, lp["conv_w"],
                 lp["w_o_a"], lp["w_o_b"], lp["w_o_c"], lp["w_o_d"], lp["w_out"], tm)
    return new, (kb, vb, kd, vd)


def kernel(x, c, ctx, c_ctx, norm_g, w_mod, b_mod, w_in, q_norm_b, k_norm_b, sink_b, conv_w, q_norm_d, k_norm_d,
           lam_q1, lam_k1, lam_q2, lam_k2, subln_d, w_o_a, w_o_b, w_o_c, w_o_d, w_out):
    b, n, d = x.shape
    lc = ctx.shape[1]
    depth = w_in.shape[0]
    rope = _rope_tables(n)
    consts = dict(cs_bd=_channel_dft(),
                  mg=jnp.asarray(np.kron(np.eye(2), np.full((HEAD_DIM, HEAD_DIM), 1.0 / HEAD_DIM)), jnp.bfloat16))
    dft_x = _dft_tables(n)
    dft_c = _dft_tables(lc)
    c_rows = jnp.zeros((8, d), jnp.float32).at[:b].set(c).at[b].set(c_ctx)
    tm = min(256, n)
    tq = min(256, n)
    tk = 1280
    for l in range(depth):
        last = l == depth - 1
        lam_init = 0.8 - 0.6 * math.exp(-0.3 * l)
        lam_vecs = jnp.stack([lam_q1[l], lam_k1[l], lam_q2[l], lam_k2[l]]).astype(jnp.float32)
        mod, lam_o = _modulation(c_rows, w_mod[l], b_mod[l], lam_vecs, lam_init)
        lam = lam_o[0:1, 0:1]
        shift, scale, gate = (mod[:b, None, j * d:(j + 1) * d] for j in range(3))
        shift_c, scale_c, gate_c = (jnp.broadcast_to(mod[b:b + 1, None, j * d:(j + 1) * d], (b, 1, d))
                                    for j in range(3))
        tile2 = lambda v: jnp.tile(v.astype(jnp.float32), 2)
        lp = dict(norm_g=norm_g[l], w_ext=_extend_w_in(w_in[l]), sink=sink_b[l].astype(jnp.float32),
                  head_gains=jnp.stack([tile2(q_norm_b[l]), tile2(k_norm_b[l]), tile2(q_norm_d[l]), tile2(k_norm_d[l])]),
                  subln=subln_d[l], conv_w=conv_w[l], w_o_a=w_o_a[l], w_o_b=w_o_b[l], w_o_c=w_o_c[l],
                  w_o_d=w_o_d[l], w_out=w_out[l])
        if last:
            parts = _project(ctx, scale_c, shift_c, lp["norm_g"], lp["w_ext"], consts["cs_bd"], consts["mg"],
                             lp["head_gains"], None, min(256, lc))
            ctx_kv = (parts[3], parts[4], parts[8], parts[9])
        else:
            ctx, ctx_kv = _mix(ctx, scale_c, shift_c, gate_c, None, lp, consts, lam, lam_init, None, dft_c,
                               min(256, lc), min(256, lc), tk, False)
        x, _ = _mix(x, scale, shift, gate, ctx_kv, lp, consts, lam, lam_init, rope, dft_x, tm, tq, tk, True)
    return x
```

```python
import functools
import math

import jax
import jax.numpy as jnp
import numpy as np
from jax import lax
from jax.experimental import pallas as pl
from jax.experimental.pallas import tpu as pltpu

HEAD_DIM = 64
LANES = 128
FOURIER_W = 384
SWA_HEADS = 8
SWA_KV = 2
SWA_W = SWA_HEADS * HEAD_DIM
WINDOW = 128
BLOCK = 128
CONV_W = 384
CONV_K = 3
DIFF_HEADS = 4
DIFF_W = DIFF_HEADS * 2 * HEAD_DIM
N_BRANCH = 4
ROPE_BASE = 10000.0
NORM_EPS = 1e-6
NEG_INF = -1e30
GRID_W = 64
DFT_MINOR = 128
SUM_ROWS = 16
LOG2E = math.log2(math.e)
SAFE_SHIFT = 60.0
VMEM_LIMIT = 56 * 1024 * 1024

_SRC = dict(a_x=(0, 384), b_q=(384, 896), b_k=(896, 1024), b_v=(1024, 1152), c_x=(1152, 1536),
            c_b=(1536, 1920), c_c=(1920, 2304), d_q=(2304, 2816), d_k=(2816, 3328), d_v=(3328, 3840),
            z=(3840, 5632), m_g=(5632, 9728))


def _bf(x):
    return x.astype(jnp.bfloat16)


def _dot(a, b):
    return jnp.dot(a, b, preferred_element_type=jnp.float32)


def _dot_nt(a, b):
    return lax.dot_general(a, b, (((1,), (1,)), ((), ())), preferred_element_type=jnp.float32)


def _params(n_axes, flags=None):
    return pltpu.CompilerParams(dimension_semantics=("arbitrary",) * n_axes, vmem_limit_bytes=VMEM_LIMIT,
                                flags=flags)


def _const_spec(shape):
    nd = len(shape)
    return pl.BlockSpec(shape, lambda *_: (0,) * nd, pipeline_mode=pl.Buffered(1))


def _mod_kernel(c_ref, w_ref, b_ref, lam_ref, mod_ref, lamo_ref, *, lam_init):
    c = c_ref[...]
    s = c * jax.nn.sigmoid(c)
    mod_ref[...] = _dot(_bf(s), w_ref[...]) + b_ref[...]
    lv = lam_ref[...]
    a1 = jnp.sum(lv[0:1, :] * lv[1:2, :], axis=-1, keepdims=True)
    a2 = jnp.sum(lv[2:3, :] * lv[3:4, :], axis=-1, keepdims=True)
    lam = jnp.exp(a1) - jnp.exp(a2) + lam_init
    lamo_ref[...] = jnp.broadcast_to(lam, lamo_ref.shape)


def _modulation(c_rows, w_mod, b_mod, lam_vecs, lam_init):
    r, d = c_rows.shape
    return pl.pallas_call(
        functools.partial(_mod_kernel, lam_init=lam_init),
        out_shape=(jax.ShapeDtypeStruct((r, 3 * d), jnp.float32),
                   jax.ShapeDtypeStruct((8, LANES), jnp.float32)),
        name="mod",
    )(c_rows, _bf(w_mod), b_mod.reshape(1, 3 * d), lam_vecs)


def _head_norm_rope(t, gain, mg, cos, sin_s, lane_lo, scale):
    ms = _dot(_bf(t * t), mg)
    y = t * lax.rsqrt(ms + NORM_EPS) * gain
    if cos is not None:
        rot = jnp.where(lane_lo, pltpu.roll(y, LANES - 16, 1), pltpu.roll(y, 16, 1))
        y = y * cos + rot * sin_s
    if scale != 1.0:
        y = y * scale
    return y


def _proj_kernel(*refs, use_rope, tm):
    if use_rope:
        (x_ref, sc_ref, sh_ref, g_ref, w_ref, cs_ref, mg_ref, hg_ref, cos_ref, sin_ref), outs = refs[:10], refs[10:]
    else:
        (x_ref, sc_ref, sh_ref, g_ref, w_ref, cs_ref, mg_ref, hg_ref), outs = refs[:8], refs[8:]
        cos_ref = sin_ref = None
    (zr_ref, zi_ref, qb_ref, kb_ref, vb_ref, uc_ref, cb_ref, qd_ref, kd_ref, vd_ref,
     sza_ref, szb_ref, szc_ref, szd_ref, gate_ref) = outs

    x = x_ref[...]
    y = x * lax.rsqrt(jnp.mean(x * x, axis=-1, keepdims=True) + NORM_EPS) * g_ref[...]
    h = _bf(y * (1.0 + sc_ref[...]) + sh_ref[...])

    def proj(c0, width):
        return _dot(h, w_ref[:, c0:c0 + width])

    col = 0
    a = _bf(proj(col, FOURIER_W))
    zz = _dot(a, cs_ref[...])
    zr_ref[...] = _bf(zz[:, :FOURIER_W])
    zi_ref[...] = _bf(zz[:, FOURIER_W:])
    col += FOURIER_W

    mg = mg_ref[...]
    if use_rope:
        cos = cos_ref[...]
        sin_s = sin_ref[...]
    else:
        cos = sin_s = None
    lane = lax.broadcasted_iota(jnp.int32, (tm, LANES), 1)
    lane_lo = (lane & 31) < 16
    qscale = HEAD_DIM ** -0.5

    def normed(c0, width, gain_row, out_ref, scale):
        t = proj(c0, width)
        gain = hg_ref[gain_row:gain_row + 1, :]
        for s in range(width // LANES):
            ts = t[:, s * LANES:(s + 1) * LANES]
            out_ref[:, s * LANES:(s + 1) * LANES] = _bf(
                _head_norm_rope(ts, gain, mg, cos, sin_s, lane_lo, scale))

    normed(col, SWA_W, 0, qb_ref, qscale * LOG2E); col += SWA_W
    normed(col, 2 * LANES, 1, kb_ref, 1.0); col += 2 * LANES
    vb_ref[...] = _bf(proj(col, 2 * LANES)); col += 2 * LANES
    c3 = proj(col, 3 * CONV_W); col += 3 * CONV_W
    uc_ref[...] = _bf(c3[:, 2 * CONV_W:] * c3[:, :CONV_W])
    cb_ref[...] = _bf(c3[:, CONV_W:2 * CONV_W])
    normed(col, DIFF_W, 2, qd_ref, qscale * LOG2E); col += DIFF_W
    normed(col, DIFF_W, 3, kd_ref, 1.0); col += DIFF_W
    vd_ref[...] = _bf(proj(col, DIFF_W)); col += DIFF_W
    for ref, width in ((sza_ref, FOURIER_W), (szb_ref, SWA_W), (szc_ref, CONV_W), (szd_ref, DIFF_W)):
        z = proj(col, width)
        ref[...] = _bf(z * jax.nn.sigmoid(z))
        col += width
    d = x.shape[-1]
    for j in range(N_BRANCH):
        gate_ref[:, j * d:(j + 1) * d] = _bf(jax.nn.sigmoid(proj(col, d)))
        col += d


_PROJ_OUT_W = (FOURIER_W, FOURIER_W, SWA_W, 2 * LANES, 2 * LANES, CONV_W, CONV_W, DIFF_W, DIFF_W, DIFF_W,
               FOURIER_W, SWA_W, CONV_W, DIFF_W)


def _project(x, scale, shift, norm_g, w_ext, cs_bd, mg, head_gains, rope, tm):
    b, n, d = x.shape
    use_rope = rope is not None
    tok = lambda w: pl.BlockSpec((None, tm, w), lambda bi, i: (bi, i, 0))
    per_b = pl.BlockSpec((None, 1, d), lambda bi, i: (bi, 0, 0))
    in_specs = [tok(d), per_b, per_b, _const_spec((1, d)), _const_spec(w_ext.shape), _const_spec(cs_bd.shape),
                _const_spec(mg.shape), _const_spec(head_gains.shape)]
    args = [x, scale, shift, norm_g.reshape(1, d), w_ext, cs_bd, mg, head_gains]
    if use_rope:
        tab = pl.BlockSpec((tm, LANES), lambda bi, i: (i, 0))
        in_specs += [tab, tab]
        args += list(rope)
    widths = _PROJ_OUT_W + (N_BRANCH * d,)
    return pl.pallas_call(
        functools.partial(_proj_kernel, use_rope=use_rope, tm=tm),
        grid=(b, n // tm),
        in_specs=in_specs,
        out_specs=[tok(w) for w in widths],
        out_shape=[jax.ShapeDtypeStruct((b, n, w), jnp.bfloat16) for w in widths],
        compiler_params=_params(2),
        name="proj",
    )(*args)


def _dft1_kernel(zr_ref, zi_ref, f_ref, tc_ref, ts_ref, o_ref, *, r, tn2):
    z = jnp.concatenate([zr_ref[...], zi_ref[...]], axis=0)
    a = _dot(f_ref[...], z)
    for j in range(tn2):
        ar = a[:r, j * FOURIER_W:(j + 1) * FOURIER_W]
        ai = a[r:, j * FOURIER_W:(j + 1) * FOURIER_W]
        tc = jnp.concatenate([tc_ref[j]] * (FOURIER_W // LANES), axis=-1)
        ts = jnp.concatenate([ts_ref[j]] * (FOURIER_W // LANES), axis=-1)
        o_ref[0, j] = _bf(ar * tc + ai * ts)
        o_ref[1, j] = _bf(ai * tc - ar * ts)


def _left_matmul_kernel(m_ref, x_ref, o_ref):
    o_ref[...] = _bf(_dot(m_ref[...], x_ref[...]))


def _left_matmul(mat, x, tc):
    b, k, c = x.shape
    rows = mat.shape[0]
    return pl.pallas_call(
        _left_matmul_kernel,
        grid=(b, c // tc),
        in_specs=[_const_spec(mat.shape), pl.BlockSpec((None, k, tc), lambda bi, i: (bi, 0, i))],
        out_specs=pl.BlockSpec((None, rows, tc), lambda bi, i: (bi, 0, i)),
        out_shape=jax.ShapeDtypeStruct((b, rows, c), jnp.bfloat16),
        compiler_params=_params(2),
        name="dft2",
    )(mat, x)


def _dft_tables(n):
    inv = 1.0 / math.sqrt(n)
    if n <= 2 * DFT_MINOR:
        k = np.arange(n)
        ang = 2.0 * np.pi * ((k[:, None] * k[None, :]) % n) / n
        return dict(direct=jnp.asarray(np.concatenate([np.cos(ang), np.sin(ang)], axis=1) * inv, jnp.bfloat16))
    r = n // DFT_MINOR
    k1 = np.arange(r)
    a1 = 2.0 * np.pi * ((k1[:, None] * k1[None, :]) % r) / r
    c1, s1 = np.cos(a1), np.sin(a1)
    f1 = np.block([[c1, s1], [-s1, c1]])
    n2 = np.arange(DFT_MINOR)
    at = 2.0 * np.pi * (n2[:, None] * k1[None, :]) / n
    tw_c = np.repeat(np.cos(at)[:, :, None], LANES, axis=2)
    tw_s = np.repeat(np.sin(at)[:, :, None], LANES, axis=2)
    a2 = 2.0 * np.pi * ((n2[:, None] * n2[None, :]) % DFT_MINOR) / DFT_MINOR
    f2 = np.concatenate([np.cos(a2), np.sin(a2)], axis=1) * inv
    return dict(f1=jnp.asarray(f1, jnp.bfloat16), tw_c=jnp.asarray(tw_c, jnp.float32),
                tw_s=jnp.asarray(tw_s, jnp.float32), f2=jnp.asarray(f2, jnp.bfloat16))


def _fourier_positions(zr, zi, tabs):
    b, n, w = zr.shape
    if "direct" in tabs:
        return _left_matmul(tabs["direct"], jnp.concatenate([zr, zi], axis=1), w)
    r = n // DFT_MINOR
    tn2 = 8
    zr2 = zr.reshape(b, r, DFT_MINOR * w)
    zi2 = zi.reshape(b, r, DFT_MINOR * w)
    zin = pl.BlockSpec((None, r, tn2 * w), lambda bi, i: (bi, 0, i))
    tw = pl.BlockSpec((tn2, r, LANES), lambda bi, i: (i, 0, 0))
    g = pl.pallas_call(
        functools.partial(_dft1_kernel, r=r, tn2=tn2),
        grid=(b, DFT_MINOR // tn2),
        in_specs=[zin, zin, _const_spec(tabs["f1"].shape), tw, tw],
        out_specs=pl.BlockSpec((None, 2, tn2, r, w), lambda bi, i: (bi, 0, i, 0, 0)),
        out_shape=jax.ShapeDtypeStruct((b, 2, DFT_MINOR, r, w), jnp.bfloat16),
        compiler_params=_params(2),
        name="dft1",
    )(zr2, zi2, tabs["f1"], tabs["tw_c"], tabs["tw_s"])
    y = _left_matmul(tabs["f2"], g.reshape(b, 2 * DFT_MINOR, r * w), 4 * w)
    return y.reshape(b, n, w)


def _win_kernel(*refs, has_local, nb):
    if has_local:
        sink_ref, q_ref, kp_ref, kc_ref, kn_ref, vp_ref, vc_ref, vn_ref, kx_ref, vx_ref, o_ref = refs
    else:
        sink_ref, q_ref, kx_ref, vx_ref, o_ref = refs
    i = pl.program_id(1)
    tq = q_ref.shape[0]
    group = SWA_HEADS // SWA_KV
    lane = lax.broadcasted_iota(jnp.int32, (1, LANES), 1)
    halves = (lane < HEAD_DIM, lane >= HEAD_DIM)
    row_lo = lax.broadcasted_iota(jnp.int32, (LANES, 1), 0) < HEAD_DIM
    if has_local:
        kr = lax.broadcasted_iota(jnp.int32, (3 * BLOCK, tq), 0)
        qc = lax.broadcasted_iota(jnp.int32, (3 * BLOCK, tq), 1)
        kpos = (i - 1) * BLOCK + kr
        ok = (jnp.abs(kr - BLOCK - qc) <= WINDOW) & (kpos >= 0) & (kpos < nb * BLOCK)
        bias = jnp.where(ok, 0.0, NEG_INF)
        bias = jnp.concatenate([bias] * group, axis=1)
    for j in range(SWA_KV):
        sl = slice(j * LANES, (j + 1) * LANES)
        if has_local:
            kd = jnp.concatenate([kp_ref[:, sl], kc_ref[:, sl], kn_ref[:, sl], kx_ref[:, sl]], axis=0)
            vd = jnp.concatenate([vp_ref[:, sl], vc_ref[:, sl], vn_ref[:, sl], vx_ref[:, sl]], axis=0)
        else:
            kd = kx_ref[:, sl]
            vd = vx_ref[:, sl]
        q_stack = []
        for t in range(group // 2):
            slab = j * (group // 2) + t
            q2 = q_ref[:, slab * LANES:(slab + 1) * LANES]
            q_stack += [jnp.where(halves[e], q2, jnp.zeros_like(q2)) for e in range(2)]
        st = _dot_nt(kd, jnp.concatenate(q_stack, axis=0))
        if has_local:
            st = jnp.concatenate([st[:3 * BLOCK] + bias, st[3 * BLOCK:]], axis=0)
        sk = sink_ref[j:j + 1, :]
        m = jnp.maximum(jnp.max(st, axis=0, keepdims=True), sk)
        p = jnp.exp2(st - m)
        inv = 1.0 / (jnp.sum(p, axis=0, keepdims=True) + jnp.exp2(sk - m))
        ot = lax.dot_general(vd, _bf(p), (((0,), (0,)), ((), ())), preferred_element_type=jnp.float32) * inv
        for t in range(group // 2):
            slab = j * (group // 2) + t
            pair = jnp.where(row_lo, ot[:, (2 * t) * tq:(2 * t + 1) * tq], ot[:, (2 * t + 1) * tq:(2 * t + 2) * tq])
            o_ref[:, slab * LANES:(slab + 1) * LANES] = _bf(pair.T)


def _window_attention(qb, kb, vb, kx, vx, sink, has_local):
    b, n, _ = qb.shape
    nb = n // BLOCK
    lx = kx.shape[1]
    qspec = pl.BlockSpec((None, BLOCK, SWA_W), lambda bi, i: (bi, i, 0))
    ctx = pl.BlockSpec((None, lx, 2 * LANES), lambda bi, i: (bi, 0, 0))
    sink_rows = jnp.repeat(sink.astype(jnp.float32) * LOG2E, BLOCK).reshape(SWA_KV, -1)
    args = [sink_rows, qb]
    specs = [_const_spec(sink_rows.shape), qspec]
    if has_local:
        prv = pl.BlockSpec((None, BLOCK, 2 * LANES), lambda bi, i: (bi, jnp.maximum(i - 1, 0), 0))
        cur = pl.BlockSpec((None, BLOCK, 2 * LANES), lambda bi, i: (bi, i, 0))
        nxt = pl.BlockSpec((None, BLOCK, 2 * LANES), lambda bi, i: (bi, jnp.minimum(i + 1, nb - 1), 0))
        args += [kb, kb, kb, vb, vb, vb]
        specs += [prv, cur, nxt, prv, cur, nxt]
    args += [kx, vx]
    specs += [ctx, ctx]
    return pl.pallas_call(
        functools.partial(_win_kernel, has_local=has_local, nb=nb),
        grid=(b, nb),
        in_specs=specs,
        out_specs=qspec,
        out_shape=jax.ShapeDtypeStruct((b, n, SWA_W), jnp.bfloat16),
        compiler_params=_params(2),
        name="win",
    )(*args)


def _diff_kernel(lam_ref, q_ref, k_ref, v_ref, g_ref, o_ref, vt_ref, kmax_ref, acc_ref, s0_ref, s1_ref, p0_ref,
                 p1_ref, *, tk, out_scale):
    qi = pl.program_id(2)
    tq = q_ref.shape[0]
    n_chunks = vt_ref.shape[0]

    sel_r = lax.broadcasted_iota(jnp.int32, (8, LANES), 0)
    sel_l = lax.broadcasted_iota(jnp.int32, (8, LANES), 1)
    sel = _bf(jnp.where((sel_l >= sel_r * HEAD_DIM) & (sel_l < (sel_r + 1) * HEAD_DIM), 1.0, 0.0))

    def sq_norms(t):
        tf = t.astype(jnp.float32)
        return _dot_nt(sel, _bf(tf * tf))

    @pl.when(qi == 0)
    def _():
        def body(ci, kmax):
            c0 = pl.multiple_of(ci * tk, tk)
            vt_ref[ci, :LANES, :] = _bf(v_ref[pl.ds(c0, tk), :].astype(jnp.float32).T)
            vt_ref[ci, LANES:, :] = jnp.ones((SUM_ROWS, tk), jnp.bfloat16)
            return jnp.maximum(kmax, jnp.max(sq_norms(k_ref[pl.ds(c0, tk), :]), axis=1, keepdims=True))
        kmax = lax.fori_loop(0, n_chunks, body, jnp.zeros((8, 1), jnp.float32))
        kmax_ref[...] = jnp.broadcast_to(kmax, kmax_ref.shape)

    q = q_ref[...]
    lane = lax.broadcasted_iota(jnp.int32, (1, LANES), 1)
    qm = (jnp.where(lane < HEAD_DIM, q, jnp.zeros_like(q)), jnp.where(lane >= HEAD_DIM, q, jnp.zeros_like(q)))
    acc_ref[...] = jnp.zeros_like(acc_ref)
    s_bufs = (s0_ref, s1_ref)
    p_bufs = (p0_ref, p1_ref)

    bound = jnp.sqrt(sq_norms(q) * kmax_ref[:, 0:1])
    shift_safe = jnp.max(bound[0:2, :]) <= SAFE_SHIFT

    @pl.when(shift_safe)
    def _():
        q_both = jnp.concatenate(qm, axis=0)
        b_both = jnp.concatenate([bound[0:1, :], bound[1:2, :]], axis=1)

        def chunk(ci, carry):
            c0 = pl.multiple_of(ci * tk, tk)
            p = _bf(jnp.exp2(_dot_nt(k_ref[pl.ds(c0, tk), :], q_both) - b_both))
            acc_ref[...] += _dot(vt_ref[ci], p)
            return carry
        lax.fori_loop(0, n_chunks, chunk, 0, unroll=True)

    @pl.when(jnp.logical_not(shift_safe))
    def _():
        _diff_online(qm, k_ref, vt_ref, acc_ref, s_bufs, p_bufs, tk=tk, tq=tq, n_chunks=n_chunks)

    lam = lam_ref[0, 0]
    a1 = acc_ref[:, :tq]
    a2 = acc_ref[:, tq:]
    o = a1[:LANES] * (1.0 / a1[LANES:LANES + 1]) - lam * (a2[:LANES] * (1.0 / a2[LANES:LANES + 1]))
    y = o * lax.rsqrt(jnp.mean(o * o, axis=0, keepdims=True) + NORM_EPS) * g_ref[...] * out_scale
    o_ref[...] = _bf(y.T)


def _diff_online(qm, k_ref, vt_ref, acc_ref, s_bufs, p_bufs, *, tk, tq, n_chunks):
    def scores(ci, slot):
        c0 = ci * tk if isinstance(ci, int) else pl.multiple_of(ci * tk, tk)
        k = k_ref[pl.ds(c0, tk), :]
        mx = []
        for c in range(2):
            st = _dot_nt(k, qm[c])
            s_bufs[slot][c] = st
            mx.append(jnp.max(st, axis=0, keepdims=True))
        return tuple(mx)

    def probs(slot, mx, m_run):
        m_new, alpha = [], []
        for c in range(2):
            m = jnp.maximum(m_run[c], mx[c])
            alpha.append(jnp.exp2(m_run[c] - m))
            p_bufs[slot][c] = _bf(jnp.exp2(s_bufs[slot][c] - m))
            m_new.append(m)
        return tuple(m_new), tuple(alpha)

    def accumulate(ci, slot, alpha):
        vt = vt_ref[ci]
        for c in range(2):
            cols = slice(c * tq, (c + 1) * tq)
            acc_ref[:, cols] = alpha[c] * acc_ref[:, cols] + _dot(vt, p_bufs[slot][c])

    def step(t, par, do_scores, do_probs, do_acc, state):
        m_run, mx, alpha = state
        mx_next = scores(t + 2, par) if do_scores else mx
        if do_probs:
            m_run, alpha_next = probs(1 - par, mx, m_run)
        else:
            alpha_next = alpha
        if do_acc:
            accumulate(t, par, alpha)
        return m_run, mx_next, alpha_next

    neg = jnp.full((1, tq), NEG_INF, jnp.float32)
    one = jnp.ones((1, tq), jnp.float32)
    state = ((neg, neg), (neg, neg), (one, one))
    for t in (-2, -1):
        state = step(t, t % 2, t + 2 < n_chunks, 0 <= t + 1 < n_chunks, False, state)
    n_steady = max(n_chunks - 2, 0)

    def pair(j, state):
        state = step(2 * j, 0, True, True, True, state)
        return step(2 * j + 1, 1, True, True, True, state)

    state = lax.fori_loop(0, n_steady // 2, pair, state)
    if n_steady % 2:
        state = step(n_steady - 1, 0, True, True, True, state)
    for t in range(n_steady, n_chunks):
        state = step(t, t % 2, False, t + 1 < n_chunks, True, state)


def _pick_tk(nk, cap):
    best = LANES
    for t in range(LANES, min(cap, nk) + 1, LANES):
        if nk % t == 0:
            best = t
    return best


def _diff_attention(qd, k_all, v_all, lam, subln_g, lam_init, tq, tk_cap):
    b, n, _ = qd.shape
    nk = k_all.shape[1]
    tk = _pick_tk(nk, tk_cap)
    smem = pl.BlockSpec(memory_space=pltpu.SMEM)
    qspec = pl.BlockSpec((None, tq, LANES), lambda bi, h, i: (bi, i, h))
    kvspec = pl.BlockSpec((None, nk, LANES), lambda bi, h, i: (bi, 0, h))
    gain = jnp.broadcast_to(subln_g.astype(jnp.float32)[:, None], (LANES, tq))
    return pl.pallas_call(
        functools.partial(_diff_kernel, tk=tk, out_scale=1.0 - lam_init),
        grid=(b, DIFF_HEADS, n // tq),
        in_specs=[smem, qspec, kvspec, kvspec, _const_spec((LANES, tq))],
        out_specs=qspec,
        out_shape=jax.ShapeDtypeStruct((b, n, DIFF_W), jnp.bfloat16),
        scratch_shapes=[pltpu.VMEM((nk // tk, LANES + SUM_ROWS, tk), jnp.bfloat16),
                        pltpu.VMEM((8, LANES), jnp.float32),
                        pltpu.VMEM((LANES + SUM_ROWS, 2 * tq), jnp.float32),
                        pltpu.VMEM((2, tk, tq), jnp.float32), pltpu.VMEM((2, tk, tq), jnp.float32),
                        pltpu.VMEM((2, tk, tq), jnp.bfloat16), pltpu.VMEM((2, tk, tq), jnp.bfloat16)],
        compiler_params=_params(3),
        name="diff",
    )(lam, qd, k_all, v_all, gain)


def _merge_kernel(x_ref, gt_ref, ya_ref, sza_ref, yb_ref, szb_ref, uc_ref, up_ref, un_ref, cb_ref, szc_ref,
                  yd_ref, szd_ref, g_ref, cw_ref, wa_ref, wb_ref, wc_ref, wd_ref, wo_ref, o_ref, *, tm, nt):
    i = pl.program_id(1)
    f32 = jnp.float32
    u = uc_ref[...].astype(f32)
    row = lax.broadcasted_iota(jnp.int32, (tm, 1), 0)
    prev_row = jnp.where(i > 0, up_ref[7:8, :].astype(f32), 0.0)
    next_row = jnp.where(i < nt - 1, un_ref[0:1, :].astype(f32), 0.0)
    u_prev = jnp.where(row == 0, prev_row, pltpu.roll(u, 1, 0))
    u_next = jnp.where(row == tm - 1, next_row, pltpu.roll(u, tm - 1, 0))
    cw = cw_ref[...]
    conv = u_prev * cw[0:1, :] + u * cw[1:2, :] + u_next * cw[2:3, :]
    yc = cb_ref[...].astype(f32) * conv
    d = x_ref.shape[-1]
    branches = ((ya_ref[...].astype(f32), sza_ref, wa_ref), (yb_ref[...].astype(f32), szb_ref, wb_ref),
                (yc, szc_ref, wc_ref), (yd_ref[...].astype(f32), szd_ref, wd_ref))
    mixed = jnp.zeros((tm, d), f32)
    for j, (y, sz_ref, w_ref) in enumerate(branches):
        t = _dot(_bf(y * sz_ref[...].astype(f32)), w_ref[...])
        mixed = mixed + g_ref[:, j * d:(j + 1) * d].astype(f32) * t
    out = _dot(_bf(mixed), wo_ref[...])
    o_ref[...] = x_ref[...] + gt_ref[...] * out


def _merge(x, gate, ya, sza, yb, szb, uc, cb, szc, yd, szd, g, conv_w, w_a, w_b, w_c, w_d, w_out, tm):
    b, n, d = x.shape
    nt = n // tm
    tok = lambda w: pl.BlockSpec((None, tm, w), lambda bi, i: (bi, i, 0))
    hb = tm // 8
    halo_p = pl.BlockSpec((None, 8, CONV_W), lambda bi, i: (bi, jnp.maximum(i * hb - 1, 0), 0))
    halo_n = pl.BlockSpec((None, 8, CONV_W), lambda bi, i: (bi, jnp.minimum((i + 1) * hb, n // 8 - 1), 0))
    per_b = pl.BlockSpec((None, 1, d), lambda bi, i: (bi, 0, 0))
    ws = [_bf(w_a), _bf(w_b), _bf(w_c), _bf(w_d), _bf(w_out)]
    return pl.pallas_call(
        functools.partial(_merge_kernel, tm=tm, nt=nt),
        grid=(b, nt),
        in_specs=[tok(d), per_b, tok(FOURIER_W), tok(FOURIER_W), tok(SWA_W), tok(SWA_W), tok(CONV_W), halo_p, halo_n,
                  tok(CONV_W), tok(CONV_W), tok(DIFF_W), tok(DIFF_W), tok(N_BRANCH * d), _const_spec(conv_w.shape)]
                 + [_const_spec(w.shape) for w in ws],
        out_specs=tok(d),
        out_shape=jax.ShapeDtypeStruct((b, n, d), jnp.float32),
        compiler_params=_params(2),
        name="merge",
    )(x, gate, ya, sza, yb, szb, uc, uc, uc, cb, szc, yd, szd, g, conv_w.astype(jnp.float32), *ws)


def _rope_tables(n):
    rows = n // GRID_W
    row = jnp.broadcast_to(jnp.arange(rows, dtype=jnp.float32)[:, None], (rows, GRID_W)).reshape(-1)
    col = jnp.broadcast_to(jnp.arange(GRID_W, dtype=jnp.float32)[None, :], (rows, GRID_W)).reshape(-1)
    nf = HEAD_DIM // 4
    inv = ROPE_BASE ** (-jnp.arange(nf, dtype=jnp.float32) / nf)
    ar = row[:, None] * inv[None, :]
    ac = col[:, None] * inv[None, :]
    cos = jnp.concatenate([jnp.cos(ar), jnp.cos(ar), jnp.cos(ac), jnp.cos(ac)], axis=-1)
    sin = jnp.concatenate([jnp.sin(ar), jnp.sin(ar), jnp.sin(ac), jnp.sin(ac)], axis=-1)
    sign = jnp.where((jnp.arange(HEAD_DIM) % 32) < 16, -1.0, 1.0)
    return jnp.tile(cos, (1, 2)), jnp.tile(sin * sign, (1, 2))


def _channel_dft():
    k = np.arange(HEAD_DIM)
    ang = 2.0 * np.pi * ((k[:, None] * k[None, :]) % HEAD_DIM) / HEAD_DIM
    eye = np.eye(FOURIER_W // HEAD_DIM)
    inv = 1.0 / math.sqrt(HEAD_DIM)
    return jnp.asarray(np.concatenate([np.kron(eye, np.cos(ang)), -np.kron(eye, np.sin(ang))], axis=1) * inv,
                       jnp.bfloat16)


def _extend_w_in(w):
    s = _SRC
    k0, k1 = s["b_k"][0], s["b_k"][0] + HEAD_DIM
    v0, v1 = s["b_v"][0], s["b_v"][0] + HEAD_DIM
    pieces = [w[:, s["a_x"][0]:s["b_q"][1]],
              w[:, k0:k1], w[:, k0:k1], w[:, k1:k1 + HEAD_DIM], w[:, k1:k1 + HEAD_DIM],
              w[:, v0:v1], w[:, v0:v1], w[:, v1:v1 + HEAD_DIM], w[:, v1:v1 + HEAD_DIM],
              w[:, s["c_x"][0]:]]
    return _bf(jnp.concatenate(pieces, axis=1))


def _mix(stream, scale, shift, gate, ctx_kv, lp, consts, lam, lam_init, rope, dft, tm, tq, tk, has_local):
    parts = _project(stream, scale, shift, lp["norm_g"], lp["w_ext"], consts["cs_bd"], consts["mg"],
                     lp["head_gains"], rope, tm)
    zr, zi, qb, kb, vb, uc, cb, qd, kd, vd, sza, szb, szc, szd, g = parts
    if ctx_kv is None:
        kx_b, vx_b, k_all, v_all = kb, vb, kd, vd
    else:
        kx_b, vx_b, kx_d, vx_d = ctx_kv
        k_all = jnp.concatenate([kx_d, kd], axis=1)
        v_all = jnp.concatenate([vx_d, vd], axis=1)
    ya = _fourier_positions(zr, zi, dft)
    yb = _window_attention(qb, kb, vb, kx_b, vx_b, lp["sink"], has_local)
    yd = _diff_attention(qd, k_all, v_all, lam, lp["subln"], lam_init, tq, tk)
    new = _merge(stream, gate, ya, sza, yb, szb, uc, cb, szc, yd, szd, g, lp["conv_w"],
                 lp["w_o_a"], lp["w_o_b"], lp["w_o_c"], lp["w_o_d"], lp["w_out"], tm)
    return new, (kb, vb, kd, vd)


def kernel(x, c, ctx, c_ctx, norm_g, w_mod, b_mod, w_in, q_norm_b, k_norm_b, sink_b, conv_w, q_norm_d, k_norm_d,
           lam_q1, lam_k1, lam_q2, lam_k2, subln_d, w_o_a, w_o_b, w_o_c, w_o_d, w_out):
    b, n, d = x.shape
    lc = ctx.shape[1]
    depth = w_in.shape[0]
    rope = _rope_tables(n)
    consts = dict(cs_bd=_channel_dft(),
                  mg=jnp.asarray(np.kron(np.eye(2), np.full((HEAD_DIM, HEAD_DIM), 1.0 / HEAD_DIM)), jnp.bfloat16))
    dft_x = _dft_tables(n)
    dft_c = _dft_tables(lc)
    c_rows = jnp.zeros((8, d), jnp.float32).at[:b].set(c).at[b].set(c_ctx)
    tm = min(256, n)
    tq = min(256, n)
    tk = 1280
    for l in range(depth):
        last = l == depth - 1
        lam_init = 0.8 - 0.6 * math.exp(-0.3 * l)
        lam_vecs = jnp.stack([lam_q1[l], lam_k1[l], lam_q2[l], lam_k2[l]]).astype(jnp.float32)
        mod, lam_o = _modulation(c_rows, w_mod[l], b_mod[l], lam_vecs, lam_init)
        lam = lam_o[0:1, 0:1]
        shift, scale, gate = (mod[:b, None, j * d:(j + 1) * d] for j in range(3))
        shift_c, scale_c, gate_c = (jnp.broadcast_to(mod[b:b + 1, None, j * d:(j + 1) * d], (b, 1, d))
                                    for j in range(3))
        tile2 = lambda v: jnp.tile(v.astype(jnp.float32), 2)
        lp = dict(norm_g=norm_g[l], w_ext=_extend_w_in(w_in[l]), sink=sink_b[l].astype(jnp.float32),
                  head_gains=jnp.stack([tile2(q_norm_b[l]), tile2(k_norm_b[l]), tile2(q_norm_d[l]), tile2(k_norm_d[l])]),
                  subln=subln_d[l], conv_w=conv_w[l], w_o_a=w_o_a[l], w_o_b=w_o_b[l], w_o_c=w_o_c[l],
                  w_o_d=w_o_d[l], w_out=w_out[l])
        if last:
            parts = _project(ctx, scale_c, shift_c, lp["norm_g"], lp["w_ext"], consts["cs_bd"], consts["mg"],
                             lp["head_gains"], None, min(256, lc))
            ctx_kv = (parts[3], parts[4], parts[8], parts[9])
        else:
            ctx, ctx_kv = _mix(ctx, scale_c, shift_c, gate_c, None, lp, consts, lam, lam_init, None, dft_c,
                               min(256, lc), min(256, lc), tk, False)
        x, _ = _mix(x, scale, shift, gate, ctx_kv, lp, consts, lam, lam_init, rope, dft_x, tm, tq, tk, True)
    return x
```

```python
import functools
import math

import jax
import jax.numpy as jnp
import numpy as np
from jax import lax
from jax.experimental import pallas as pl
from jax.experimental.pallas import tpu as pltpu

HEAD_DIM = 64
LANES = 128
FOURIER_W = 384
SWA_HEADS = 8
SWA_KV = 2
SWA_W = SWA_HEADS * HEAD_DIM
WINDOW = 128
BLOCK = 128
CONV_W = 384
CONV_K = 3
DIFF_HEADS = 4
DIFF_W = DIFF_HEADS * 2 * HEAD_DIM
N_BRANCH = 4
ROPE_BASE = 10000.0
NORM_EPS = 1e-6
NEG_INF = -1e30
GRID_W = 64
DFT_MINOR = 128
SUM_ROWS = 16
LOG2E = math.log2(math.e)
SAFE_SHIFT = 60.0
VMEM_LIMIT = 56 * 1024 * 1024

_SRC = dict(a_x=(0, 384), b_q=(384, 896), b_k=(896, 1024), b_v=(1024, 1152), c_x=(1152, 1536),
            c_b=(1536, 1920), c_c=(1920, 2304), d_q=(2304, 2816), d_k=(2816, 3328), d_v=(3328, 3840),
            z=(3840, 5632), m_g=(5632, 9728))


def _bf(x):
    return x.astype(jnp.bfloat16)


def _dot(a, b):
    return jnp.dot(a, b, preferred_element_type=jnp.float32)


def _dot_nt(a, b):
    return lax.dot_general(a, b, (((1,), (1,)), ((), ())), preferred_element_type=jnp.float32)


def _params(n_axes, flags=None):
    return pltpu.CompilerParams(dimension_semantics=("arbitrary",) * n_axes, vmem_limit_bytes=VMEM_LIMIT,
                                flags=flags)


def _const_spec(shape):
    nd = len(shape)
    return pl.BlockSpec(shape, lambda *_: (0,) * nd, pipeline_mode=pl.Buffered(1))


def _mod_kernel(c_ref, w_ref, b_ref, lam_ref, mod_ref, lamo_ref, *, lam_init):
    c = c_ref[...]
    s = c * jax.nn.sigmoid(c)
    mod_ref[...] = _dot(_bf(s), w_ref[...]) + b_ref[...]
    lv = lam_ref[...]
    a1 = jnp.sum(lv[0:1, :] * lv[1:2, :], axis=-1, keepdims=True)
    a2 = jnp.sum(lv[2:3, :] * lv[3:4, :], axis=-1, keepdims=True)
    lam = jnp.exp(a1) - jnp.exp(a2) + lam_init
    lamo_ref[...] = jnp.broadcast_to(lam, lamo_ref.shape)


def _modulation(c_rows, w_mod, b_mod, lam_vecs, lam_init):
    r, d = c_rows.shape
    return pl.pallas_call(
        functools.partial(_mod_kernel, lam_init=lam_init),
        out_shape=(jax.ShapeDtypeStruct((r, 3 * d), jnp.float32),
                   jax.ShapeDtypeStruct((8, LANES), jnp.float32)),
        name="mod",
    )(c_rows, _bf(w_mod), b_mod.reshape(1, 3 * d), lam_vecs)


def _head_norm_rope(t, gain, mg, cos, sin_s, lane_lo, scale):
    ms = _dot(_bf(t * t), mg)
    y = t * lax.rsqrt(ms + NORM_EPS) * gain
    if cos is not None:
        rot = jnp.where(lane_lo, pltpu.roll(y, LANES - 16, 1), pltpu.roll(y, 16, 1))
        y = y * cos + rot * sin_s
    if scale != 1.0:
        y = y * scale
    return y


def _proj_kernel(*refs, use_rope, tm):
    if use_rope:
        (x_ref, sc_ref, sh_ref, g_ref, w_ref, cs_ref, mg_ref, hg_ref, cos_ref, sin_ref), outs = refs[:10], refs[10:]
    else:
        (x_ref, sc_ref, sh_ref, g_ref, w_ref, cs_ref, mg_ref, hg_ref), outs = refs[:8], refs[8:]
        cos_ref = sin_ref = None
    (zr_ref, zi_ref, qb_ref, kb_ref, vb_ref, uc_ref, cb_ref, qd_ref, kd_ref, vd_ref,
     sza_ref, szb_ref, szc_ref, szd_ref, gate_ref) = outs

    x = x_ref[...]
    y = x * lax.rsqrt(jnp.mean(x * x, axis=-1, keepdims=True) + NORM_EPS) * g_ref[...]
    h = _bf(y * (1.0 + sc_ref[...]) + sh_ref[...])

    def proj(c0, width):
        return _dot(h, w_ref[:, c0:c0 + width])

    col = 0
    a = _bf(proj(col, FOURIER_W))
    zz = _dot(a, cs_ref[...])
    zr_ref[...] = _bf(zz[:, :FOURIER_W])
    zi_ref[...] = _bf(zz[:, FOURIER_W:])
    col += FOURIER_W

    mg = mg_ref[...]
    if use_rope:
        cos = cos_ref[...]
        sin_s = sin_ref[...]
    else:
        cos = sin_s = None
    lane = lax.broadcasted_iota(jnp.int32, (tm, LANES), 1)
    lane_lo = (lane & 31) < 16
    qscale = HEAD_DIM ** -0.5

    def normed(c0, width, gain_row, out_ref, scale):
        t = proj(c0, width)
        gain = hg_ref[gain_row:gain_row + 1, :]
        for s in range(width // LANES):
            ts = t[:, s * LANES:(s + 1) * LANES]
            out_ref[:, s * LANES:(s + 1) * LANES] = _bf(
                _head_norm_rope(ts, gain, mg, cos, sin_s, lane_lo, scale))

    normed(col, SWA_W, 0, qb_ref, qscale * LOG2E); col += SWA_W
    normed(col, 2 * LANES, 1, kb_ref, 1.0); col += 2 * LANES
    vb_ref[...] = _bf(proj(col, 2 * LANES)); col += 2 * LANES
    c3 = proj(col, 3 * CONV_W); col += 3 * CONV_W
    uc_ref[...] = _bf(c3[:, 2 * CONV_W:] * c3[:, :CONV_W])
    cb_ref[...] = _bf(c3[:, CONV_W:2 * CONV_W])
    normed(col, DIFF_W, 2, qd_ref, qscale * LOG2E); col += DIFF_W
    normed(col, DIFF_W, 3, kd_ref, 1.0); col += DIFF_W
    vd_ref[...] = _bf(proj(col, DIFF_W)); col += DIFF_W
    for ref, width in ((sza_ref, FOURIER_W), (szb_ref, SWA_W), (szc_ref, CONV_W), (szd_ref, DIFF_W)):
        z = proj(col, width)
        ref[...] = _bf(z * jax.nn.sigmoid(z))
        col += width
    d = x.shape[-1]
    for j in range(N_BRANCH):
        gate_ref[:, j * d:(j + 1) * d] = _bf(jax.nn.sigmoid(proj(col, d)))
        col += d


_PROJ_OUT_W = (FOURIER_W, FOURIER_W, SWA_W, 2 * LANES, 2 * LANES, CONV_W, CONV_W, DIFF_W, DIFF_W, DIFF_W,
               FOURIER_W, SWA_W, CONV_W, DIFF_W)


def _project(x, scale, shift, norm_g, w_ext, cs_bd, mg, head_gains, rope, tm):
    b, n, d = x.shape
    use_rope = rope is not None
    tok = lambda w: pl.BlockSpec((None, tm, w), lambda bi, i: (bi, i, 0))
    per_b = pl.BlockSpec((None, 1, d), lambda bi, i: (bi, 0, 0))
    in_specs = [tok(d), per_b, per_b, _const_spec((1, d)), _const_spec(w_ext.shape), _const_spec(cs_bd.shape),
                _const_spec(mg.shape), _const_spec(head_gains.shape)]
    args = [x, scale, shift, norm_g.reshape(1, d), w_ext, cs_bd, mg, head_gains]
    if use_rope:
        tab = pl.BlockSpec((tm, LANES), lambda bi, i: (i, 0))
        in_specs += [tab, tab]
        args += list(rope)
    widths = _PROJ_OUT_W + (N_BRANCH * d,)
    return pl.pallas_call(
        functools.partial(_proj_kernel, use_rope=use_rope, tm=tm),
        grid=(b, n // tm),
        in_specs=in_specs,
        out_specs=[tok(w) for w in widths],
        out_shape=[jax.ShapeDtypeStruct((b, n, w), jnp.bfloat16) for w in widths],
        compiler_params=_params(2),
        name="proj",
    )(*args)


def _dft1_kernel(zr_ref, zi_ref, f_ref, tc_ref, ts_ref, o_ref, *, r, tn2):
    z = jnp.concatenate([zr_ref[...], zi_ref[...]], axis=0)
    a = _dot(f_ref[...], z)
    for j in range(tn2):
        ar = a[:r, j * FOURIER_W:(j + 1) * FOURIER_W]
        ai = a[r:, j * FOURIER_W:(j + 1) * FOURIER_W]
        tc = jnp.concatenate([tc_ref[j]] * (FOURIER_W // LANES), axis=-1)
        ts = jnp.concatenate([ts_ref[j]] * (FOURIER_W // LANES), axis=-1)
        o_ref[0, j] = _bf(ar * tc + ai * ts)
        o_ref[1, j] = _bf(ai * tc - ar * ts)


def _left_matmul_kernel(m_ref, x_ref, o_ref):
    o_ref[...] = _bf(_dot(m_ref[...], x_ref[...]))


def _left_matmul(mat, x, tc):
    b, k, c = x.shape
    rows = mat.shape[0]
    return pl.pallas_call(
        _left_matmul_kernel,
        grid=(b, c // tc),
        in_specs=[_const_spec(mat.shape), pl.BlockSpec((None, k, tc), lambda bi, i: (bi, 0, i))],
        out_specs=pl.BlockSpec((None, rows, tc), lambda bi, i: (bi, 0, i)),
        out_shape=jax.ShapeDtypeStruct((b, rows, c), jnp.bfloat16),
        compiler_params=_params(2),
        name="dft2",
    )(mat, x)


def _dft_tables(n):
    inv = 1.0 / math.sqrt(n)
    if n <= 2 * DFT_MINOR:
        k = np.arange(n)
        ang = 2.0 * np.pi * ((k[:, None] * k[None, :]) % n) / n
        return dict(direct=jnp.asarray(np.concatenate([np.cos(ang), np.sin(ang)], axis=1) * inv, jnp.bfloat16))
    r = n // DFT_MINOR
    k1 = np.arange(r)
    a1 = 2.0 * np.pi * ((k1[:, None] * k1[None, :]) % r) / r
    c1, s1 = np.cos(a1), np.sin(a1)
    f1 = np.block([[c1, s1], [-s1, c1]])
    n2 = np.arange(DFT_MINOR)
    at = 2.0 * np.pi * (n2[:, None] * k1[None, :]) / n
    tw_c = np.repeat(np.cos(at)[:, :, None], LANES, axis=2)
    tw_s = np.repeat(np.sin(at)[:, :, None], LANES, axis=2)
    a2 = 2.0 * np.pi * ((n2[:, None] * n2[None, :]) % DFT_MINOR) / DFT_MINOR
    f2 = np.concatenate([np.cos(a2), np.sin(a2)], axis=1) * inv
    return dict(f1=jnp.asarray(f1, jnp.bfloat16), tw_c=jnp.asarray(tw_c, jnp.float32),
                tw_s=jnp.asarray(tw_s, jnp.float32), f2=jnp.asarray(f2, jnp.bfloat16))


def _fourier_positions(zr, zi, tabs):
    b, n, w = zr.shape
    if "direct" in tabs:
        return _left_matmul(tabs["direct"], jnp.concatenate([zr, zi], axis=1), w)
    r = n // DFT_MINOR
    tn2 = 8
    zr2 = zr.reshape(b, r, DFT_MINOR * w)
    zi2 = zi.reshape(b, r, DFT_MINOR * w)
    zin = pl.BlockSpec((None, r, tn2 * w), lambda bi, i: (bi, 0, i))
    tw = pl.BlockSpec((tn2, r, LANES), lambda bi, i: (i, 0, 0))
    g = pl.pallas_call(
        functools.partial(_dft1_kernel, r=r, tn2=tn2),
        grid=(b, DFT_MINOR // tn2),
        in_specs=[zin, zin, _const_spec(tabs["f1"].shape), tw, tw],
        out_specs=pl.BlockSpec((None, 2, tn2, r, w), lambda bi, i: (bi, 0, i, 0, 0)),
        out_shape=jax.ShapeDtypeStruct((b, 2, DFT_MINOR, r, w), jnp.bfloat16),
        compiler_params=_params(2),
        name="dft1",
    )(zr2, zi2, tabs["f1"], tabs["tw_c"], tabs["tw_s"])
    y = _left_matmul(tabs["f2"], g.reshape(b, 2 * DFT_MINOR, r * w), 4 * w)
    return y.reshape(b, n, w)


def _win_kernel(*refs, has_local, nb):
    if has_local:
        sink_ref, q_ref, kp_ref, kc_ref, kn_ref, vp_ref, vc_ref, vn_ref, kx_ref, vx_ref, o_ref = refs
    else:
        sink_ref, q_ref, kx_ref, vx_ref, o_ref = refs
    i = pl.program_id(1)
    tq = q_ref.shape[0]
    group = SWA_HEADS // SWA_KV
    lane = lax.broadcasted_iota(jnp.int32, (1, LANES), 1)
    halves = (lane < HEAD_DIM, lane >= HEAD_DIM)
    row_lo = lax.broadcasted_iota(jnp.int32, (LANES, 1), 0) < HEAD_DIM
    if has_local:
        kr = lax.broadcasted_iota(jnp.int32, (3 * BLOCK, tq), 0)
        qc = lax.broadcasted_iota(jnp.int32, (3 * BLOCK, tq), 1)
        kpos = (i - 1) * BLOCK + kr
        ok = (jnp.abs(kr - BLOCK - qc) <= WINDOW) & (kpos >= 0) & (kpos < nb * BLOCK)
        bias = jnp.where(ok, 0.0, NEG_INF)
        bias = jnp.concatenate([bias] * group, axis=1)
    for j in range(SWA_KV):
        sl = slice(j * LANES, (j + 1) * LANES)
        if has_local:
            kd = jnp.concatenate([kp_ref[:, sl], kc_ref[:, sl], kn_ref[:, sl], kx_ref[:, sl]], axis=0)
            vd = jnp.concatenate([vp_ref[:, sl], vc_ref[:, sl], vn_ref[:, sl], vx_ref[:, sl]], axis=0)
        else:
            kd = kx_ref[:, sl]
            vd = vx_ref[:, sl]
        q_stack = []
        for t in range(group // 2):
            slab = j * (group // 2) + t
            q2 = q_ref[:, slab * LANES:(slab + 1) * LANES]
            q_stack += [jnp.where(halves[e], q2, jnp.zeros_like(q2)) for e in range(2)]
        st = _dot_nt(kd, jnp.concatenate(q_stack, axis=0))
        if has_local:
            st = jnp.concatenate([st[:3 * BLOCK] + bias, st[3 * BLOCK:]], axis=0)
        sk = sink_ref[j:j + 1, :]
        m = jnp.maximum(jnp.max(st, axis=0, keepdims=True), sk)
        p = jnp.exp2(st - m)
        inv = 1.0 / (jnp.sum(p, axis=0, keepdims=True) + jnp.exp2(sk - m))
        ot = lax.dot_general(vd, _bf(p), (((0,), (0,)), ((), ())), preferred_element_type=jnp.float32) * inv
        for t in range(group // 2):
            slab = j * (group // 2) + t
            pair = jnp.where(row_lo, ot[:, (2 * t) * tq:(2 * t + 1) * tq], ot[:, (2 * t + 1) * tq:(2 * t + 2) * tq])
            o_ref[:, slab * LANES:(slab + 1) * LANES] = _bf(pair.T)


def _window_attention(qb, kb, vb, kx, vx, sink, has_local):
    b, n, _ = qb.shape
    nb = n // BLOCK
    lx = kx.shape[1]
    qspec = pl.BlockSpec((None, BLOCK, SWA_W), lambda bi, i: (bi, i, 0))
    ctx = pl.BlockSpec((None, lx, 2 * LANES), lambda bi, i: (bi, 0, 0))
    sink_rows = jnp.repeat(sink.astype(jnp.float32) * LOG2E, BLOCK).reshape(SWA_KV, -1)
    args = [sink_rows, qb]
    specs = [_const_spec(sink_rows.shape), qspec]
    if has_local:
        prv = pl.BlockSpec((None, BLOCK, 2 * LANES), lambda bi, i: (bi, jnp.maximum(i - 1, 0), 0))
        cur = pl.BlockSpec((None, BLOCK, 2 * LANES), lambda bi, i: (bi, i, 0))
        nxt = pl.BlockSpec((None, BLOCK, 2 * LANES), lambda bi, i: (bi, jnp.minimum(i + 1, nb - 1), 0))
        args += [kb, kb, kb, vb, vb, vb]
        specs += [prv, cur, nxt, prv, cur, nxt]
    args += [kx, vx]
    specs += [ctx, ctx]
    return pl.pallas_call(
        functools.partial(_win_kernel, has_local=has_local, nb=nb),
        grid=(b, nb),
        in_specs=specs,
        out_specs=qspec,
        out_shape=jax.ShapeDtypeStruct((b, n, SWA_W), jnp.bfloat16),
        compiler_params=_params(2),
        name="win",
    )(*args)


def _diff_kernel(lam_ref, q_ref, k_ref, v_ref, g_ref, o_ref, vt_ref, kmax_ref, acc_ref, s0_ref, s1_ref, p0_ref,
                 p1_ref, *, tk, out_scale):
    qi = pl.program_id(2)
    tq = q_ref.shape[0]
    n_chunks = vt_ref.shape[0]

    sel_r = lax.broadcasted_iota(jnp.int32, (8, LANES), 0)
    sel_l = lax.broadcasted_iota(jnp.int32, (8, LANES), 1)
    sel = _bf(jnp.where((sel_l >= sel_r * HEAD_DIM) & (sel_l < (sel_r + 1) * HEAD_DIM), 1.0, 0.0))

    def sq_norms(t):
        tf = t.astype(jnp.float32)
        return _dot_nt(sel, _bf(tf * tf))

    @pl.when(qi == 0)
    def _():
        def body(ci, kmax):
            c0 = pl.multiple_of(ci * tk, tk)
            vt_ref[ci, :LANES, :] = _bf(v_ref[pl.ds(c0, tk), :].astype(jnp.float32).T)
            vt_ref[ci, LANES:, :] = jnp.ones((SUM_ROWS, tk), jnp.bfloat16)
            return jnp.maximum(kmax, jnp.max(sq_norms(k_ref[pl.ds(c0, tk), :]), axis=1, keepdims=True))
        kmax = lax.fori_loop(0, n_chunks, body, jnp.zeros((8, 1), jnp.float32))
        kmax_ref[...] = jnp.broadcast_to(kmax, kmax_ref.shape)

    q = q_ref[...]
    lane = lax.broadcasted_iota(jnp.int32, (1, LANES), 1)
    qm = (jnp.where(lane < HEAD_DIM, q, jnp.zeros_like(q)), jnp.where(lane >= HEAD_DIM, q, jnp.zeros_like(q)))
    acc_ref[...] = jnp.zeros_like(acc_ref)
    s_bufs = (s0_ref, s1_ref)
    p_bufs = (p0_ref, p1_ref)

    bound = jnp.sqrt(sq_norms(q) * kmax_ref[:, 0:1])
    shift_safe = jnp.max(bound[0:2, :]) <= SAFE_SHIFT

    @pl.when(shift_safe)
    def _():
        q_both = jnp.concatenate(qm, axis=0)
        b_both = jnp.concatenate([bound[0:1, :], bound[1:2, :]], axis=1)

        def chunk(ci, denom):
            c0 = pl.multiple_of(ci * tk, tk)
            p = jnp.exp2(_dot_nt(k_ref[pl.ds(c0, tk), :], q_both) - b_both)
            acc_ref[:LANES, :] += _dot(vt_ref[ci, :LANES, :], _bf(p))
            return denom + jnp.sum(p, axis=0, keepdims=True)
        denom = lax.fori_loop(0, n_chunks, chunk, jnp.zeros((1, 2 * tq), jnp.float32), unroll=True)
        acc_ref[LANES:, :] = jnp.broadcast_to(denom, (SUM_ROWS, 2 * tq))

    @pl.when(jnp.logical_not(shift_safe))
    def _():
        _diff_online(qm, k_ref, vt_ref, acc_ref, s_bufs, p_bufs, tk=tk, tq=tq, n_chunks=n_chunks)

    lam = lam_ref[0, 0]
    a1 = acc_ref[:, :tq]
    a2 = acc_ref[:, tq:]
    o = a1[:LANES] * (1.0 / a1[LANES:LANES + 1]) - lam * (a2[:LANES] * (1.0 / a2[LANES:LANES + 1]))
    y = o * lax.rsqrt(jnp.mean(o * o, axis=0, keepdims=True) + NORM_EPS) * g_ref[...] * out_scale
    o_ref[...] = _bf(y.T)


def _diff_online(qm, k_ref, vt_ref, acc_ref, s_bufs, p_bufs, *, tk, tq, n_chunks):
    def scores(ci, slot):
        c0 = ci * tk if isinstance(ci, int) else pl.multiple_of(ci * tk, tk)
        k = k_ref[pl.ds(c0, tk), :]
        mx = []
        for c in range(2):
            st = _dot_nt(k, qm[c])
            s_bufs[slot][c] = st
            mx.append(jnp.max(st, axis=0, keepdims=True))
        return tuple(mx)

    def probs(slot, mx, m_run):
        m_new, alpha = [], []
        for c in range(2):
            m = jnp.maximum(m_run[c], mx[c])
            alpha.append(jnp.exp2(m_run[c] - m))
            p_bufs[slot][c] = _bf(jnp.exp2(s_bufs[slot][c] - m))
            m_new.append(m)
        return tuple(m_new), tuple(alpha)

    def accumulate(ci, slot, alpha):
        vt = vt_ref[ci]
        for c in range(2):
            cols = slice(c * tq, (c + 1) * tq)
            acc_ref[:, cols] = alpha[c] * acc_ref[:, cols] + _dot(vt, p_bufs[slot][c])

    def step(t, par, do_scores, do_probs, do_acc, state):
        m_run, mx, alpha = state
        mx_next = scores(t + 2, par) if do_scores else mx
        if do_probs:
            m_run, alpha_next = probs(1 - par, mx, m_run)
        else:
            alpha_next = alpha
        if do_acc:
            accumulate(t, par, alpha)
        return m_run, mx_next, alpha_next

    neg = jnp.full((1, tq), NEG_INF, jnp.float32)
    one = jnp.ones((1, tq), jnp.float32)
    state = ((neg, neg), (neg, neg), (one, one))
    for t in (-2, -1):
        state = step(t, t % 2, t + 2 < n_chunks, 0 <= t + 1 < n_chunks, False, state)
    n_steady = max(n_chunks - 2, 0)

    def pair(j, state):
        state = step(2 * j, 0, True, True, True, state)
        return step(2 * j + 1, 1, True, True, True, state)

    state = lax.fori_loop(0, n_steady // 2, pair, state)
    if n_steady % 2:
        state = step(n_steady - 1, 0, True, True, True, state)
    for t in range(n_steady, n_chunks):
        state = step(t, t % 2, False, t + 1 < n_chunks, True, state)


def _pick_tk(nk, cap):
    best = LANES
    for t in range(LANES, min(cap, nk) + 1, LANES):
        if nk % t == 0:
            best = t
    return best


def _diff_attention(qd, k_all, v_all, lam, subln_g, lam_init, tq, tk_cap):
    b, n, _ = qd.shape
    nk = k_all.shape[1]
    tk = _pick_tk(nk, tk_cap)
    smem = pl.BlockSpec(memory_space=pltpu.SMEM)
    qspec = pl.BlockSpec((None, tq, LANES), lambda bi, h, i: (bi, i, h))
    kvspec = pl.BlockSpec((None, nk, LANES), lambda bi, h, i: (bi, 0, h))
    gain = jnp.broadcast_to(subln_g.astype(jnp.float32)[:, None], (LANES, tq))
    return pl.pallas_call(
        functools.partial(_diff_kernel, tk=tk, out_scale=1.0 - lam_init),
        grid=(b, DIFF_HEADS, n // tq),
        in_specs=[smem, qspec, kvspec, kvspec, _const_spec((LANES, tq))],
        out_specs=qspec,
        out_shape=jax.ShapeDtypeStruct((b, n, DIFF_W), jnp.bfloat16),
        scratch_shapes=[pltpu.VMEM((nk // tk, LANES + SUM_ROWS, tk), jnp.bfloat16),
                        pltpu.VMEM((8, LANES), jnp.float32),
                        pltpu.VMEM((LANES + SUM_ROWS, 2 * tq), jnp.float32),
                        pltpu.VMEM((2, tk, tq), jnp.float32), pltpu.VMEM((2, tk, tq), jnp.float32),
                        pltpu.VMEM((2, tk, tq), jnp.bfloat16), pltpu.VMEM((2, tk, tq), jnp.bfloat16)],
        compiler_params=_params(3),
        name="diff",
    )(lam, qd, k_all, v_all, gain)


def _merge_kernel(x_ref, gt_ref, ya_ref, sza_ref, yb_ref, szb_ref, uc_ref, up_ref, un_ref, cb_ref, szc_ref,
                  yd_ref, szd_ref, g_ref, cw_ref, wa_ref, wb_ref, wc_ref, wd_ref, wo_ref, o_ref, *, tm, nt):
    i = pl.program_id(1)
    f32 = jnp.float32
    u = uc_ref[...].astype(f32)
    row = lax.broadcasted_iota(jnp.int32, (tm, 1), 0)
    prev_row = jnp.where(i > 0, up_ref[7:8, :].astype(f32), 0.0)
    next_row = jnp.where(i < nt - 1, un_ref[0:1, :].astype(f32), 0.0)
    u_prev = jnp.where(row == 0, prev_row, pltpu.roll(u, 1, 0))
    u_next = jnp.where(row == tm - 1, next_row, pltpu.roll(u, tm - 1, 0))
    cw = cw_ref[...]
    conv = u_prev * cw[0:1, :] + u * cw[1:2, :] + u_next * cw[2:3, :]
    yc = cb_ref[...].astype(f32) * conv
    d = x_ref.shape[-1]
    branches = ((ya_ref[...].astype(f32), sza_ref, wa_ref), (yb_ref[...].astype(f32), szb_ref, wb_ref),
                (yc, szc_ref, wc_ref), (yd_ref[...].astype(f32), szd_ref, wd_ref))
    mixed = jnp.zeros((tm, d), f32)
    for j, (y, sz_ref, w_ref) in enumerate(branches):
        t = _dot(_bf(y * sz_ref[...].astype(f32)), w_ref[...])
        mixed = mixed + g_ref[:, j * d:(j + 1) * d].astype(f32) * t
    out = _dot(_bf(mixed), wo_ref[...])
    o_ref[...] = x_ref[...] + gt_ref[...] * out


def _merge(x, gate, ya, sza, yb, szb, uc, cb, szc, yd, szd, g, conv_w, w_a, w_b, w_c, w_d, w_out, tm):
    b, n, d = x.shape
    nt = n // tm
    tok = lambda w: pl.BlockSpec((None, tm, w), lambda bi, i: (bi, i, 0))
    hb = tm // 8
    halo_p = pl.BlockSpec((None, 8, CONV_W), lambda bi, i: (bi, jnp.maximum(i * hb - 1, 0), 0))
    halo_n = pl.BlockSpec((None, 8, CONV_W), lambda bi, i: (bi, jnp.minimum((i + 1) * hb, n // 8 - 1), 0))
    per_b = pl.BlockSpec((None, 1, d), lambda bi, i: (bi, 0, 0))
    ws = [_bf(w_a), _bf(w_b), _bf(w_c), _bf(w_d), _bf(w_out)]
    return pl.pallas_call(
        functools.partial(_merge_kernel, tm=tm, nt=nt),
        grid=(b, nt),
        in_specs=[tok(d), per_b, tok(FOURIER_W), tok(FOURIER_W), tok(SWA_W), tok(SWA_W), tok(CONV_W), halo_p, halo_n,
                  tok(CONV_W), tok(CONV_W), tok(DIFF_W), tok(DIFF_W), tok(N_BRANCH * d), _const_spec(conv_w.shape)]
                 + [_const_spec(w.shape) for w in ws],
        out_specs=tok(d),
        out_shape=jax.ShapeDtypeStruct((b, n, d), jnp.float32),
        compiler_params=_params(2),
        name="merge",
    )(x, gate, ya, sza, yb, szb, uc, uc, uc, cb, szc, yd, szd, g, conv_w.astype(jnp.float32), *ws)


def _rope_tables(n):
    rows = n // GRID_W
    row = jnp.broadcast_to(jnp.arange(rows, dtype=jnp.float32)[:, None], (rows, GRID_W)).reshape(-1)
    col = jnp.broadcast_to(jnp.arange(GRID_W, dtype=jnp.float32)[None, :], (rows, GRID_W)).reshape(-1)
    nf = HEAD_DIM // 4
    inv = ROPE_BASE ** (-jnp.arange(nf, dtype=jnp.float32) / nf)
    ar = row[:, None] * inv[None, :]
    ac = col[:, None] * inv[None, :]
    cos = jnp.concatenate([jnp.cos(ar), jnp.cos(ar), jnp.cos(ac), jnp.cos(ac)], axis=-1)
    sin = jnp.concatenate([jnp.sin(ar), jnp.sin(ar), jnp.sin(ac), jnp.sin(ac)], axis=-1)
    sign = jnp.where((jnp.arange(HEAD_DIM) % 32) < 16, -1.0, 1.0)
    return jnp.tile(cos, (1, 2)), jnp.tile(sin * sign, (1, 2))


def _channel_dft():
    k = np.arange(HEAD_DIM)
    ang = 2.0 * np.pi * ((k[:, None] * k[None, :]) % HEAD_DIM) / HEAD_DIM
    eye = np.eye(FOURIER_W // HEAD_DIM)
    inv = 1.0 / math.sqrt(HEAD_DIM)
    return jnp.asarray(np.concatenate([np.kron(eye, np.cos(ang)), -np.kron(eye, np.sin(ang))], axis=1) * inv,
                       jnp.bfloat16)


def _extend_w_in(w):
    s = _SRC
    k0, k1 = s["b_k"][0], s["b_k"][0] + HEAD_DIM
    v0, v1 = s["b_v"][0], s["b_v"][0] + HEAD_DIM
    pieces = [w[:, s["a_x"][0]:s["b_q"][1]],
              w[:, k0:k1], w[:, k0:k1], w[:, k1:k1 + HEAD_DIM], w[:, k1:k1 + HEAD_DIM],
              w[:, v0:v1], w[:, v0:v1], w[:, v1:v1 + HEAD_DIM], w[:, v1:v1 + HEAD_DIM],
              w[:, s["c_x"][0]:]]
    return _bf(jnp.concatenate(pieces, axis=1))


def _mix(stream, scale, shift, gate, ctx_kv, lp, consts, lam, lam_init, rope, dft, tm, tq, tk, has_local):
    parts = _project(stream, scale, shift, lp["norm_g"], lp["w_ext"], consts["cs_bd"], consts["mg"],
                     lp["head_gains"], rope, tm)
    zr, zi, qb, kb, vb, uc, cb, qd, kd, vd, sza, szb, szc, szd, g = parts
    if ctx_kv is None:
        kx_b, vx_b, k_all, v_all = kb, vb, kd, vd
    else:
        kx_b, vx_b, kx_d, vx_d = ctx_kv
        k_all = jnp.concatenate([kx_d, kd], axis=1)
        v_all = jnp.concatenate([vx_d, vd], axis=1)
    ya = _fourier_positions(zr, zi, dft)
    yb = _window_attention(qb, kb, vb, kx_b, vx_b, lp["sink"], has_local)
    yd = _diff_attention(qd, k_all, v_all, lam, lp["subln"], lam_init, tq, tk)
    new = _merge(stream, gate, ya, sza, yb, szb, uc, cb, szc, yd, szd, g, lp["conv_w"],
                 lp["w_o_a"], lp["w_o_b"], lp["w_o_c"], lp["w_o_d"], lp["w_out"], tm)
    return new, (kb, vb, kd, vd)


def kernel(x, c, ctx, c_ctx, norm_g, w_mod, b_mod, w_in, q_norm_b, k_norm_b, sink_b, conv_w, q_norm_d, k_norm_d,
           lam_q1, lam_k1, lam_q2, lam_k2, subln_d, w_o_a, w_o_b, w_o_c, w_o_d, w_out):
    b, n, d = x.shape
    lc = ctx.shape[1]
    depth = w_in.shape[0]
    rope = _rope_tables(n)
    consts = dict(cs_bd=_channel_dft(),
                  mg=jnp.asarray(np.kron(np.eye(2), np.full((HEAD_DIM, HEAD_DIM), 1.0 / HEAD_DIM)), jnp.bfloat16))
    dft_x = _dft_tables(n)
    dft_c = _dft_tables(lc)
    c_rows = jnp.zeros((8, d), jnp.float32).at[:b].set(c).at[b].set(c_ctx)
    tm = min(512, n)
    tq = min(512, n)
    tk = 1280
    for l in range(depth):
        last = l == depth - 1
        lam_init = 0.8 - 0.6 * math.exp(-0.3 * l)
        lam_vecs = jnp.stack([lam_q1[l], lam_k1[l], lam_q2[l], lam_k2[l]]).astype(jnp.float32)
        mod, lam_o = _modulation(c_rows, w_mod[l], b_mod[l], lam_vecs, lam_init)
        lam = lam_o[0:1, 0:1]
        shift, scale, gate = (mod[:b, None, j * d:(j + 1) * d] for j in range(3))
        shift_c, scale_c, gate_c = (jnp.broadcast_to(mod[b:b + 1, None, j * d:(j + 1) * d], (b, 1, d))
                                    for j in range(3))
        tile2 = lambda v: jnp.tile(v.astype(jnp.float32), 2)
        lp = dict(norm_g=norm_g[l], w_ext=_extend_w_in(w_in[l]), sink=sink_b[l].astype(jnp.float32),
                  head_gains=jnp.stack([tile2(q_norm_b[l]), tile2(k_norm_b[l]), tile2(q_norm_d[l]), tile2(k_norm_d[l])]),
                  subln=subln_d[l], conv_w=conv_w[l], w_o_a=w_o_a[l], w_o_b=w_o_b[l], w_o_c=w_o_c[l],
                  w_o_d=w_o_d[l], w_out=w_out[l])
        if last:
            parts = _project(ctx, scale_c, shift_c, lp["norm_g"], lp["w_ext"], consts["cs_bd"], consts["mg"],
                             lp["head_gains"], None, min(256, lc))
            ctx_kv = (parts[3], parts[4], parts[8], parts[9])
        else:
            ctx, ctx_kv = _mix(ctx, scale_c, shift_c, gate_c, None, lp, consts, lam, lam_init, None, dft_c,
                               min(256, lc), min(256, lc), tk, False)
        x, _ = _mix(x, scale, shift, gate, ctx_kv, lp, consts, lam, lam_init, rope, dft_x, tm, tq, tk, True)
    return x
```

```python
import functools
import math

import jax
import jax.numpy as jnp
import numpy as np
from jax import lax
from jax.experimental import pallas as pl
from jax.experimental.pallas import tpu as pltpu

HEAD_DIM = 64
LANES = 128
FOURIER_W = 384
SWA_HEADS = 8
SWA_KV = 2
SWA_W = SWA_HEADS * HEAD_DIM
WINDOW = 128
BLOCK = 128
CONV_W = 384
CONV_K = 3
DIFF_HEADS = 4
DIFF_W = DIFF_HEADS * 2 * HEAD_DIM
N_BRANCH = 4
ROPE_BASE = 10000.0
NORM_EPS = 1e-6
NEG_INF = -1e30
GRID_W = 64
DFT_MINOR = 128
SUM_ROWS = 16
LOG2E = math.log2(math.e)
SAFE_SHIFT = 60.0
VMEM_LIMIT = 56 * 1024 * 1024

_SRC = dict(a_x=(0, 384), b_q=(384, 896), b_k=(896, 1024), b_v=(1024, 1152), c_x=(1152, 1536),
            c_b=(1536, 1920), c_c=(1920, 2304), d_q=(2304, 2816), d_k=(2816, 3328), d_v=(3328, 3840),
            z=(3840, 5632), m_g=(5632, 9728))


def _bf(x):
    return x.astype(jnp.bfloat16)


def _dot(a, b):
    return jnp.dot(a, b, preferred_element_type=jnp.float32)


def _dot_nt(a, b):
    return lax.dot_general(a, b, (((1,), (1,)), ((), ())), preferred_element_type=jnp.float32)


def _params(n_axes, flags=None):
    return pltpu.CompilerParams(dimension_semantics=("arbitrary",) * n_axes, vmem_limit_bytes=VMEM_LIMIT,
                                flags=flags)


def _const_spec(shape):
    nd = len(shape)
    return pl.BlockSpec(shape, lambda *_: (0,) * nd, pipeline_mode=pl.Buffered(1))


def _mod_kernel(c_ref, w_ref, b_ref, lam_ref, mod_ref, lamo_ref, *, lam_init):
    c = c_ref[...]
    s = c * jax.nn.sigmoid(c)
    mod_ref[...] = _dot(_bf(s), w_ref[...]) + b_ref[...]
    lv = lam_ref[...]
    a1 = jnp.sum(lv[0:1, :] * lv[1:2, :], axis=-1, keepdims=True)
    a2 = jnp.sum(lv[2:3, :] * lv[3:4, :], axis=-1, keepdims=True)
    lam = jnp.exp(a1) - jnp.exp(a2) + lam_init
    lamo_ref[...] = jnp.broadcast_to(lam, lamo_ref.shape)


def _modulation(c_rows, w_mod, b_mod, lam_vecs, lam_init):
    r, d = c_rows.shape
    return pl.pallas_call(
        functools.partial(_mod_kernel, lam_init=lam_init),
        out_shape=(jax.ShapeDtypeStruct((r, 3 * d), jnp.float32),
                   jax.ShapeDtypeStruct((8, LANES), jnp.float32)),
        name="mod",
    )(c_rows, _bf(w_mod), b_mod.reshape(1, 3 * d), lam_vecs)


def _head_norm_rope(t, gain, cos, sin_s, lane_lo, scale):
    sq = t * t
    first = lax.broadcasted_iota(jnp.int32, t.shape, 1) < HEAD_DIM
    ms = jnp.where(first, jnp.sum(jnp.where(first, sq, 0.0), axis=-1, keepdims=True),
                   jnp.sum(jnp.where(first, 0.0, sq), axis=-1, keepdims=True)) * (1.0 / HEAD_DIM)
    y = t * lax.rsqrt(ms + NORM_EPS) * gain
    if cos is not None:
        rot = jnp.where(lane_lo, pltpu.roll(y, LANES - 16, 1), pltpu.roll(y, 16, 1))
        y = y * cos + rot * sin_s
    if scale != 1.0:
        y = y * scale
    return y


def _proj_kernel(*refs, use_rope, tm):
    if use_rope:
        (x_ref, sc_ref, sh_ref, g_ref, w_ref, cs_ref, hg_ref, cos_ref, sin_ref), outs = refs[:9], refs[9:]
    else:
        (x_ref, sc_ref, sh_ref, g_ref, w_ref, cs_ref, hg_ref), outs = refs[:7], refs[7:]
        cos_ref = sin_ref = None
    (zr_ref, zi_ref, qb_ref, kb_ref, vb_ref, uc_ref, cb_ref, qd_ref, kd_ref, vd_ref,
     sza_ref, szb_ref, szc_ref, szd_ref, gate_ref) = outs

    x = x_ref[...]
    y = x * lax.rsqrt(jnp.mean(x * x, axis=-1, keepdims=True) + NORM_EPS) * g_ref[...]
    h = _bf(y * (1.0 + sc_ref[...]) + sh_ref[...])

    def proj(c0, width):
        return _dot(h, w_ref[:, c0:c0 + width])

    col = 0
    a = _bf(proj(col, FOURIER_W))
    zz = _dot(a, cs_ref[...])
    zr_ref[...] = _bf(zz[:, :FOURIER_W])
    zi_ref[...] = _bf(zz[:, FOURIER_W:])
    col += FOURIER_W

    if use_rope:
        cos = cos_ref[...]
        sin_s = sin_ref[...]
    else:
        cos = sin_s = None
    lane = lax.broadcasted_iota(jnp.int32, (tm, LANES), 1)
    lane_lo = (lane & 31) < 16
    qscale = HEAD_DIM ** -0.5

    def normed(c0, width, gain_row, out_ref, scale):
        t = proj(c0, width)
        gain = hg_ref[gain_row:gain_row + 1, :]
        for s in range(width // LANES):
            ts = t[:, s * LANES:(s + 1) * LANES]
            out_ref[:, s * LANES:(s + 1) * LANES] = _bf(
                _head_norm_rope(ts, gain, cos, sin_s, lane_lo, scale))

    normed(col, SWA_W, 0, qb_ref, qscale * LOG2E); col += SWA_W
    normed(col, 2 * LANES, 1, kb_ref, 1.0); col += 2 * LANES
    vb_ref[...] = _bf(proj(col, 2 * LANES)); col += 2 * LANES
    c3 = proj(col, 3 * CONV_W); col += 3 * CONV_W
    uc_ref[...] = _bf(c3[:, 2 * CONV_W:] * c3[:, :CONV_W])
    cb_ref[...] = _bf(c3[:, CONV_W:2 * CONV_W])
    normed(col, DIFF_W, 2, qd_ref, qscale * LOG2E); col += DIFF_W
    normed(col, DIFF_W, 3, kd_ref, 1.0); col += DIFF_W
    vd_ref[...] = _bf(proj(col, DIFF_W)); col += DIFF_W
    for ref, width in ((sza_ref, FOURIER_W), (szb_ref, SWA_W), (szc_ref, CONV_W), (szd_ref, DIFF_W)):
        z = proj(col, width)
        ref[...] = _bf(z * jax.nn.sigmoid(z))
        col += width
    d = x.shape[-1]
    for j in range(N_BRANCH):
        gate_ref[:, j * d:(j + 1) * d] = _bf(jax.nn.sigmoid(proj(col, d)))
        col += d


_PROJ_OUT_W = (FOURIER_W, FOURIER_W, SWA_W, 2 * LANES, 2 * LANES, CONV_W, CONV_W, DIFF_W, DIFF_W, DIFF_W,
               FOURIER_W, SWA_W, CONV_W, DIFF_W)


def _project(x, scale, shift, norm_g, w_ext, cs_bd, head_gains, rope, tm):
    b, n, d = x.shape
    use_rope = rope is not None
    tok = lambda w: pl.BlockSpec((None, tm, w), lambda bi, i: (bi, i, 0))
    per_b = pl.BlockSpec((None, 1, d), lambda bi, i: (bi, 0, 0))
    in_specs = [tok(d), per_b, per_b, _const_spec((1, d)), _const_spec(w_ext.shape), _const_spec(cs_bd.shape),
                _const_spec(head_gains.shape)]
    args = [x, scale, shift, norm_g.reshape(1, d), w_ext, cs_bd, head_gains]
    if use_rope:
        tab = pl.BlockSpec((tm, LANES), lambda bi, i: (i, 0))
        in_specs += [tab, tab]
        args += list(rope)
    widths = _PROJ_OUT_W + (N_BRANCH * d,)
    return pl.pallas_call(
        functools.partial(_proj_kernel, use_rope=use_rope, tm=tm),
        grid=(b, n // tm),
        in_specs=in_specs,
        out_specs=[tok(w) for w in widths],
        out_shape=[jax.ShapeDtypeStruct((b, n, w), jnp.bfloat16) for w in widths],
        compiler_params=_params(2),
        name="proj",
    )(*args)


def _dft1_kernel(zr_ref, zi_ref, f_ref, tc_ref, ts_ref, o_ref, *, r, tn2):
    z = jnp.concatenate([zr_ref[...], zi_ref[...]], axis=0)
    a = _dot(f_ref[...], z)
    for j in range(tn2):
        ar = a[:r, j * FOURIER_W:(j + 1) * FOURIER_W]
        ai = a[r:, j * FOURIER_W:(j + 1) * FOURIER_W]
        tc = jnp.concatenate([tc_ref[j]] * (FOURIER_W // LANES), axis=-1)
        ts = jnp.concatenate([ts_ref[j]] * (FOURIER_W // LANES), axis=-1)
        o_ref[0, j] = _bf(ar * tc + ai * ts)
        o_ref[1, j] = _bf(ai * tc - ar * ts)


def _left_matmul_kernel(m_ref, x_ref, o_ref):
    o_ref[...] = _bf(_dot(m_ref[...], x_ref[...]))


def _left_matmul(mat, x, tc):
    b, k, c = x.shape
    rows = mat.shape[0]
    return pl.pallas_call(
        _left_matmul_kernel,
        grid=(b, c // tc),
        in_specs=[_const_spec(mat.shape), pl.BlockSpec((None, k, tc), lambda bi, i: (bi, 0, i))],
        out_specs=pl.BlockSpec((None, rows, tc), lambda bi, i: (bi, 0, i)),
        out_shape=jax.ShapeDtypeStruct((b, rows, c), jnp.bfloat16),
        compiler_params=_params(2),
        name="dft2",
    )(mat, x)


def _dft_tables(n):
    inv = 1.0 / math.sqrt(n)
    if n <= 2 * DFT_MINOR:
        k = np.arange(n)
        ang = 2.0 * np.pi * ((k[:, None] * k[None, :]) % n) / n
        return dict(direct=jnp.asarray(np.concatenate([np.cos(ang), np.sin(ang)], axis=1) * inv, jnp.bfloat16))
    r = n // DFT_MINOR
    k1 = np.arange(r)
    a1 = 2.0 * np.pi * ((k1[:, None] * k1[None, :]) % r) / r
    c1, s1 = np.cos(a1), np.sin(a1)
    f1 = np.block([[c1, s1], [-s1, c1]])
    n2 = np.arange(DFT_MINOR)
    at = 2.0 * np.pi * (n2[:, None] * k1[None, :]) / n
    tw_c = np.repeat(np.cos(at)[:, :, None], LANES, axis=2)
    tw_s = np.repeat(np.sin(at)[:, :, None], LANES, axis=2)
    a2 = 2.0 * np.pi * ((n2[:, None] * n2[None, :]) % DFT_MINOR) / DFT_MINOR
    f2 = np.concatenate([np.cos(a2), np.sin(a2)], axis=1) * inv
    return dict(f1=jnp.asarray(f1, jnp.bfloat16), tw_c=jnp.asarray(tw_c, jnp.float32),
                tw_s=jnp.asarray(tw_s, jnp.float32), f2=jnp.asarray(f2, jnp.bfloat16))


def _fourier_positions(zr, zi, tabs):
    b, n, w = zr.shape
    if "direct" in tabs:
        return _left_matmul(tabs["direct"], jnp.concatenate([zr, zi], axis=1), w)
    r = n // DFT_MINOR
    tn2 = 8
    zr2 = zr.reshape(b, r, DFT_MINOR * w)
    zi2 = zi.reshape(b, r, DFT_MINOR * w)
    zin = pl.BlockSpec((None, r, tn2 * w), lambda bi, i: (bi, 0, i))
    tw = pl.BlockSpec((tn2, r, LANES), lambda bi, i: (i, 0, 0))
    g = pl.pallas_call(
        functools.partial(_dft1_kernel, r=r, tn2=tn2),
        grid=(b, DFT_MINOR // tn2),
        in_specs=[zin, zin, _const_spec(tabs["f1"].shape), tw, tw],
        out_specs=pl.BlockSpec((None, 2, tn2, r, w), lambda bi, i: (bi, 0, i, 0, 0)),
        out_shape=jax.ShapeDtypeStruct((b, 2, DFT_MINOR, r, w), jnp.bfloat16),
        compiler_params=_params(2),
        name="dft1",
    )(zr2, zi2, tabs["f1"], tabs["tw_c"], tabs["tw_s"])
    y = _left_matmul(tabs["f2"], g.reshape(b, 2 * DFT_MINOR, r * w), 4 * w)
    return y.reshape(b, n, w)


def _win_kernel(*refs, has_local, nb):
    if has_local:
        bound_ref, sink_ref, q_ref, kp_ref, kc_ref, kn_ref, vp_ref, vc_ref, vn_ref, kx_ref, vx_ref, o_ref = refs
    else:
        bound_ref, sink_ref, q_ref, kx_ref, vx_ref, o_ref = refs
    i = pl.program_id(1)
    tq = q_ref.shape[0]
    group = SWA_HEADS // SWA_KV
    lane = lax.broadcasted_iota(jnp.int32, (1, LANES), 1)
    halves = (lane < HEAD_DIM, lane >= HEAD_DIM)
    row_lo = lax.broadcasted_iota(jnp.int32, (LANES, 1), 0) < HEAD_DIM
    if has_local:
        kr = lax.broadcasted_iota(jnp.int32, (3 * BLOCK, tq), 0)
        qc = lax.broadcasted_iota(jnp.int32, (3 * BLOCK, tq), 1)
        kpos = (i - 1) * BLOCK + kr
        ok = (jnp.abs(kr - BLOCK - qc) <= WINDOW) & (kpos >= 0) & (kpos < nb * BLOCK)
        bias = jnp.where(ok, 0.0, NEG_INF)
        bias = jnp.concatenate([bias] * group, axis=1)
    bound = bound_ref[0, 0]

    def attend(j, use_bound):
        sl = slice(j * LANES, (j + 1) * LANES)
        if has_local:
            kd = jnp.concatenate([kp_ref[:, sl], kc_ref[:, sl], kn_ref[:, sl], kx_ref[:, sl]], axis=0)
            vd = jnp.concatenate([vp_ref[:, sl], vc_ref[:, sl], vn_ref[:, sl], vx_ref[:, sl]], axis=0)
        else:
            kd = kx_ref[:, sl]
            vd = vx_ref[:, sl]
        q_stack = []
        for t in range(group // 2):
            slab = j * (group // 2) + t
            q2 = q_ref[:, slab * LANES:(slab + 1) * LANES]
            q_stack += [jnp.where(halves[e], q2, jnp.zeros_like(q2)) for e in range(2)]
        st = _dot_nt(kd, jnp.concatenate(q_stack, axis=0))
        if has_local:
            st = jnp.concatenate([st[:3 * BLOCK] + bias, st[3 * BLOCK:]], axis=0)
        sk = sink_ref[j:j + 1, :]
        m = jnp.maximum(bound, sk) if use_bound else jnp.maximum(jnp.max(st, axis=0, keepdims=True), sk)
        p = jnp.exp2(st - m)
        inv = 1.0 / (jnp.sum(p, axis=0, keepdims=True) + jnp.exp2(sk - m))
        ot = lax.dot_general(vd, _bf(p), (((0,), (0,)), ((), ())), preferred_element_type=jnp.float32) * inv
        for t in range(group // 2):
            slab = j * (group // 2) + t
            pair = jnp.where(row_lo, ot[:, (2 * t) * tq:(2 * t + 1) * tq], ot[:, (2 * t + 1) * tq:(2 * t + 2) * tq])
            o_ref[:, slab * LANES:(slab + 1) * LANES] = _bf(pair.T)

    shift_safe = bound <= SAFE_SHIFT

    @pl.when(shift_safe)
    def _():
        for j in range(SWA_KV):
            attend(j, True)

    @pl.when(jnp.logical_not(shift_safe))
    def _():
        for j in range(SWA_KV):
            attend(j, False)


def _score_bound(q_gain, k_gain):
    g = jnp.max(jnp.abs(q_gain.astype(jnp.float32))) * jnp.max(jnp.abs(k_gain.astype(jnp.float32)))
    return (1.02 * HEAD_DIM ** 0.5 * LOG2E * g).reshape(1, 1)


def _window_attention(qb, kb, vb, kx, vx, sink, bound, has_local):
    b, n, _ = qb.shape
    nb = n // BLOCK
    lx = kx.shape[1]
    qspec = pl.BlockSpec((None, BLOCK, SWA_W), lambda bi, i: (bi, i, 0))
    ctx = pl.BlockSpec((None, lx, 2 * LANES), lambda bi, i: (bi, 0, 0))
    sink_rows = jnp.repeat(sink.astype(jnp.float32) * LOG2E, BLOCK).reshape(SWA_KV, -1)
    args = [bound, sink_rows, qb]
    specs = [pl.BlockSpec(memory_space=pltpu.SMEM), _const_spec(sink_rows.shape), qspec]
    if has_local:
        prv = pl.BlockSpec((None, BLOCK, 2 * LANES), lambda bi, i: (bi, jnp.maximum(i - 1, 0), 0))
        cur = pl.BlockSpec((None, BLOCK, 2 * LANES), lambda bi, i: (bi, i, 0))
        nxt = pl.BlockSpec((None, BLOCK, 2 * LANES), lambda bi, i: (bi, jnp.minimum(i + 1, nb - 1), 0))
        args += [kb, kb, kb, vb, vb, vb]
        specs += [prv, cur, nxt, prv, cur, nxt]
    args += [kx, vx]
    specs += [ctx, ctx]
    return pl.pallas_call(
        functools.partial(_win_kernel, has_local=has_local, nb=nb),
        grid=(b, nb),
        in_specs=specs,
        out_specs=qspec,
        out_shape=jax.ShapeDtypeStruct((b, n, SWA_W), jnp.bfloat16),
        compiler_params=_params(2),
        name="win",
    )(*args)


def _diff_kernel(lam_ref, bound_ref, q_ref, *refs, n_parts, tk, out_scale):
    k_parts, v_parts = refs[:n_parts], refs[n_parts:2 * n_parts]
    g_ref, o_ref, k_ref, vt_ref, acc_ref, s0_ref, s1_ref, p0_ref, p1_ref = refs[2 * n_parts:]
    qi = pl.program_id(2)
    tq = q_ref.shape[0]
    n_chunks = vt_ref.shape[0]

    @pl.when(qi == 0)
    def _():
        row = 0
        for kp, vp in zip(k_parts, v_parts):
            rows = kp.shape[0]
            k_ref[row:row + rows, :] = kp[...]
            done = 0
            while done < rows:
                ci, off = divmod(row + done, tk)
                take = min(tk - off, rows - done)
                vt_ref[ci, :LANES, off:off + take] = _bf(vp[done:done + take, :].astype(jnp.float32).T)
                done += take
            row += rows

        vt_ref[:, LANES:, :] = jnp.ones((n_chunks, SUM_ROWS, tk), jnp.bfloat16)

    q = q_ref[...]
    lane = lax.broadcasted_iota(jnp.int32, (1, LANES), 1)
    qm = (jnp.where(lane < HEAD_DIM, q, jnp.zeros_like(q)), jnp.where(lane >= HEAD_DIM, q, jnp.zeros_like(q)))
    acc_ref[...] = jnp.zeros_like(acc_ref)
    s_bufs = (s0_ref, s1_ref)
    p_bufs = (p0_ref, p1_ref)

    bound = bound_ref[0, 0]
    shift_safe = bound <= SAFE_SHIFT

    @pl.when(shift_safe)
    def _():
        q_both = jnp.concatenate(qm, axis=0)

        def chunk(ci, denom):
            c0 = pl.multiple_of(ci * tk, tk)
            p = jnp.exp2(_dot_nt(k_ref[pl.ds(c0, tk), :], q_both) - bound)
            acc_ref[:LANES, :] += _dot(vt_ref[ci, :LANES, :], _bf(p))
            return denom + jnp.sum(p, axis=0, keepdims=True)
        denom = lax.fori_loop(0, n_chunks, chunk, jnp.zeros((1, 2 * tq), jnp.float32), unroll=True)
        acc_ref[LANES:, :] = jnp.broadcast_to(denom, (SUM_ROWS, 2 * tq))

    @pl.when(jnp.logical_not(shift_safe))
    def _():
        _diff_online(qm, k_ref, vt_ref, acc_ref, s_bufs, p_bufs, tk=tk, tq=tq, n_chunks=n_chunks)

    lam = lam_ref[0, 0]
    a1 = acc_ref[:, :tq]
    a2 = acc_ref[:, tq:]
    o = a1[:LANES] * (1.0 / a1[LANES:LANES + 1]) - lam * (a2[:LANES] * (1.0 / a2[LANES:LANES + 1]))
    y = o * lax.rsqrt(jnp.mean(o * o, axis=0, keepdims=True) + NORM_EPS) * g_ref[...] * out_scale
    o_ref[...] = _bf(y.T)


def _diff_online(qm, k_ref, vt_ref, acc_ref, s_bufs, p_bufs, *, tk, tq, n_chunks):
    def scores(ci, slot):
        c0 = ci * tk if isinstance(ci, int) else pl.multiple_of(ci * tk, tk)
        k = k_ref[pl.ds(c0, tk), :]
        mx = []
        for c in range(2):
            st = _dot_nt(k, qm[c])
            s_bufs[slot][c] = st
            mx.append(jnp.max(st, axis=0, keepdims=True))
        return tuple(mx)

    def probs(slot, mx, m_run):
        m_new, alpha = [], []
        for c in range(2):
            m = jnp.maximum(m_run[c], mx[c])
            alpha.append(jnp.exp2(m_run[c] - m))
            p_bufs[slot][c] = _bf(jnp.exp2(s_bufs[slot][c] - m))
            m_new.append(m)
        return tuple(m_new), tuple(alpha)

    def accumulate(ci, slot, alpha):
        vt = vt_ref[ci]
        for c in range(2):
            cols = slice(c * tq, (c + 1) * tq)
            acc_ref[:, cols] = alpha[c] * acc_ref[:, cols] + _dot(vt, p_bufs[slot][c])

    def step(t, par, do_scores, do_probs, do_acc, state):
        m_run, mx, alpha = state
        mx_next = scores(t + 2, par) if do_scores else mx
        if do_probs:
            m_run, alpha_next = probs(1 - par, mx, m_run)
        else:
            alpha_next = alpha
        if do_acc:
            accumulate(t, par, alpha)
        return m_run, mx_next, alpha_next

    neg = jnp.full((1, tq), NEG_INF, jnp.float32)
    one = jnp.ones((1, tq), jnp.float32)
    state = ((neg, neg), (neg, neg), (one, one))
    for t in (-2, -1):
        state = step(t, t % 2, t + 2 < n_chunks, 0 <= t + 1 < n_chunks, False, state)
    n_steady = max(n_chunks - 2, 0)

    def pair(j, state):
        state = step(2 * j, 0, True, True, True, state)
        return step(2 * j + 1, 1, True, True, True, state)

    state = lax.fori_loop(0, n_steady // 2, pair, state)
    if n_steady % 2:
        state = step(n_steady - 1, 0, True, True, True, state)
    for t in range(n_steady, n_chunks):
        state = step(t, t % 2, False, t + 1 < n_chunks, True, state)


def _pick_tk(nk, cap):
    best = LANES
    for t in range(LANES, min(cap, nk) + 1, LANES):
        if nk % t == 0:
            best = t
    return best


def _diff_attention(qd, k_parts, v_parts, lam, bound, subln_g, lam_init, tq, tk_cap):
    b, n, _ = qd.shape
    nk = sum(k.shape[1] for k in k_parts)
    tk = _pick_tk(nk, tk_cap)
    smem = pl.BlockSpec(memory_space=pltpu.SMEM)
    qspec = pl.BlockSpec((None, tq, LANES), lambda bi, h, i: (bi, i, h))
    kvspecs = [pl.BlockSpec((None, k.shape[1], LANES), lambda bi, h, i: (bi, 0, h)) for k in k_parts]
    gain = jnp.broadcast_to(subln_g.astype(jnp.float32)[:, None], (LANES, tq))
    return pl.pallas_call(
        functools.partial(_diff_kernel, n_parts=len(k_parts), tk=tk, out_scale=1.0 - lam_init),
        grid=(b, DIFF_HEADS, n // tq),
        in_specs=[smem, smem, qspec] + kvspecs + kvspecs + [_const_spec((LANES, tq))],
        out_specs=qspec,
        out_shape=jax.ShapeDtypeStruct((b, n, DIFF_W), jnp.bfloat16),
        scratch_shapes=[pltpu.VMEM((nk, LANES), jnp.bfloat16),
                        pltpu.VMEM((nk // tk, LANES + SUM_ROWS, tk), jnp.bfloat16),
                        pltpu.VMEM((LANES + SUM_ROWS, 2 * tq), jnp.float32),
                        pltpu.VMEM((2, tk, tq), jnp.float32), pltpu.VMEM((2, tk, tq), jnp.float32),
                        pltpu.VMEM((2, tk, tq), jnp.bfloat16), pltpu.VMEM((2, tk, tq), jnp.bfloat16)],
        compiler_params=_params(3),
        name="diff",
    )(lam, bound, qd, *k_parts, *v_parts, gain)


def _merge_kernel(x_ref, gt_ref, ya_ref, sza_ref, yb_ref, szb_ref, uc_ref, up_ref, un_ref, cb_ref, szc_ref,
                  yd_ref, szd_ref, g_ref, cw_ref, wa_ref, wb_ref, wc_ref, wd_ref, wo_ref, o_ref, *, tm, nt):
    i = pl.program_id(1)
    f32 = jnp.float32
    u = uc_ref[...].astype(f32)
    row = lax.broadcasted_iota(jnp.int32, (tm, 1), 0)
    prev_row = jnp.where(i > 0, up_ref[7:8, :].astype(f32), 0.0)
    next_row = jnp.where(i < nt - 1, un_ref[0:1, :].astype(f32), 0.0)
    u_prev = jnp.where(row == 0, prev_row, pltpu.roll(u, 1, 0))
    u_next = jnp.where(row == tm - 1, next_row, pltpu.roll(u, tm - 1, 0))
    cw = cw_ref[...]
    conv = u_prev * cw[0:1, :] + u * cw[1:2, :] + u_next * cw[2:3, :]
    yc = cb_ref[...].astype(f32) * conv
    d = x_ref.shape[-1]
    branches = ((ya_ref[...].astype(f32), sza_ref, wa_ref), (yb_ref[...].astype(f32), szb_ref, wb_ref),
                (yc, szc_ref, wc_ref), (yd_ref[...].astype(f32), szd_ref, wd_ref))
    mixed = jnp.zeros((tm, d), f32)
    for j, (y, sz_ref, w_ref) in enumerate(branches):
        t = _dot(_bf(y * sz_ref[...].astype(f32)), w_ref[...])
        mixed = mixed + g_ref[:, j * d:(j + 1) * d].astype(f32) * t
    out = _dot(_bf(mixed), wo_ref[...])
    o_ref[...] = x_ref[...] + gt_ref[...] * out


def _merge(x, gate, ya, sza, yb, szb, uc, cb, szc, yd, szd, g, conv_w, w_a, w_b, w_c, w_d, w_out, tm):
    b, n, d = x.shape
    nt = n // tm
    tok = lambda w: pl.BlockSpec((None, tm, w), lambda bi, i: (bi, i, 0))
    hb = tm // 8
    halo_p = pl.BlockSpec((None, 8, CONV_W), lambda bi, i: (bi, jnp.maximum(i * hb - 1, 0), 0))
    halo_n = pl.BlockSpec((None, 8, CONV_W), lambda bi, i: (bi, jnp.minimum((i + 1) * hb, n // 8 - 1), 0))
    per_b = pl.BlockSpec((None, 1, d), lambda bi, i: (bi, 0, 0))
    ws = [_bf(w_a), _bf(w_b), _bf(w_c), _bf(w_d), _bf(w_out)]
    return pl.pallas_call(
        functools.partial(_merge_kernel, tm=tm, nt=nt),
        grid=(b, nt),
        in_specs=[tok(d), per_b, tok(FOURIER_W), tok(FOURIER_W), tok(SWA_W), tok(SWA_W), tok(CONV_W), halo_p, halo_n,
                  tok(CONV_W), tok(CONV_W), tok(DIFF_W), tok(DIFF_W), tok(N_BRANCH * d), _const_spec(conv_w.shape)]
                 + [_const_spec(w.shape) for w in ws],
        out_specs=tok(d),
        out_shape=jax.ShapeDtypeStruct((b, n, d), jnp.float32),
        compiler_params=_params(2),
        name="merge",
    )(x, gate, ya, sza, yb, szb, uc, uc, uc, cb, szc, yd, szd, g, conv_w.astype(jnp.float32), *ws)


def _rope_tables(n):
    rows = n // GRID_W
    row = jnp.broadcast_to(jnp.arange(rows, dtype=jnp.float32)[:, None], (rows, GRID_W)).reshape(-1)
    col = jnp.broadcast_to(jnp.arange(GRID_W, dtype=jnp.float32)[None, :], (rows, GRID_W)).reshape(-1)
    nf = HEAD_DIM // 4
    inv = ROPE_BASE ** (-jnp.arange(nf, dtype=jnp.float32) / nf)
    ar = row[:, None] * inv[None, :]
    ac = col[:, None] * inv[None, :]
    cos = jnp.concatenate([jnp.cos(ar), jnp.cos(ar), jnp.cos(ac), jnp.cos(ac)], axis=-1)
    sin = jnp.concatenate([jnp.sin(ar), jnp.sin(ar), jnp.sin(ac), jnp.sin(ac)], axis=-1)
    sign = jnp.where((jnp.arange(HEAD_DIM) % 32) < 16, -1.0, 1.0)
    return jnp.tile(cos, (1, 2)), jnp.tile(sin * sign, (1, 2))


def _channel_dft():
    k = np.arange(HEAD_DIM)
    ang = 2.0 * np.pi * ((k[:, None] * k[None, :]) % HEAD_DIM) / HEAD_DIM
    eye = np.eye(FOURIER_W // HEAD_DIM)
    inv = 1.0 / math.sqrt(HEAD_DIM)
    return jnp.asarray(np.concatenate([np.kron(eye, np.cos(ang)), -np.kron(eye, np.sin(ang))], axis=1) * inv,
                       jnp.bfloat16)


def _extend_w_in(w):
    s = _SRC
    k0, k1 = s["b_k"][0], s["b_k"][0] + HEAD_DIM
    v0, v1 = s["b_v"][0], s["b_v"][0] + HEAD_DIM
    pieces = [w[:, s["a_x"][0]:s["b_q"][1]],
              w[:, k0:k1], w[:, k0:k1], w[:, k1:k1 + HEAD_DIM], w[:, k1:k1 + HEAD_DIM],
              w[:, v0:v1], w[:, v0:v1], w[:, v1:v1 + HEAD_DIM], w[:, v1:v1 + HEAD_DIM],
              w[:, s["c_x"][0]:]]
    return _bf(jnp.concatenate(pieces, axis=1))


def _mix(stream, scale, shift, gate, ctx_kv, lp, consts, lam, lam_init, rope, dft, tm, tq, tk, has_local):
    parts = _project(stream, scale, shift, lp["norm_g"], lp["w_ext"], consts["cs_bd"], lp["head_gains"], rope, tm)
    zr, zi, qb, kb, vb, uc, cb, qd, kd, vd, sza, szb, szc, szd, g = parts
    if ctx_kv is None:
        kx_b, vx_b, k_parts, v_parts = kb, vb, (kd,), (vd,)
    else:
        kx_b, vx_b, kx_d, vx_d = ctx_kv
        k_parts, v_parts = (kx_d, kd), (vx_d, vd)
    ya = _fourier_positions(zr, zi, dft)
    yb = _window_attention(qb, kb, vb, kx_b, vx_b, lp["sink"], lp["bound_b"], has_local)
    yd = _diff_attention(qd, k_parts, v_parts, lam, lp["bound_d"], lp["subln"], lam_init, tq, tk)
    new = _merge(stream, gate, ya, sza, yb, szb, uc, cb, szc, yd, szd, g, lp["conv_w"],
                 lp["w_o_a"], lp["w_o_b"], lp["w_o_c"], lp["w_o_d"], lp["w_out"], tm)
    return new, (kb, vb, kd, vd)


def kernel(x, c, ctx, c_ctx, norm_g, w_mod, b_mod, w_in, q_norm_b, k_norm_b, sink_b, conv_w, q_norm_d, k_norm_d,
           lam_q1, lam_k1, lam_q2, lam_k2, subln_d, w_o_a, w_o_b, w_o_c, w_o_d, w_out):
    b, n, d = x.shape
    lc = ctx.shape[1]
    depth = w_in.shape[0]
    rope = _rope_tables(n)
    consts = dict(cs_bd=_channel_dft())
    dft_x = _dft_tables(n)
    dft_c = _dft_tables(lc)
    c_rows = jnp.zeros((8, d), jnp.float32).at[:b].set(c).at[b].set(c_ctx)
    tm = min(512, n)
    tq = min(512, n)
    tk = 1280
    for l in range(depth):
        last = l == depth - 1
        lam_init = 0.8 - 0.6 * math.exp(-0.3 * l)
        lam_vecs = jnp.stack([lam_q1[l], lam_k1[l], lam_q2[l], lam_k2[l]]).astype(jnp.float32)
        mod, lam_o = _modulation(c_rows, w_mod[l], b_mod[l], lam_vecs, lam_init)
        lam = lam_o[0:1, 0:1]
        shift, scale, gate = (mod[:b, None, j * d:(j + 1) * d] for j in range(3))
        shift_c, scale_c, gate_c = (jnp.broadcast_to(mod[b:b + 1, None, j * d:(j + 1) * d], (b, 1, d))
                                    for j in range(3))
        tile2 = lambda v: jnp.tile(v.astype(jnp.float32), 2)
        lp = dict(norm_g=norm_g[l], w_ext=_extend_w_in(w_in[l]), sink=sink_b[l].astype(jnp.float32),
                  head_gains=jnp.stack([tile2(q_norm_b[l]), tile2(k_norm_b[l]), tile2(q_norm_d[l]), tile2(k_norm_d[l])]),
                  bound_b=_score_bound(q_norm_b[l], k_norm_b[l]), bound_d=_score_bound(q_norm_d[l], k_norm_d[l]),
                  subln=subln_d[l], conv_w=conv_w[l], w_o_a=w_o_a[l], w_o_b=w_o_b[l], w_o_c=w_o_c[l],
                  w_o_d=w_o_d[l], w_out=w_out[l])
        if last:
            parts = _project(ctx, scale_c, shift_c, lp["norm_g"], lp["w_ext"], consts["cs_bd"], lp["head_gains"],
                             None, min(256, lc))
            ctx_kv = (parts[3], parts[4], parts[8], parts[9])
        else:
            ctx, ctx_kv = _mix(ctx, scale_c, shift_c, gate_c, None, lp, consts, lam, lam_init, None, dft_c,
                               min(256, lc), min(256, lc), tk, False)
        x, _ = _mix(x, scale, shift, gate, ctx_kv, lp, consts, lam, lam_init, rope, dft_x, tm, tq, tk, True)
    return x
```

```python
import functools
import math

import jax
import jax.numpy as jnp
import numpy as np
from jax import lax
from jax.experimental import pallas as pl
from jax.experimental.pallas import tpu as pltpu

HEAD_DIM = 64
LANES = 128
FOURIER_W = 384
SWA_HEADS = 8
SWA_KV = 2
SWA_W = SWA_HEADS * HEAD_DIM
WINDOW = 128
BLOCK = 128
CONV_W = 384
CONV_K = 3
DIFF_HEADS = 4
DIFF_W = DIFF_HEADS * 2 * HEAD_DIM
N_BRANCH = 4
ROPE_BASE = 10000.0
NORM_EPS = 1e-6
NEG_INF = -1e30
GRID_W = 64
DFT_MINOR = 128
SUM_ROWS = 16
LOG2E = math.log2(math.e)
SAFE_SHIFT = 60.0
VMEM_LIMIT = 56 * 1024 * 1024

_SRC = dict(a_x=(0, 384), b_q=(384, 896), b_k=(896, 1024), b_v=(1024, 1152), c_x=(1152, 1536),
            c_b=(1536, 1920), c_c=(1920, 2304), d_q=(2304, 2816), d_k=(2816, 3328), d_v=(3328, 3840),
            z=(3840, 5632), m_g=(5632, 9728))


def _bf(x):
    return x.astype(jnp.bfloat16)


def _dot(a, b):
    return jnp.dot(a, b, preferred_element_type=jnp.float32)


def _dot_nt(a, b):
    return lax.dot_general(a, b, (((1,), (1,)), ((), ())), preferred_element_type=jnp.float32)


def _params(n_axes, flags=None):
    return pltpu.CompilerParams(dimension_semantics=("arbitrary",) * n_axes, vmem_limit_bytes=VMEM_LIMIT,
                                flags=flags)


def _const_spec(shape):
    nd = len(shape)
    return pl.BlockSpec(shape, lambda *_: (0,) * nd, pipeline_mode=pl.Buffered(1))


def _mod_kernel(c_ref, w_ref, b_ref, lam_ref, mod_ref, lamo_ref, *, lam_init):
    c = c_ref[...]
    s = c * jax.nn.sigmoid(c)
    mod_ref[...] = _dot(_bf(s), w_ref[...]) + b_ref[...]
    lv = lam_ref[...]
    a1 = jnp.sum(lv[0:1, :] * lv[1:2, :], axis=-1, keepdims=True)
    a2 = jnp.sum(lv[2:3, :] * lv[3:4, :], axis=-1, keepdims=True)
    lam = jnp.exp(a1) - jnp.exp(a2) + lam_init
    lamo_ref[...] = jnp.broadcast_to(lam, lamo_ref.shape)


def _modulation(c_rows, w_mod, b_mod, lam_vecs, lam_init):
    r, d = c_rows.shape
    return pl.pallas_call(
        functools.partial(_mod_kernel, lam_init=lam_init),
        out_shape=(jax.ShapeDtypeStruct((r, 3 * d), jnp.float32),
                   jax.ShapeDtypeStruct((8, LANES), jnp.float32)),
        name="mod",
    )(c_rows, _bf(w_mod), b_mod.reshape(1, 3 * d), lam_vecs)


def _head_norm_rope(t, gain, cos, sin_s, lane_lo, scale):
    sq = t * t
    first = lax.broadcasted_iota(jnp.int32, t.shape, 1) < HEAD_DIM
    ms = jnp.where(first, jnp.sum(jnp.where(first, sq, 0.0), axis=-1, keepdims=True),
                   jnp.sum(jnp.where(first, 0.0, sq), axis=-1, keepdims=True)) * (1.0 / HEAD_DIM)
    y = t * lax.rsqrt(ms + NORM_EPS) * gain
    if cos is not None:
        rot = jnp.where(lane_lo, pltpu.roll(y, LANES - 16, 1), pltpu.roll(y, 16, 1))
        y = y * cos + rot * sin_s
    if scale != 1.0:
        y = y * scale
    return y


def _proj_kernel(*refs, use_rope, tm):
    if use_rope:
        (x_ref, sc_ref, sh_ref, g_ref, w_ref, cs_ref, hg_ref, cos_ref, sin_ref), outs = refs[:9], refs[9:]
    else:
        (x_ref, sc_ref, sh_ref, g_ref, w_ref, cs_ref, hg_ref), outs = refs[:7], refs[7:]
        cos_ref = sin_ref = None
    (zr_ref, zi_ref, qb_ref, kb_ref, vb_ref, uc_ref, cb_ref, qd_ref, kd_ref, vd_ref,
     sza_ref, szb_ref, szc_ref, szd_ref, gate_ref) = outs

    x = x_ref[...]
    y = x * lax.rsqrt(jnp.mean(x * x, axis=-1, keepdims=True) + NORM_EPS) * g_ref[...]
    h = _bf(y * (1.0 + sc_ref[...]) + sh_ref[...])

    def proj(c0, width):
        return _dot(h, w_ref[:, c0:c0 + width])

    col = 0
    a = _bf(proj(col, FOURIER_W))
    zz = _dot(a, cs_ref[...])
    zr_ref[...] = _bf(zz[:, :FOURIER_W])
    zi_ref[...] = _bf(zz[:, FOURIER_W:])
    col += FOURIER_W

    if use_rope:
        cos = cos_ref[...]
        sin_s = sin_ref[...]
    else:
        cos = sin_s = None
    lane = lax.broadcasted_iota(jnp.int32, (tm, LANES), 1)
    lane_lo = (lane & 31) < 16
    qscale = HEAD_DIM ** -0.5

    def normed(c0, width, gain_row, out_ref, scale):
        t = proj(c0, width)
        gain = hg_ref[gain_row:gain_row + 1, :]
        for s in range(width // LANES):
            ts = t[:, s * LANES:(s + 1) * LANES]
            out_ref[:, s * LANES:(s + 1) * LANES] = _bf(
                _head_norm_rope(ts, gain, cos, sin_s, lane_lo, scale))

    normed(col, SWA_W, 0, qb_ref, qscale * LOG2E); col += SWA_W
    normed(col, 2 * LANES, 1, kb_ref, 1.0); col += 2 * LANES
    vb_ref[...] = _bf(proj(col, 2 * LANES)); col += 2 * LANES
    c3 = proj(col, 3 * CONV_W); col += 3 * CONV_W
    uc_ref[...] = _bf(c3[:, 2 * CONV_W:] * c3[:, :CONV_W])
    cb_ref[...] = _bf(c3[:, CONV_W:2 * CONV_W])
    normed(col, DIFF_W, 2, qd_ref, qscale * LOG2E); col += DIFF_W
    normed(col, DIFF_W, 3, kd_ref, 1.0); col += DIFF_W
    vd_ref[...] = _bf(proj(col, DIFF_W)); col += DIFF_W
    for ref, width in ((sza_ref, FOURIER_W), (szb_ref, SWA_W), (szc_ref, CONV_W), (szd_ref, DIFF_W)):
        z = proj(col, width)
        ref[...] = _bf(z * jax.nn.sigmoid(z))
        col += width
    d = x.shape[-1]
    for j in range(N_BRANCH):
        gate_ref[:, j * d:(j + 1) * d] = _bf(jax.nn.sigmoid(proj(col, d)))
        col += d


_PROJ_OUT_W = (FOURIER_W, FOURIER_W, SWA_W, 2 * LANES, 2 * LANES, CONV_W, CONV_W, DIFF_W, DIFF_W, DIFF_W,
               FOURIER_W, SWA_W, CONV_W, DIFF_W)


def _project(x, scale, shift, norm_g, w_ext, cs_bd, head_gains, rope, tm):
    b, n, d = x.shape
    use_rope = rope is not None
    tok = lambda w: pl.BlockSpec((None, tm, w), lambda bi, i: (bi, i, 0))
    per_b = pl.BlockSpec((None, 1, d), lambda bi, i: (bi, 0, 0))
    in_specs = [tok(d), per_b, per_b, _const_spec((1, d)), _const_spec(w_ext.shape), _const_spec(cs_bd.shape),
                _const_spec(head_gains.shape)]
    args = [x, scale, shift, norm_g.reshape(1, d), w_ext, cs_bd, head_gains]
    if use_rope:
        tab = pl.BlockSpec((tm, LANES), lambda bi, i: (i, 0))
        in_specs += [tab, tab]
        args += list(rope)
    widths = _PROJ_OUT_W + (N_BRANCH * d,)
    return pl.pallas_call(
        functools.partial(_proj_kernel, use_rope=use_rope, tm=tm),
        grid=(b, n // tm),
        in_specs=in_specs,
        out_specs=[tok(w) for w in widths],
        out_shape=[jax.ShapeDtypeStruct((b, n, w), jnp.bfloat16) for w in widths],
        compiler_params=_params(2),
        name="proj",
    )(*args)


def _dft1_kernel(zr_ref, zi_ref, f_ref, tc_ref, ts_ref, o_ref, *, r, tn2):
    z = jnp.concatenate([zr_ref[...], zi_ref[...]], axis=0)
    a = _dot(f_ref[...], z)
    for j in range(tn2):
        ar = a[:r, j * FOURIER_W:(j + 1) * FOURIER_W]
        ai = a[r:, j * FOURIER_W:(j + 1) * FOURIER_W]
        tc = jnp.concatenate([tc_ref[j]] * (FOURIER_W // LANES), axis=-1)
        ts = jnp.concatenate([ts_ref[j]] * (FOURIER_W // LANES), axis=-1)
        o_ref[0, j] = _bf(ar * tc + ai * ts)
        o_ref[1, j] = _bf(ai * tc - ar * ts)


def _left_matmul_kernel(m_ref, x_ref, o_ref):
    o_ref[...] = _bf(_dot(m_ref[...], x_ref[...]))


def _left_matmul(mat, x, tc):
    b, k, c = x.shape
    rows = mat.shape[0]
    return pl.pallas_call(
        _left_matmul_kernel,
        grid=(b, c // tc),
        in_specs=[_const_spec(mat.shape), pl.BlockSpec((None, k, tc), lambda bi, i: (bi, 0, i))],
        out_specs=pl.BlockSpec((None, rows, tc), lambda bi, i: (bi, 0, i)),
        out_shape=jax.ShapeDtypeStruct((b, rows, c), jnp.bfloat16),
        compiler_params=_params(2),
        name="dft2",
    )(mat, x)


def _dft_tables(n):
    inv = 1.0 / math.sqrt(n)
    if n <= 2 * DFT_MINOR:
        k = np.arange(n)
        ang = 2.0 * np.pi * ((k[:, None] * k[None, :]) % n) / n
        return dict(direct=jnp.asarray(np.concatenate([np.cos(ang), np.sin(ang)], axis=1) * inv, jnp.bfloat16))
    r = n // DFT_MINOR
    k1 = np.arange(r)
    a1 = 2.0 * np.pi * ((k1[:, None] * k1[None, :]) % r) / r
    c1, s1 = np.cos(a1), np.sin(a1)
    f1 = np.block([[c1, s1], [-s1, c1]])
    n2 = np.arange(DFT_MINOR)
    at = 2.0 * np.pi * (n2[:, None] * k1[None, :]) / n
    tw_c = np.repeat(np.cos(at)[:, :, None], LANES, axis=2)
    tw_s = np.repeat(np.sin(at)[:, :, None], LANES, axis=2)
    a2 = 2.0 * np.pi * ((n2[:, None] * n2[None, :]) % DFT_MINOR) / DFT_MINOR
    f2 = np.concatenate([np.cos(a2), np.sin(a2)], axis=1) * inv
    return dict(f1=jnp.asarray(f1, jnp.bfloat16), tw_c=jnp.asarray(tw_c, jnp.float32),
                tw_s=jnp.asarray(tw_s, jnp.float32), f2=jnp.asarray(f2, jnp.bfloat16))


def _fourier_positions(zr, zi, tabs):
    b, n, w = zr.shape
    if "direct" in tabs:
        return _left_matmul(tabs["direct"], jnp.concatenate([zr, zi], axis=1), w)
    r = n // DFT_MINOR
    tn2 = 8
    zr2 = zr.reshape(b, r, DFT_MINOR * w)
    zi2 = zi.reshape(b, r, DFT_MINOR * w)
    zin = pl.BlockSpec((None, r, tn2 * w), lambda bi, i: (bi, 0, i))
    tw = pl.BlockSpec((tn2, r, LANES), lambda bi, i: (i, 0, 0))
    g = pl.pallas_call(
        functools.partial(_dft1_kernel, r=r, tn2=tn2),
        grid=(b, DFT_MINOR // tn2),
        in_specs=[zin, zin, _const_spec(tabs["f1"].shape), tw, tw],
        out_specs=pl.BlockSpec((None, 2, tn2, r, w), lambda bi, i: (bi, 0, i, 0, 0)),
        out_shape=jax.ShapeDtypeStruct((b, 2, DFT_MINOR, r, w), jnp.bfloat16),
        compiler_params=_params(2),
        name="dft1",
    )(zr2, zi2, tabs["f1"], tabs["tw_c"], tabs["tw_s"])
    y = _left_matmul(tabs["f2"], g.reshape(b, 2 * DFT_MINOR, r * w), 4 * w)
    return y.reshape(b, n, w)


def _win_kernel(*refs, has_local, nb, nq):
    if has_local:
        bound_ref, sink_ref, q_ref, kp_ref, kc_ref, kn_ref, vp_ref, vc_ref, vn_ref, kx_ref, vx_ref, o_ref = refs
    else:
        bound_ref, sink_ref, q_ref, kx_ref, vx_ref, o_ref = refs
    i = pl.program_id(1)
    tq = BLOCK
    group = SWA_HEADS // SWA_KV
    lane = lax.broadcasted_iota(jnp.int32, (1, LANES), 1)
    halves = (lane < HEAD_DIM, lane >= HEAD_DIM)
    row_lo = lax.broadcasted_iota(jnp.int32, (LANES, 1), 0) < HEAD_DIM
    if has_local:
        kr = lax.broadcasted_iota(jnp.int32, (3 * BLOCK, tq), 0)
        qc = lax.broadcasted_iota(jnp.int32, (3 * BLOCK, tq), 1)
        in_band = jnp.abs(kr - BLOCK - qc) <= WINDOW
    bound = bound_ref[0, 0]

    def attend(s, j, use_bound):
        sl = slice(j * LANES, (j + 1) * LANES)
        rows = slice(s * BLOCK, (s + 1) * BLOCK)
        if has_local:
            def window(prev_ref, cur_ref, next_ref):
                first = prev_ref[:, sl] if s == 0 else cur_ref[(s - 1) * BLOCK:s * BLOCK, sl]
                last = next_ref[:, sl] if s == nq - 1 else cur_ref[(s + 1) * BLOCK:(s + 2) * BLOCK, sl]
                return [first, cur_ref[rows, sl], last]
            kd = jnp.concatenate(window(kp_ref, kc_ref, kn_ref) + [kx_ref[:, sl]], axis=0)
            vd = jnp.concatenate(window(vp_ref, vc_ref, vn_ref) + [vx_ref[:, sl]], axis=0)
        else:
            kd = kx_ref[:, sl]
            vd = vx_ref[:, sl]
        q_stack = []
        for t in range(group // 2):
            slab = j * (group // 2) + t
            q2 = q_ref[rows, slab * LANES:(slab + 1) * LANES]
            q_stack += [jnp.where(halves[e], q2, jnp.zeros_like(q2)) for e in range(2)]
        st = _dot_nt(kd, jnp.concatenate(q_stack, axis=0))
        if has_local:
            kpos = (i * nq + s - 1) * BLOCK + kr
            bias = jnp.where(in_band & (kpos >= 0) & (kpos < nb * BLOCK), 0.0, NEG_INF)
            st = jnp.concatenate([st[:3 * BLOCK] + jnp.concatenate([bias] * group, axis=1), st[3 * BLOCK:]], axis=0)
        sk = sink_ref[j:j + 1, :]
        m = jnp.maximum(bound, sk) if use_bound else jnp.maximum(jnp.max(st, axis=0, keepdims=True), sk)
        p = jnp.exp2(st - m)
        inv = 1.0 / (jnp.sum(p, axis=0, keepdims=True) + jnp.exp2(sk - m))
        ot = lax.dot_general(vd, _bf(p), (((0,), (0,)), ((), ())), preferred_element_type=jnp.float32) * inv
        for t in range(group // 2):
            slab = j * (group // 2) + t
            pair = jnp.where(row_lo, ot[:, (2 * t) * tq:(2 * t + 1) * tq], ot[:, (2 * t + 1) * tq:(2 * t + 2) * tq])
            o_ref[rows, slab * LANES:(slab + 1) * LANES] = _bf(pair.T)

    shift_safe = bound <= SAFE_SHIFT

    @pl.when(shift_safe)
    def _():
        for s in range(nq):
            for j in range(SWA_KV):
                attend(s, j, True)

    @pl.when(jnp.logical_not(shift_safe))
    def _():
        for s in range(nq):
            for j in range(SWA_KV):
                attend(s, j, False)


def _score_bound(q_gain, k_gain):
    g = jnp.max(jnp.abs(q_gain.astype(jnp.float32))) * jnp.max(jnp.abs(k_gain.astype(jnp.float32)))
    return (1.02 * HEAD_DIM ** 0.5 * LOG2E * g).reshape(1, 1)


def _window_attention(qb, kb, vb, kx, vx, sink, bound, has_local):
    b, n, _ = qb.shape
    nb = n // BLOCK
    nq = 4 if nb % 4 == 0 else 1
    lx = kx.shape[1]
    qspec = pl.BlockSpec((None, nq * BLOCK, SWA_W), lambda bi, i: (bi, i, 0))
    ctx = pl.BlockSpec((None, lx, 2 * LANES), lambda bi, i: (bi, 0, 0))
    sink_rows = jnp.repeat(sink.astype(jnp.float32) * LOG2E, BLOCK).reshape(SWA_KV, -1)
    args = [bound, sink_rows, qb]
    specs = [pl.BlockSpec(memory_space=pltpu.SMEM), _const_spec(sink_rows.shape), qspec]
    if has_local:
        prv = pl.BlockSpec((None, BLOCK, 2 * LANES), lambda bi, i: (bi, jnp.maximum(i * nq - 1, 0), 0))
        cur = pl.BlockSpec((None, nq * BLOCK, 2 * LANES), lambda bi, i: (bi, i, 0))
        nxt = pl.BlockSpec((None, BLOCK, 2 * LANES), lambda bi, i: (bi, jnp.minimum((i + 1) * nq, nb - 1), 0))
        args += [kb, kb, kb, vb, vb, vb]
        specs += [prv, cur, nxt, prv, cur, nxt]
    args += [kx, vx]
    specs += [ctx, ctx]
    return pl.pallas_call(
        functools.partial(_win_kernel, has_local=has_local, nb=nb, nq=nq),
        grid=(b, nb // nq),
        in_specs=specs,
        out_specs=qspec,
        out_shape=jax.ShapeDtypeStruct((b, n, SWA_W), jnp.bfloat16),
        compiler_params=_params(2),
        name="win",
    )(*args)


def _diff_kernel(lam_ref, bound_ref, q_ref, *refs, n_parts, tk, out_scale):
    k_parts, v_parts = refs[:n_parts], refs[n_parts:2 * n_parts]
    g_ref, o_ref, k_ref, vt_ref, acc_ref, s0_ref, s1_ref, p0_ref, p1_ref = refs[2 * n_parts:]
    qi = pl.program_id(2)
    tq = q_ref.shape[0]
    n_chunks = vt_ref.shape[0]

    @pl.when(qi == 0)
    def _():
        row = 0
        for kp, vp in zip(k_parts, v_parts):
            rows = kp.shape[0]
            k_ref[row:row + rows, :] = kp[...]
            done = 0
            while done < rows:
                ci, off = divmod(row + done, tk)
                take = min(tk - off, rows - done)
                vt_ref[ci, :LANES, off:off + take] = _bf(vp[done:done + take, :].astype(jnp.float32).T)
                done += take
            row += rows

        vt_ref[:, LANES:, :] = jnp.ones((n_chunks, SUM_ROWS, tk), jnp.bfloat16)

    q = q_ref[...]
    lane = lax.broadcasted_iota(jnp.int32, (1, LANES), 1)
    qm = (jnp.where(lane < HEAD_DIM, q, jnp.zeros_like(q)), jnp.where(lane >= HEAD_DIM, q, jnp.zeros_like(q)))
    acc_ref[...] = jnp.zeros_like(acc_ref)
    s_bufs = (s0_ref, s1_ref)
    p_bufs = (p0_ref, p1_ref)

    bound = bound_ref[0, 0]
    shift_safe = bound <= SAFE_SHIFT

    @pl.when(shift_safe)
    def _():
        q_both = jnp.concatenate(qm, axis=0)

        def chunk(ci, denom):
            c0 = pl.multiple_of(ci * tk, tk)
            p = jnp.exp2(_dot_nt(k_ref[pl.ds(c0, tk), :], q_both) - bound)
            acc_ref[:LANES, :] += _dot(vt_ref[ci, :LANES, :], _bf(p))
            return denom + jnp.sum(p, axis=0, keepdims=True)
        denom = lax.fori_loop(0, n_chunks, chunk, jnp.zeros((1, 2 * tq), jnp.float32), unroll=True)
        acc_ref[LANES:, :] = jnp.broadcast_to(denom, (SUM_ROWS, 2 * tq))

    @pl.when(jnp.logical_not(shift_safe))
    def _():
        _diff_online(qm, k_ref, vt_ref, acc_ref, s_bufs, p_bufs, tk=tk, tq=tq, n_chunks=n_chunks)

    lam = lam_ref[0, 0]
    a1 = acc_ref[:, :tq]
    a2 = acc_ref[:, tq:]
    o = a1[:LANES] * (1.0 / a1[LANES:LANES + 1]) - lam * (a2[:LANES] * (1.0 / a2[LANES:LANES + 1]))
    y = o * lax.rsqrt(jnp.mean(o * o, axis=0, keepdims=True) + NORM_EPS) * g_ref[...] * out_scale
    o_ref[...] = _bf(y.T)


def _diff_online(qm, k_ref, vt_ref, acc_ref, s_bufs, p_bufs, *, tk, tq, n_chunks):
    def scores(ci, slot):
        c0 = ci * tk if isinstance(ci, int) else pl.multiple_of(ci * tk, tk)
        k = k_ref[pl.ds(c0, tk), :]
        mx = []
        for c in range(2):
            st = _dot_nt(k, qm[c])
            s_bufs[slot][c] = st
            mx.append(jnp.max(st, axis=0, keepdims=True))
        return tuple(mx)

    def probs(slot, mx, m_run):
        m_new, alpha = [], []
        for c in range(2):
            m = jnp.maximum(m_run[c], mx[c])
            alpha.append(jnp.exp2(m_run[c] - m))
            p_bufs[slot][c] = _bf(jnp.exp2(s_bufs[slot][c] - m))
            m_new.append(m)
        return tuple(m_new), tuple(alpha)

    def accumulate(ci, slot, alpha):
        vt = vt_ref[ci]
        for c in range(2):
            cols = slice(c * tq, (c + 1) * tq)
            acc_ref[:, cols] = alpha[c] * acc_ref[:, cols] + _dot(vt, p_bufs[slot][c])

    def step(t, par, do_scores, do_probs, do_acc, state):
        m_run, mx, alpha = state
        mx_next = scores(t + 2, par) if do_scores else mx
        if do_probs:
            m_run, alpha_next = probs(1 - par, mx, m_run)
        else:
            alpha_next = alpha
        if do_acc:
            accumulate(t, par, alpha)
        return m_run, mx_next, alpha_next

    neg = jnp.full((1, tq), NEG_INF, jnp.float32)
    one = jnp.ones((1, tq), jnp.float32)
    state = ((neg, neg), (neg, neg), (one, one))
    for t in (-2, -1):
        state = step(t, t % 2, t + 2 < n_chunks, 0 <= t + 1 < n_chunks, False, state)
    n_steady = max(n_chunks - 2, 0)

    def pair(j, state):
        state = step(2 * j, 0, True, True, True, state)
        return step(2 * j + 1, 1, True, True, True, state)

    state = lax.fori_loop(0, n_steady // 2, pair, state)
    if n_steady % 2:
        state = step(n_steady - 1, 0, True, True, True, state)
    for t in range(n_steady, n_chunks):
        state = step(t, t % 2, False, t + 1 < n_chunks, True, state)


def _pick_tk(nk, cap):
    best = LANES
    for t in range(LANES, min(cap, nk) + 1, LANES):
        if nk % t == 0:
            best = t
    return best


def _diff_attention(qd, k_parts, v_parts, lam, bound, subln_g, lam_init, tq, tk_cap):
    b, n, _ = qd.shape
    nk = sum(k.shape[1] for k in k_parts)
    tk = _pick_tk(nk, tk_cap)
    smem = pl.BlockSpec(memory_space=pltpu.SMEM)
    qspec = pl.BlockSpec((None, tq, LANES), lambda bi, h, i: (bi, i, h))
    kvspecs = [pl.BlockSpec((None, k.shape[1], LANES), lambda bi, h, i: (bi, 0, h)) for k in k_parts]
    gain = jnp.broadcast_to(subln_g.astype(jnp.float32)[:, None], (LANES, tq))
    return pl.pallas_call(
        functools.partial(_diff_kernel, n_parts=len(k_parts), tk=tk, out_scale=1.0 - lam_init),
        grid=(b, DIFF_HEADS, n // tq),
        in_specs=[smem, smem, qspec] + kvspecs + kvspecs + [_const_spec((LANES, tq))],
        out_specs=qspec,
        out_shape=jax.ShapeDtypeStruct((b, n, DIFF_W), jnp.bfloat16),
        scratch_shapes=[pltpu.VMEM((nk, LANES), jnp.bfloat16),
                        pltpu.VMEM((nk // tk, LANES + SUM_ROWS, tk), jnp.bfloat16),
                        pltpu.VMEM((LANES + SUM_ROWS, 2 * tq), jnp.float32),
                        pltpu.VMEM((2, tk, tq), jnp.float32), pltpu.VMEM((2, tk, tq), jnp.float32),
                        pltpu.VMEM((2, tk, tq), jnp.bfloat16), pltpu.VMEM((2, tk, tq), jnp.bfloat16)],
        compiler_params=_params(3),
        name="diff",
    )(lam, bound, qd, *k_parts, *v_parts, gain)


def _merge_kernel(x_ref, gt_ref, ya_ref, sza_ref, yb_ref, szb_ref, uc_ref, up_ref, un_ref, cb_ref, szc_ref,
                  yd_ref, szd_ref, g_ref, cw_ref, wa_ref, wb_ref, wc_ref, wd_ref, wo_ref, o_ref, *, tm, nt):
    i = pl.program_id(1)
    f32 = jnp.float32
    u = uc_ref[...].astype(f32)
    row = lax.broadcasted_iota(jnp.int32, (tm, 1), 0)
    prev_row = jnp.where(i > 0, up_ref[7:8, :].astype(f32), 0.0)
    next_row = jnp.where(i < nt - 1, un_ref[0:1, :].astype(f32), 0.0)
    u_prev = jnp.where(row == 0, prev_row, pltpu.roll(u, 1, 0))
    u_next = jnp.where(row == tm - 1, next_row, pltpu.roll(u, tm - 1, 0))
    cw = cw_ref[...]
    conv = u_prev * cw[0:1, :] + u * cw[1:2, :] + u_next * cw[2:3, :]
    yc = cb_ref[...].astype(f32) * conv
    d = x_ref.shape[-1]
    branches = ((ya_ref[...].astype(f32), sza_ref, wa_ref), (yb_ref[...].astype(f32), szb_ref, wb_ref),
                (yc, szc_ref, wc_ref), (yd_ref[...].astype(f32), szd_ref, wd_ref))
    mixed = jnp.zeros((tm, d), f32)
    for j, (y, sz_ref, w_ref) in enumerate(branches):
        t = _dot(_bf(y * sz_ref[...].astype(f32)), w_ref[...])
        mixed = mixed + g_ref[:, j * d:(j + 1) * d].astype(f32) * t
    out = _dot(_bf(mixed), wo_ref[...])
    o_ref[...] = x_ref[...] + gt_ref[...] * out


def _merge(x, gate, ya, sza, yb, szb, uc, cb, szc, yd, szd, g, conv_w, w_a, w_b, w_c, w_d, w_out, tm):
    b, n, d = x.shape
    nt = n // tm
    tok = lambda w: pl.BlockSpec((None, tm, w), lambda bi, i: (bi, i, 0))
    hb = tm // 8
    halo_p = pl.BlockSpec((None, 8, CONV_W), lambda bi, i: (bi, jnp.maximum(i * hb - 1, 0), 0))
    halo_n = pl.BlockSpec((None, 8, CONV_W), lambda bi, i: (bi, jnp.minimum((i + 1) * hb, n // 8 - 1), 0))
    per_b = pl.BlockSpec((None, 1, d), lambda bi, i: (bi, 0, 0))
    ws = [_bf(w_a), _bf(w_b), _bf(w_c), _bf(w_d), _bf(w_out)]
    return pl.pallas_call(
        functools.partial(_merge_kernel, tm=tm, nt=nt),
        grid=(b, nt),
        in_specs=[tok(d), per_b, tok(FOURIER_W), tok(FOURIER_W), tok(SWA_W), tok(SWA_W), tok(CONV_W), halo_p, halo_n,
                  tok(CONV_W), tok(CONV_W), tok(DIFF_W), tok(DIFF_W), tok(N_BRANCH * d), _const_spec(conv_w.shape)]
                 + [_const_spec(w.shape) for w in ws],
        out_specs=tok(d),
        out_shape=jax.ShapeDtypeStruct((b, n, d), jnp.float32),
        compiler_params=_params(2),
        name="merge",
    )(x, gate, ya, sza, yb, szb, uc, uc, uc, cb, szc, yd, szd, g, conv_w.astype(jnp.float32), *ws)


def _rope_tables(n):
    rows = n // GRID_W
    row = jnp.broadcast_to(jnp.arange(rows, dtype=jnp.float32)[:, None], (rows, GRID_W)).reshape(-1)
    col = jnp.broadcast_to(jnp.arange(GRID_W, dtype=jnp.float32)[None, :], (rows, GRID_W)).reshape(-1)
    nf = HEAD_DIM // 4
    inv = ROPE_BASE ** (-jnp.arange(nf, dtype=jnp.float32) / nf)
    ar = row[:, None] * inv[None, :]
    ac = col[:, None] * inv[None, :]
    cos = jnp.concatenate([jnp.cos(ar), jnp.cos(ar), jnp.cos(ac), jnp.cos(ac)], axis=-1)
    sin = jnp.concatenate([jnp.sin(ar), jnp.sin(ar), jnp.sin(ac), jnp.sin(ac)], axis=-1)
    sign = jnp.where((jnp.arange(HEAD_DIM) % 32) < 16, -1.0, 1.0)
    return jnp.tile(cos, (1, 2)), jnp.tile(sin * sign, (1, 2))


def _channel_dft():
    k = np.arange(HEAD_DIM)
    ang = 2.0 * np.pi * ((k[:, None] * k[None, :]) % HEAD_DIM) / HEAD_DIM
    eye = np.eye(FOURIER_W // HEAD_DIM)
    inv = 1.0 / math.sqrt(HEAD_DIM)
    return jnp.asarray(np.concatenate([np.kron(eye, np.cos(ang)), -np.kron(eye, np.sin(ang))], axis=1) * inv,
                       jnp.bfloat16)


def _extend_w_in(w):
    s = _SRC
    k0, k1 = s["b_k"][0], s["b_k"][0] + HEAD_DIM
    v0, v1 = s["b_v"][0], s["b_v"][0] + HEAD_DIM
    pieces = [w[:, s["a_x"][0]:s["b_q"][1]],
              w[:, k0:k1], w[:, k0:k1], w[:, k1:k1 + HEAD_DIM], w[:, k1:k1 + HEAD_DIM],
              w[:, v0:v1], w[:, v0:v1], w[:, v1:v1 + HEAD_DIM], w[:, v1:v1 + HEAD_DIM],
              w[:, s["c_x"][0]:]]
    return _bf(jnp.concatenate(pieces, axis=1))


def _mix(stream, scale, shift, gate, ctx_kv, lp, consts, lam, lam_init, rope, dft, tm, tq, tk, has_local):
    parts = _project(stream, scale, shift, lp["norm_g"], lp["w_ext"], consts["cs_bd"], lp["head_gains"], rope, tm)
    zr, zi, qb, kb, vb, uc, cb, qd, kd, vd, sza, szb, szc, szd, g = parts
    if ctx_kv is None:
        kx_b, vx_b, k_parts, v_parts = kb, vb, (kd,), (vd,)
    else:
        kx_b, vx_b, kx_d, vx_d = ctx_kv
        k_parts, v_parts = (kx_d, kd), (vx_d, vd)
    ya = _fourier_positions(zr, zi, dft)
    yb = _window_attention(qb, kb, vb, kx_b, vx_b, lp["sink"], lp["bound_b"], has_local)
    yd = _diff_attention(qd, k_parts, v_parts, lam, lp["bound_d"], lp["subln"], lam_init, tq, tk)
    new = _merge(stream, gate, ya, sza, yb, szb, uc, cb, szc, yd, szd, g, lp["conv_w"],
                 lp["w_o_a"], lp["w_o_b"], lp["w_o_c"], lp["w_o_d"], lp["w_out"], tm)
    return new, (kb, vb, kd, vd)


def kernel(x, c, ctx, c_ctx, norm_g, w_mod, b_mod, w_in, q_norm_b, k_norm_b, sink_b, conv_w, q_norm_d, k_norm_d,
           lam_q1, lam_k1, lam_q2, lam_k2, subln_d, w_o_a, w_o_b, w_o_c, w_o_d, w_out):
    b, n, d = x.shape
    lc = ctx.shape[1]
    depth = w_in.shape[0]
    rope = _rope_tables(n)
    consts = dict(cs_bd=_channel_dft())
    dft_x = _dft_tables(n)
    dft_c = _dft_tables(lc)
    c_rows = jnp.zeros((8, d), jnp.float32).at[:b].set(c).at[b].set(c_ctx)
    tm = min(512, n)
    tq = min(512, n)
    tk = 1280
    for l in range(depth):
        last = l == depth - 1
        lam_init = 0.8 - 0.6 * math.exp(-0.3 * l)
        lam_vecs = jnp.stack([lam_q1[l], lam_k1[l], lam_q2[l], lam_k2[l]]).astype(jnp.float32)
        mod, lam_o = _modulation(c_rows, w_mod[l], b_mod[l], lam_vecs, lam_init)
        lam = lam_o[0:1, 0:1]
        shift, scale, gate = (mod[:b, None, j * d:(j + 1) * d] for j in range(3))
        shift_c, scale_c, gate_c = (jnp.broadcast_to(mod[b:b + 1, None, j * d:(j + 1) * d], (b, 1, d))
                                    for j in range(3))
        tile2 = lambda v: jnp.tile(v.astype(jnp.float32), 2)
        lp = dict(norm_g=norm_g[l], w_ext=_extend_w_in(w_in[l]), sink=sink_b[l].astype(jnp.float32),
                  head_gains=jnp.stack([tile2(q_norm_b[l]), tile2(k_norm_b[l]), tile2(q_norm_d[l]), tile2(k_norm_d[l])]),
                  bound_b=_score_bound(q_norm_b[l], k_norm_b[l]), bound_d=_score_bound(q_norm_d[l], k_norm_d[l]),
                  subln=subln_d[l], conv_w=conv_w[l], w_o_a=w_o_a[l], w_o_b=w_o_b[l], w_o_c=w_o_c[l],
                  w_o_d=w_o_d[l], w_out=w_out[l])
        if last:
            parts = _project(ctx, scale_c, shift_c, lp["norm_g"], lp["w_ext"], consts["cs_bd"], lp["head_gains"],
                             None, min(256, lc))
            ctx_kv = (parts[3], parts[4], parts[8], parts[9])
        else:
            ctx, ctx_kv = _mix(ctx, scale_c, shift_c, gate_c, None, lp, consts, lam, lam_init, None, dft_c,
                               min(256, lc), min(256, lc), tk, False)
        x, _ = _mix(x, scale, shift, gate, ctx_kv, lp, consts, lam, lam_init, rope, dft_x, tm, tq, tk, True)
    return x
```

```python
import functools
import math

import jax
import jax.numpy as jnp
import numpy as np
from jax import lax
from jax.experimental import pallas as pl
from jax.experimental.pallas import tpu as pltpu

HEAD_DIM = 64
LANES = 128
FOURIER_W = 384
SWA_HEADS = 8
SWA_KV = 2
SWA_W = SWA_HEADS * HEAD_DIM
WINDOW = 128
BLOCK = 128
CONV_W = 384
CONV_K = 3
DIFF_HEADS = 4
DIFF_W = DIFF_HEADS * 2 * HEAD_DIM
N_BRANCH = 4
ROPE_BASE = 10000.0
NORM_EPS = 1e-6
NEG_INF = -1e30
GRID_W = 64
DFT_MINOR = 128
SUM_ROWS = 16
LOG2E = math.log2(math.e)
SAFE_SHIFT = 60.0
VMEM_LIMIT = 56 * 1024 * 1024


def _bf(x):
    return x.astype(jnp.bfloat16)


def _dot(a, b):
    return jnp.dot(a, b, preferred_element_type=jnp.float32)


def _dot_nt(a, b):
    return lax.dot_general(a, b, (((1,), (1,)), ((), ())), preferred_element_type=jnp.float32)


def _params(n_axes, flags=None):
    return pltpu.CompilerParams(dimension_semantics=("arbitrary",) * n_axes, vmem_limit_bytes=VMEM_LIMIT,
                                flags=flags)


def _const_spec(shape):
    nd = len(shape)
    return pl.BlockSpec(shape, lambda *_: (0,) * nd, pipeline_mode=pl.Buffered(1))


def _mod_kernel(c_ref, w_ref, b_ref, lam_ref, mod_ref, lamo_ref, *, lam_init):
    c = c_ref[...]
    s = c * jax.nn.sigmoid(c)
    mod_ref[...] = _dot(_bf(s), w_ref[...]) + b_ref[...]
    lv = lam_ref[...]
    a1 = jnp.sum(lv[0:1, :] * lv[1:2, :], axis=-1, keepdims=True)
    a2 = jnp.sum(lv[2:3, :] * lv[3:4, :], axis=-1, keepdims=True)
    lam = jnp.exp(a1) - jnp.exp(a2) + lam_init
    lamo_ref[...] = jnp.broadcast_to(lam, lamo_ref.shape)


def _modulation(c_rows, w_mod, b_mod, lam_vecs, lam_init):
    r, d = c_rows.shape
    return pl.pallas_call(
        functools.partial(_mod_kernel, lam_init=lam_init),
        out_shape=(jax.ShapeDtypeStruct((r, 3 * d), jnp.float32),
                   jax.ShapeDtypeStruct((8, LANES), jnp.float32)),
        name="mod",
    )(c_rows, _bf(w_mod), b_mod.reshape(1, 3 * d), lam_vecs)


def _head_norm_rope(t, gain, cos, sin_s, lane_lo, scale):
    sq = t * t
    first = lax.broadcasted_iota(jnp.int32, t.shape, 1) < HEAD_DIM
    ms = jnp.where(first, jnp.sum(jnp.where(first, sq, 0.0), axis=-1, keepdims=True),
                   jnp.sum(jnp.where(first, 0.0, sq), axis=-1, keepdims=True)) * (1.0 / HEAD_DIM)
    y = t * lax.rsqrt(ms + NORM_EPS) * gain
    if cos is not None:
        rot = jnp.where(lane_lo, pltpu.roll(y, LANES - 16, 1), pltpu.roll(y, 16, 1))
        y = y * cos + rot * sin_s
    if scale != 1.0:
        y = y * scale
    return y


def _proj_kernel(*refs, use_rope, tm):
    if use_rope:
        (x_ref, sc_ref, sh_ref, g_ref, w_ref, cs_ref, hg_ref, cos_ref, sin_ref), outs = refs[:9], refs[9:]
    else:
        (x_ref, sc_ref, sh_ref, g_ref, w_ref, cs_ref, hg_ref), outs = refs[:7], refs[7:]
        cos_ref = sin_ref = None
    (zr_ref, zi_ref, qb_ref, kb_ref, vb_ref, uc_ref, cb_ref, qd_ref, kd_ref, vd_ref,
     sza_ref, szb_ref, szc_ref, szd_ref, gate_ref) = outs

    x = x_ref[...]
    y = x * lax.rsqrt(jnp.mean(x * x, axis=-1, keepdims=True) + NORM_EPS) * g_ref[...]
    h = _bf(y * (1.0 + sc_ref[...]) + sh_ref[...])

    def proj(c0, width):
        return _dot(h, w_ref[:, c0:c0 + width])

    col = 0
    a = _bf(proj(col, FOURIER_W))
    zz = _dot(a, cs_ref[...])
    zr_ref[...] = _bf(zz[:, :FOURIER_W])
    zi_ref[...] = _bf(zz[:, FOURIER_W:])
    col += FOURIER_W

    if use_rope:
        cos = cos_ref[...]
        sin_s = sin_ref[...]
    else:
        cos = sin_s = None
    lane = lax.broadcasted_iota(jnp.int32, (tm, LANES), 1)
    lane_lo = (lane & 31) < 16
    qscale = HEAD_DIM ** -0.5

    def normed(c0, width, gain_row, out_ref, scale):
        t = proj(c0, width)
        gain = hg_ref[gain_row:gain_row + 1, :]
        for s in range(width // LANES):
            ts = t[:, s * LANES:(s + 1) * LANES]
            out_ref[:, s * LANES:(s + 1) * LANES] = _bf(
                _head_norm_rope(ts, gain, cos, sin_s, lane_lo, scale))

    head_lo = lane < HEAD_DIM

    def store_duplicated(pair, out_ref):
        swapped = pltpu.roll(pair, HEAD_DIM, 1)
        out_ref[:, :LANES] = _bf(jnp.where(head_lo, pair, swapped))
        out_ref[:, LANES:] = _bf(jnp.where(head_lo, swapped, pair))

    normed(col, SWA_W, 0, qb_ref, qscale * LOG2E); col += SWA_W
    store_duplicated(_head_norm_rope(proj(col, LANES), hg_ref[1:2, :], cos, sin_s, lane_lo, 1.0), kb_ref)
    col += LANES
    store_duplicated(proj(col, LANES), vb_ref); col += LANES
    c3 = proj(col, 3 * CONV_W); col += 3 * CONV_W
    uc_ref[...] = _bf(c3[:, 2 * CONV_W:] * c3[:, :CONV_W])
    cb_ref[...] = _bf(c3[:, CONV_W:2 * CONV_W])
    normed(col, DIFF_W, 2, qd_ref, qscale * LOG2E); col += DIFF_W
    normed(col, DIFF_W, 3, kd_ref, 1.0); col += DIFF_W
    vd_ref[...] = _bf(proj(col, DIFF_W)); col += DIFF_W
    for ref, width in ((sza_ref, FOURIER_W), (szb_ref, SWA_W), (szc_ref, CONV_W), (szd_ref, DIFF_W)):
        z = proj(col, width)
        ref[...] = _bf(z * jax.nn.sigmoid(z))
        col += width
    d = x.shape[-1]
    for j in range(N_BRANCH):
        gate_ref[:, j * d:(j + 1) * d] = _bf(jax.nn.sigmoid(proj(col, d)))
        col += d


_PROJ_OUT_W = (FOURIER_W, FOURIER_W, SWA_W, 2 * LANES, 2 * LANES, CONV_W, CONV_W, DIFF_W, DIFF_W, DIFF_W,
               FOURIER_W, SWA_W, CONV_W, DIFF_W)


def _project(x, scale, shift, norm_g, w_ext, cs_bd, head_gains, rope, tm):
    b, n, d = x.shape
    use_rope = rope is not None
    tok = lambda w: pl.BlockSpec((None, tm, w), lambda bi, i: (bi, i, 0))
    per_b = pl.BlockSpec((None, 1, d), lambda bi, i: (bi, 0, 0))
    in_specs = [tok(d), per_b, per_b, _const_spec((1, d)), _const_spec(w_ext.shape), _const_spec(cs_bd.shape),
                _const_spec(head_gains.shape)]
    args = [x, scale, shift, norm_g.reshape(1, d), w_ext, cs_bd, head_gains]
    if use_rope:
        tab = pl.BlockSpec((tm, LANES), lambda bi, i: (i, 0))
        in_specs += [tab, tab]
        args += list(rope)
    widths = _PROJ_OUT_W + (N_BRANCH * d,)
    return pl.pallas_call(
        functools.partial(_proj_kernel, use_rope=use_rope, tm=tm),
        grid=(b, n // tm),
        in_specs=in_specs,
        out_specs=[tok(w) for w in widths],
        out_shape=[jax.ShapeDtypeStruct((b, n, w), jnp.bfloat16) for w in widths],
        compiler_params=_params(2),
        name="proj",
    )(*args)


def _dft1_kernel(zr_ref, zi_ref, f_ref, tc_ref, ts_ref, o_ref, *, r, tn2):
    z = jnp.concatenate([zr_ref[...], zi_ref[...]], axis=0)
    a = _dot(f_ref[...], z)
    for j in range(tn2):
        ar = a[:r, j * FOURIER_W:(j + 1) * FOURIER_W]
        ai = a[r:, j * FOURIER_W:(j + 1) * FOURIER_W]
        tc = jnp.concatenate([tc_ref[j]] * (FOURIER_W // LANES), axis=-1)
        ts = jnp.concatenate([ts_ref[j]] * (FOURIER_W // LANES), axis=-1)
        o_ref[0, j] = _bf(ar * tc + ai * ts)
        o_ref[1, j] = _bf(ai * tc - ar * ts)


def _left_matmul_kernel(m_ref, x_ref, o_ref):
    o_ref[...] = _bf(_dot(m_ref[...], x_ref[...]))


def _left_matmul(mat, x, tc):
    b, k, c = x.shape
    rows = mat.shape[0]
    return pl.pallas_call(
        _left_matmul_kernel,
        grid=(b, c // tc),
        in_specs=[_const_spec(mat.shape), pl.BlockSpec((None, k, tc), lambda bi, i: (bi, 0, i))],
        out_specs=pl.BlockSpec((None, rows, tc), lambda bi, i: (bi, 0, i)),
        out_shape=jax.ShapeDtypeStruct((b, rows, c), jnp.bfloat16),
        compiler_params=_params(2),
        name="dft2",
    )(mat, x)


def _dft_tables(n):
    inv = 1.0 / math.sqrt(n)
    if n <= 2 * DFT_MINOR:
        k = np.arange(n)
        ang = 2.0 * np.pi * ((k[:, None] * k[None, :]) % n) / n
        return dict(direct=jnp.asarray(np.concatenate([np.cos(ang), np.sin(ang)], axis=1) * inv, jnp.bfloat16))
    r = n // DFT_MINOR
    k1 = np.arange(r)
    a1 = 2.0 * np.pi * ((k1[:, None] * k1[None, :]) % r) / r
    c1, s1 = np.cos(a1), np.sin(a1)
    f1 = np.block([[c1, s1], [-s1, c1]])
    n2 = np.arange(DFT_MINOR)
    at = 2.0 * np.pi * (n2[:, None] * k1[None, :]) / n
    tw_c = np.repeat(np.cos(at)[:, :, None], LANES, axis=2)
    tw_s = np.repeat(np.sin(at)[:, :, None], LANES, axis=2)
    a2 = 2.0 * np.pi * ((n2[:, None] * n2[None, :]) % DFT_MINOR) / DFT_MINOR
    f2 = np.concatenate([np.cos(a2), np.sin(a2)], axis=1) * inv
    return dict(f1=jnp.asarray(f1, jnp.bfloat16), tw_c=jnp.asarray(tw_c, jnp.float32),
                tw_s=jnp.asarray(tw_s, jnp.float32), f2=jnp.asarray(f2, jnp.bfloat16))


def _fourier_positions(zr, zi, tabs):
    b, n, w = zr.shape
    if "direct" in tabs:
        return _left_matmul(tabs["direct"], jnp.concatenate([zr, zi], axis=1), w)
    r = n // DFT_MINOR
    tn2 = 16
    zr2 = zr.reshape(b, r, DFT_MINOR * w)
    zi2 = zi.reshape(b, r, DFT_MINOR * w)
    zin = pl.BlockSpec((None, r, tn2 * w), lambda bi, i: (bi, 0, i))
    tw = pl.BlockSpec((tn2, r, LANES), lambda bi, i: (i, 0, 0))
    g = pl.pallas_call(
        functools.partial(_dft1_kernel, r=r, tn2=tn2),
        grid=(b, DFT_MINOR // tn2),
        in_specs=[zin, zin, _const_spec(tabs["f1"].shape), tw, tw],
        out_specs=pl.BlockSpec((None, 2, tn2, r, w), lambda bi, i: (bi, 0, i, 0, 0)),
        out_shape=jax.ShapeDtypeStruct((b, 2, DFT_MINOR, r, w), jnp.bfloat16),
        compiler_params=_params(2),
        name="dft1",
    )(zr2, zi2, tabs["f1"], tabs["tw_c"], tabs["tw_s"])
    y = _left_matmul(tabs["f2"], g.reshape(b, 2 * DFT_MINOR, r * w), (16 if r % 16 == 0 else 4) * w)
    return y.reshape(b, n, w)


def _win_kernel(*refs, has_local, nb, nq):
    if has_local:
        bound_ref, sink_ref, q_ref, kp_ref, kc_ref, kn_ref, vp_ref, vc_ref, vn_ref, kx_ref, vx_ref, o_ref = refs
    else:
        bound_ref, sink_ref, q_ref, kx_ref, vx_ref, o_ref = refs
    i = pl.program_id(1)
    tq = BLOCK
    group = SWA_HEADS // SWA_KV
    lane = lax.broadcasted_iota(jnp.int32, (1, LANES), 1)
    halves = (lane < HEAD_DIM, lane >= HEAD_DIM)
    row_lo = lax.broadcasted_iota(jnp.int32, (LANES, 1), 0) < HEAD_DIM
    if has_local:
        kr = lax.broadcasted_iota(jnp.int32, (3 * BLOCK, tq), 0)
        qc = lax.broadcasted_iota(jnp.int32, (3 * BLOCK, tq), 1)
        in_band = jnp.abs(kr - BLOCK - qc) <= WINDOW
    bound = bound_ref[0, 0]

    def attend(s, j, use_bound):
        sl = slice(j * LANES, (j + 1) * LANES)
        rows = slice(s * BLOCK, (s + 1) * BLOCK)
        if has_local:
            def window(prev_ref, cur_ref, next_ref):
                first = prev_ref[:, sl] if s == 0 else cur_ref[(s - 1) * BLOCK:s * BLOCK, sl]
                last = next_ref[:, sl] if s == nq - 1 else cur_ref[(s + 1) * BLOCK:(s + 2) * BLOCK, sl]
                return [first, cur_ref[rows, sl], last]
            kd = jnp.concatenate(window(kp_ref, kc_ref, kn_ref) + [kx_ref[:, sl]], axis=0)
            vd = jnp.concatenate(window(vp_ref, vc_ref, vn_ref) + [vx_ref[:, sl]], axis=0)
        else:
            kd = kx_ref[:, sl]
            vd = vx_ref[:, sl]
        q_stack = []
        for t in range(group // 2):
            slab = j * (group // 2) + t
            q2 = q_ref[rows, slab * LANES:(slab + 1) * LANES]
            q_stack += [jnp.where(halves[e], q2, jnp.zeros_like(q2)) for e in range(2)]
        st = _dot_nt(kd, jnp.concatenate(q_stack, axis=0))
        if has_local:
            kpos = (i * nq + s - 1) * BLOCK + kr
            bias = jnp.where(in_band & (kpos >= 0) & (kpos < nb * BLOCK), 0.0, NEG_INF)
            st = jnp.concatenate([st[:3 * BLOCK] + jnp.concatenate([bias] * group, axis=1), st[3 * BLOCK:]], axis=0)
        sk = sink_ref[j:j + 1, :]
        m = jnp.maximum(bound, sk) if use_bound else jnp.maximum(jnp.max(st, axis=0, keepdims=True), sk)
        p = jnp.exp2(st - m)
        inv = 1.0 / (jnp.sum(p, axis=0, keepdims=True) + jnp.exp2(sk - m))
        ot = lax.dot_general(vd, _bf(p), (((0,), (0,)), ((), ())), preferred_element_type=jnp.float32) * inv
        for t in range(group // 2):
            slab = j * (group // 2) + t
            pair = jnp.where(row_lo, ot[:, (2 * t) * tq:(2 * t + 1) * tq], ot[:, (2 * t + 1) * tq:(2 * t + 2) * tq])
            o_ref[rows, slab * LANES:(slab + 1) * LANES] = _bf(pair.T)

    shift_safe = bound <= SAFE_SHIFT

    @pl.when(shift_safe)
    def _():
        for s in range(nq):
            for j in range(SWA_KV):
                attend(s, j, True)

    @pl.when(jnp.logical_not(shift_safe))
    def _():
        for s in range(nq):
            for j in range(SWA_KV):
                attend(s, j, False)


def _score_bound(q_gain, k_gain):
    g = jnp.max(jnp.abs(q_gain.astype(jnp.float32))) * jnp.max(jnp.abs(k_gain.astype(jnp.float32)))
    return (1.02 * HEAD_DIM ** 0.5 * LOG2E * g).reshape(1, 1)


def _window_attention(qb, kb, vb, kx, vx, sink, bound, has_local):
    b, n, _ = qb.shape
    nb = n // BLOCK
    nq = 4 if nb % 4 == 0 else 1
    lx = kx.shape[1]
    qspec = pl.BlockSpec((None, nq * BLOCK, SWA_W), lambda bi, i: (bi, i, 0))
    ctx = pl.BlockSpec((None, lx, 2 * LANES), lambda bi, i: (bi, 0, 0))
    sink_rows = jnp.repeat(sink.astype(jnp.float32) * LOG2E, BLOCK).reshape(SWA_KV, -1)
    args = [bound, sink_rows, qb]
    specs = [pl.BlockSpec(memory_space=pltpu.SMEM), _const_spec(sink_rows.shape), qspec]
    if has_local:
        prv = pl.BlockSpec((None, BLOCK, 2 * LANES), lambda bi, i: (bi, jnp.maximum(i * nq - 1, 0), 0))
        cur = pl.BlockSpec((None, nq * BLOCK, 2 * LANES), lambda bi, i: (bi, i, 0))
        nxt = pl.BlockSpec((None, BLOCK, 2 * LANES), lambda bi, i: (bi, jnp.minimum((i + 1) * nq, nb - 1), 0))
        args += [kb, kb, kb, vb, vb, vb]
        specs += [prv, cur, nxt, prv, cur, nxt]
    args += [kx, vx]
    specs += [ctx, ctx]
    return pl.pallas_call(
        functools.partial(_win_kernel, has_local=has_local, nb=nb, nq=nq),
        grid=(b, nb // nq),
        in_specs=specs,
        out_specs=qspec,
        out_shape=jax.ShapeDtypeStruct((b, n, SWA_W), jnp.bfloat16),
        compiler_params=_params(2),
        name="win",
    )(*args)


def _diff_kernel(lam_ref, bound_ref, q_ref, *refs, n_parts, tk, out_scale):
    k_parts, v_parts = refs[:n_parts], refs[n_parts:2 * n_parts]
    g_ref, o_ref, k_ref, vt_ref, acc_ref, s0_ref, s1_ref, p0_ref, p1_ref = refs[2 * n_parts:]
    qi = pl.program_id(2)
    tq = q_ref.shape[0]
    n_chunks = vt_ref.shape[0]

    @pl.when(qi == 0)
    def _():
        row = 0
        for kp, vp in zip(k_parts, v_parts):
            rows = kp.shape[0]
            k_ref[row:row + rows, :] = kp[...]
            done = 0
            while done < rows:
                ci, off = divmod(row + done, tk)
                take = min(tk - off, rows - done)
                vt_ref[ci, :LANES, off:off + take] = _bf(vp[done:done + take, :].astype(jnp.float32).T)
                done += take
            row += rows

        vt_ref[:, LANES:, :] = jnp.ones((n_chunks, SUM_ROWS, tk), jnp.bfloat16)

    q = q_ref[...]
    lane = lax.broadcasted_iota(jnp.int32, (1, LANES), 1)
    qm = (jnp.where(lane < HEAD_DIM, q, jnp.zeros_like(q)), jnp.where(lane >= HEAD_DIM, q, jnp.zeros_like(q)))
    acc_ref[...] = jnp.zeros_like(acc_ref)
    s_bufs = (s0_ref, s1_ref)
    p_bufs = (p0_ref, p1_ref)

    bound = bound_ref[0, 0]
    shift_safe = bound <= SAFE_SHIFT

    @pl.when(shift_safe)
    def _():
        q_both = jnp.concatenate(qm, axis=0)

        def chunk(ci, denom):
            c0 = pl.multiple_of(ci * tk, tk)
            p = jnp.exp2(_dot_nt(k_ref[pl.ds(c0, tk), :], q_both) - bound)
            acc_ref[:LANES, :] += _dot(vt_ref[ci, :LANES, :], _bf(p))
            return denom + jnp.sum(p, axis=0, keepdims=True)
        denom = lax.fori_loop(0, n_chunks, chunk, jnp.zeros((1, 2 * tq), jnp.float32), unroll=True)
        acc_ref[LANES:, :] = jnp.broadcast_to(denom, (SUM_ROWS, 2 * tq))

    @pl.when(jnp.logical_not(shift_safe))
    def _():
        _diff_online(qm, k_ref, vt_ref, acc_ref, s_bufs, p_bufs, tk=tk, tq=tq, n_chunks=n_chunks)

    lam = lam_ref[0, 0]
    a1 = acc_ref[:, :tq]
    a2 = acc_ref[:, tq:]
    o = a1[:LANES] * (1.0 / a1[LANES:LANES + 1]) - lam * (a2[:LANES] * (1.0 / a2[LANES:LANES + 1]))
    y = o * lax.rsqrt(jnp.mean(o * o, axis=0, keepdims=True) + NORM_EPS) * g_ref[...] * out_scale
    o_ref[...] = _bf(y.T)


def _diff_online(qm, k_ref, vt_ref, acc_ref, s_bufs, p_bufs, *, tk, tq, n_chunks):
    def scores(ci, slot):
        c0 = ci * tk if isinstance(ci, int) else pl.multiple_of(ci * tk, tk)
        k = k_ref[pl.ds(c0, tk), :]
        mx = []
        for c in range(2):
            st = _dot_nt(k, qm[c])
            s_bufs[slot][c] = st
            mx.append(jnp.max(st, axis=0, keepdims=True))
        return tuple(mx)

    def probs(slot, mx, m_run):
        m_new, alpha = [], []
        for c in range(2):
            m = jnp.maximum(m_run[c], mx[c])
            alpha.append(jnp.exp2(m_run[c] - m))
            p_bufs[slot][c] = _bf(jnp.exp2(s_bufs[slot][c] - m))
            m_new.append(m)
        return tuple(m_new), tuple(alpha)

    def accumulate(ci, slot, alpha):
        vt = vt_ref[ci]
        for c in range(2):
            cols = slice(c * tq, (c + 1) * tq)
            acc_ref[:, cols] = alpha[c] * acc_ref[:, cols] + _dot(vt, p_bufs[slot][c])

    def step(t, par, do_scores, do_probs, do_acc, state):
        m_run, mx, alpha = state
        mx_next = scores(t + 2, par) if do_scores else mx
        if do_probs:
            m_run, alpha_next = probs(1 - par, mx, m_run)
        else:
            alpha_next = alpha
        if do_acc:
            accumulate(t, par, alpha)
        return m_run, mx_next, alpha_next

    neg = jnp.full((1, tq), NEG_INF, jnp.float32)
    one = jnp.ones((1, tq), jnp.float32)
    state = ((neg, neg), (neg, neg), (one, one))
    for t in (-2, -1):
        state = step(t, t % 2, t + 2 < n_chunks, 0 <= t + 1 < n_chunks, False, state)
    n_steady = max(n_chunks - 2, 0)

    def pair(j, state):
        state = step(2 * j, 0, True, True, True, state)
        return step(2 * j + 1, 1, True, True, True, state)

    state = lax.fori_loop(0, n_steady // 2, pair, state)
    if n_steady % 2:
        state = step(n_steady - 1, 0, True, True, True, state)
    for t in range(n_steady, n_chunks):
        state = step(t, t % 2, False, t + 1 < n_chunks, True, state)


def _pick_tk(nk, cap):
    best = LANES
    for t in range(LANES, min(cap, nk) + 1, LANES):
        if nk % t == 0:
            best = t
    return best


def _diff_attention(qd, k_parts, v_parts, lam, bound, subln_g, lam_init, tq, tk_cap):
    b, n, _ = qd.shape
    nk = sum(k.shape[1] for k in k_parts)
    tk = _pick_tk(nk, tk_cap)
    smem = pl.BlockSpec(memory_space=pltpu.SMEM)
    qspec = pl.BlockSpec((None, tq, LANES), lambda bi, h, i: (bi, i, h))
    kvspecs = [pl.BlockSpec((None, k.shape[1], LANES), lambda bi, h, i: (bi, 0, h)) for k in k_parts]
    gain = jnp.broadcast_to(subln_g.astype(jnp.float32)[:, None], (LANES, tq))
    return pl.pallas_call(
        functools.partial(_diff_kernel, n_parts=len(k_parts), tk=tk, out_scale=1.0 - lam_init),
        grid=(b, DIFF_HEADS, n // tq),
        in_specs=[smem, smem, qspec] + kvspecs + kvspecs + [_const_spec((LANES, tq))],
        out_specs=qspec,
        out_shape=jax.ShapeDtypeStruct((b, n, DIFF_W), jnp.bfloat16),
        scratch_shapes=[pltpu.VMEM((nk, LANES), jnp.bfloat16),
                        pltpu.VMEM((nk // tk, LANES + SUM_ROWS, tk), jnp.bfloat16),
                        pltpu.VMEM((LANES + SUM_ROWS, 2 * tq), jnp.float32),
                        pltpu.VMEM((2, tk, tq), jnp.float32), pltpu.VMEM((2, tk, tq), jnp.float32),
                        pltpu.VMEM((2, tk, tq), jnp.bfloat16), pltpu.VMEM((2, tk, tq), jnp.bfloat16)],
        compiler_params=_params(3),
        name="diff",
    )(lam, bound, qd, *k_parts, *v_parts, gain)


def _merge_kernel(x_ref, gt_ref, ya_ref, sza_ref, yb_ref, szb_ref, uc_ref, up_ref, un_ref, cb_ref, szc_ref,
                  yd_ref, szd_ref, g_ref, cw_ref, wa_ref, wb_ref, wc_ref, wd_ref, wo_ref, o_ref, *, tm, nt):
    i = pl.program_id(1)
    f32 = jnp.float32
    u = uc_ref[...].astype(f32)
    row = lax.broadcasted_iota(jnp.int32, (tm, 1), 0)
    prev_row = jnp.where(i > 0, up_ref[7:8, :].astype(f32), 0.0)
    next_row = jnp.where(i < nt - 1, un_ref[0:1, :].astype(f32), 0.0)
    u_prev = jnp.where(row == 0, prev_row, pltpu.roll(u, 1, 0))
    u_next = jnp.where(row == tm - 1, next_row, pltpu.roll(u, tm - 1, 0))
    cw = cw_ref[...]
    conv = u_prev * cw[0:1, :] + u * cw[1:2, :] + u_next * cw[2:3, :]
    yc = cb_ref[...].astype(f32) * conv
    d = x_ref.shape[-1]
    branches = ((ya_ref[...].astype(f32), sza_ref, wa_ref), (yb_ref[...].astype(f32), szb_ref, wb_ref),
                (yc, szc_ref, wc_ref), (yd_ref[...].astype(f32), szd_ref, wd_ref))
    mixed = jnp.zeros((tm, d), f32)
    for j, (y, sz_ref, w_ref) in enumerate(branches):
        t = _dot(_bf(y * sz_ref[...].astype(f32)), w_ref[...])
        mixed = mixed + g_ref[:, j * d:(j + 1) * d].astype(f32) * t
    out = _dot(_bf(mixed), wo_ref[...])
    o_ref[...] = x_ref[...] + gt_ref[...] * out


def _merge(x, gate, ya, sza, yb, szb, uc, cb, szc, yd, szd, g, conv_w, w_a, w_b, w_c, w_d, w_out, tm):
    b, n, d = x.shape
    nt = n // tm
    tok = lambda w: pl.BlockSpec((None, tm, w), lambda bi, i: (bi, i, 0))
    hb = tm // 8
    halo_p = pl.BlockSpec((None, 8, CONV_W), lambda bi, i: (bi, jnp.maximum(i * hb - 1, 0), 0))
    halo_n = pl.BlockSpec((None, 8, CONV_W), lambda bi, i: (bi, jnp.minimum((i + 1) * hb, n // 8 - 1), 0))
    per_b = pl.BlockSpec((None, 1, d), lambda bi, i: (bi, 0, 0))
    ws = [_bf(w_a), _bf(w_b), _bf(w_c), _bf(w_d), _bf(w_out)]
    return pl.pallas_call(
        functools.partial(_merge_kernel, tm=tm, nt=nt),
        grid=(b, nt),
        in_specs=[tok(d), per_b, tok(FOURIER_W), tok(FOURIER_W), tok(SWA_W), tok(SWA_W), tok(CONV_W), halo_p, halo_n,
                  tok(CONV_W), tok(CONV_W), tok(DIFF_W), tok(DIFF_W), tok(N_BRANCH * d), _const_spec(conv_w.shape)]
                 + [_const_spec(w.shape) for w in ws],
        out_specs=tok(d),
        out_shape=jax.ShapeDtypeStruct((b, n, d), jnp.float32),
        compiler_params=_params(2),
        name="merge",
    )(x, gate, ya, sza, yb, szb, uc, uc, uc, cb, szc, yd, szd, g, conv_w.astype(jnp.float32), *ws)


def _rope_tables(n):
    rows = n // GRID_W
    row = jnp.broadcast_to(jnp.arange(rows, dtype=jnp.float32)[:, None], (rows, GRID_W)).reshape(-1)
    col = jnp.broadcast_to(jnp.arange(GRID_W, dtype=jnp.float32)[None, :], (rows, GRID_W)).reshape(-1)
    nf = HEAD_DIM // 4
    inv = ROPE_BASE ** (-jnp.arange(nf, dtype=jnp.float32) / nf)
    ar = row[:, None] * inv[None, :]
    ac = col[:, None] * inv[None, :]
    cos = jnp.concatenate([jnp.cos(ar), jnp.cos(ar), jnp.cos(ac), jnp.cos(ac)], axis=-1)
    sin = jnp.concatenate([jnp.sin(ar), jnp.sin(ar), jnp.sin(ac), jnp.sin(ac)], axis=-1)
    sign = jnp.where((jnp.arange(HEAD_DIM) % 32) < 16, -1.0, 1.0)
    return jnp.tile(cos, (1, 2)), jnp.tile(sin * sign, (1, 2))


def _channel_dft():
    k = np.arange(HEAD_DIM)
    ang = 2.0 * np.pi * ((k[:, None] * k[None, :]) % HEAD_DIM) / HEAD_DIM
    eye = np.eye(FOURIER_W // HEAD_DIM)
    inv = 1.0 / math.sqrt(HEAD_DIM)
    return jnp.asarray(np.concatenate([np.kron(eye, np.cos(ang)), -np.kron(eye, np.sin(ang))], axis=1) * inv,
                       jnp.bfloat16)


def _mix(stream, scale, shift, gate, ctx_kv, lp, consts, lam, lam_init, rope, dft, tm, tq, tk, has_local):
    parts = _project(stream, scale, shift, lp["norm_g"], lp["w_ext"], consts["cs_bd"], lp["head_gains"], rope, tm)
    zr, zi, qb, kb, vb, uc, cb, qd, kd, vd, sza, szb, szc, szd, g = parts
    if ctx_kv is None:
        kx_b, vx_b, k_parts, v_parts = kb, vb, (kd,), (vd,)
    else:
        kx_b, vx_b, kx_d, vx_d = ctx_kv
        k_parts, v_parts = (kx_d, kd), (vx_d, vd)
    ya = _fourier_positions(zr, zi, dft)
    yb = _window_attention(qb, kb, vb, kx_b, vx_b, lp["sink"], lp["bound_b"], has_local)
    yd = _diff_attention(qd, k_parts, v_parts, lam, lp["bound_d"], lp["subln"], lam_init, tq, tk)
    new = _merge(stream, gate, ya, sza, yb, szb, uc, cb, szc, yd, szd, g, lp["conv_w"],
                 lp["w_o_a"], lp["w_o_b"], lp["w_o_c"], lp["w_o_d"], lp["w_out"], tm)
    return new, (kb, vb, kd, vd)


def kernel(x, c, ctx, c_ctx, norm_g, w_mod, b_mod, w_in, q_norm_b, k_norm_b, sink_b, conv_w, q_norm_d, k_norm_d,
           lam_q1, lam_k1, lam_q2, lam_k2, subln_d, w_o_a, w_o_b, w_o_c, w_o_d, w_out):
    b, n, d = x.shape
    lc = ctx.shape[1]
    depth = w_in.shape[0]
    rope = _rope_tables(n)
    consts = dict(cs_bd=_channel_dft())
    dft_x = _dft_tables(n)
    dft_c = _dft_tables(lc)
    c_rows = jnp.zeros((8, d), jnp.float32).at[:b].set(c).at[b].set(c_ctx)
    tm = min(512, n)
    tq = min(512, n)
    tk = 1280
    for l in range(depth):
        last = l == depth - 1
        lam_init = 0.8 - 0.6 * math.exp(-0.3 * l)
        lam_vecs = jnp.stack([lam_q1[l], lam_k1[l], lam_q2[l], lam_k2[l]]).astype(jnp.float32)
        mod, lam_o = _modulation(c_rows, w_mod[l], b_mod[l], lam_vecs, lam_init)
        lam = lam_o[0:1, 0:1]
        shift, scale, gate = (mod[:b, None, j * d:(j + 1) * d] for j in range(3))
        shift_c, scale_c, gate_c = (jnp.broadcast_to(mod[b:b + 1, None, j * d:(j + 1) * d], (b, 1, d))
                                    for j in range(3))
        tile2 = lambda v: jnp.tile(v.astype(jnp.float32), 2)
        lp = dict(norm_g=norm_g[l], w_ext=_bf(w_in[l]), sink=sink_b[l].astype(jnp.float32),
                  head_gains=jnp.stack([tile2(q_norm_b[l]), tile2(k_norm_b[l]), tile2(q_norm_d[l]), tile2(k_norm_d[l])]),
                  bound_b=_score_bound(q_norm_b[l], k_norm_b[l]), bound_d=_score_bound(q_norm_d[l], k_norm_d[l]),
                  subln=subln_d[l], conv_w=conv_w[l], w_o_a=w_o_a[l], w_o_b=w_o_b[l], w_o_c=w_o_c[l],
                  w_o_d=w_o_d[l], w_out=w_out[l])
        if last:
            parts = _project(ctx, scale_c, shift_c, lp["norm_g"], lp["w_ext"], consts["cs_bd"], lp["head_gains"],
                             None, min(256, lc))
            ctx_kv = (parts[3], parts[4], parts[8], parts[9])
        else:
            ctx, ctx_kv = _mix(ctx, scale_c, shift_c, gate_c, None, lp, consts, lam, lam_init, None, dft_c,
                               min(256, lc), min(256, lc), tk, False)
        x, _ = _mix(x, scale, shift, gate, ctx_kv, lp, consts, lam, lam_init, rope, dft_x, tm, tq, tk, True)
    return x
```

```python
import functools
import math

import jax
import jax.numpy as jnp
import numpy as np
from jax import lax
from jax.experimental import pallas as pl
from jax.experimental.pallas import tpu as pltpu

HEAD_DIM = 64
LANES = 128
FOURIER_W = 384
SWA_HEADS = 8
SWA_KV = 2
SWA_W = SWA_HEADS * HEAD_DIM
WINDOW = 128
BLOCK = 128
CONV_W = 384
CONV_K = 3
DIFF_HEADS = 4
DIFF_W = DIFF_HEADS * 2 * HEAD_DIM
N_BRANCH = 4
ROPE_BASE = 10000.0
NORM_EPS = 1e-6
NEG_INF = -1e30
GRID_W = 64
DFT_MINOR = 128
SUM_ROWS = 16
LOG2E = math.log2(math.e)
SAFE_SHIFT = 60.0
VMEM_LIMIT = 56 * 1024 * 1024


def _bf(x):
    return x.astype(jnp.bfloat16)


def _dot(a, b):
    return jnp.dot(a, b, preferred_element_type=jnp.float32)


def _dot_nt(a, b):
    return lax.dot_general(a, b, (((1,), (1,)), ((), ())), preferred_element_type=jnp.float32)


def _params(n_axes, flags=None):
    return pltpu.CompilerParams(dimension_semantics=("arbitrary",) * n_axes, vmem_limit_bytes=VMEM_LIMIT,
                                flags=flags)


def _const_spec(shape):
    nd = len(shape)
    return pl.BlockSpec(shape, lambda *_: (0,) * nd, pipeline_mode=pl.Buffered(1))


def _mod_kernel(c_ref, w_ref, b_ref, lam_ref, mod_ref, lamo_ref, *, lam_init):
    c = c_ref[...]
    s = c * jax.nn.sigmoid(c)
    mod_ref[...] = _dot(_bf(s), w_ref[...]) + b_ref[...]
    lv = lam_ref[...]
    a1 = jnp.sum(lv[0:1, :] * lv[1:2, :], axis=-1, keepdims=True)
    a2 = jnp.sum(lv[2:3, :] * lv[3:4, :], axis=-1, keepdims=True)
    lam = jnp.exp(a1) - jnp.exp(a2) + lam_init
    lamo_ref[...] = jnp.broadcast_to(lam, lamo_ref.shape)


def _modulation(c_rows, w_mod, b_mod, lam_vecs, lam_init):
    r, d = c_rows.shape
    return pl.pallas_call(
        functools.partial(_mod_kernel, lam_init=lam_init),
        out_shape=(jax.ShapeDtypeStruct((r, 3 * d), jnp.float32),
                   jax.ShapeDtypeStruct((8, LANES), jnp.float32)),
        name="mod",
    )(c_rows, _bf(w_mod), b_mod.reshape(1, 3 * d), lam_vecs)


def _head_norm_rope(t, gain, cos, sin_s, lane_lo, scale):
    sq = t * t
    first = lax.broadcasted_iota(jnp.int32, t.shape, 1) < HEAD_DIM
    ms = jnp.where(first, jnp.sum(jnp.where(first, sq, 0.0), axis=-1, keepdims=True),
                   jnp.sum(jnp.where(first, 0.0, sq), axis=-1, keepdims=True)) * (1.0 / HEAD_DIM)
    y = t * lax.rsqrt(ms + NORM_EPS) * gain
    if cos is not None:
        rot = jnp.where(lane_lo, pltpu.roll(y, LANES - 16, 1), pltpu.roll(y, 16, 1))
        y = y * cos + rot * sin_s
    if scale != 1.0:
        y = y * scale
    return y


def _proj_kernel(*refs, use_rope, tm):
    if use_rope:
        (x_ref, sc_ref, sh_ref, g_ref, w_ref, cs_ref, hg_ref, cos_ref, sin_ref), outs = refs[:9], refs[9:]
    else:
        (x_ref, sc_ref, sh_ref, g_ref, w_ref, cs_ref, hg_ref), outs = refs[:7], refs[7:]
        cos_ref = sin_ref = None
    (zr_ref, zi_ref, qb_ref, kb_ref, vb_ref, uc_ref, cb_ref, qd_ref, kd_ref, vd_ref,
     sza_ref, szb_ref, szc_ref, szd_ref, gate_ref) = outs

    x = x_ref[...]
    y = x * lax.rsqrt(jnp.mean(x * x, axis=-1, keepdims=True) + NORM_EPS) * g_ref[...]
    h = _bf(y * (1.0 + sc_ref[...]) + sh_ref[...])

    def proj(c0, width):
        return _dot(h, w_ref[:, c0:c0 + width])

    col = 0
    a = _bf(proj(col, FOURIER_W))
    zz = _dot(a, cs_ref[...])
    zr_ref[...] = zz[:, :FOURIER_W].astype(zr_ref.dtype)
    zi_ref[...] = zz[:, FOURIER_W:].astype(zi_ref.dtype)
    col += FOURIER_W

    if use_rope:
        cos = cos_ref[...]
        sin_s = sin_ref[...]
    else:
        cos = sin_s = None
    lane = lax.broadcasted_iota(jnp.int32, (tm, LANES), 1)
    lane_lo = (lane & 31) < 16
    qscale = HEAD_DIM ** -0.5

    def normed(c0, width, gain_row, out_ref, scale):
        t = proj(c0, width)
        gain = hg_ref[gain_row:gain_row + 1, :]
        for s in range(width // LANES):
            ts = t[:, s * LANES:(s + 1) * LANES]
            out_ref[:, s * LANES:(s + 1) * LANES] = _bf(
                _head_norm_rope(ts, gain, cos, sin_s, lane_lo, scale))

    head_lo = lane < HEAD_DIM

    def store_duplicated(pair, out_ref):
        swapped = pltpu.roll(pair, HEAD_DIM, 1)
        out_ref[:, :LANES] = _bf(jnp.where(head_lo, pair, swapped))
        out_ref[:, LANES:] = _bf(jnp.where(head_lo, swapped, pair))

    normed(col, SWA_W, 0, qb_ref, qscale * LOG2E); col += SWA_W
    store_duplicated(_head_norm_rope(proj(col, LANES), hg_ref[1:2, :], cos, sin_s, lane_lo, 1.0), kb_ref)
    col += LANES
    store_duplicated(proj(col, LANES), vb_ref); col += LANES
    c3 = proj(col, 3 * CONV_W); col += 3 * CONV_W
    uc_ref[...] = _bf(c3[:, 2 * CONV_W:] * c3[:, :CONV_W])
    cb_ref[...] = _bf(c3[:, CONV_W:2 * CONV_W])
    normed(col, DIFF_W, 2, qd_ref, qscale * LOG2E); col += DIFF_W
    normed(col, DIFF_W, 3, kd_ref, 1.0); col += DIFF_W
    vd_ref[...] = _bf(proj(col, DIFF_W)); col += DIFF_W
    for ref, width in ((sza_ref, FOURIER_W), (szb_ref, SWA_W), (szc_ref, CONV_W), (szd_ref, DIFF_W)):
        z = proj(col, width)
        ref[...] = _bf(z * jax.nn.sigmoid(z))
        col += width
    d = x.shape[-1]
    for j in range(N_BRANCH):
        gate_ref[:, j * d:(j + 1) * d] = _bf(jax.nn.sigmoid(proj(col, d)))
        col += d


_PROJ_OUT_W = (FOURIER_W, FOURIER_W, SWA_W, 2 * LANES, 2 * LANES, CONV_W, CONV_W, DIFF_W, DIFF_W, DIFF_W,
               FOURIER_W, SWA_W, CONV_W, DIFF_W)


def _project(x, scale, shift, norm_g, w_ext, cs_bd, head_gains, rope, tm):
    b, n, d = x.shape
    use_rope = rope is not None
    tok = lambda w: pl.BlockSpec((None, tm, w), lambda bi, i: (bi, i, 0))
    per_b = pl.BlockSpec((None, 1, d), lambda bi, i: (bi, 0, 0))
    in_specs = [tok(d), per_b, per_b, _const_spec((1, d)), _const_spec(w_ext.shape), _const_spec(cs_bd.shape),
                _const_spec(head_gains.shape)]
    args = [x, scale, shift, norm_g.reshape(1, d), w_ext, cs_bd, head_gains]
    if use_rope:
        tab = pl.BlockSpec((tm, LANES), lambda bi, i: (i, 0))
        in_specs += [tab, tab]
        args += list(rope)
    widths = _PROJ_OUT_W + (N_BRANCH * d,)
    dtypes = [jnp.float32 if (j < 2 and n > 2 * DFT_MINOR) else jnp.bfloat16 for j in range(len(widths))]
    return pl.pallas_call(
        functools.partial(_proj_kernel, use_rope=use_rope, tm=tm),
        grid=(b, n // tm),
        in_specs=in_specs,
        out_specs=[tok(w) for w in widths],
        out_shape=[jax.ShapeDtypeStruct((b, n, w), dt) for w, dt in zip(widths, dtypes)],
        compiler_params=_params(2),
        name="proj",
    )(*args)


def _dft1_kernel(zr_ref, zi_ref, f_ref, tc_ref, ts_ref, o_ref, *, r, tn2):
    for j in range(tn2):
        z = _bf(jnp.concatenate([zr_ref[:, j, :], zi_ref[:, j, :]], axis=0))
        a = _dot(f_ref[...], z)
        ar, ai = a[:r], a[r:]
        tc = jnp.concatenate([tc_ref[j]] * (FOURIER_W // LANES), axis=-1)
        ts = jnp.concatenate([ts_ref[j]] * (FOURIER_W // LANES), axis=-1)
        o_ref[0, j] = _bf(ar * tc + ai * ts)
        o_ref[1, j] = _bf(ai * tc - ar * ts)


def _left_matmul_kernel(m_ref, x_ref, o_ref):
    o_ref[...] = _bf(_dot(m_ref[...], x_ref[...]))


def _left_matmul(mat, x, tc):
    b, k, c = x.shape
    rows = mat.shape[0]
    return pl.pallas_call(
        _left_matmul_kernel,
        grid=(b, c // tc),
        in_specs=[_const_spec(mat.shape), pl.BlockSpec((None, k, tc), lambda bi, i: (bi, 0, i))],
        out_specs=pl.BlockSpec((None, rows, tc), lambda bi, i: (bi, 0, i)),
        out_shape=jax.ShapeDtypeStruct((b, rows, c), jnp.bfloat16),
        compiler_params=_params(2),
        name="dft2",
    )(mat, x)


def _dft_tables(n):
    inv = 1.0 / math.sqrt(n)
    if n <= 2 * DFT_MINOR:
        k = np.arange(n)
        ang = 2.0 * np.pi * ((k[:, None] * k[None, :]) % n) / n
        return dict(direct=jnp.asarray(np.concatenate([np.cos(ang), np.sin(ang)], axis=1) * inv, jnp.bfloat16))
    r = n // DFT_MINOR
    k1 = np.arange(r)
    a1 = 2.0 * np.pi * ((k1[:, None] * k1[None, :]) % r) / r
    c1, s1 = np.cos(a1), np.sin(a1)
    f1 = np.block([[c1, s1], [-s1, c1]])
    n2 = np.arange(DFT_MINOR)
    at = 2.0 * np.pi * (n2[:, None] * k1[None, :]) / n
    tw_c = np.repeat(np.cos(at)[:, :, None], LANES, axis=2)
    tw_s = np.repeat(np.sin(at)[:, :, None], LANES, axis=2)
    a2 = 2.0 * np.pi * ((n2[:, None] * n2[None, :]) % DFT_MINOR) / DFT_MINOR
    f2 = np.concatenate([np.cos(a2), np.sin(a2)], axis=1) * inv
    return dict(f1=jnp.asarray(f1, jnp.bfloat16), tw_c=jnp.asarray(tw_c, jnp.float32),
                tw_s=jnp.asarray(tw_s, jnp.float32), f2=jnp.asarray(f2, jnp.bfloat16))


def _fourier_positions(zr, zi, tabs):
    b, n, w = zr.shape
    if "direct" in tabs:
        return _left_matmul(tabs["direct"], jnp.concatenate([zr, zi], axis=1), w)
    r = n // DFT_MINOR
    tn2 = 8
    zin = pl.BlockSpec((None, r, tn2, w), lambda bi, i: (bi, 0, i, 0))
    tw = pl.BlockSpec((tn2, r, LANES), lambda bi, i: (i, 0, 0))
    g = pl.pallas_call(
        functools.partial(_dft1_kernel, r=r, tn2=tn2),
        grid=(b, DFT_MINOR // tn2),
        in_specs=[zin, zin, _const_spec(tabs["f1"].shape), tw, tw],
        out_specs=pl.BlockSpec((None, 2, tn2, r, w), lambda bi, i: (bi, 0, i, 0, 0)),
        out_shape=jax.ShapeDtypeStruct((b, 2, DFT_MINOR, r, w), jnp.bfloat16),
        compiler_params=_params(2),
        name="dft1",
    )(zr.reshape(b, r, DFT_MINOR, w), zi.reshape(b, r, DFT_MINOR, w), tabs["f1"], tabs["tw_c"], tabs["tw_s"])
    y = _left_matmul(tabs["f2"], g.reshape(b, 2 * DFT_MINOR, r * w), (16 if r % 16 == 0 else 4) * w)
    return y.reshape(b, n, w)


def _win_kernel(*refs, has_local, nb, nq):
    if has_local:
        bound_ref, sink_ref, q_ref, kp_ref, kc_ref, kn_ref, vp_ref, vc_ref, vn_ref, kx_ref, vx_ref, o_ref = refs
    else:
        bound_ref, sink_ref, q_ref, kx_ref, vx_ref, o_ref = refs
    i = pl.program_id(1)
    tq = BLOCK
    group = SWA_HEADS // SWA_KV
    lane = lax.broadcasted_iota(jnp.int32, (1, LANES), 1)
    halves = (lane < HEAD_DIM, lane >= HEAD_DIM)
    row_lo = lax.broadcasted_iota(jnp.int32, (LANES, 1), 0) < HEAD_DIM
    if has_local:
        kr = lax.broadcasted_iota(jnp.int32, (3 * BLOCK, tq), 0)
        qc = lax.broadcasted_iota(jnp.int32, (3 * BLOCK, tq), 1)
        in_band = jnp.abs(kr - BLOCK - qc) <= WINDOW
    bound = bound_ref[0, 0]

    def attend(s, j, use_bound):
        sl = slice(j * LANES, (j + 1) * LANES)
        rows = slice(s * BLOCK, (s + 1) * BLOCK)
        if has_local:
            def window(prev_ref, cur_ref, next_ref):
                first = prev_ref[:, sl] if s == 0 else cur_ref[(s - 1) * BLOCK:s * BLOCK, sl]
                last = next_ref[:, sl] if s == nq - 1 else cur_ref[(s + 1) * BLOCK:(s + 2) * BLOCK, sl]
                return [first, cur_ref[rows, sl], last]
            kd = jnp.concatenate(window(kp_ref, kc_ref, kn_ref) + [kx_ref[:, sl]], axis=0)
            vd = jnp.concatenate(window(vp_ref, vc_ref, vn_ref) + [vx_ref[:, sl]], axis=0)
        else:
            kd = kx_ref[:, sl]
            vd = vx_ref[:, sl]
        q_stack = []
        for t in range(group // 2):
            slab = j * (group // 2) + t
            q2 = q_ref[rows, slab * LANES:(slab + 1) * LANES]
            q_stack += [jnp.where(halves[e], q2, jnp.zeros_like(q2)) for e in range(2)]
        st = _dot_nt(kd, jnp.concatenate(q_stack, axis=0))
        if has_local:
            kpos = (i * nq + s - 1) * BLOCK + kr
            bias = jnp.where(in_band & (kpos >= 0) & (kpos < nb * BLOCK), 0.0, NEG_INF)
            st = jnp.concatenate([st[:3 * BLOCK] + jnp.concatenate([bias] * group, axis=1), st[3 * BLOCK:]], axis=0)
        sk = sink_ref[j:j + 1, :]
        m = jnp.maximum(bound, sk) if use_bound else jnp.maximum(jnp.max(st, axis=0, keepdims=True), sk)
        p = jnp.exp2(st - m)
        inv = 1.0 / (jnp.sum(p, axis=0, keepdims=True) + jnp.exp2(sk - m))
        ot = lax.dot_general(vd, _bf(p), (((0,), (0,)), ((), ())), preferred_element_type=jnp.float32) * inv
        for t in range(group // 2):
            slab = j * (group // 2) + t
            pair = jnp.where(row_lo, ot[:, (2 * t) * tq:(2 * t + 1) * tq], ot[:, (2 * t + 1) * tq:(2 * t + 2) * tq])
            o_ref[rows, slab * LANES:(slab + 1) * LANES] = _bf(pair.T)

    shift_safe = bound <= SAFE_SHIFT

    @pl.when(shift_safe)
    def _():
        for s in range(nq):
            for j in range(SWA_KV):
                attend(s, j, True)

    @pl.when(jnp.logical_not(shift_safe))
    def _():
        for s in range(nq):
            for j in range(SWA_KV):
                attend(s, j, False)


def _score_bound(q_gain, k_gain):
    g = jnp.max(jnp.abs(q_gain.astype(jnp.float32))) * jnp.max(jnp.abs(k_gain.astype(jnp.float32)))
    return (1.02 * HEAD_DIM ** 0.5 * LOG2E * g).reshape(1, 1)


def _window_attention(qb, kb, vb, kx, vx, sink, bound, has_local):
    b, n, _ = qb.shape
    nb = n // BLOCK
    nq = 4 if nb % 4 == 0 else 1
    lx = kx.shape[1]
    qspec = pl.BlockSpec((None, nq * BLOCK, SWA_W), lambda bi, i: (bi, i, 0))
    ctx = pl.BlockSpec((None, lx, 2 * LANES), lambda bi, i: (bi, 0, 0))
    sink_rows = jnp.repeat(sink.astype(jnp.float32) * LOG2E, BLOCK).reshape(SWA_KV, -1)
    args = [bound, sink_rows, qb]
    specs = [pl.BlockSpec(memory_space=pltpu.SMEM), _const_spec(sink_rows.shape), qspec]
    if has_local:
        prv = pl.BlockSpec((None, BLOCK, 2 * LANES), lambda bi, i: (bi, jnp.maximum(i * nq - 1, 0), 0))
        cur = pl.BlockSpec((None, nq * BLOCK, 2 * LANES), lambda bi, i: (bi, i, 0))
        nxt = pl.BlockSpec((None, BLOCK, 2 * LANES), lambda bi, i: (bi, jnp.minimum((i + 1) * nq, nb - 1), 0))
        args += [kb, kb, kb, vb, vb, vb]
        specs += [prv, cur, nxt, prv, cur, nxt]
    args += [kx, vx]
    specs += [ctx, ctx]
    return pl.pallas_call(
        functools.partial(_win_kernel, has_local=has_local, nb=nb, nq=nq),
        grid=(b, nb // nq),
        in_specs=specs,
        out_specs=qspec,
        out_shape=jax.ShapeDtypeStruct((b, n, SWA_W), jnp.bfloat16),
        compiler_params=_params(2),
        name="win",
    )(*args)


def _diff_kernel(lam_ref, bound_ref, q_ref, *refs, n_parts, tk, out_scale):
    k_parts, v_parts = refs[:n_parts], refs[n_parts:2 * n_parts]
    g_ref, o_ref, k_ref, vt_ref, acc_ref, s0_ref, s1_ref, p0_ref, p1_ref = refs[2 * n_parts:]
    qi = pl.program_id(2)
    tq = q_ref.shape[0]
    n_chunks = vt_ref.shape[0]

    @pl.when(qi == 0)
    def _():
        row = 0
        for kp, vp in zip(k_parts, v_parts):
            rows = kp.shape[0]
            k_ref[row:row + rows, :] = kp[...]
            done = 0
            while done < rows:
                ci, off = divmod(row + done, tk)
                take = min(tk - off, rows - done)
                vt_ref[ci, :LANES, off:off + take] = _bf(vp[done:done + take, :].astype(jnp.float32).T)
                done += take
            row += rows

        vt_ref[:, LANES:, :] = jnp.ones((n_chunks, SUM_ROWS, tk), jnp.bfloat16)

    q = q_ref[...]
    lane = lax.broadcasted_iota(jnp.int32, (1, LANES), 1)
    qm = (jnp.where(lane < HEAD_DIM, q, jnp.zeros_like(q)), jnp.where(lane >= HEAD_DIM, q, jnp.zeros_like(q)))
    acc_ref[...] = jnp.zeros_like(acc_ref)
    s_bufs = (s0_ref, s1_ref)
    p_bufs = (p0_ref, p1_ref)

    bound = bound_ref[0, 0]
    shift_safe = bound <= SAFE_SHIFT

    @pl.when(shift_safe)
    def _():
        q_both = jnp.concatenate(qm, axis=0)

        def chunk(ci, denom):
            c0 = pl.multiple_of(ci * tk, tk)
            p = jnp.exp2(_dot_nt(k_ref[pl.ds(c0, tk), :], q_both) - bound)
            acc_ref[:LANES, :] += _dot(vt_ref[ci, :LANES, :], _bf(p))
            return denom + jnp.sum(p, axis=0, keepdims=True)
        denom = lax.fori_loop(0, n_chunks, chunk, jnp.zeros((1, 2 * tq), jnp.float32), unroll=True)
        acc_ref[LANES:, :] = jnp.broadcast_to(denom, (SUM_ROWS, 2 * tq))

    @pl.when(jnp.logical_not(shift_safe))
    def _():
        _diff_online(qm, k_ref, vt_ref, acc_ref, s_bufs, p_bufs, tk=tk, tq=tq, n_chunks=n_chunks)

    lam = lam_ref[0, 0]
    a1 = acc_ref[:, :tq]
    a2 = acc_ref[:, tq:]
    o = a1[:LANES] * (1.0 / a1[LANES:LANES + 1]) - lam * (a2[:LANES] * (1.0 / a2[LANES:LANES + 1]))
    y = o * lax.rsqrt(jnp.mean(o * o, axis=0, keepdims=True) + NORM_EPS) * g_ref[...] * out_scale
    o_ref[...] = _bf(y.T)


def _diff_online(qm, k_ref, vt_ref, acc_ref, s_bufs, p_bufs, *, tk, tq, n_chunks):
    def scores(ci, slot):
        c0 = ci * tk if isinstance(ci, int) else pl.multiple_of(ci * tk, tk)
        k = k_ref[pl.ds(c0, tk), :]
        mx = []
        for c in range(2):
            st = _dot_nt(k, qm[c])
            s_bufs[slot][c] = st
            mx.append(jnp.max(st, axis=0, keepdims=True))
        return tuple(mx)

    def probs(slot, mx, m_run):
        m_new, alpha = [], []
        for c in range(2):
            m = jnp.maximum(m_run[c], mx[c])
            alpha.append(jnp.exp2(m_run[c] - m))
            p_bufs[slot][c] = _bf(jnp.exp2(s_bufs[slot][c] - m))
            m_new.append(m)
        return tuple(m_new), tuple(alpha)

    def accumulate(ci, slot, alpha):
        vt = vt_ref[ci]
        for c in range(2):
            cols = slice(c * tq, (c + 1) * tq)
            acc_ref[:, cols] = alpha[c] * acc_ref[:, cols] + _dot(vt, p_bufs[slot][c])

    def step(t, par, do_scores, do_probs, do_acc, state):
        m_run, mx, alpha = state
        mx_next = scores(t + 2, par) if do_scores else mx
        if do_probs:
            m_run, alpha_next = probs(1 - par, mx, m_run)
        else:
            alpha_next = alpha
        if do_acc:
            accumulate(t, par, alpha)
        return m_run, mx_next, alpha_next

    neg = jnp.full((1, tq), NEG_INF, jnp.float32)
    one = jnp.ones((1, tq), jnp.float32)
    state = ((neg, neg), (neg, neg), (one, one))
    for t in (-2, -1):
        state = step(t, t % 2, t + 2 < n_chunks, 0 <= t + 1 < n_chunks, False, state)
    n_steady = max(n_chunks - 2, 0)

    def pair(j, state):
        state = step(2 * j, 0, True, True, True, state)
        return step(2 * j + 1, 1, True, True, True, state)

    state = lax.fori_loop(0, n_steady // 2, pair, state)
    if n_steady % 2:
        state = step(n_steady - 1, 0, True, True, True, state)
    for t in range(n_steady, n_chunks):
        state = step(t, t % 2, False, t + 1 < n_chunks, True, state)


def _pick_tk(nk, cap):
    best = LANES
    for t in range(LANES, min(cap, nk) + 1, LANES):
        if nk % t == 0:
            best = t
    return best


def _diff_attention(qd, k_parts, v_parts, lam, bound, subln_g, lam_init, tq, tk_cap):
    b, n, _ = qd.shape
    nk = sum(k.shape[1] for k in k_parts)
    tk = _pick_tk(nk, tk_cap)
    smem = pl.BlockSpec(memory_space=pltpu.SMEM)
    qspec = pl.BlockSpec((None, tq, LANES), lambda bi, h, i: (bi, i, h))
    kvspecs = [pl.BlockSpec((None, k.shape[1], LANES), lambda bi, h, i: (bi, 0, h)) for k in k_parts]
    gain = jnp.broadcast_to(subln_g.astype(jnp.float32)[:, None], (LANES, tq))
    return pl.pallas_call(
        functools.partial(_diff_kernel, n_parts=len(k_parts), tk=tk, out_scale=1.0 - lam_init),
        grid=(b, DIFF_HEADS, n // tq),
        in_specs=[smem, smem, qspec] + kvspecs + kvspecs + [_const_spec((LANES, tq))],
        out_specs=qspec,
        out_shape=jax.ShapeDtypeStruct((b, n, DIFF_W), jnp.bfloat16),
        scratch_shapes=[pltpu.VMEM((nk, LANES), jnp.bfloat16),
                        pltpu.VMEM((nk // tk, LANES + SUM_ROWS, tk), jnp.bfloat16),
                        pltpu.VMEM((LANES + SUM_ROWS, 2 * tq), jnp.float32),
                        pltpu.VMEM((2, tk, tq), jnp.float32), pltpu.VMEM((2, tk, tq), jnp.float32),
                        pltpu.VMEM((2, tk, tq), jnp.bfloat16), pltpu.VMEM((2, tk, tq), jnp.bfloat16)],
        compiler_params=_params(3),
        name="diff",
    )(lam, bound, qd, *k_parts, *v_parts, gain)


def _merge_kernel(x_ref, gt_ref, ya_ref, sza_ref, yb_ref, szb_ref, uc_ref, up_ref, un_ref, cb_ref, szc_ref,
                  yd_ref, szd_ref, g_ref, cw_ref, wa_ref, wb_ref, wc_ref, wd_ref, wo_ref, o_ref, *, tm, nt):
    i = pl.program_id(1)
    f32 = jnp.float32
    u = uc_ref[...].astype(f32)
    row = lax.broadcasted_iota(jnp.int32, (tm, 1), 0)
    prev_row = jnp.where(i > 0, up_ref[7:8, :].astype(f32), 0.0)
    next_row = jnp.where(i < nt - 1, un_ref[0:1, :].astype(f32), 0.0)
    u_prev = jnp.where(row == 0, prev_row, pltpu.roll(u, 1, 0))
    u_next = jnp.where(row == tm - 1, next_row, pltpu.roll(u, tm - 1, 0))
    cw = cw_ref[...]
    conv = u_prev * cw[0:1, :] + u * cw[1:2, :] + u_next * cw[2:3, :]
    yc = cb_ref[...].astype(f32) * conv
    d = x_ref.shape[-1]
    branches = ((ya_ref[...].astype(f32), sza_ref, wa_ref), (yb_ref[...].astype(f32), szb_ref, wb_ref),
                (yc, szc_ref, wc_ref), (yd_ref[...].astype(f32), szd_ref, wd_ref))
    mixed = jnp.zeros((tm, d), f32)
    for j, (y, sz_ref, w_ref) in enumerate(branches):
        t = _dot(_bf(y * sz_ref[...].astype(f32)), w_ref[...])
        mixed = mixed + g_ref[:, j * d:(j + 1) * d].astype(f32) * t
    out = _dot(_bf(mixed), wo_ref[...])
    o_ref[...] = x_ref[...] + gt_ref[...] * out


def _merge(x, gate, ya, sza, yb, szb, uc, cb, szc, yd, szd, g, conv_w, w_a, w_b, w_c, w_d, w_out, tm):
    b, n, d = x.shape
    nt = n // tm
    tok = lambda w: pl.BlockSpec((None, tm, w), lambda bi, i: (bi, i, 0))
    hb = tm // 8
    halo_p = pl.BlockSpec((None, 8, CONV_W), lambda bi, i: (bi, jnp.maximum(i * hb - 1, 0), 0))
    halo_n = pl.BlockSpec((None, 8, CONV_W), lambda bi, i: (bi, jnp.minimum((i + 1) * hb, n // 8 - 1), 0))
    per_b = pl.BlockSpec((None, 1, d), lambda bi, i: (bi, 0, 0))
    ws = [_bf(w_a), _bf(w_b), _bf(w_c), _bf(w_d), _bf(w_out)]
    return pl.pallas_call(
        functools.partial(_merge_kernel, tm=tm, nt=nt),
        grid=(b, nt),
        in_specs=[tok(d), per_b, tok(FOURIER_W), tok(FOURIER_W), tok(SWA_W), tok(SWA_W), tok(CONV_W), halo_p, halo_n,
                  tok(CONV_W), tok(CONV_W), tok(DIFF_W), tok(DIFF_W), tok(N_BRANCH * d), _const_spec(conv_w.shape)]
                 + [_const_spec(w.shape) for w in ws],
        out_specs=tok(d),
        out_shape=jax.ShapeDtypeStruct((b, n, d), jnp.float32),
        compiler_params=_params(2),
        name="merge",
    )(x, gate, ya, sza, yb, szb, uc, uc, uc, cb, szc, yd, szd, g, conv_w.astype(jnp.float32), *ws)


def _rope_tables(n):
    rows = n // GRID_W
    row = jnp.broadcast_to(jnp.arange(rows, dtype=jnp.float32)[:, None], (rows, GRID_W)).reshape(-1)
    col = jnp.broadcast_to(jnp.arange(GRID_W, dtype=jnp.float32)[None, :], (rows, GRID_W)).reshape(-1)
    nf = HEAD_DIM // 4
    inv = ROPE_BASE ** (-jnp.arange(nf, dtype=jnp.float32) / nf)
    ar = row[:, None] * inv[None, :]
    ac = col[:, None] * inv[None, :]
    cos = jnp.concatenate([jnp.cos(ar), jnp.cos(ar), jnp.cos(ac), jnp.cos(ac)], axis=-1)
    sin = jnp.concatenate([jnp.sin(ar), jnp.sin(ar), jnp.sin(ac), jnp.sin(ac)], axis=-1)
    sign = jnp.where((jnp.arange(HEAD_DIM) % 32) < 16, -1.0, 1.0)
    return jnp.tile(cos, (1, 2)), jnp.tile(sin * sign, (1, 2))


def _channel_dft():
    k = np.arange(HEAD_DIM)
    ang = 2.0 * np.pi * ((k[:, None] * k[None, :]) % HEAD_DIM) / HEAD_DIM
    eye = np.eye(FOURIER_W // HEAD_DIM)
    inv = 1.0 / math.sqrt(HEAD_DIM)
    return jnp.asarray(np.concatenate([np.kron(eye, np.cos(ang)), -np.kron(eye, np.sin(ang))], axis=1) * inv,
                       jnp.bfloat16)


def _mix(stream, scale, shift, gate, ctx_kv, lp, consts, lam, lam_init, rope, dft, tm, tq, tk, has_local):
    parts = _project(stream, scale, shift, lp["norm_g"], lp["w_ext"], consts["cs_bd"], lp["head_gains"], rope, tm)
    zr, zi, qb, kb, vb, uc, cb, qd, kd, vd, sza, szb, szc, szd, g = parts
    if ctx_kv is None:
        kx_b, vx_b, k_parts, v_parts = kb, vb, (kd,), (vd,)
    else:
        kx_b, vx_b, kx_d, vx_d = ctx_kv
        k_parts, v_parts = (kx_d, kd), (vx_d, vd)
    ya = _fourier_positions(zr, zi, dft)
    yb = _window_attention(qb, kb, vb, kx_b, vx_b, lp["sink"], lp["bound_b"], has_local)
    yd = _diff_attention(qd, k_parts, v_parts, lam, lp["bound_d"], lp["subln"], lam_init, tq, tk)
    new = _merge(stream, gate, ya, sza, yb, szb, uc, cb, szc, yd, szd, g, lp["conv_w"],
                 lp["w_o_a"], lp["w_o_b"], lp["w_o_c"], lp["w_o_d"], lp["w_out"], tm)
    return new, (kb, vb, kd, vd)


def kernel(x, c, ctx, c_ctx, norm_g, w_mod, b_mod, w_in, q_norm_b, k_norm_b, sink_b, conv_w, q_norm_d, k_norm_d,
           lam_q1, lam_k1, lam_q2, lam_k2, subln_d, w_o_a, w_o_b, w_o_c, w_o_d, w_out):
    b, n, d = x.shape
    lc = ctx.shape[1]
    depth = w_in.shape[0]
    rope = _rope_tables(n)
    consts = dict(cs_bd=_channel_dft())
    dft_x = _dft_tables(n)
    dft_c = _dft_tables(lc)
    c_rows = jnp.zeros((8, d), jnp.float32).at[:b].set(c).at[b].set(c_ctx)
    tm = min(512, n)
    tq = min(512, n)
    tk = 1280
    for l in range(depth):
        last = l == depth - 1
        lam_init = 0.8 - 0.6 * math.exp(-0.3 * l)
        lam_vecs = jnp.stack([lam_q1[l], lam_k1[l], lam_q2[l], lam_k2[l]]).astype(jnp.float32)
        mod, lam_o = _modulation(c_rows, w_mod[l], b_mod[l], lam_vecs, lam_init)
        lam = lam_o[0:1, 0:1]
        shift, scale, gate = (mod[:b, None, j * d:(j + 1) * d] for j in range(3))
        shift_c, scale_c, gate_c = (jnp.broadcast_to(mod[b:b + 1, None, j * d:(j + 1) * d], (b, 1, d))
                                    for j in range(3))
        tile2 = lambda v: jnp.tile(v.astype(jnp.float32), 2)
        lp = dict(norm_g=norm_g[l], w_ext=_bf(w_in[l]), sink=sink_b[l].astype(jnp.float32),
                  head_gains=jnp.stack([tile2(q_norm_b[l]), tile2(k_norm_b[l]), tile2(q_norm_d[l]), tile2(k_norm_d[l])]),
                  bound_b=_score_bound(q_norm_b[l], k_norm_b[l]), bound_d=_score_bound(q_norm_d[l], k_norm_d[l]),
                  subln=subln_d[l], conv_w=conv_w[l], w_o_a=w_o_a[l], w_o_b=w_o_b[l], w_o_c=w_o_c[l],
                  w_o_d=w_o_d[l], w_out=w_out[l])
        if last:
            parts = _project(ctx, scale_c, shift_c, lp["norm_g"], lp["w_ext"], consts["cs_bd"], lp["head_gains"],
                             None, min(256, lc))
            ctx_kv = (parts[3], parts[4], parts[8], parts[9])
        else:
            ctx, ctx_kv = _mix(ctx, scale_c, shift_c, gate_c, None, lp, consts, lam, lam_init, None, dft_c,
                               min(256, lc), min(256, lc), tk, False)
        x, _ = _mix(x, scale, shift, gate, ctx_kv, lp, consts, lam, lam_init, rope, dft_x, tm, tq, tk, True)
    return x
```

```python
import functools
import math

import jax
import jax.numpy as jnp
import numpy as np
from jax import lax
from jax.experimental import pallas as pl
from jax.experimental.pallas import tpu as pltpu

HEAD_DIM = 64
LANES = 128
FOURIER_W = 384
SWA_HEADS = 8
SWA_KV = 2
SWA_W = SWA_HEADS * HEAD_DIM
WINDOW = 128
BLOCK = 128
CONV_W = 384
DIFF_HEADS = 4
DIFF_W = DIFF_HEADS * 2 * HEAD_DIM
N_BRANCH = 4
ROPE_BASE = 10000.0
NORM_EPS = 1e-6
NEG_INF = -1e30
GRID_W = 64
DFT_MINOR = 128
SUM_ROWS = 16
LOG2E = math.log2(math.e)
SAFE_SHIFT = 60.0
VMEM_LIMIT = 56 * 1024 * 1024


def _bf(x):
    return x.astype(jnp.bfloat16)


def _dot(a, b):
    return jnp.dot(a, b, preferred_element_type=jnp.float32)


def _dot_nt(a, b):
    return lax.dot_general(a, b, (((1,), (1,)), ((), ())), preferred_element_type=jnp.float32)


def _params(n_axes):
    return pltpu.CompilerParams(dimension_semantics=("arbitrary",) * n_axes, vmem_limit_bytes=VMEM_LIMIT)


def _const_spec(shape):
    nd = len(shape)
    return pl.BlockSpec(shape, lambda *_: (0,) * nd, pipeline_mode=pl.Buffered(1))


def _mod_kernel(c_ref, w_ref, b_ref, lam_ref, mod_ref, lamo_ref, *, lam_init):
    c = c_ref[...]
    s = c * jax.nn.sigmoid(c)
    mod_ref[...] = _dot(_bf(s), w_ref[...]) + b_ref[...]
    lv = lam_ref[...]
    a1 = jnp.sum(lv[0:1, :] * lv[1:2, :], axis=-1, keepdims=True)
    a2 = jnp.sum(lv[2:3, :] * lv[3:4, :], axis=-1, keepdims=True)
    lam = jnp.exp(a1) - jnp.exp(a2) + lam_init
    lamo_ref[...] = jnp.broadcast_to(lam, lamo_ref.shape)


def _modulation(c_rows, w_mod, b_mod, lam_vecs, lam_init):
    r, d = c_rows.shape
    return pl.pallas_call(
        functools.partial(_mod_kernel, lam_init=lam_init),
        out_shape=(jax.ShapeDtypeStruct((r, 3 * d), jnp.float32),
                   jax.ShapeDtypeStruct((8, LANES), jnp.float32)),
        name="mod",
    )(c_rows, _bf(w_mod), b_mod.reshape(1, 3 * d), lam_vecs)


def _head_norm_rope(t, gain, cos, sin_s, lane_lo, scale):
    sq = t * t
    first = lax.broadcasted_iota(jnp.int32, t.shape, 1) < HEAD_DIM
    ms = jnp.where(first, jnp.sum(jnp.where(first, sq, 0.0), axis=-1, keepdims=True),
                   jnp.sum(jnp.where(first, 0.0, sq), axis=-1, keepdims=True)) * (1.0 / HEAD_DIM)
    y = t * lax.rsqrt(ms + NORM_EPS) * gain
    if cos is not None:
        rot = jnp.where(lane_lo, pltpu.roll(y, LANES - 16, 1), pltpu.roll(y, 16, 1))
        y = y * cos + rot * sin_s
    if scale != 1.0:
        y = y * scale
    return y


def _proj_kernel(*refs, use_rope, tm):
    if use_rope:
        (x_ref, sc_ref, sh_ref, g_ref, w_ref, cs_ref, hg_ref, cos_ref, sin_ref), outs = refs[:9], refs[9:]
    else:
        (x_ref, sc_ref, sh_ref, g_ref, w_ref, cs_ref, hg_ref), outs = refs[:7], refs[7:]
        cos_ref = sin_ref = None
    (zr_ref, zi_ref, qb_ref, kb_ref, vb_ref, uc_ref, cb_ref, qd_ref, kd_ref, vd_ref,
     sza_ref, szb_ref, szc_ref, szd_ref, gate_ref) = outs

    x = x_ref[...]
    y = x * lax.rsqrt(jnp.mean(x * x, axis=-1, keepdims=True) + NORM_EPS) * g_ref[...]
    h = _bf(y * (1.0 + sc_ref[...]) + sh_ref[...])

    def proj(c0, width):
        return _dot(h, w_ref[:, c0:c0 + width])

    col = 0
    a = _bf(proj(col, FOURIER_W))
    zz = _dot(a, cs_ref[...])
    zr_ref[...] = zz[:, :FOURIER_W].astype(zr_ref.dtype)
    zi_ref[...] = zz[:, FOURIER_W:].astype(zi_ref.dtype)
    col += FOURIER_W

    if use_rope:
        cos = cos_ref[...]
        sin_s = sin_ref[...]
    else:
        cos = sin_s = None
    lane = lax.broadcasted_iota(jnp.int32, (tm, LANES), 1)
    lane_lo = (lane & 31) < 16
    qscale = HEAD_DIM ** -0.5

    def normed(c0, width, gain_row, out_ref, scale):
        t = proj(c0, width)
        gain = hg_ref[gain_row:gain_row + 1, :]
        for s in range(width // LANES):
            ts = t[:, s * LANES:(s + 1) * LANES]
            out_ref[:, s * LANES:(s + 1) * LANES] = _bf(
                _head_norm_rope(ts, gain, cos, sin_s, lane_lo, scale))

    head_lo = lane < HEAD_DIM

    def store_duplicated(pair, out_ref):
        swapped = pltpu.roll(pair, HEAD_DIM, 1)
        out_ref[:, :LANES] = _bf(jnp.where(head_lo, pair, swapped))
        out_ref[:, LANES:] = _bf(jnp.where(head_lo, swapped, pair))

    normed(col, SWA_W, 0, qb_ref, qscale * LOG2E); col += SWA_W
    store_duplicated(_head_norm_rope(proj(col, LANES), hg_ref[1:2, :], cos, sin_s, lane_lo, 1.0), kb_ref)
    col += LANES
    store_duplicated(proj(col, LANES), vb_ref); col += LANES
    c3 = proj(col, 3 * CONV_W); col += 3 * CONV_W
    uc_ref[...] = _bf(c3[:, 2 * CONV_W:] * c3[:, :CONV_W])
    cb_ref[...] = _bf(c3[:, CONV_W:2 * CONV_W])
    normed(col, DIFF_W, 2, qd_ref, qscale * LOG2E); col += DIFF_W
    normed(col, DIFF_W, 3, kd_ref, 1.0); col += DIFF_W
    vd_ref[...] = _bf(proj(col, DIFF_W)); col += DIFF_W
    for ref, width in ((sza_ref, FOURIER_W), (szb_ref, SWA_W), (szc_ref, CONV_W), (szd_ref, DIFF_W)):
        z = proj(col, width)
        ref[...] = _bf(z * jax.nn.sigmoid(z))
        col += width
    d = x.shape[-1]
    for j in range(N_BRANCH):
        gate_ref[:, j * d:(j + 1) * d] = _bf(jax.nn.sigmoid(proj(col, d)))
        col += d


_PROJ_OUT_W = (FOURIER_W, FOURIER_W, SWA_W, 2 * LANES, 2 * LANES, CONV_W, CONV_W, DIFF_W, DIFF_W, DIFF_W,
               FOURIER_W, SWA_W, CONV_W, DIFF_W)


def _project(x, scale, shift, norm_g, w_ext, cs_bd, head_gains, rope, tm):
    b, n, d = x.shape
    use_rope = rope is not None
    tok = lambda w: pl.BlockSpec((None, tm, w), lambda bi, i: (bi, i, 0))
    per_b = pl.BlockSpec((None, 1, d), lambda bi, i: (bi, 0, 0))
    in_specs = [tok(d), per_b, per_b, _const_spec((1, d)), _const_spec(w_ext.shape), _const_spec(cs_bd.shape),
                _const_spec(head_gains.shape)]
    args = [x, scale, shift, norm_g.reshape(1, d), w_ext, cs_bd, head_gains]
    if use_rope:
        tab = pl.BlockSpec((tm, LANES), lambda bi, i: (i, 0))
        in_specs += [tab, tab]
        args += list(rope)
    widths = _PROJ_OUT_W + (N_BRANCH * d,)
    dtypes = [jnp.float32 if (j < 2 and n > 2 * DFT_MINOR) else jnp.bfloat16 for j in range(len(widths))]
    return pl.pallas_call(
        functools.partial(_proj_kernel, use_rope=use_rope, tm=tm),
        grid=(b, n // tm),
        in_specs=in_specs,
        out_specs=[tok(w) for w in widths],
        out_shape=[jax.ShapeDtypeStruct((b, n, w), dt) for w, dt in zip(widths, dtypes)],
        compiler_params=_params(2),
        name="proj",
    )(*args)


def _dft1_kernel(zr_ref, zi_ref, f_ref, tc_ref, ts_ref, o_ref, *, r, tn2):
    for j in range(tn2):
        z = _bf(jnp.concatenate([zr_ref[:, j, :], zi_ref[:, j, :]], axis=0))
        a = _dot(f_ref[...], z)
        ar, ai = a[:r], a[r:]
        tc = jnp.concatenate([tc_ref[j]] * (FOURIER_W // LANES), axis=-1)
        ts = jnp.concatenate([ts_ref[j]] * (FOURIER_W // LANES), axis=-1)
        o_ref[0, j] = _bf(ar * tc + ai * ts)
        o_ref[1, j] = _bf(ai * tc - ar * ts)


def _left_matmul_kernel(m_ref, x_ref, o_ref):
    o_ref[...] = _bf(_dot(m_ref[...], x_ref[...]))


def _left_matmul(mat, x, tc):
    b, k, c = x.shape
    rows = mat.shape[0]
    return pl.pallas_call(
        _left_matmul_kernel,
        grid=(b, c // tc),
        in_specs=[_const_spec(mat.shape), pl.BlockSpec((None, k, tc), lambda bi, i: (bi, 0, i))],
        out_specs=pl.BlockSpec((None, rows, tc), lambda bi, i: (bi, 0, i)),
        out_shape=jax.ShapeDtypeStruct((b, rows, c), jnp.bfloat16),
        compiler_params=_params(2),
        name="dft2",
    )(mat, x)


def _dft_tables(n):
    inv = 1.0 / math.sqrt(n)
    if n <= 2 * DFT_MINOR:
        k = np.arange(n)
        ang = 2.0 * np.pi * ((k[:, None] * k[None, :]) % n) / n
        return dict(direct=jnp.asarray(np.concatenate([np.cos(ang), np.sin(ang)], axis=1) * inv, jnp.bfloat16))
    r = n // DFT_MINOR
    k1 = np.arange(r)
    a1 = 2.0 * np.pi * ((k1[:, None] * k1[None, :]) % r) / r
    c1, s1 = np.cos(a1), np.sin(a1)
    f1 = np.block([[c1, s1], [-s1, c1]])
    n2 = np.arange(DFT_MINOR)
    at = 2.0 * np.pi * (n2[:, None] * k1[None, :]) / n
    tw_c = np.repeat(np.cos(at)[:, :, None], LANES, axis=2)
    tw_s = np.repeat(np.sin(at)[:, :, None], LANES, axis=2)
    a2 = 2.0 * np.pi * ((n2[:, None] * n2[None, :]) % DFT_MINOR) / DFT_MINOR
    f2 = np.concatenate([np.cos(a2), np.sin(a2)], axis=1) * inv
    return dict(f1=jnp.asarray(f1, jnp.bfloat16), tw_c=jnp.asarray(tw_c, jnp.float32),
                tw_s=jnp.asarray(tw_s, jnp.float32), f2=jnp.asarray(f2, jnp.bfloat16))


def _fourier_positions(zr, zi, tabs):
    b, n, w = zr.shape
    if "direct" in tabs:
        return _left_matmul(tabs["direct"], jnp.concatenate([zr, zi], axis=1), w)
    r = n // DFT_MINOR
    tn2 = 8
    zin = pl.BlockSpec((None, r, tn2, w), lambda bi, i: (bi, 0, i, 0))
    tw = pl.BlockSpec((tn2, r, LANES), lambda bi, i: (i, 0, 0))
    g = pl.pallas_call(
        functools.partial(_dft1_kernel, r=r, tn2=tn2),
        grid=(b, DFT_MINOR // tn2),
        in_specs=[zin, zin, _const_spec(tabs["f1"].shape), tw, tw],
        out_specs=pl.BlockSpec((None, 2, tn2, r, w), lambda bi, i: (bi, 0, i, 0, 0)),
        out_shape=jax.ShapeDtypeStruct((b, 2, DFT_MINOR, r, w), jnp.bfloat16),
        compiler_params=_params(2),
        name="dft1",
    )(zr.reshape(b, r, DFT_MINOR, w), zi.reshape(b, r, DFT_MINOR, w), tabs["f1"], tabs["tw_c"], tabs["tw_s"])
    y = _left_matmul(tabs["f2"], g.reshape(b, 2 * DFT_MINOR, r * w), (16 if r % 16 == 0 else 4) * w)
    return y.reshape(b, n, w)


def _win_kernel(*refs, has_local, nb, nq):
    if has_local:
        bound_ref, sink_ref, q_ref, kp_ref, kc_ref, kn_ref, vp_ref, vc_ref, vn_ref, kx_ref, vx_ref, o_ref = refs
    else:
        bound_ref, sink_ref, q_ref, kx_ref, vx_ref, o_ref = refs
    i = pl.program_id(1)
    tq = BLOCK
    group = SWA_HEADS // SWA_KV
    lane = lax.broadcasted_iota(jnp.int32, (1, LANES), 1)
    halves = (lane < HEAD_DIM, lane >= HEAD_DIM)
    row_lo = lax.broadcasted_iota(jnp.int32, (LANES, 1), 0) < HEAD_DIM
    if has_local:
        kr = lax.broadcasted_iota(jnp.int32, (3 * BLOCK, tq), 0)
        qc = lax.broadcasted_iota(jnp.int32, (3 * BLOCK, tq), 1)
        in_band = jnp.abs(kr - BLOCK - qc) <= WINDOW
    bound = bound_ref[0, 0]

    def attend(s, j, use_bound):
        sl = slice(j * LANES, (j + 1) * LANES)
        rows = slice(s * BLOCK, (s + 1) * BLOCK)
        if has_local:
            def window(prev_ref, cur_ref, next_ref):
                first = prev_ref[:, sl] if s == 0 else cur_ref[(s - 1) * BLOCK:s * BLOCK, sl]
                last = next_ref[:, sl] if s == nq - 1 else cur_ref[(s + 1) * BLOCK:(s + 2) * BLOCK, sl]
                return [first, cur_ref[rows, sl], last]
            kd = jnp.concatenate(window(kp_ref, kc_ref, kn_ref) + [kx_ref[:, sl]], axis=0)
            vd = jnp.concatenate(window(vp_ref, vc_ref, vn_ref) + [vx_ref[:, sl]], axis=0)
        else:
            kd = kx_ref[:, sl]
            vd = vx_ref[:, sl]
        q_stack = []
        for t in range(group // 2):
            slab = j * (group // 2) + t
            q2 = q_ref[rows, slab * LANES:(slab + 1) * LANES]
            q_stack += [jnp.where(halves[e], q2, jnp.zeros_like(q2)) for e in range(2)]
        st = _dot_nt(kd, jnp.concatenate(q_stack, axis=0))
        if has_local:
            kpos = (i * nq + s - 1) * BLOCK + kr
            bias = jnp.where(in_band & (kpos >= 0) & (kpos < nb * BLOCK), 0.0, NEG_INF)
            st = jnp.concatenate([st[:3 * BLOCK] + jnp.concatenate([bias] * group, axis=1), st[3 * BLOCK:]], axis=0)
        sk = sink_ref[j:j + 1, :]
        m = jnp.maximum(bound, sk) if use_bound else jnp.maximum(jnp.max(st, axis=0, keepdims=True), sk)
        p = jnp.exp2(st - m)
        inv = 1.0 / (jnp.sum(p, axis=0, keepdims=True) + jnp.exp2(sk - m))
        ot = lax.dot_general(vd, _bf(p), (((0,), (0,)), ((), ())), preferred_element_type=jnp.float32) * inv
        for t in range(group // 2):
            slab = j * (group // 2) + t
            pair = jnp.where(row_lo, ot[:, (2 * t) * tq:(2 * t + 1) * tq], ot[:, (2 * t + 1) * tq:(2 * t + 2) * tq])
            o_ref[rows, slab * LANES:(slab + 1) * LANES] = _bf(pair.T)

    shift_safe = bound <= SAFE_SHIFT

    @pl.when(shift_safe)
    def _():
        for s in range(nq):
            for j in range(SWA_KV):
                attend(s, j, True)

    @pl.when(jnp.logical_not(shift_safe))
    def _():
        for s in range(nq):
            for j in range(SWA_KV):
                attend(s, j, False)


def _score_bound(q_gain, k_gain):
    g = jnp.max(jnp.abs(q_gain.astype(jnp.float32))) * jnp.max(jnp.abs(k_gain.astype(jnp.float32)))
    return (1.02 * HEAD_DIM ** 0.5 * LOG2E * g).reshape(1, 1)


def _window_attention(qb, kb, vb, kx, vx, sink, bound, has_local):
    b, n, _ = qb.shape
    nb = n // BLOCK
    nq = 4 if nb % 4 == 0 else 1
    lx = kx.shape[1]
    qspec = pl.BlockSpec((None, nq * BLOCK, SWA_W), lambda bi, i: (bi, i, 0))
    ctx = pl.BlockSpec((None, lx, 2 * LANES), lambda bi, i: (bi, 0, 0))
    sink_rows = jnp.repeat(sink.astype(jnp.float32) * LOG2E, BLOCK).reshape(SWA_KV, -1)
    args = [bound, sink_rows, qb]
    specs = [pl.BlockSpec(memory_space=pltpu.SMEM), _const_spec(sink_rows.shape), qspec]
    if has_local:
        prv = pl.BlockSpec((None, BLOCK, 2 * LANES), lambda bi, i: (bi, jnp.maximum(i * nq - 1, 0), 0))
        cur = pl.BlockSpec((None, nq * BLOCK, 2 * LANES), lambda bi, i: (bi, i, 0))
        nxt = pl.BlockSpec((None, BLOCK, 2 * LANES), lambda bi, i: (bi, jnp.minimum((i + 1) * nq, nb - 1), 0))
        args += [kb, kb, kb, vb, vb, vb]
        specs += [prv, cur, nxt, prv, cur, nxt]
    args += [kx, vx]
    specs += [ctx, ctx]
    return pl.pallas_call(
        functools.partial(_win_kernel, has_local=has_local, nb=nb, nq=nq),
        grid=(b, nb // nq),
        in_specs=specs,
        out_specs=qspec,
        out_shape=jax.ShapeDtypeStruct((b, n, SWA_W), jnp.bfloat16),
        compiler_params=_params(2),
        name="win",
    )(*args)


def _diff_kernel(lam_ref, bound_ref, q_ref, *refs, n_parts, tk, out_scale):
    k_parts, v_parts = refs[:n_parts], refs[n_parts:2 * n_parts]
    g_ref, o_ref, k_ref, vt_ref, acc_ref, s0_ref, s1_ref, p0_ref, p1_ref = refs[2 * n_parts:]
    qi = pl.program_id(2)
    tq = q_ref.shape[0]
    n_chunks = vt_ref.shape[0]

    @pl.when(qi == 0)
    def _():
        row = 0
        for kp, vp in zip(k_parts, v_parts):
            rows = kp.shape[0]
            k_ref[row:row + rows, :] = kp[...]
            done = 0
            while done < rows:
                ci, off = divmod(row + done, tk)
                take = min(tk - off, rows - done)
                vt_ref[ci, :LANES, off:off + take] = _bf(vp[done:done + take, :].astype(jnp.float32).T)
                done += take
            row += rows

        vt_ref[:, LANES:, :] = jnp.ones((n_chunks, SUM_ROWS, tk), jnp.bfloat16)

    q = q_ref[...]
    lane = lax.broadcasted_iota(jnp.int32, (1, LANES), 1)
    qm = (jnp.where(lane < HEAD_DIM, q, jnp.zeros_like(q)), jnp.where(lane >= HEAD_DIM, q, jnp.zeros_like(q)))
    s_bufs = (s0_ref, s1_ref)
    p_bufs = (p0_ref, p1_ref)

    bound = bound_ref[0, 0]
    shift_safe = bound <= SAFE_SHIFT

    @pl.when(shift_safe)
    def _():
        q_both = jnp.concatenate(qm, axis=0)

        def chunk(ci, carry):
            denom, pv = carry
            c0 = pl.multiple_of(ci * tk, tk)
            p = jnp.exp2(_dot_nt(k_ref[pl.ds(c0, tk), :], q_both) - bound)
            return denom + jnp.sum(p, axis=0, keepdims=True), pv + _dot(vt_ref[ci, :LANES, :], _bf(p))
        init = (jnp.zeros((1, 2 * tq), jnp.float32), jnp.zeros((LANES, 2 * tq), jnp.float32))
        denom, pv = lax.fori_loop(0, n_chunks, chunk, init, unroll=True)
        acc_ref[:LANES, :] = pv
        acc_ref[LANES:, :] = jnp.broadcast_to(denom, (SUM_ROWS, 2 * tq))

    @pl.when(jnp.logical_not(shift_safe))
    def _():
        acc_ref[...] = jnp.zeros_like(acc_ref)
        _diff_online(qm, k_ref, vt_ref, acc_ref, s_bufs, p_bufs, tk=tk, tq=tq, n_chunks=n_chunks)

    lam = lam_ref[0, 0]
    a1 = acc_ref[:, :tq]
    a2 = acc_ref[:, tq:]
    o = a1[:LANES] * (1.0 / a1[LANES:LANES + 1]) - lam * (a2[:LANES] * (1.0 / a2[LANES:LANES + 1]))
    y = o * lax.rsqrt(jnp.mean(o * o, axis=0, keepdims=True) + NORM_EPS) * g_ref[...] * out_scale
    o_ref[...] = _bf(y.T)


def _diff_online(qm, k_ref, vt_ref, acc_ref, s_bufs, p_bufs, *, tk, tq, n_chunks):
    def scores(ci, slot):
        c0 = ci * tk if isinstance(ci, int) else pl.multiple_of(ci * tk, tk)
        k = k_ref[pl.ds(c0, tk), :]
        mx = []
        for c in range(2):
            st = _dot_nt(k, qm[c])
            s_bufs[slot][c] = st
            mx.append(jnp.max(st, axis=0, keepdims=True))
        return tuple(mx)

    def probs(slot, mx, m_run):
        m_new, alpha = [], []
        for c in range(2):
            m = jnp.maximum(m_run[c], mx[c])
            alpha.append(jnp.exp2(m_run[c] - m))
            p_bufs[slot][c] = _bf(jnp.exp2(s_bufs[slot][c] - m))
            m_new.append(m)
        return tuple(m_new), tuple(alpha)

    def accumulate(ci, slot, alpha):
        vt = vt_ref[ci]
        for c in range(2):
            cols = slice(c * tq, (c + 1) * tq)
            acc_ref[:, cols] = alpha[c] * acc_ref[:, cols] + _dot(vt, p_bufs[slot][c])

    def step(t, par, do_scores, do_probs, do_acc, state):
        m_run, mx, alpha = state
        mx_next = scores(t + 2, par) if do_scores else mx
        if do_probs:
            m_run, alpha_next = probs(1 - par, mx, m_run)
        else:
            alpha_next = alpha
        if do_acc:
            accumulate(t, par, alpha)
        return m_run, mx_next, alpha_next

    neg = jnp.full((1, tq), NEG_INF, jnp.float32)
    one = jnp.ones((1, tq), jnp.float32)
    state = ((neg, neg), (neg, neg), (one, one))
    for t in (-2, -1):
        state = step(t, t % 2, t + 2 < n_chunks, 0 <= t + 1 < n_chunks, False, state)
    n_steady = max(n_chunks - 2, 0)

    def pair(j, state):
        state = step(2 * j, 0, True, True, True, state)
        return step(2 * j + 1, 1, True, True, True, state)

    state = lax.fori_loop(0, n_steady // 2, pair, state)
    if n_steady % 2:
        state = step(n_steady - 1, 0, True, True, True, state)
    for t in range(n_steady, n_chunks):
        state = step(t, t % 2, False, t + 1 < n_chunks, True, state)


def _pick_tk(nk, cap):
    best = LANES
    for t in range(LANES, min(cap, nk) + 1, LANES):
        if nk % t == 0:
            best = t
    return best


def _diff_attention(qd, k_parts, v_parts, lam, bound, subln_g, lam_init, tq, tk_cap):
    b, n, _ = qd.shape
    nk = sum(k.shape[1] for k in k_parts)
    tk = _pick_tk(nk, tk_cap)
    smem = pl.BlockSpec(memory_space=pltpu.SMEM)
    qspec = pl.BlockSpec((None, tq, LANES), lambda bi, h, i: (bi, i, h))
    kvspecs = [pl.BlockSpec((None, k.shape[1], LANES), lambda bi, h, i: (bi, 0, h)) for k in k_parts]
    gain = jnp.broadcast_to(subln_g.astype(jnp.float32)[:, None], (LANES, tq))
    return pl.pallas_call(
        functools.partial(_diff_kernel, n_parts=len(k_parts), tk=tk, out_scale=1.0 - lam_init),
        grid=(b, DIFF_HEADS, n // tq),
        in_specs=[smem, smem, qspec] + kvspecs + kvspecs + [_const_spec((LANES, tq))],
        out_specs=qspec,
        out_shape=jax.ShapeDtypeStruct((b, n, DIFF_W), jnp.bfloat16),
        scratch_shapes=[pltpu.VMEM((nk, LANES), jnp.bfloat16),
                        pltpu.VMEM((nk // tk, LANES + SUM_ROWS, tk), jnp.bfloat16),
                        pltpu.VMEM((LANES + SUM_ROWS, 2 * tq), jnp.float32),
                        pltpu.VMEM((2, tk, tq), jnp.float32), pltpu.VMEM((2, tk, tq), jnp.float32),
                        pltpu.VMEM((2, tk, tq), jnp.bfloat16), pltpu.VMEM((2, tk, tq), jnp.bfloat16)],
        compiler_params=_params(3),
        name="diff",
    )(lam, bound, qd, *k_parts, *v_parts, gain)


def _merge_kernel(x_ref, gt_ref, ya_ref, sza_ref, yb_ref, szb_ref, uc_ref, up_ref, un_ref, cb_ref, szc_ref,
                  yd_ref, szd_ref, g_ref, cw_ref, wa_ref, wb_ref, wc_ref, wd_ref, wo_ref, o_ref, *, tm, nt):
    i = pl.program_id(1)
    f32 = jnp.float32
    u = uc_ref[...].astype(f32)
    row = lax.broadcasted_iota(jnp.int32, (tm, 1), 0)
    prev_row = jnp.where(i > 0, up_ref[7:8, :].astype(f32), 0.0)
    next_row = jnp.where(i < nt - 1, un_ref[0:1, :].astype(f32), 0.0)
    u_prev = jnp.where(row == 0, prev_row, pltpu.roll(u, 1, 0))
    u_next = jnp.where(row == tm - 1, next_row, pltpu.roll(u, tm - 1, 0))
    cw = cw_ref[...]
    conv = u_prev * cw[0:1, :] + u * cw[1:2, :] + u_next * cw[2:3, :]
    yc = cb_ref[...].astype(f32) * conv
    d = x_ref.shape[-1]
    branches = ((ya_ref[...].astype(f32), sza_ref, wa_ref), (yb_ref[...].astype(f32), szb_ref, wb_ref),
                (yc, szc_ref, wc_ref), (yd_ref[...].astype(f32), szd_ref, wd_ref))
    mixed = jnp.zeros((tm, d), f32)
    for j, (y, sz_ref, w_ref) in enumerate(branches):
        t = _dot(_bf(y * sz_ref[...].astype(f32)), w_ref[...])
        mixed = mixed + g_ref[:, j * d:(j + 1) * d].astype(f32) * t
    out = _dot(_bf(mixed), wo_ref[...])
    o_ref[...] = x_ref[...] + gt_ref[...] * out


def _merge(x, gate, ya, sza, yb, szb, uc, cb, szc, yd, szd, g, conv_w, w_a, w_b, w_c, w_d, w_out, tm):
    b, n, d = x.shape
    nt = n // tm
    tok = lambda w: pl.BlockSpec((None, tm, w), lambda bi, i: (bi, i, 0))
    hb = tm // 8
    halo_p = pl.BlockSpec((None, 8, CONV_W), lambda bi, i: (bi, jnp.maximum(i * hb - 1, 0), 0))
    halo_n = pl.BlockSpec((None, 8, CONV_W), lambda bi, i: (bi, jnp.minimum((i + 1) * hb, n // 8 - 1), 0))
    per_b = pl.BlockSpec((None, 1, d), lambda bi, i: (bi, 0, 0))
    ws = [_bf(w_a), _bf(w_b), _bf(w_c), _bf(w_d), _bf(w_out)]
    return pl.pallas_call(
        functools.partial(_merge_kernel, tm=tm, nt=nt),
        grid=(b, nt),
        in_specs=[tok(d), per_b, tok(FOURIER_W), tok(FOURIER_W), tok(SWA_W), tok(SWA_W), tok(CONV_W), halo_p, halo_n,
                  tok(CONV_W), tok(CONV_W), tok(DIFF_W), tok(DIFF_W), tok(N_BRANCH * d), _const_spec(conv_w.shape)]
                 + [_const_spec(w.shape) for w in ws],
        out_specs=tok(d),
        out_shape=jax.ShapeDtypeStruct((b, n, d), jnp.float32),
        compiler_params=_params(2),
        name="merge",
    )(x, gate, ya, sza, yb, szb, uc, uc, uc, cb, szc, yd, szd, g, conv_w.astype(jnp.float32), *ws)


def _rope_tables(n):
    rows = n // GRID_W
    row = jnp.broadcast_to(jnp.arange(rows, dtype=jnp.float32)[:, None], (rows, GRID_W)).reshape(-1)
    col = jnp.broadcast_to(jnp.arange(GRID_W, dtype=jnp.float32)[None, :], (rows, GRID_W)).reshape(-1)
    nf = HEAD_DIM // 4
    inv = ROPE_BASE ** (-jnp.arange(nf, dtype=jnp.float32) / nf)
    ar = row[:, None] * inv[None, :]
    ac = col[:, None] * inv[None, :]
    cos = jnp.concatenate([jnp.cos(ar), jnp.cos(ar), jnp.cos(ac), jnp.cos(ac)], axis=-1)
    sin = jnp.concatenate([jnp.sin(ar), jnp.sin(ar), jnp.sin(ac), jnp.sin(ac)], axis=-1)
    sign = jnp.where((jnp.arange(HEAD_DIM) % 32) < 16, -1.0, 1.0)
    return jnp.tile(cos, (1, 2)), jnp.tile(sin * sign, (1, 2))


def _channel_dft():
    k = np.arange(HEAD_DIM)
    ang = 2.0 * np.pi * ((k[:, None] * k[None, :]) % HEAD_DIM) / HEAD_DIM
    eye = np.eye(FOURIER_W // HEAD_DIM)
    inv = 1.0 / math.sqrt(HEAD_DIM)
    return jnp.asarray(np.concatenate([np.kron(eye, np.cos(ang)), -np.kron(eye, np.sin(ang))], axis=1) * inv,
                       jnp.bfloat16)


def _mix(stream, scale, shift, gate, ctx_kv, lp, consts, lam, lam_init, rope, dft, tm, tq, tk, has_local):
    parts = _project(stream, scale, shift, lp["norm_g"], lp["w_ext"], consts["cs_bd"], lp["head_gains"], rope, tm)
    zr, zi, qb, kb, vb, uc, cb, qd, kd, vd, sza, szb, szc, szd, g = parts
    if ctx_kv is None:
        kx_b, vx_b, k_parts, v_parts = kb, vb, (kd,), (vd,)
    else:
        kx_b, vx_b, kx_d, vx_d = ctx_kv
        k_parts, v_parts = (kx_d, kd), (vx_d, vd)
    ya = _fourier_positions(zr, zi, dft)
    yb = _window_attention(qb, kb, vb, kx_b, vx_b, lp["sink"], lp["bound_b"], has_local)
    yd = _diff_attention(qd, k_parts, v_parts, lam, lp["bound_d"], lp["subln"], lam_init, tq, tk)
    new = _merge(stream, gate, ya, sza, yb, szb, uc, cb, szc, yd, szd, g, lp["conv_w"],
                 lp["w_o_a"], lp["w_o_b"], lp["w_o_c"], lp["w_o_d"], lp["w_out"], tm)
    return new, (kb, vb, kd, vd)


def kernel(x, c, ctx, c_ctx, norm_g, w_mod, b_mod, w_in, q_norm_b, k_norm_b, sink_b, conv_w, q_norm_d, k_norm_d,
           lam_q1, lam_k1, lam_q2, lam_k2, subln_d, w_o_a, w_o_b, w_o_c, w_o_d, w_out):
    b, n, d = x.shape
    lc = ctx.shape[1]
    depth = w_in.shape[0]
    rope = _rope_tables(n)
    consts = dict(cs_bd=_channel_dft())
    dft_x = _dft_tables(n)
    dft_c = _dft_tables(lc)
    c_rows = jnp.zeros((8, d), jnp.float32).at[:b].set(c).at[b].set(c_ctx)
    tm = min(512, n)
    tq = min(512, n)
    tk = 1280
    for l in range(depth):
        last = l == depth - 1
        lam_init = 0.8 - 0.6 * math.exp(-0.3 * l)
        lam_vecs = jnp.stack([lam_q1[l], lam_k1[l], lam_q2[l], lam_k2[l]]).astype(jnp.float32)
        mod, lam_o = _modulation(c_rows, w_mod[l], b_mod[l], lam_vecs, lam_init)
        lam = lam_o[0:1, 0:1]
        shift, scale, gate = (mod[:b, None, j * d:(j + 1) * d] for j in range(3))
        shift_c, scale_c, gate_c = (jnp.broadcast_to(mod[b:b + 1, None, j * d:(j + 1) * d], (b, 1, d))
                                    for j in range(3))
        tile2 = lambda v: jnp.tile(v.astype(jnp.float32), 2)
        lp = dict(norm_g=norm_g[l], w_ext=_bf(w_in[l]), sink=sink_b[l].astype(jnp.float32),
                  head_gains=jnp.stack([tile2(q_norm_b[l]), tile2(k_norm_b[l]), tile2(q_norm_d[l]), tile2(k_norm_d[l])]),
                  bound_b=_score_bound(q_norm_b[l], k_norm_b[l]), bound_d=_score_bound(q_norm_d[l], k_norm_d[l]),
                  subln=subln_d[l], conv_w=conv_w[l], w_o_a=w_o_a[l], w_o_b=w_o_b[l], w_o_c=w_o_c[l],
                  w_o_d=w_o_d[l], w_out=w_out[l])
        if last:
            parts = _project(ctx, scale_c, shift_c, lp["norm_g"], lp["w_ext"], consts["cs_bd"], lp["head_gains"],
                             None, min(256, lc))
            ctx_kv = (parts[3], parts[4], parts[8], parts[9])
        else:
            ctx, ctx_kv = _mix(ctx, scale_c, shift_c, gate_c, None, lp, consts, lam, lam_init, None, dft_c,
                               min(256, lc), min(256, lc), tk, False)
        x, _ = _mix(x, scale, shift, gate, ctx_kv, lp, consts, lam, lam_init, rope, dft_x, tm, tq, tk, True)
    return x
```

```python
import functools
import math

import jax
import jax.numpy as jnp
import numpy as np
from jax import lax
from jax.experimental import pallas as pl
from jax.experimental.pallas import tpu as pltpu

HEAD_DIM = 64
LANES = 128
FOURIER_W = 384
SWA_HEADS = 8
SWA_KV = 2
SWA_W = SWA_HEADS * HEAD_DIM
WINDOW = 128
BLOCK = 128
CONV_W = 384
DIFF_HEADS = 4
DIFF_W = DIFF_HEADS * 2 * HEAD_DIM
N_BRANCH = 4
ROPE_BASE = 10000.0
NORM_EPS = 1e-6
NEG_INF = -1e30
GRID_W = 64
DFT_MINOR = 128
SUM_ROWS = 16
LOG2E = math.log2(math.e)
SAFE_SHIFT = 60.0
VMEM_LIMIT = 56 * 1024 * 1024


def _bf(x):
    return x.astype(jnp.bfloat16)


def _dot(a, b):
    return jnp.dot(a, b, preferred_element_type=jnp.float32)


def _dot_nt(a, b):
    return lax.dot_general(a, b, (((1,), (1,)), ((), ())), preferred_element_type=jnp.float32)


def _params(n_axes):
    return pltpu.CompilerParams(dimension_semantics=("arbitrary",) * n_axes, vmem_limit_bytes=VMEM_LIMIT)


def _const_spec(shape):
    nd = len(shape)
    return pl.BlockSpec(shape, lambda *_: (0,) * nd, pipeline_mode=pl.Buffered(1))


def _mod_kernel(c_ref, w_ref, b_ref, lam_ref, mod_ref, lamo_ref, *, lam_init):
    c = c_ref[...]
    s = c * jax.nn.sigmoid(c)
    mod_ref[...] = _dot(_bf(s), w_ref[...]) + b_ref[...]
    lv = lam_ref[...]
    a1 = jnp.sum(lv[0:1, :] * lv[1:2, :], axis=-1, keepdims=True)
    a2 = jnp.sum(lv[2:3, :] * lv[3:4, :], axis=-1, keepdims=True)
    lam = jnp.exp(a1) - jnp.exp(a2) + lam_init
    lamo_ref[...] = jnp.broadcast_to(lam, lamo_ref.shape)


def _modulation(c_rows, w_mod, b_mod, lam_vecs, lam_init):
    r, d = c_rows.shape
    return pl.pallas_call(
        functools.partial(_mod_kernel, lam_init=lam_init),
        out_shape=(jax.ShapeDtypeStruct((r, 3 * d), jnp.float32),
                   jax.ShapeDtypeStruct((8, LANES), jnp.float32)),
        name="mod",
    )(c_rows, _bf(w_mod), b_mod.reshape(1, 3 * d), lam_vecs)


def _head_norm_rope(t, gain, cos, sin_s, lane_lo, scale):
    sq = t * t
    first = lax.broadcasted_iota(jnp.int32, t.shape, 1) < HEAD_DIM
    ms = jnp.where(first, jnp.sum(jnp.where(first, sq, 0.0), axis=-1, keepdims=True),
                   jnp.sum(jnp.where(first, 0.0, sq), axis=-1, keepdims=True)) * (1.0 / HEAD_DIM)
    y = t * lax.rsqrt(ms + NORM_EPS) * gain
    if cos is not None:
        rot = jnp.where(lane_lo, pltpu.roll(y, LANES - 16, 1), pltpu.roll(y, 16, 1))
        y = y * cos + rot * sin_s
    if scale != 1.0:
        y = y * scale
    return y


def _proj_kernel(*refs, use_rope, tm):
    if use_rope:
        (x_ref, sc_ref, sh_ref, g_ref, w_ref, cs_ref, hg_ref, cos_ref, sin_ref), outs = refs[:9], refs[9:]
    else:
        (x_ref, sc_ref, sh_ref, g_ref, w_ref, cs_ref, hg_ref), outs = refs[:7], refs[7:]
        cos_ref = sin_ref = None
    (zr_ref, zi_ref, qb_ref, kb_ref, vb_ref, uc_ref, cb_ref, qd_ref, kd_ref, vd_ref,
     sza_ref, szb_ref, szc_ref, szd_ref, gate_ref) = outs

    x = x_ref[...]
    y = x * lax.rsqrt(jnp.mean(x * x, axis=-1, keepdims=True) + NORM_EPS) * g_ref[...]
    h = _bf(y * (1.0 + sc_ref[...]) + sh_ref[...])

    state = dict(col=0, run=None, off=0)

    def start_run(width):
        state.update(run=_dot(h, w_ref[:, state["col"]:state["col"] + width]), off=0)
        state["col"] += width

    def take(width):
        piece = state["run"][:, state["off"]:state["off"] + width]
        state["off"] += width
        return piece

    start_run(FOURIER_W + SWA_W + LANES)
    zz = _dot(_bf(take(FOURIER_W)), cs_ref[...])
    zr_ref[...] = zz[:, :FOURIER_W].astype(zr_ref.dtype)
    zi_ref[...] = zz[:, FOURIER_W:].astype(zi_ref.dtype)

    if use_rope:
        cos = cos_ref[...]
        sin_s = sin_ref[...]
    else:
        cos = sin_s = None
    lane = lax.broadcasted_iota(jnp.int32, (tm, LANES), 1)
    lane_lo = (lane & 31) < 16
    qscale = HEAD_DIM ** -0.5

    def normed(width, gain_row, out_ref, scale):
        t = take(width)
        gain = hg_ref[gain_row:gain_row + 1, :]
        for s in range(width // LANES):
            ts = t[:, s * LANES:(s + 1) * LANES]
            out_ref[:, s * LANES:(s + 1) * LANES] = _bf(
                _head_norm_rope(ts, gain, cos, sin_s, lane_lo, scale))

    head_lo = lane < HEAD_DIM

    def store_duplicated(pair, out_ref):
        swapped = pltpu.roll(pair, HEAD_DIM, 1)
        out_ref[:, :LANES] = _bf(jnp.where(head_lo, pair, swapped))
        out_ref[:, LANES:] = _bf(jnp.where(head_lo, swapped, pair))

    normed(SWA_W, 0, qb_ref, qscale * LOG2E)
    store_duplicated(_head_norm_rope(take(LANES), hg_ref[1:2, :], cos, sin_s, lane_lo, 1.0), kb_ref)
    start_run(LANES + 3 * CONV_W)
    store_duplicated(take(LANES), vb_ref)
    c_x, c_b, c_c = take(CONV_W), take(CONV_W), take(CONV_W)
    uc_ref[...] = _bf(c_c * c_x)
    cb_ref[...] = _bf(c_b)
    start_run(DIFF_W)
    normed(DIFF_W, 2, qd_ref, qscale * LOG2E)
    start_run(DIFF_W)
    normed(DIFF_W, 3, kd_ref, 1.0)
    start_run(DIFF_W)
    vd_ref[...] = _bf(take(DIFF_W))
    start_run(FOURIER_W + SWA_W + CONV_W + DIFF_W)
    for ref, width in ((sza_ref, FOURIER_W), (szb_ref, SWA_W), (szc_ref, CONV_W), (szd_ref, DIFF_W)):
        z = take(width)
        ref[...] = _bf(z * jax.nn.sigmoid(z))
    d = x.shape[-1]
    for j in range(N_BRANCH):
        start_run(d)
        gate_ref[:, j * d:(j + 1) * d] = _bf(jax.nn.sigmoid(take(d)))


_PROJ_OUT_W = (FOURIER_W, FOURIER_W, SWA_W, 2 * LANES, 2 * LANES, CONV_W, CONV_W, DIFF_W, DIFF_W, DIFF_W,
               FOURIER_W, SWA_W, CONV_W, DIFF_W)


def _project(x, scale, shift, norm_g, w_ext, cs_bd, head_gains, rope, tm):
    b, n, d = x.shape
    use_rope = rope is not None
    tok = lambda w: pl.BlockSpec((None, tm, w), lambda bi, i: (bi, i, 0))
    per_b = pl.BlockSpec((None, 1, d), lambda bi, i: (bi, 0, 0))
    in_specs = [tok(d), per_b, per_b, _const_spec((1, d)), _const_spec(w_ext.shape), _const_spec(cs_bd.shape),
                _const_spec(head_gains.shape)]
    args = [x, scale, shift, norm_g.reshape(1, d), w_ext, cs_bd, head_gains]
    if use_rope:
        tab = pl.BlockSpec((tm, LANES), lambda bi, i: (i, 0))
        in_specs += [tab, tab]
        args += list(rope)
    widths = _PROJ_OUT_W + (N_BRANCH * d,)
    dtypes = [jnp.float32 if (j < 2 and n > 2 * DFT_MINOR) else jnp.bfloat16 for j in range(len(widths))]
    return pl.pallas_call(
        functools.partial(_proj_kernel, use_rope=use_rope, tm=tm),
        grid=(b, n // tm),
        in_specs=in_specs,
        out_specs=[tok(w) for w in widths],
        out_shape=[jax.ShapeDtypeStruct((b, n, w), dt) for w, dt in zip(widths, dtypes)],
        compiler_params=_params(2),
        name="proj",
    )(*args)


def _dft1_kernel(zr_ref, zi_ref, f_ref, tc_ref, ts_ref, o_ref, *, r, tn2):
    for j in range(tn2):
        z = _bf(jnp.concatenate([zr_ref[:, j, :], zi_ref[:, j, :]], axis=0))
        a = _dot(f_ref[...], z)
        ar, ai = a[:r], a[r:]
        tc = jnp.concatenate([tc_ref[j]] * (FOURIER_W // LANES), axis=-1)
        ts = jnp.concatenate([ts_ref[j]] * (FOURIER_W // LANES), axis=-1)
        o_ref[0, j] = _bf(ar * tc + ai * ts)
        o_ref[1, j] = _bf(ai * tc - ar * ts)


def _left_matmul_kernel(m_ref, x_ref, o_ref):
    o_ref[...] = _bf(_dot(m_ref[...], x_ref[...]))


def _left_matmul(mat, x, tc):
    b, k, c = x.shape
    rows = mat.shape[0]
    return pl.pallas_call(
        _left_matmul_kernel,
        grid=(b, c // tc),
        in_specs=[_const_spec(mat.shape), pl.BlockSpec((None, k, tc), lambda bi, i: (bi, 0, i))],
        out_specs=pl.BlockSpec((None, rows, tc), lambda bi, i: (bi, 0, i)),
        out_shape=jax.ShapeDtypeStruct((b, rows, c), jnp.bfloat16),
        compiler_params=_params(2),
        name="dft2",
    )(mat, x)


def _dft_tables(n):
    inv = 1.0 / math.sqrt(n)
    if n <= 2 * DFT_MINOR:
        k = np.arange(n)
        ang = 2.0 * np.pi * ((k[:, None] * k[None, :]) % n) / n
        return dict(direct=jnp.asarray(np.concatenate([np.cos(ang), np.sin(ang)], axis=1) * inv, jnp.bfloat16))
    r = n // DFT_MINOR
    k1 = np.arange(r)
    a1 = 2.0 * np.pi * ((k1[:, None] * k1[None, :]) % r) / r
    c1, s1 = np.cos(a1), np.sin(a1)
    f1 = np.block([[c1, s1], [-s1, c1]])
    n2 = np.arange(DFT_MINOR)
    at = 2.0 * np.pi * (n2[:, None] * k1[None, :]) / n
    tw_c = np.repeat(np.cos(at)[:, :, None], LANES, axis=2)
    tw_s = np.repeat(np.sin(at)[:, :, None], LANES, axis=2)
    a2 = 2.0 * np.pi * ((n2[:, None] * n2[None, :]) % DFT_MINOR) / DFT_MINOR
    f2 = np.concatenate([np.cos(a2), np.sin(a2)], axis=1) * inv
    return dict(f1=jnp.asarray(f1, jnp.bfloat16), tw_c=jnp.asarray(tw_c, jnp.float32),
                tw_s=jnp.asarray(tw_s, jnp.float32), f2=jnp.asarray(f2, jnp.bfloat16))


def _fourier_positions(zr, zi, tabs):
    b, n, w = zr.shape
    if "direct" in tabs:
        return _left_matmul(tabs["direct"], jnp.concatenate([zr, zi], axis=1), w)
    r = n // DFT_MINOR
    tn2 = 8
    zin = pl.BlockSpec((None, r, tn2, w), lambda bi, i: (bi, 0, i, 0))
    tw = pl.BlockSpec((tn2, r, LANES), lambda bi, i: (i, 0, 0))
    g = pl.pallas_call(
        functools.partial(_dft1_kernel, r=r, tn2=tn2),
        grid=(b, DFT_MINOR // tn2),
        in_specs=[zin, zin, _const_spec(tabs["f1"].shape), tw, tw],
        out_specs=pl.BlockSpec((None, 2, tn2, r, w), lambda bi, i: (bi, 0, i, 0, 0)),
        out_shape=jax.ShapeDtypeStruct((b, 2, DFT_MINOR, r, w), jnp.bfloat16),
        compiler_params=_params(2),
        name="dft1",
    )(zr.reshape(b, r, DFT_MINOR, w), zi.reshape(b, r, DFT_MINOR, w), tabs["f1"], tabs["tw_c"], tabs["tw_s"])
    y = _left_matmul(tabs["f2"], g.reshape(b, 2 * DFT_MINOR, r * w), (16 if r % 16 == 0 else 4) * w)
    return y.reshape(b, n, w)


def _win_kernel(*refs, has_local, nb, nq):
    if has_local:
        bound_ref, sink_ref, q_ref, kp_ref, kc_ref, kn_ref, vp_ref, vc_ref, vn_ref, kx_ref, vx_ref, o_ref = refs
    else:
        bound_ref, sink_ref, q_ref, kx_ref, vx_ref, o_ref = refs
    i = pl.program_id(1)
    tq = BLOCK
    group = SWA_HEADS // SWA_KV
    lane = lax.broadcasted_iota(jnp.int32, (1, LANES), 1)
    halves = (lane < HEAD_DIM, lane >= HEAD_DIM)
    row_lo = lax.broadcasted_iota(jnp.int32, (LANES, 1), 0) < HEAD_DIM
    if has_local:
        kr = lax.broadcasted_iota(jnp.int32, (3 * BLOCK, tq), 0)
        qc = lax.broadcasted_iota(jnp.int32, (3 * BLOCK, tq), 1)
        in_band = jnp.abs(kr - BLOCK - qc) <= WINDOW
    bound = bound_ref[0, 0]

    def attend(s, j, use_bound):
        sl = slice(j * LANES, (j + 1) * LANES)
        rows = slice(s * BLOCK, (s + 1) * BLOCK)
        if has_local:
            def window(prev_ref, cur_ref, next_ref):
                first = prev_ref[:, sl] if s == 0 else cur_ref[(s - 1) * BLOCK:s * BLOCK, sl]
                last = next_ref[:, sl] if s == nq - 1 else cur_ref[(s + 1) * BLOCK:(s + 2) * BLOCK, sl]
                return [first, cur_ref[rows, sl], last]
            kd = jnp.concatenate(window(kp_ref, kc_ref, kn_ref) + [kx_ref[:, sl]], axis=0)
            vd = jnp.concatenate(window(vp_ref, vc_ref, vn_ref) + [vx_ref[:, sl]], axis=0)
        else:
            kd = kx_ref[:, sl]
            vd = vx_ref[:, sl]
        q_stack = []
        for t in range(group // 2):
            slab = j * (group // 2) + t
            q2 = q_ref[rows, slab * LANES:(slab + 1) * LANES]
            q_stack += [jnp.where(halves[e], q2, jnp.zeros_like(q2)) for e in range(2)]
        st = _dot_nt(kd, jnp.concatenate(q_stack, axis=0))
        if has_local:
            kpos = (i * nq + s - 1) * BLOCK + kr
            bias = jnp.where(in_band & (kpos >= 0) & (kpos < nb * BLOCK), 0.0, NEG_INF)
            st = jnp.concatenate([st[:3 * BLOCK] + jnp.concatenate([bias] * group, axis=1), st[3 * BLOCK:]], axis=0)
        sk = sink_ref[j:j + 1, :]
        m = jnp.maximum(bound, sk) if use_bound else jnp.maximum(jnp.max(st, axis=0, keepdims=True), sk)
        p = jnp.exp2(st - m)
        inv = 1.0 / (jnp.sum(p, axis=0, keepdims=True) + jnp.exp2(sk - m))
        ot = lax.dot_general(vd, _bf(p), (((0,), (0,)), ((), ())), preferred_element_type=jnp.float32) * inv
        for t in range(group // 2):
            slab = j * (group // 2) + t
            pair = jnp.where(row_lo, ot[:, (2 * t) * tq:(2 * t + 1) * tq], ot[:, (2 * t + 1) * tq:(2 * t + 2) * tq])
            o_ref[rows, slab * LANES:(slab + 1) * LANES] = _bf(pair.T)

    shift_safe = bound <= SAFE_SHIFT

    @pl.when(shift_safe)
    def _():
        for s in range(nq):
            for j in range(SWA_KV):
                attend(s, j, True)

    @pl.when(jnp.logical_not(shift_safe))
    def _():
        for s in range(nq):
            for j in range(SWA_KV):
                attend(s, j, False)


def _score_bound(q_gain, k_gain):
    g = jnp.max(jnp.abs(q_gain.astype(jnp.float32))) * jnp.max(jnp.abs(k_gain.astype(jnp.float32)))
    return (1.02 * HEAD_DIM ** 0.5 * LOG2E * g).reshape(1, 1)


def _window_attention(qb, kb, vb, kx, vx, sink, bound, has_local):
    b, n, _ = qb.shape
    nb = n // BLOCK
    nq = 4 if nb % 4 == 0 else 1
    lx = kx.shape[1]
    qspec = pl.BlockSpec((None, nq * BLOCK, SWA_W), lambda bi, i: (bi, i, 0))
    ctx = pl.BlockSpec((None, lx, 2 * LANES), lambda bi, i: (bi, 0, 0))
    sink_rows = jnp.repeat(sink.astype(jnp.float32) * LOG2E, BLOCK).reshape(SWA_KV, -1)
    args = [bound, sink_rows, qb]
    specs = [pl.BlockSpec(memory_space=pltpu.SMEM), _const_spec(sink_rows.shape), qspec]
    if has_local:
        prv = pl.BlockSpec((None, BLOCK, 2 * LANES), lambda bi, i: (bi, jnp.maximum(i * nq - 1, 0), 0))
        cur = pl.BlockSpec((None, nq * BLOCK, 2 * LANES), lambda bi, i: (bi, i, 0))
        nxt = pl.BlockSpec((None, BLOCK, 2 * LANES), lambda bi, i: (bi, jnp.minimum((i + 1) * nq, nb - 1), 0))
        args += [kb, kb, kb, vb, vb, vb]
        specs += [prv, cur, nxt, prv, cur, nxt]
    args += [kx, vx]
    specs += [ctx, ctx]
    return pl.pallas_call(
        functools.partial(_win_kernel, has_local=has_local, nb=nb, nq=nq),
        grid=(b, nb // nq),
        in_specs=specs,
        out_specs=qspec,
        out_shape=jax.ShapeDtypeStruct((b, n, SWA_W), jnp.bfloat16),
        compiler_params=_params(2),
        name="win",
    )(*args)


def _diff_kernel(lam_ref, bound_ref, q_ref, *refs, n_parts, tk, out_scale):
    k_parts, v_parts = refs[:n_parts], refs[n_parts:2 * n_parts]
    g_ref, o_ref, k_ref, vt_ref, acc_ref, s0_ref, s1_ref, p0_ref, p1_ref = refs[2 * n_parts:]
    qi = pl.program_id(2)
    tq = q_ref.shape[0]
    n_chunks = vt_ref.shape[0]

    @pl.when(qi == 0)
    def _():
        row = 0
        for kp, vp in zip(k_parts, v_parts):
            rows = kp.shape[0]
            k_ref[row:row + rows, :] = kp[...]
            done = 0
            while done < rows:
                ci, off = divmod(row + done, tk)
                take = min(tk - off, rows - done)
                vt_ref[ci, :LANES, off:off + take] = _bf(vp[done:done + take, :].astype(jnp.float32).T)
                done += take
            row += rows

        vt_ref[:, LANES:, :] = jnp.ones((n_chunks, SUM_ROWS, tk), jnp.bfloat16)

    q = q_ref[...]
    lane = lax.broadcasted_iota(jnp.int32, (1, LANES), 1)
    qm = (jnp.where(lane < HEAD_DIM, q, jnp.zeros_like(q)), jnp.where(lane >= HEAD_DIM, q, jnp.zeros_like(q)))
    s_bufs = (s0_ref, s1_ref)
    p_bufs = (p0_ref, p1_ref)

    bound = bound_ref[0, 0]
    shift_safe = bound <= SAFE_SHIFT

    @pl.when(shift_safe)
    def _():
        q_both = jnp.concatenate(qm, axis=0)

        def chunk(ci, carry):
            denom, pv = carry
            c0 = pl.multiple_of(ci * tk, tk)
            p = jnp.exp2(_dot_nt(k_ref[pl.ds(c0, tk), :], q_both) - bound)
            return denom + jnp.sum(p, axis=0, keepdims=True), pv + _dot(vt_ref[ci, :LANES, :], _bf(p))
        init = (jnp.zeros((1, 2 * tq), jnp.float32), jnp.zeros((LANES, 2 * tq), jnp.float32))
        denom, pv = lax.fori_loop(0, n_chunks, chunk, init, unroll=True)
        acc_ref[:LANES, :] = pv
        acc_ref[LANES:, :] = jnp.broadcast_to(denom, (SUM_ROWS, 2 * tq))

    @pl.when(jnp.logical_not(shift_safe))
    def _():
        acc_ref[...] = jnp.zeros_like(acc_ref)
        _diff_online(qm, k_ref, vt_ref, acc_ref, s_bufs, p_bufs, tk=tk, tq=tq, n_chunks=n_chunks)

    lam = lam_ref[0, 0]
    a1 = acc_ref[:, :tq]
    a2 = acc_ref[:, tq:]
    o = a1[:LANES] * (1.0 / a1[LANES:LANES + 1]) - lam * (a2[:LANES] * (1.0 / a2[LANES:LANES + 1]))
    y = o * lax.rsqrt(jnp.mean(o * o, axis=0, keepdims=True) + NORM_EPS) * g_ref[...] * out_scale
    o_ref[...] = _bf(y.T)


def _diff_online(qm, k_ref, vt_ref, acc_ref, s_bufs, p_bufs, *, tk, tq, n_chunks):
    def scores(ci, slot):
        c0 = ci * tk if isinstance(ci, int) else pl.multiple_of(ci * tk, tk)
        k = k_ref[pl.ds(c0, tk), :]
        mx = []
        for c in range(2):
            st = _dot_nt(k, qm[c])
            s_bufs[slot][c] = st
            mx.append(jnp.max(st, axis=0, keepdims=True))
        return tuple(mx)

    def probs(slot, mx, m_run):
        m_new, alpha = [], []
        for c in range(2):
            m = jnp.maximum(m_run[c], mx[c])
            alpha.append(jnp.exp2(m_run[c] - m))
            p_bufs[slot][c] = _bf(jnp.exp2(s_bufs[slot][c] - m))
            m_new.append(m)
        return tuple(m_new), tuple(alpha)

    def accumulate(ci, slot, alpha):
        vt = vt_ref[ci]
        for c in range(2):
            cols = slice(c * tq, (c + 1) * tq)
            acc_ref[:, cols] = alpha[c] * acc_ref[:, cols] + _dot(vt, p_bufs[slot][c])

    def step(t, par, do_scores, do_probs, do_acc, state):
        m_run, mx, alpha = state
        mx_next = scores(t + 2, par) if do_scores else mx
        if do_probs:
            m_run, alpha_next = probs(1 - par, mx, m_run)
        else:
            alpha_next = alpha
        if do_acc:
            accumulate(t, par, alpha)
        return m_run, mx_next, alpha_next

    neg = jnp.full((1, tq), NEG_INF, jnp.float32)
    one = jnp.ones((1, tq), jnp.float32)
    state = ((neg, neg), (neg, neg), (one, one))
    for t in (-2, -1):
        state = step(t, t % 2, t + 2 < n_chunks, 0 <= t + 1 < n_chunks, False, state)
    n_steady = max(n_chunks - 2, 0)

    def pair(j, state):
        state = step(2 * j, 0, True, True, True, state)
        return step(2 * j + 1, 1, True, True, True, state)

    state = lax.fori_loop(0, n_steady // 2, pair, state)
    if n_steady % 2:
        state = step(n_steady - 1, 0, True, True, True, state)
    for t in range(n_steady, n_chunks):
        state = step(t, t % 2, False, t + 1 < n_chunks, True, state)


def _pick_tk(nk, cap):
    best = LANES
    for t in range(LANES, min(cap, nk) + 1, LANES):
        if nk % t == 0:
            best = t
    return best


def _diff_attention(qd, k_parts, v_parts, lam, bound, subln_g, lam_init, tq, tk_cap):
    b, n, _ = qd.shape
    nk = sum(k.shape[1] for k in k_parts)
    tk = _pick_tk(nk, tk_cap)
    smem = pl.BlockSpec(memory_space=pltpu.SMEM)
    qspec = pl.BlockSpec((None, tq, LANES), lambda bi, h, i: (bi, i, h))
    kvspecs = [pl.BlockSpec((None, k.shape[1], LANES), lambda bi, h, i: (bi, 0, h)) for k in k_parts]
    gain = jnp.broadcast_to(subln_g.astype(jnp.float32)[:, None], (LANES, tq))
    return pl.pallas_call(
        functools.partial(_diff_kernel, n_parts=len(k_parts), tk=tk, out_scale=1.0 - lam_init),
        grid=(b, DIFF_HEADS, n // tq),
        in_specs=[smem, smem, qspec] + kvspecs + kvspecs + [_const_spec((LANES, tq))],
        out_specs=qspec,
        out_shape=jax.ShapeDtypeStruct((b, n, DIFF_W), jnp.bfloat16),
        scratch_shapes=[pltpu.VMEM((nk, LANES), jnp.bfloat16),
                        pltpu.VMEM((nk // tk, LANES + SUM_ROWS, tk), jnp.bfloat16),
                        pltpu.VMEM((LANES + SUM_ROWS, 2 * tq), jnp.float32),
                        pltpu.VMEM((2, tk, tq), jnp.float32), pltpu.VMEM((2, tk, tq), jnp.float32),
                        pltpu.VMEM((2, tk, tq), jnp.bfloat16), pltpu.VMEM((2, tk, tq), jnp.bfloat16)],
        compiler_params=_params(3),
        name="diff",
    )(lam, bound, qd, *k_parts, *v_parts, gain)


def _merge_kernel(x_ref, gt_ref, ya_ref, sza_ref, yb_ref, szb_ref, uc_ref, up_ref, un_ref, cb_ref, szc_ref,
                  yd_ref, szd_ref, g_ref, cw_ref, wa_ref, wb_ref, wc_ref, wd_ref, wo_ref, o_ref, *, tm, nt):
    i = pl.program_id(1)
    f32 = jnp.float32
    u = uc_ref[...].astype(f32)
    row = lax.broadcasted_iota(jnp.int32, (tm, 1), 0)
    prev_row = jnp.where(i > 0, up_ref[7:8, :].astype(f32), 0.0)
    next_row = jnp.where(i < nt - 1, un_ref[0:1, :].astype(f32), 0.0)
    u_prev = jnp.where(row == 0, prev_row, pltpu.roll(u, 1, 0))
    u_next = jnp.where(row == tm - 1, next_row, pltpu.roll(u, tm - 1, 0))
    cw = cw_ref[...]
    conv = u_prev * cw[0:1, :] + u * cw[1:2, :] + u_next * cw[2:3, :]
    yc = cb_ref[...].astype(f32) * conv
    d = x_ref.shape[-1]
    branches = ((ya_ref[...].astype(f32), sza_ref, wa_ref), (yb_ref[...].astype(f32), szb_ref, wb_ref),
                (yc, szc_ref, wc_ref), (yd_ref[...].astype(f32), szd_ref, wd_ref))
    mixed = jnp.zeros((tm, d), f32)
    for j, (y, sz_ref, w_ref) in enumerate(branches):
        t = _dot(_bf(y * sz_ref[...].astype(f32)), w_ref[...])
        mixed = mixed + g_ref[:, j * d:(j + 1) * d].astype(f32) * t
    out = _dot(_bf(mixed), wo_ref[...])
    o_ref[...] = x_ref[...] + gt_ref[...] * out


def _merge(x, gate, ya, sza, yb, szb, uc, cb, szc, yd, szd, g, conv_w, w_a, w_b, w_c, w_d, w_out, tm):
    b, n, d = x.shape
    nt = n // tm
    tok = lambda w: pl.BlockSpec((None, tm, w), lambda bi, i: (bi, i, 0))
    hb = tm // 8
    halo_p = pl.BlockSpec((None, 8, CONV_W), lambda bi, i: (bi, jnp.maximum(i * hb - 1, 0), 0))
    halo_n = pl.BlockSpec((None, 8, CONV_W), lambda bi, i: (bi, jnp.minimum((i + 1) * hb, n // 8 - 1), 0))
    per_b = pl.BlockSpec((None, 1, d), lambda bi, i: (bi, 0, 0))
    ws = [_bf(w_a), _bf(w_b), _bf(w_c), _bf(w_d), _bf(w_out)]
    return pl.pallas_call(
        functools.partial(_merge_kernel, tm=tm, nt=nt),
        grid=(b, nt),
        in_specs=[tok(d), per_b, tok(FOURIER_W), tok(FOURIER_W), tok(SWA_W), tok(SWA_W), tok(CONV_W), halo_p, halo_n,
                  tok(CONV_W), tok(CONV_W), tok(DIFF_W), tok(DIFF_W), tok(N_BRANCH * d), _const_spec(conv_w.shape)]
                 + [_const_spec(w.shape) for w in ws],
        out_specs=tok(d),
        out_shape=jax.ShapeDtypeStruct((b, n, d), jnp.float32),
        compiler_params=_params(2),
        name="merge",
    )(x, gate, ya, sza, yb, szb, uc, uc, uc, cb, szc, yd, szd, g, conv_w.astype(jnp.float32), *ws)


def _rope_tables(n):
    rows = n // GRID_W
    row = jnp.broadcast_to(jnp.arange(rows, dtype=jnp.float32)[:, None], (rows, GRID_W)).reshape(-1)
    col = jnp.broadcast_to(jnp.arange(GRID_W, dtype=jnp.float32)[None, :], (rows, GRID_W)).reshape(-1)
    nf = HEAD_DIM // 4
    inv = ROPE_BASE ** (-jnp.arange(nf, dtype=jnp.float32) / nf)
    ar = row[:, None] * inv[None, :]
    ac = col[:, None] * inv[None, :]
    cos = jnp.concatenate([jnp.cos(ar), jnp.cos(ar), jnp.cos(ac), jnp.cos(ac)], axis=-1)
    sin = jnp.concatenate([jnp.sin(ar), jnp.sin(ar), jnp.sin(ac), jnp.sin(ac)], axis=-1)
    sign = jnp.where((jnp.arange(HEAD_DIM) % 32) < 16, -1.0, 1.0)
    return jnp.tile(cos, (1, 2)), jnp.tile(sin * sign, (1, 2))


def _channel_dft():
    k = np.arange(HEAD_DIM)
    ang = 2.0 * np.pi * ((k[:, None] * k[None, :]) % HEAD_DIM) / HEAD_DIM
    eye = np.eye(FOURIER_W // HEAD_DIM)
    inv = 1.0 / math.sqrt(HEAD_DIM)
    return jnp.asarray(np.concatenate([np.kron(eye, np.cos(ang)), -np.kron(eye, np.sin(ang))], axis=1) * inv,
                       jnp.bfloat16)


def _mix(stream, scale, shift, gate, ctx_kv, lp, consts, lam, lam_init, rope, dft, tm, tq, tk, has_local):
    parts = _project(stream, scale, shift, lp["norm_g"], lp["w_ext"], consts["cs_bd"], lp["head_gains"], rope, tm)
    zr, zi, qb, kb, vb, uc, cb, qd, kd, vd, sza, szb, szc, szd, g = parts
    if ctx_kv is None:
        kx_b, vx_b, k_parts, v_parts = kb, vb, (kd,), (vd,)
    else:
        kx_b, vx_b, kx_d, vx_d = ctx_kv
        k_parts, v_parts = (kx_d, kd), (vx_d, vd)
    ya = _fourier_positions(zr, zi, dft)
    yb = _window_attention(qb, kb, vb, kx_b, vx_b, lp["sink"], lp["bound_b"], has_local)
    yd = _diff_attention(qd, k_parts, v_parts, lam, lp["bound_d"], lp["subln"], lam_init, tq, tk)
    new = _merge(stream, gate, ya, sza, yb, szb, uc, cb, szc, yd, szd, g, lp["conv_w"],
                 lp["w_o_a"], lp["w_o_b"], lp["w_o_c"], lp["w_o_d"], lp["w_out"], tm)
    return new, (kb, vb, kd, vd)


def kernel(x, c, ctx, c_ctx, norm_g, w_mod, b_mod, w_in, q_norm_b, k_norm_b, sink_b, conv_w, q_norm_d, k_norm_d,
           lam_q1, lam_k1, lam_q2, lam_k2, subln_d, w_o_a, w_o_b, w_o_c, w_o_d, w_out):
    b, n, d = x.shape
    lc = ctx.shape[1]
    depth = w_in.shape[0]
    rope = _rope_tables(n)
    consts = dict(cs_bd=_channel_dft())
    dft_x = _dft_tables(n)
    dft_c = _dft_tables(lc)
    c_rows = jnp.zeros((8, d), jnp.float32).at[:b].set(c).at[b].set(c_ctx)
    tm = min(512, n)
    tq = min(512, n)
    tk = 1280
    for l in range(depth):
        last = l == depth - 1
        lam_init = 0.8 - 0.6 * math.exp(-0.3 * l)
        lam_vecs = jnp.stack([lam_q1[l], lam_k1[l], lam_q2[l], lam_k2[l]]).astype(jnp.float32)
        mod, lam_o = _modulation(c_rows, w_mod[l], b_mod[l], lam_vecs, lam_init)
        lam = lam_o[0:1, 0:1]
        shift, scale, gate = (mod[:b, None, j * d:(j + 1) * d] for j in range(3))
        shift_c, scale_c, gate_c = (jnp.broadcast_to(mod[b:b + 1, None, j * d:(j + 1) * d], (b, 1, d))
                                    for j in range(3))
        tile2 = lambda v: jnp.tile(v.astype(jnp.float32), 2)
        lp = dict(norm_g=norm_g[l], w_ext=_bf(w_in[l]), sink=sink_b[l].astype(jnp.float32),
                  head_gains=jnp.stack([tile2(q_norm_b[l]), tile2(k_norm_b[l]), tile2(q_norm_d[l]), tile2(k_norm_d[l])]),
                  bound_b=_score_bound(q_norm_b[l], k_norm_b[l]), bound_d=_score_bound(q_norm_d[l], k_norm_d[l]),
                  subln=subln_d[l], conv_w=conv_w[l], w_o_a=w_o_a[l], w_o_b=w_o_b[l], w_o_c=w_o_c[l],
                  w_o_d=w_o_d[l], w_out=w_out[l])
        if last:
            parts = _project(ctx, scale_c, shift_c, lp["norm_g"], lp["w_ext"], consts["cs_bd"], lp["head_gains"],
                             None, min(256, lc))
            ctx_kv = (parts[3], parts[4], parts[8], parts[9])
        else:
            ctx, ctx_kv = _mix(ctx, scale_c, shift_c, gate_c, None, lp, consts, lam, lam_init, None, dft_c,
                               min(256, lc), min(256, lc), tk, False)
        x, _ = _mix(x, scale, shift, gate, ctx_kv, lp, consts, lam, lam_init, rope, dft_x, tm, tq, tk, True)
    return x
```

```python
import functools
import math

import jax
import jax.numpy as jnp
import numpy as np
from jax import lax
from jax.experimental import pallas as pl
from jax.experimental.pallas import tpu as pltpu

HEAD_DIM = 64
LANES = 128
FOURIER_W = 384
SWA_HEADS = 8
SWA_KV = 2
SWA_W = SWA_HEADS * HEAD_DIM
WINDOW = 128
BLOCK = 128
CONV_W = 384
DIFF_HEADS = 4
DIFF_W = DIFF_HEADS * 2 * HEAD_DIM
N_BRANCH = 4
ROPE_BASE = 10000.0
NORM_EPS = 1e-6
NEG_INF = -1e30
GRID_W = 64
DFT_MINOR = 128
SUM_ROWS = 16
LOG2E = math.log2(math.e)
SAFE_SHIFT = 60.0
VMEM_LIMIT = 56 * 1024 * 1024


def _bf(x):
    return x.astype(jnp.bfloat16)


def _dot(a, b):
    return jnp.dot(a, b, preferred_element_type=jnp.float32)


def _dot_nt(a, b):
    return lax.dot_general(a, b, (((1,), (1,)), ((), ())), preferred_element_type=jnp.float32)


def _params(n_axes):
    return pltpu.CompilerParams(dimension_semantics=("arbitrary",) * n_axes, vmem_limit_bytes=VMEM_LIMIT)


def _const_spec(shape):
    nd = len(shape)
    return pl.BlockSpec(shape, lambda *_: (0,) * nd, pipeline_mode=pl.Buffered(1))


def _mod_kernel(c_ref, w_ref, b_ref, lam_ref, mod_ref, lamo_ref, *, lam_init):
    c = c_ref[...]
    s = c * jax.nn.sigmoid(c)
    mod_ref[...] = _dot(_bf(s), w_ref[...]) + b_ref[...]
    lv = lam_ref[...]
    a1 = jnp.sum(lv[0:1, :] * lv[1:2, :], axis=-1, keepdims=True)
    a2 = jnp.sum(lv[2:3, :] * lv[3:4, :], axis=-1, keepdims=True)
    lam = jnp.exp(a1) - jnp.exp(a2) + lam_init
    lamo_ref[...] = jnp.broadcast_to(lam, lamo_ref.shape)


def _modulation(c_rows, w_mod, b_mod, lam_vecs, lam_init):
    r, d = c_rows.shape
    return pl.pallas_call(
        functools.partial(_mod_kernel, lam_init=lam_init),
        out_shape=(jax.ShapeDtypeStruct((r, 3 * d), jnp.float32),
                   jax.ShapeDtypeStruct((8, LANES), jnp.float32)),
        name="mod",
    )(c_rows, _bf(w_mod), b_mod.reshape(1, 3 * d), lam_vecs)


def _head_norm_rope(t, gain, cos, sin_s, lane_lo, scale):
    sq = t * t
    first = lax.broadcasted_iota(jnp.int32, t.shape, 1) < HEAD_DIM
    ms = jnp.where(first, jnp.sum(jnp.where(first, sq, 0.0), axis=-1, keepdims=True),
                   jnp.sum(jnp.where(first, 0.0, sq), axis=-1, keepdims=True)) * (1.0 / HEAD_DIM)
    y = t * lax.rsqrt(ms + NORM_EPS) * gain
    if cos is not None:
        rot = jnp.where(lane_lo, pltpu.roll(y, LANES - 16, 1), pltpu.roll(y, 16, 1))
        y = y * cos + rot * sin_s
    if scale != 1.0:
        y = y * scale
    return y


def _proj_kernel(*refs, use_rope, tm):
    if use_rope:
        (x_ref, sc_ref, sh_ref, g_ref, w_ref, cs_ref, hg_ref, cos_ref, sin_ref), outs = refs[:9], refs[9:]
    else:
        (x_ref, sc_ref, sh_ref, g_ref, w_ref, cs_ref, hg_ref), outs = refs[:7], refs[7:]
        cos_ref = sin_ref = None
    (zr_ref, zi_ref, qb_ref, kb_ref, vb_ref, uc_ref, cb_ref, qd_ref, kd_ref, vd_ref,
     sza_ref, szb_ref, szc_ref, szd_ref, gate_ref) = outs

    x = x_ref[...]
    y = x * lax.rsqrt(jnp.mean(x * x, axis=-1, keepdims=True) + NORM_EPS) * g_ref[...]
    h = _bf(y * (1.0 + sc_ref[...]) + sh_ref[...])

    def proj(c0, width):
        return _dot(h, w_ref[:, c0:c0 + width])

    col = 0
    a = _bf(proj(col, FOURIER_W))
    zz = _dot(a, cs_ref[...])
    zr_ref[...] = zz[:, :FOURIER_W].astype(zr_ref.dtype)
    zi_ref[...] = zz[:, FOURIER_W:].astype(zi_ref.dtype)
    col += FOURIER_W

    if use_rope:
        cos = cos_ref[...]
        sin_s = sin_ref[...]
    else:
        cos = sin_s = None
    lane = lax.broadcasted_iota(jnp.int32, (tm, LANES), 1)
    lane_lo = (lane & 31) < 16
    qscale = HEAD_DIM ** -0.5

    def normed(c0, width, gain_row, out_ref, scale):
        t = proj(c0, width)
        gain = hg_ref[gain_row:gain_row + 1, :]
        for s in range(width // LANES):
            ts = t[:, s * LANES:(s + 1) * LANES]
            out_ref[:, s * LANES:(s + 1) * LANES] = _bf(
                _head_norm_rope(ts, gain, cos, sin_s, lane_lo, scale))

    head_lo = lane < HEAD_DIM

    def store_duplicated(pair, out_ref):
        swapped = pltpu.roll(pair, HEAD_DIM, 1)
        out_ref[:, :LANES] = _bf(jnp.where(head_lo, pair, swapped))
        out_ref[:, LANES:] = _bf(jnp.where(head_lo, swapped, pair))

    normed(col, SWA_W, 0, qb_ref, qscale * LOG2E); col += SWA_W
    kv = proj(col, 2 * LANES); col += 2 * LANES
    store_duplicated(_head_norm_rope(kv[:, :LANES], hg_ref[1:2, :], cos, sin_s, lane_lo, 1.0), kb_ref)
    store_duplicated(kv[:, LANES:], vb_ref)
    c3 = proj(col, 3 * CONV_W); col += 3 * CONV_W
    uc_ref[...] = _bf(c3[:, 2 * CONV_W:] * c3[:, :CONV_W])
    cb_ref[...] = _bf(c3[:, CONV_W:2 * CONV_W])
    normed(col, DIFF_W, 2, qd_ref, qscale * LOG2E); col += DIFF_W
    normed(col, DIFF_W, 3, kd_ref, 1.0); col += DIFF_W
    vd_ref[...] = _bf(proj(col, DIFF_W)); col += DIFF_W
    for ref, width in ((sza_ref, FOURIER_W), (szb_ref, SWA_W), (szc_ref, CONV_W), (szd_ref, DIFF_W)):
        z = proj(col, width)
        ref[...] = _bf(z * jax.nn.sigmoid(z))
        col += width
    d = x.shape[-1]
    for j in range(N_BRANCH):
        gate_ref[:, j * d:(j + 1) * d] = _bf(jax.nn.sigmoid(proj(col, d)))
        col += d


_PROJ_OUT_W = (FOURIER_W, FOURIER_W, SWA_W, 2 * LANES, 2 * LANES, CONV_W, CONV_W, DIFF_W, DIFF_W, DIFF_W,
               FOURIER_W, SWA_W, CONV_W, DIFF_W)


def _project(x, scale, shift, norm_g, w_ext, cs_bd, head_gains, rope, tm):
    b, n, d = x.shape
    use_rope = rope is not None
    tok = lambda w: pl.BlockSpec((None, tm, w), lambda bi, i: (bi, i, 0))
    per_b = pl.BlockSpec((None, 1, d), lambda bi, i: (bi, 0, 0))
    in_specs = [tok(d), per_b, per_b, _const_spec((1, d)), _const_spec(w_ext.shape), _const_spec(cs_bd.shape),
                _const_spec(head_gains.shape)]
    args = [x, scale, shift, norm_g.reshape(1, d), w_ext, cs_bd, head_gains]
    if use_rope:
        tab = pl.BlockSpec((tm, LANES), lambda bi, i: (i, 0))
        in_specs += [tab, tab]
        args += list(rope)
    widths = _PROJ_OUT_W + (N_BRANCH * d,)
    dtypes = [jnp.float32 if (j < 2 and n > 2 * DFT_MINOR) else jnp.bfloat16 for j in range(len(widths))]
    return pl.pallas_call(
        functools.partial(_proj_kernel, use_rope=use_rope, tm=tm),
        grid=(b, n // tm),
        in_specs=in_specs,
        out_specs=[tok(w) for w in widths],
        out_shape=[jax.ShapeDtypeStruct((b, n, w), dt) for w, dt in zip(widths, dtypes)],
        compiler_params=_params(2),
        name="proj",
    )(*args)


def _dft1_kernel(zr_ref, zi_ref, f_ref, tc_ref, ts_ref, o_ref, *, r, tn2):
    for j in range(tn2):
        z = _bf(jnp.concatenate([zr_ref[:, j, :], zi_ref[:, j, :]], axis=0))
        a = _dot(f_ref[...], z)
        ar, ai = a[:r], a[r:]
        tc = jnp.concatenate([tc_ref[j]] * (FOURIER_W // LANES), axis=-1)
        ts = jnp.concatenate([ts_ref[j]] * (FOURIER_W // LANES), axis=-1)
        o_ref[0, j] = _bf(ar * tc + ai * ts)
        o_ref[1, j] = _bf(ai * tc - ar * ts)


def _left_matmul_kernel(m_ref, x_ref, o_ref):
    o_ref[...] = _bf(_dot(m_ref[...], x_ref[...]))


def _left_matmul(mat, x, tc):
    b, k, c = x.shape
    rows = mat.shape[0]
    return pl.pallas_call(
        _left_matmul_kernel,
        grid=(b, c // tc),
        in_specs=[_const_spec(mat.shape), pl.BlockSpec((None, k, tc), lambda bi, i: (bi, 0, i))],
        out_specs=pl.BlockSpec((None, rows, tc), lambda bi, i: (bi, 0, i)),
        out_shape=jax.ShapeDtypeStruct((b, rows, c), jnp.bfloat16),
        compiler_params=_params(2),
        name="dft2",
    )(mat, x)


def _dft_tables(n):
    inv = 1.0 / math.sqrt(n)
    if n <= 2 * DFT_MINOR:
        k = np.arange(n)
        ang = 2.0 * np.pi * ((k[:, None] * k[None, :]) % n) / n
        return dict(direct=jnp.asarray(np.concatenate([np.cos(ang), np.sin(ang)], axis=1) * inv, jnp.bfloat16))
    r = n // DFT_MINOR
    k1 = np.arange(r)
    a1 = 2.0 * np.pi * ((k1[:, None] * k1[None, :]) % r) / r
    c1, s1 = np.cos(a1), np.sin(a1)
    f1 = np.block([[c1, s1], [-s1, c1]])
    n2 = np.arange(DFT_MINOR)
    at = 2.0 * np.pi * (n2[:, None] * k1[None, :]) / n
    tw_c = np.repeat(np.cos(at)[:, :, None], LANES, axis=2)
    tw_s = np.repeat(np.sin(at)[:, :, None], LANES, axis=2)
    a2 = 2.0 * np.pi * ((n2[:, None] * n2[None, :]) % DFT_MINOR) / DFT_MINOR
    f2 = np.concatenate([np.cos(a2), np.sin(a2)], axis=1) * inv
    return dict(f1=jnp.asarray(f1, jnp.bfloat16), tw_c=jnp.asarray(tw_c, jnp.float32),
                tw_s=jnp.asarray(tw_s, jnp.float32), f2=jnp.asarray(f2, jnp.bfloat16))


def _fourier_positions(zr, zi, tabs):
    b, n, w = zr.shape
    if "direct" in tabs:
        return _left_matmul(tabs["direct"], jnp.concatenate([zr, zi], axis=1), w)
    r = n // DFT_MINOR
    tn2 = 8
    zin = pl.BlockSpec((None, r, tn2, w), lambda bi, i: (bi, 0, i, 0))
    tw = pl.BlockSpec((tn2, r, LANES), lambda bi, i: (i, 0, 0))
    g = pl.pallas_call(
        functools.partial(_dft1_kernel, r=r, tn2=tn2),
        grid=(b, DFT_MINOR // tn2),
        in_specs=[zin, zin, _const_spec(tabs["f1"].shape), tw, tw],
        out_specs=pl.BlockSpec((None, 2, tn2, r, w), lambda bi, i: (bi, 0, i, 0, 0)),
        out_shape=jax.ShapeDtypeStruct((b, 2, DFT_MINOR, r, w), jnp.bfloat16),
        compiler_params=_params(2),
        name="dft1",
    )(zr.reshape(b, r, DFT_MINOR, w), zi.reshape(b, r, DFT_MINOR, w), tabs["f1"], tabs["tw_c"], tabs["tw_s"])
    y = _left_matmul(tabs["f2"], g.reshape(b, 2 * DFT_MINOR, r * w), (16 if r % 16 == 0 else 4) * w)
    return y.reshape(b, n, w)


def _win_kernel(*refs, has_local, nb, nq):
    if has_local:
        bound_ref, sink_ref, q_ref, kp_ref, kc_ref, kn_ref, vp_ref, vc_ref, vn_ref, kx_ref, vx_ref, o_ref = refs
    else:
        bound_ref, sink_ref, q_ref, kx_ref, vx_ref, o_ref = refs
    i = pl.program_id(1)
    tq = BLOCK
    group = SWA_HEADS // SWA_KV
    lane = lax.broadcasted_iota(jnp.int32, (1, LANES), 1)
    halves = (lane < HEAD_DIM, lane >= HEAD_DIM)
    row_lo = lax.broadcasted_iota(jnp.int32, (LANES, 1), 0) < HEAD_DIM
    if has_local:
        kr = lax.broadcasted_iota(jnp.int32, (3 * BLOCK, tq), 0)
        qc = lax.broadcasted_iota(jnp.int32, (3 * BLOCK, tq), 1)
        in_band = jnp.abs(kr - BLOCK - qc) <= WINDOW
    bound = bound_ref[0, 0]

    def attend(s, j, use_bound):
        sl = slice(j * LANES, (j + 1) * LANES)
        rows = slice(s * BLOCK, (s + 1) * BLOCK)
        if has_local:
            def window(prev_ref, cur_ref, next_ref):
                first = prev_ref[:, sl] if s == 0 else cur_ref[(s - 1) * BLOCK:s * BLOCK, sl]
                last = next_ref[:, sl] if s == nq - 1 else cur_ref[(s + 1) * BLOCK:(s + 2) * BLOCK, sl]
                return [first, cur_ref[rows, sl], last]
            kd = jnp.concatenate(window(kp_ref, kc_ref, kn_ref) + [kx_ref[:, sl]], axis=0)
            vd = jnp.concatenate(window(vp_ref, vc_ref, vn_ref) + [vx_ref[:, sl]], axis=0)
        else:
            kd = kx_ref[:, sl]
            vd = vx_ref[:, sl]
        q_stack = []
        for t in range(group // 2):
            slab = j * (group // 2) + t
            q2 = q_ref[rows, slab * LANES:(slab + 1) * LANES]
            q_stack += [jnp.where(halves[e], q2, jnp.zeros_like(q2)) for e in range(2)]
        st = _dot_nt(kd, jnp.concatenate(q_stack, axis=0))
        if has_local:
            kpos = (i * nq + s - 1) * BLOCK + kr
            bias = jnp.where(in_band & (kpos >= 0) & (kpos < nb * BLOCK), 0.0, NEG_INF)
            st = jnp.concatenate([st[:3 * BLOCK] + jnp.concatenate([bias] * group, axis=1), st[3 * BLOCK:]], axis=0)
        sk = sink_ref[j:j + 1, :]
        m = jnp.maximum(bound, sk) if use_bound else jnp.maximum(jnp.max(st, axis=0, keepdims=True), sk)
        p = jnp.exp2(st - m)
        inv = 1.0 / (jnp.sum(p, axis=0, keepdims=True) + jnp.exp2(sk - m))
        ot = lax.dot_general(vd, _bf(p), (((0,), (0,)), ((), ())), preferred_element_type=jnp.float32) * inv
        for t in range(group // 2):
            slab = j * (group // 2) + t
            pair = jnp.where(row_lo, ot[:, (2 * t) * tq:(2 * t + 1) * tq], ot[:, (2 * t + 1) * tq:(2 * t + 2) * tq])
            o_ref[rows, slab * LANES:(slab + 1) * LANES] = _bf(pair.T)

    shift_safe = bound <= SAFE_SHIFT

    @pl.when(shift_safe)
    def _():
        for s in range(nq):
            for j in range(SWA_KV):
                attend(s, j, True)

    @pl.when(jnp.logical_not(shift_safe))
    def _():
        for s in range(nq):
            for j in range(SWA_KV):
                attend(s, j, False)


def _score_bound(q_gain, k_gain):
    g = jnp.max(jnp.abs(q_gain.astype(jnp.float32))) * jnp.max(jnp.abs(k_gain.astype(jnp.float32)))
    return (1.02 * HEAD_DIM ** 0.5 * LOG2E * g).reshape(1, 1)


def _window_attention(qb, kb, vb, kx, vx, sink, bound, has_local):
    b, n, _ = qb.shape
    nb = n // BLOCK
    nq = 4 if nb % 4 == 0 else 1
    lx = kx.shape[1]
    qspec = pl.BlockSpec((None, nq * BLOCK, SWA_W), lambda bi, i: (bi, i, 0))
    ctx = pl.BlockSpec((None, lx, 2 * LANES), lambda bi, i: (bi, 0, 0))
    sink_rows = jnp.repeat(sink.astype(jnp.float32) * LOG2E, BLOCK).reshape(SWA_KV, -1)
    args = [bound, sink_rows, qb]
    specs = [pl.BlockSpec(memory_space=pltpu.SMEM), _const_spec(sink_rows.shape), qspec]
    if has_local:
        prv = pl.BlockSpec((None, BLOCK, 2 * LANES), lambda bi, i: (bi, jnp.maximum(i * nq - 1, 0), 0))
        cur = pl.BlockSpec((None, nq * BLOCK, 2 * LANES), lambda bi, i: (bi, i, 0))
        nxt = pl.BlockSpec((None, BLOCK, 2 * LANES), lambda bi, i: (bi, jnp.minimum((i + 1) * nq, nb - 1), 0))
        args += [kb, kb, kb, vb, vb, vb]
        specs += [prv, cur, nxt, prv, cur, nxt]
    args += [kx, vx]
    specs += [ctx, ctx]
    return pl.pallas_call(
        functools.partial(_win_kernel, has_local=has_local, nb=nb, nq=nq),
        grid=(b, nb // nq),
        in_specs=specs,
        out_specs=qspec,
        out_shape=jax.ShapeDtypeStruct((b, n, SWA_W), jnp.bfloat16),
        compiler_params=_params(2),
        name="win",
    )(*args)


def _diff_kernel(lam_ref, bound_ref, q_ref, *refs, n_parts, tk, out_scale):
    k_parts, v_parts = refs[:n_parts], refs[n_parts:2 * n_parts]
    g_ref, o_ref, k_ref, vt_ref, acc_ref, s0_ref, s1_ref, p0_ref, p1_ref = refs[2 * n_parts:]
    qi = pl.program_id(2)
    tq = q_ref.shape[0]
    n_chunks = vt_ref.shape[0]

    @pl.when(qi == 0)
    def _():
        row = 0
        for kp, vp in zip(k_parts, v_parts):
            rows = kp.shape[0]
            k_ref[row:row + rows, :] = kp[...]
            done = 0
            while done < rows:
                ci, off = divmod(row + done, tk)
                take = min(tk - off, rows - done)
                vt_ref[ci, :LANES, off:off + take] = _bf(vp[done:done + take, :].astype(jnp.float32).T)
                done += take
            row += rows

        vt_ref[:, LANES:, :] = jnp.ones((n_chunks, SUM_ROWS, tk), jnp.bfloat16)

    q = q_ref[...]
    lane = lax.broadcasted_iota(jnp.int32, (1, LANES), 1)
    qm = (jnp.where(lane < HEAD_DIM, q, jnp.zeros_like(q)), jnp.where(lane >= HEAD_DIM, q, jnp.zeros_like(q)))
    s_bufs = (s0_ref, s1_ref)
    p_bufs = (p0_ref, p1_ref)

    bound = bound_ref[0, 0]
    shift_safe = bound <= SAFE_SHIFT

    @pl.when(shift_safe)
    def _():
        q_both = jnp.concatenate(qm, axis=0)

        def chunk(ci, carry):
            denom, pv = carry
            c0 = pl.multiple_of(ci * tk, tk)
            p = jnp.exp2(_dot_nt(k_ref[pl.ds(c0, tk), :], q_both) - bound)
            return denom + jnp.sum(p, axis=0, keepdims=True), pv + _dot(vt_ref[ci, :LANES, :], _bf(p))
        init = (jnp.zeros((1, 2 * tq), jnp.float32), jnp.zeros((LANES, 2 * tq), jnp.float32))
        denom, pv = lax.fori_loop(0, n_chunks, chunk, init, unroll=True)
        acc_ref[:LANES, :] = pv
        acc_ref[LANES:, :] = jnp.broadcast_to(denom, (SUM_ROWS, 2 * tq))

    @pl.when(jnp.logical_not(shift_safe))
    def _():
        acc_ref[...] = jnp.zeros_like(acc_ref)
        _diff_online(qm, k_ref, vt_ref, acc_ref, s_bufs, p_bufs, tk=tk, tq=tq, n_chunks=n_chunks)

    lam = lam_ref[0, 0]
    a1 = acc_ref[:, :tq]
    a2 = acc_ref[:, tq:]
    o = a1[:LANES] * (1.0 / a1[LANES:LANES + 1]) - lam * (a2[:LANES] * (1.0 / a2[LANES:LANES + 1]))
    y = o * lax.rsqrt(jnp.mean(o * o, axis=0, keepdims=True) + NORM_EPS) * g_ref[...] * out_scale
    o_ref[...] = _bf(y.T)


def _diff_online(qm, k_ref, vt_ref, acc_ref, s_bufs, p_bufs, *, tk, tq, n_chunks):
    def scores(ci, slot):
        c0 = ci * tk if isinstance(ci, int) else pl.multiple_of(ci * tk, tk)
        k = k_ref[pl.ds(c0, tk), :]
        mx = []
        for c in range(2):
            st = _dot_nt(k, qm[c])
            s_bufs[slot][c] = st
            mx.append(jnp.max(st, axis=0, keepdims=True))
        return tuple(mx)

    def probs(slot, mx, m_run):
        m_new, alpha = [], []
        for c in range(2):
            m = jnp.maximum(m_run[c], mx[c])
            alpha.append(jnp.exp2(m_run[c] - m))
            p_bufs[slot][c] = _bf(jnp.exp2(s_bufs[slot][c] - m))
            m_new.append(m)
        return tuple(m_new), tuple(alpha)

    def accumulate(ci, slot, alpha):
        vt = vt_ref[ci]
        for c in range(2):
            cols = slice(c * tq, (c + 1) * tq)
            acc_ref[:, cols] = alpha[c] * acc_ref[:, cols] + _dot(vt, p_bufs[slot][c])

    def step(t, par, do_scores, do_probs, do_acc, state):
        m_run, mx, alpha = state
        mx_next = scores(t + 2, par) if do_scores else mx
        if do_probs:
            m_run, alpha_next = probs(1 - par, mx, m_run)
        else:
            alpha_next = alpha
        if do_acc:
            accumulate(t, par, alpha)
        return m_run, mx_next, alpha_next

    neg = jnp.full((1, tq), NEG_INF, jnp.float32)
    one = jnp.ones((1, tq), jnp.float32)
    state = ((neg, neg), (neg, neg), (one, one))
    for t in (-2, -1):
        state = step(t, t % 2, t + 2 < n_chunks, 0 <= t + 1 < n_chunks, False, state)
    n_steady = max(n_chunks - 2, 0)

    def pair(j, state):
        state = step(2 * j, 0, True, True, True, state)
        return step(2 * j + 1, 1, True, True, True, state)

    state = lax.fori_loop(0, n_steady // 2, pair, state)
    if n_steady % 2:
        state = step(n_steady - 1, 0, True, True, True, state)
    for t in range(n_steady, n_chunks):
        state = step(t, t % 2, False, t + 1 < n_chunks, True, state)


def _pick_tk(nk, cap):
    best = LANES
    for t in range(LANES, min(cap, nk) + 1, LANES):
        if nk % t == 0:
            best = t
    return best


def _diff_attention(qd, k_parts, v_parts, lam, bound, subln_g, lam_init, tq, tk_cap):
    b, n, _ = qd.shape
    nk = sum(k.shape[1] for k in k_parts)
    tk = _pick_tk(nk, tk_cap)
    smem = pl.BlockSpec(memory_space=pltpu.SMEM)
    qspec = pl.BlockSpec((None, tq, LANES), lambda bi, h, i: (bi, i, h))
    kvspecs = [pl.BlockSpec((None, k.shape[1], LANES), lambda bi, h, i: (bi, 0, h)) for k in k_parts]
    gain = jnp.broadcast_to(subln_g.astype(jnp.float32)[:, None], (LANES, tq))
    return pl.pallas_call(
        functools.partial(_diff_kernel, n_parts=len(k_parts), tk=tk, out_scale=1.0 - lam_init),
        grid=(b, DIFF_HEADS, n // tq),
        in_specs=[smem, smem, qspec] + kvspecs + kvspecs + [_const_spec((LANES, tq))],
        out_specs=qspec,
        out_shape=jax.ShapeDtypeStruct((b, n, DIFF_W), jnp.bfloat16),
        scratch_shapes=[pltpu.VMEM((nk, LANES), jnp.bfloat16),
                        pltpu.VMEM((nk // tk, LANES + SUM_ROWS, tk), jnp.bfloat16),
                        pltpu.VMEM((LANES + SUM_ROWS, 2 * tq), jnp.float32),
                        pltpu.VMEM((2, tk, tq), jnp.float32), pltpu.VMEM((2, tk, tq), jnp.float32),
                        pltpu.VMEM((2, tk, tq), jnp.bfloat16), pltpu.VMEM((2, tk, tq), jnp.bfloat16)],
        compiler_params=_params(3),
        name="diff",
    )(lam, bound, qd, *k_parts, *v_parts, gain)


def _merge_kernel(x_ref, gt_ref, ya_ref, sza_ref, yb_ref, szb_ref, uc_ref, up_ref, un_ref, cb_ref, szc_ref,
                  yd_ref, szd_ref, g_ref, cw_ref, wa_ref, wb_ref, wc_ref, wd_ref, wo_ref, o_ref, *, tm, nt):
    i = pl.program_id(1)
    f32 = jnp.float32
    u = uc_ref[...].astype(f32)
    row = lax.broadcasted_iota(jnp.int32, (tm, 1), 0)
    prev_row = jnp.where(i > 0, up_ref[7:8, :].astype(f32), 0.0)
    next_row = jnp.where(i < nt - 1, un_ref[0:1, :].astype(f32), 0.0)
    u_prev = jnp.where(row == 0, prev_row, pltpu.roll(u, 1, 0))
    u_next = jnp.where(row == tm - 1, next_row, pltpu.roll(u, tm - 1, 0))
    cw = cw_ref[...]
    conv = u_prev * cw[0:1, :] + u * cw[1:2, :] + u_next * cw[2:3, :]
    yc = cb_ref[...].astype(f32) * conv
    d = x_ref.shape[-1]
    branches = ((ya_ref[...].astype(f32), sza_ref, wa_ref), (yb_ref[...].astype(f32), szb_ref, wb_ref),
                (yc, szc_ref, wc_ref), (yd_ref[...].astype(f32), szd_ref, wd_ref))
    mixed = jnp.zeros((tm, d), f32)
    for j, (y, sz_ref, w_ref) in enumerate(branches):
        t = _dot(_bf(y * sz_ref[...].astype(f32)), w_ref[...])
        mixed = mixed + g_ref[:, j * d:(j + 1) * d].astype(f32) * t
    out = _dot(_bf(mixed), wo_ref[...])
    o_ref[...] = x_ref[...] + gt_ref[...] * out


def _merge(x, gate, ya, sza, yb, szb, uc, cb, szc, yd, szd, g, conv_w, w_a, w_b, w_c, w_d, w_out, tm):
    b, n, d = x.shape
    nt = n // tm
    tok = lambda w: pl.BlockSpec((None, tm, w), lambda bi, i: (bi, i, 0))
    hb = tm // 8
    halo_p = pl.BlockSpec((None, 8, CONV_W), lambda bi, i: (bi, jnp.maximum(i * hb - 1, 0), 0))
    halo_n = pl.BlockSpec((None, 8, CONV_W), lambda bi, i: (bi, jnp.minimum((i + 1) * hb, n // 8 - 1), 0))
    per_b = pl.BlockSpec((None, 1, d), lambda bi, i: (bi, 0, 0))
    ws = [_bf(w_a), _bf(w_b), _bf(w_c), _bf(w_d), _bf(w_out)]
    return pl.pallas_call(
        functools.partial(_merge_kernel, tm=tm, nt=nt),
        grid=(b, nt),
        in_specs=[tok(d), per_b, tok(FOURIER_W), tok(FOURIER_W), tok(SWA_W), tok(SWA_W), tok(CONV_W), halo_p, halo_n,
                  tok(CONV_W), tok(CONV_W), tok(DIFF_W), tok(DIFF_W), tok(N_BRANCH * d), _const_spec(conv_w.shape)]
                 + [_const_spec(w.shape) for w in ws],
        out_specs=tok(d),
        out_shape=jax.ShapeDtypeStruct((b, n, d), jnp.float32),
        compiler_params=_params(2),
        name="merge",
    )(x, gate, ya, sza, yb, szb, uc, uc, uc, cb, szc, yd, szd, g, conv_w.astype(jnp.float32), *ws)


def _rope_tables(n):
    f32 = np.float32
    rows = n // GRID_W
    row = np.broadcast_to(np.arange(rows, dtype=f32)[:, None], (rows, GRID_W)).reshape(-1)
    col = np.broadcast_to(np.arange(GRID_W, dtype=f32)[None, :], (rows, GRID_W)).reshape(-1)
    nf = HEAD_DIM // 4
    inv = f32(ROPE_BASE) ** (-np.arange(nf, dtype=f32) / f32(nf))
    ar = row[:, None] * inv[None, :]
    ac = col[:, None] * inv[None, :]
    cos = np.concatenate([np.cos(ar), np.cos(ar), np.cos(ac), np.cos(ac)], axis=-1)
    sin = np.concatenate([np.sin(ar), np.sin(ar), np.sin(ac), np.sin(ac)], axis=-1)
    sign = np.where((np.arange(HEAD_DIM) % 32) < 16, f32(-1.0), f32(1.0))
    return jnp.asarray(np.tile(cos, (1, 2)), jnp.float32), jnp.asarray(np.tile(sin * sign, (1, 2)), jnp.float32)


def _channel_dft():
    k = np.arange(HEAD_DIM)
    ang = 2.0 * np.pi * ((k[:, None] * k[None, :]) % HEAD_DIM) / HEAD_DIM
    eye = np.eye(FOURIER_W // HEAD_DIM)
    inv = 1.0 / math.sqrt(HEAD_DIM)
    return jnp.asarray(np.concatenate([np.kron(eye, np.cos(ang)), -np.kron(eye, np.sin(ang))], axis=1) * inv,
                       jnp.bfloat16)


def _mix(stream, scale, shift, gate, ctx_kv, lp, consts, lam, lam_init, rope, dft, tm, tq, tk, has_local):
    parts = _project(stream, scale, shift, lp["norm_g"], lp["w_ext"], consts["cs_bd"], lp["head_gains"], rope, tm)
    zr, zi, qb, kb, vb, uc, cb, qd, kd, vd, sza, szb, szc, szd, g = parts
    if ctx_kv is None:
        kx_b, vx_b, k_parts, v_parts = kb, vb, (kd,), (vd,)
    else:
        kx_b, vx_b, kx_d, vx_d = ctx_kv
        k_parts, v_parts = (kx_d, kd), (vx_d, vd)
    ya = _fourier_positions(zr, zi, dft)
    yb = _window_attention(qb, kb, vb, kx_b, vx_b, lp["sink"], lp["bound_b"], has_local)
    yd = _diff_attention(qd, k_parts, v_parts, lam, lp["bound_d"], lp["subln"], lam_init, tq, tk)
    new = _merge(stream, gate, ya, sza, yb, szb, uc, cb, szc, yd, szd, g, lp["conv_w"],
                 lp["w_o_a"], lp["w_o_b"], lp["w_o_c"], lp["w_o_d"], lp["w_out"], tm)
    return new, (kb, vb, kd, vd)


def kernel(x, c, ctx, c_ctx, norm_g, w_mod, b_mod, w_in, q_norm_b, k_norm_b, sink_b, conv_w, q_norm_d, k_norm_d,
           lam_q1, lam_k1, lam_q2, lam_k2, subln_d, w_o_a, w_o_b, w_o_c, w_o_d, w_out):
    b, n, d = x.shape
    lc = ctx.shape[1]
    depth = w_in.shape[0]
    rope = _rope_tables(n)
    consts = dict(cs_bd=_channel_dft())
    dft_x = _dft_tables(n)
    dft_c = _dft_tables(lc)
    c_rows = jnp.zeros((8, d), jnp.float32).at[:b].set(c).at[b].set(c_ctx)
    tm = min(512, n)
    tq = min(512, n)
    tk = 1280
    for l in range(depth):
        last = l == depth - 1
        lam_init = 0.8 - 0.6 * math.exp(-0.3 * l)
        lam_vecs = jnp.stack([lam_q1[l], lam_k1[l], lam_q2[l], lam_k2[l]]).astype(jnp.float32)
        mod, lam_o = _modulation(c_rows, w_mod[l], b_mod[l], lam_vecs, lam_init)
        lam = lam_o[0:1, 0:1]
        shift, scale, gate = (mod[:b, None, j * d:(j + 1) * d] for j in range(3))
        shift_c, scale_c, gate_c = (jnp.broadcast_to(mod[b:b + 1, None, j * d:(j + 1) * d], (b, 1, d))
                                    for j in range(3))
        tile2 = lambda v: jnp.tile(v.astype(jnp.float32), 2)
        lp = dict(norm_g=norm_g[l], w_ext=_bf(w_in[l]), sink=sink_b[l].astype(jnp.float32),
                  head_gains=jnp.stack([tile2(q_norm_b[l]), tile2(k_norm_b[l]), tile2(q_norm_d[l]), tile2(k_norm_d[l])]),
                  bound_b=_score_bound(q_norm_b[l], k_norm_b[l]), bound_d=_score_bound(q_norm_d[l], k_norm_d[l]),
                  subln=subln_d[l], conv_w=conv_w[l], w_o_a=w_o_a[l], w_o_b=w_o_b[l], w_o_c=w_o_c[l],
                  w_o_d=w_o_d[l], w_out=w_out[l])
        if last:
            parts = _project(ctx, scale_c, shift_c, lp["norm_g"], lp["w_ext"], consts["cs_bd"], lp["head_gains"],
                             None, min(256, lc))
            ctx_kv = (parts[3], parts[4], parts[8], parts[9])
        else:
            ctx, ctx_kv = _mix(ctx, scale_c, shift_c, gate_c, None, lp, consts, lam, lam_init, None, dft_c,
                               min(256, lc), min(256, lc), tk, False)
        x, _ = _mix(x, scale, shift, gate, ctx_kv, lp, consts, lam, lam_init, rope, dft_x, tm, tq, tk, True)
    return x
```

```python
import functools
import math

import jax
import jax.numpy as jnp
import numpy as np
from jax import lax
from jax.experimental import pallas as pl
from jax.experimental.pallas import tpu as pltpu

HEAD_DIM = 64
LANES = 128
FOURIER_W = 384
SWA_HEADS = 8
SWA_KV = 2
SWA_W = SWA_HEADS * HEAD_DIM
WINDOW = 128
BLOCK = 128
CONV_W = 384
DIFF_HEADS = 4
DIFF_W = DIFF_HEADS * 2 * HEAD_DIM
N_BRANCH = 4
ROPE_BASE = 10000.0
NORM_EPS = 1e-6
NEG_INF = -1e30
GRID_W = 64
DFT_MINOR = 128
SUM_ROWS = 16
LOG2E = math.log2(math.e)
SAFE_SHIFT = 60.0
VMEM_LIMIT = 56 * 1024 * 1024


def _bf(x):
    return x.astype(jnp.bfloat16)


def _dot(a, b):
    return jnp.dot(a, b, preferred_element_type=jnp.float32)


def _dot_nt(a, b):
    return lax.dot_general(a, b, (((1,), (1,)), ((), ())), preferred_element_type=jnp.float32)


def _params(n_axes):
    return pltpu.CompilerParams(dimension_semantics=("arbitrary",) * n_axes, vmem_limit_bytes=VMEM_LIMIT)


def _const_spec(shape):
    nd = len(shape)
    return pl.BlockSpec(shape, lambda *_: (0,) * nd, pipeline_mode=pl.Buffered(1))


def _mod_kernel(c_ref, w_ref, b_ref, lam_ref, mod_ref, lamo_ref, *, lam_init):
    c = c_ref[...]
    s = c * jax.nn.sigmoid(c)
    mod_ref[...] = _dot(_bf(s), w_ref[...]) + b_ref[...]
    lv = lam_ref[...]
    a1 = jnp.sum(lv[0:1, :] * lv[1:2, :], axis=-1, keepdims=True)
    a2 = jnp.sum(lv[2:3, :] * lv[3:4, :], axis=-1, keepdims=True)
    lam = jnp.exp(a1) - jnp.exp(a2) + lam_init
    lamo_ref[...] = jnp.broadcast_to(lam, lamo_ref.shape)


def _modulation(c_rows, w_mod, b_mod, lam_vecs, lam_init):
    r, d = c_rows.shape
    return pl.pallas_call(
        functools.partial(_mod_kernel, lam_init=lam_init),
        out_shape=(jax.ShapeDtypeStruct((r, 3 * d), jnp.float32),
                   jax.ShapeDtypeStruct((8, LANES), jnp.float32)),
        name="mod",
    )(c_rows, _bf(w_mod), b_mod.reshape(1, 3 * d), lam_vecs)


def _head_norm_rope(t, gain, cos, sin_s, lane_lo, scale):
    sq = t * t
    first = lax.broadcasted_iota(jnp.int32, t.shape, 1) < HEAD_DIM
    ms = jnp.where(first, jnp.sum(jnp.where(first, sq, 0.0), axis=-1, keepdims=True),
                   jnp.sum(jnp.where(first, 0.0, sq), axis=-1, keepdims=True)) * (1.0 / HEAD_DIM)
    y = t * lax.rsqrt(ms + NORM_EPS) * gain
    if cos is not None:
        rot = jnp.where(lane_lo, pltpu.roll(y, LANES - 16, 1), pltpu.roll(y, 16, 1))
        y = y * cos + rot * sin_s
    if scale != 1.0:
        y = y * scale
    return y


def _proj_kernel(*refs, use_rope, tm):
    if use_rope:
        (x_ref, sc_ref, sh_ref, g_ref, w_ref, cs_ref, hg_ref, cos_ref, sin_ref), outs = refs[:9], refs[9:]
    else:
        (x_ref, sc_ref, sh_ref, g_ref, w_ref, cs_ref, hg_ref), outs = refs[:7], refs[7:]
        cos_ref = sin_ref = None
    (zr_ref, zi_ref, qb_ref, kb_ref, vb_ref, uc_ref, cb_ref, qd_ref, kd_ref, vd_ref,
     sza_ref, szb_ref, szc_ref, szd_ref, gate_ref) = outs

    x = x_ref[...]
    y = x * lax.rsqrt(jnp.mean(x * x, axis=-1, keepdims=True) + NORM_EPS) * g_ref[...]
    h = _bf(y * (1.0 + sc_ref[...]) + sh_ref[...])

    def proj(c0, width):
        return _dot(h, w_ref[:, c0:c0 + width])

    col = 0
    a = _bf(proj(col, FOURIER_W))
    zz = _dot(a, cs_ref[...])
    zr_ref[...] = zz[:, :FOURIER_W].astype(zr_ref.dtype)
    zi_ref[...] = zz[:, FOURIER_W:].astype(zi_ref.dtype)
    col += FOURIER_W

    if use_rope:
        cos = cos_ref[...]
        sin_s = sin_ref[...]
    else:
        cos = sin_s = None
    lane = lax.broadcasted_iota(jnp.int32, (tm, LANES), 1)
    lane_lo = (lane & 31) < 16
    qscale = HEAD_DIM ** -0.5

    def normed(c0, width, gain_row, out_ref, scale):
        t = proj(c0, width)
        gain = hg_ref[gain_row:gain_row + 1, :]
        for s in range(width // LANES):
            ts = t[:, s * LANES:(s + 1) * LANES]
            out_ref[:, s * LANES:(s + 1) * LANES] = _bf(
                _head_norm_rope(ts, gain, cos, sin_s, lane_lo, scale))

    head_lo = lane < HEAD_DIM

    def store_duplicated(pair, out_ref):
        swapped = pltpu.roll(pair, HEAD_DIM, 1)
        out_ref[:, :LANES] = _bf(jnp.where(head_lo, pair, swapped))
        out_ref[:, LANES:] = _bf(jnp.where(head_lo, swapped, pair))

    normed(col, SWA_W, 0, qb_ref, qscale * LOG2E); col += SWA_W
    kv = proj(col, 2 * LANES); col += 2 * LANES
    store_duplicated(_head_norm_rope(kv[:, :LANES], hg_ref[1:2, :], cos, sin_s, lane_lo, 1.0), kb_ref)
    store_duplicated(kv[:, LANES:], vb_ref)
    c3 = proj(col, 3 * CONV_W); col += 3 * CONV_W
    uc_ref[...] = _bf(c3[:, 2 * CONV_W:] * c3[:, :CONV_W])
    cb_ref[...] = _bf(c3[:, CONV_W:2 * CONV_W])
    normed(col, DIFF_W, 2, qd_ref, qscale * LOG2E); col += DIFF_W
    normed(col, DIFF_W, 3, kd_ref, 1.0); col += DIFF_W
    vd_ref[...] = _bf(proj(col, DIFF_W)); col += DIFF_W
    z3 = proj(col, FOURIER_W + SWA_W + CONV_W)
    off = 0
    for ref, width in ((sza_ref, FOURIER_W), (szb_ref, SWA_W), (szc_ref, CONV_W)):
        z = z3[:, off:off + width]
        ref[...] = _bf(z * jax.nn.sigmoid(z))
        off += width
    col += off
    z = proj(col, DIFF_W); col += DIFF_W
    szd_ref[...] = _bf(z * jax.nn.sigmoid(z))
    d = x.shape[-1]
    for j in range(N_BRANCH):
        gate_ref[:, j * d:(j + 1) * d] = _bf(jax.nn.sigmoid(proj(col, d)))
        col += d


_PROJ_OUT_W = (FOURIER_W, FOURIER_W, SWA_W, 2 * LANES, 2 * LANES, CONV_W, CONV_W, DIFF_W, DIFF_W, DIFF_W,
               FOURIER_W, SWA_W, CONV_W, DIFF_W)


def _project(x, scale, shift, norm_g, w_ext, cs_bd, head_gains, rope, tm):
    b, n, d = x.shape
    use_rope = rope is not None
    tok = lambda w: pl.BlockSpec((None, tm, w), lambda bi, i: (bi, i, 0))
    per_b = pl.BlockSpec((None, 1, d), lambda bi, i: (bi, 0, 0))
    in_specs = [tok(d), per_b, per_b, _const_spec((1, d)), _const_spec(w_ext.shape), _const_spec(cs_bd.shape),
                _const_spec(head_gains.shape)]
    args = [x, scale, shift, norm_g.reshape(1, d), w_ext, cs_bd, head_gains]
    if use_rope:
        tab = pl.BlockSpec((tm, LANES), lambda bi, i: (i, 0))
        in_specs += [tab, tab]
        args += list(rope)
    widths = _PROJ_OUT_W + (N_BRANCH * d,)
    dtypes = [jnp.float32 if (j < 2 and n > 2 * DFT_MINOR) else jnp.bfloat16 for j in range(len(widths))]
    return pl.pallas_call(
        functools.partial(_proj_kernel, use_rope=use_rope, tm=tm),
        grid=(b, n // tm),
        in_specs=in_specs,
        out_specs=[tok(w) for w in widths],
        out_shape=[jax.ShapeDtypeStruct((b, n, w), dt) for w, dt in zip(widths, dtypes)],
        compiler_params=_params(2),
        name="proj",
    )(*args)


def _dft1_kernel(zr_ref, zi_ref, f_ref, tc_ref, ts_ref, o_ref, *, r, tn2):
    for j in range(tn2):
        z = _bf(jnp.concatenate([zr_ref[:, j, :], zi_ref[:, j, :]], axis=0))
        a = _dot(f_ref[...], z)
        ar, ai = a[:r], a[r:]
        tc = jnp.concatenate([tc_ref[j]] * (FOURIER_W // LANES), axis=-1)
        ts = jnp.concatenate([ts_ref[j]] * (FOURIER_W // LANES), axis=-1)
        o_ref[0, j] = _bf(ar * tc + ai * ts)
        o_ref[1, j] = _bf(ai * tc - ar * ts)


def _left_matmul_kernel(m_ref, x_ref, o_ref):
    o_ref[...] = _bf(_dot(m_ref[...], x_ref[...]))


def _left_matmul(mat, x, tc):
    b, k, c = x.shape
    rows = mat.shape[0]
    return pl.pallas_call(
        _left_matmul_kernel,
        grid=(b, c // tc),
        in_specs=[_const_spec(mat.shape), pl.BlockSpec((None, k, tc), lambda bi, i: (bi, 0, i))],
        out_specs=pl.BlockSpec((None, rows, tc), lambda bi, i: (bi, 0, i)),
        out_shape=jax.ShapeDtypeStruct((b, rows, c), jnp.bfloat16),
        compiler_params=_params(2),
        name="dft2",
    )(mat, x)


def _dft_tables(n):
    inv = 1.0 / math.sqrt(n)
    if n <= 2 * DFT_MINOR:
        k = np.arange(n)
        ang = 2.0 * np.pi * ((k[:, None] * k[None, :]) % n) / n
        return dict(direct=jnp.asarray(np.concatenate([np.cos(ang), np.sin(ang)], axis=1) * inv, jnp.bfloat16))
    r = n // DFT_MINOR
    k1 = np.arange(r)
    a1 = 2.0 * np.pi * ((k1[:, None] * k1[None, :]) % r) / r
    c1, s1 = np.cos(a1), np.sin(a1)
    f1 = np.block([[c1, s1], [-s1, c1]])
    n2 = np.arange(DFT_MINOR)
    at = 2.0 * np.pi * (n2[:, None] * k1[None, :]) / n
    tw_c = np.repeat(np.cos(at)[:, :, None], LANES, axis=2)
    tw_s = np.repeat(np.sin(at)[:, :, None], LANES, axis=2)
    a2 = 2.0 * np.pi * ((n2[:, None] * n2[None, :]) % DFT_MINOR) / DFT_MINOR
    f2 = np.concatenate([np.cos(a2), np.sin(a2)], axis=1) * inv
    return dict(f1=jnp.asarray(f1, jnp.bfloat16), tw_c=jnp.asarray(tw_c, jnp.float32),
                tw_s=jnp.asarray(tw_s, jnp.float32), f2=jnp.asarray(f2, jnp.bfloat16))


def _fourier_positions(zr, zi, tabs):
    b, n, w = zr.shape
    if "direct" in tabs:
        return _left_matmul(tabs["direct"], jnp.concatenate([zr, zi], axis=1), w)
    r = n // DFT_MINOR
    tn2 = 8
    zin = pl.BlockSpec((None, r, tn2, w), lambda bi, i: (bi, 0, i, 0))
    tw = pl.BlockSpec((tn2, r, LANES), lambda bi, i: (i, 0, 0))
    g = pl.pallas_call(
        functools.partial(_dft1_kernel, r=r, tn2=tn2),
        grid=(b, DFT_MINOR // tn2),
        in_specs=[zin, zin, _const_spec(tabs["f1"].shape), tw, tw],
        out_specs=pl.BlockSpec((None, 2, tn2, r, w), lambda bi, i: (bi, 0, i, 0, 0)),
        out_shape=jax.ShapeDtypeStruct((b, 2, DFT_MINOR, r, w), jnp.bfloat16),
        compiler_params=_params(2),
        name="dft1",
    )(zr.reshape(b, r, DFT_MINOR, w), zi.reshape(b, r, DFT_MINOR, w), tabs["f1"], tabs["tw_c"], tabs["tw_s"])
    y = _left_matmul(tabs["f2"], g.reshape(b, 2 * DFT_MINOR, r * w), (16 if r % 16 == 0 else 4) * w)
    return y.reshape(b, n, w)


def _win_kernel(*refs, has_local, nb, nq):
    if has_local:
        bound_ref, sink_ref, q_ref, kp_ref, kc_ref, kn_ref, vp_ref, vc_ref, vn_ref, kx_ref, vx_ref, o_ref = refs
    else:
        bound_ref, sink_ref, q_ref, kx_ref, vx_ref, o_ref = refs
    i = pl.program_id(1)
    tq = BLOCK
    group = SWA_HEADS // SWA_KV
    lane = lax.broadcasted_iota(jnp.int32, (1, LANES), 1)
    halves = (lane < HEAD_DIM, lane >= HEAD_DIM)
    row_lo = lax.broadcasted_iota(jnp.int32, (LANES, 1), 0) < HEAD_DIM
    if has_local:
        kr = lax.broadcasted_iota(jnp.int32, (3 * BLOCK, tq), 0)
        qc = lax.broadcasted_iota(jnp.int32, (3 * BLOCK, tq), 1)
        in_band = jnp.abs(kr - BLOCK - qc) <= WINDOW
    bound = bound_ref[0, 0]

    def attend(s, j, use_bound):
        sl = slice(j * LANES, (j + 1) * LANES)
        rows = slice(s * BLOCK, (s + 1) * BLOCK)
        if has_local:
            def window(prev_ref, cur_ref, next_ref):
                first = prev_ref[:, sl] if s == 0 else cur_ref[(s - 1) * BLOCK:s * BLOCK, sl]
                last = next_ref[:, sl] if s == nq - 1 else cur_ref[(s + 1) * BLOCK:(s + 2) * BLOCK, sl]
                return [first, cur_ref[rows, sl], last]
            kd = jnp.concatenate(window(kp_ref, kc_ref, kn_ref) + [kx_ref[:, sl]], axis=0)
            vd = jnp.concatenate(window(vp_ref, vc_ref, vn_ref) + [vx_ref[:, sl]], axis=0)
        else:
            kd = kx_ref[:, sl]
            vd = vx_ref[:, sl]
        q_stack = []
        for t in range(group // 2):
            slab = j * (group // 2) + t
            q2 = q_ref[rows, slab * LANES:(slab + 1) * LANES]
            q_stack += [jnp.where(halves[e], q2, jnp.zeros_like(q2)) for e in range(2)]
        st = _dot_nt(kd, jnp.concatenate(q_stack, axis=0))
        if has_local:
            kpos = (i * nq + s - 1) * BLOCK + kr
            bias = jnp.where(in_band & (kpos >= 0) & (kpos < nb * BLOCK), 0.0, NEG_INF)
            st = jnp.concatenate([st[:3 * BLOCK] + jnp.concatenate([bias] * group, axis=1), st[3 * BLOCK:]], axis=0)
        sk = sink_ref[j:j + 1, :]
        m = jnp.maximum(bound, sk) if use_bound else jnp.maximum(jnp.max(st, axis=0, keepdims=True), sk)
        p = jnp.exp2(st - m)
        inv = 1.0 / (jnp.sum(p, axis=0, keepdims=True) + jnp.exp2(sk - m))
        ot = lax.dot_general(vd, _bf(p), (((0,), (0,)), ((), ())), preferred_element_type=jnp.float32) * inv
        for t in range(group // 2):
            slab = j * (group // 2) + t
            pair = jnp.where(row_lo, ot[:, (2 * t) * tq:(2 * t + 1) * tq], ot[:, (2 * t + 1) * tq:(2 * t + 2) * tq])
            o_ref[rows, slab * LANES:(slab + 1) * LANES] = _bf(pair.T)

    shift_safe = bound <= SAFE_SHIFT

    @pl.when(shift_safe)
    def _():
        for s in range(nq):
            for j in range(SWA_KV):
                attend(s, j, True)

    @pl.when(jnp.logical_not(shift_safe))
    def _():
        for s in range(nq):
            for j in range(SWA_KV):
                attend(s, j, False)


def _score_bound(q_gain, k_gain):
    g = jnp.max(jnp.abs(q_gain.astype(jnp.float32))) * jnp.max(jnp.abs(k_gain.astype(jnp.float32)))
    return (1.02 * HEAD_DIM ** 0.5 * LOG2E * g).reshape(1, 1)


def _window_attention(qb, kb, vb, kx, vx, sink, bound, has_local):
    b, n, _ = qb.shape
    nb = n // BLOCK
    nq = 4 if nb % 4 == 0 else 1
    lx = kx.shape[1]
    qspec = pl.BlockSpec((None, nq * BLOCK, SWA_W), lambda bi, i: (bi, i, 0))
    ctx = pl.BlockSpec((None, lx, 2 * LANES), lambda bi, i: (bi, 0, 0))
    sink_rows = jnp.repeat(sink.astype(jnp.float32) * LOG2E, BLOCK).reshape(SWA_KV, -1)
    args = [bound, sink_rows, qb]
    specs = [pl.BlockSpec(memory_space=pltpu.SMEM), _const_spec(sink_rows.shape), qspec]
    if has_local:
        prv = pl.BlockSpec((None, BLOCK, 2 * LANES), lambda bi, i: (bi, jnp.maximum(i * nq - 1, 0), 0))
        cur = pl.BlockSpec((None, nq * BLOCK, 2 * LANES), lambda bi, i: (bi, i, 0))
        nxt = pl.BlockSpec((None, BLOCK, 2 * LANES), lambda bi, i: (bi, jnp.minimum((i + 1) * nq, nb - 1), 0))
        args += [kb, kb, kb, vb, vb, vb]
        specs += [prv, cur, nxt, prv, cur, nxt]
    args += [kx, vx]
    specs += [ctx, ctx]
    return pl.pallas_call(
        functools.partial(_win_kernel, has_local=has_local, nb=nb, nq=nq),
        grid=(b, nb // nq),
        in_specs=specs,
        out_specs=qspec,
        out_shape=jax.ShapeDtypeStruct((b, n, SWA_W), jnp.bfloat16),
        compiler_params=_params(2),
        name="win",
    )(*args)


def _diff_kernel(lam_ref, bound_ref, q_ref, *refs, n_parts, tk, out_scale):
    k_parts, v_parts = refs[:n_parts], refs[n_parts:2 * n_parts]
    g_ref, o_ref, k_ref, vt_ref, acc_ref, s0_ref, s1_ref, p0_ref, p1_ref = refs[2 * n_parts:]
    qi = pl.program_id(2)
    tq = q_ref.shape[0]
    n_chunks = vt_ref.shape[0]

    @pl.when(qi == 0)
    def _():
        row = 0
        for kp, vp in zip(k_parts, v_parts):
            rows = kp.shape[0]
            k_ref[row:row + rows, :] = kp[...]
            done = 0
            while done < rows:
                ci, off = divmod(row + done, tk)
                take = min(tk - off, rows - done)
                vt_ref[ci, :LANES, off:off + take] = _bf(vp[done:done + take, :].astype(jnp.float32).T)
                done += take
            row += rows

        vt_ref[:, LANES:, :] = jnp.ones((n_chunks, SUM_ROWS, tk), jnp.bfloat16)

    q = q_ref[...]
    lane = lax.broadcasted_iota(jnp.int32, (1, LANES), 1)
    qm = (jnp.where(lane < HEAD_DIM, q, jnp.zeros_like(q)), jnp.where(lane >= HEAD_DIM, q, jnp.zeros_like(q)))
    s_bufs = (s0_ref, s1_ref)
    p_bufs = (p0_ref, p1_ref)

    bound = bound_ref[0, 0]
    shift_safe = bound <= SAFE_SHIFT

    @pl.when(shift_safe)
    def _():
        q_both = jnp.concatenate(qm, axis=0)

        def chunk(ci, carry):
            denom, pv = carry
            c0 = pl.multiple_of(ci * tk, tk)
            p = jnp.exp2(_dot_nt(k_ref[pl.ds(c0, tk), :], q_both) - bound)
            return denom + jnp.sum(p, axis=0, keepdims=True), pv + _dot(vt_ref[ci, :LANES, :], _bf(p))
        init = (jnp.zeros((1, 2 * tq), jnp.float32), jnp.zeros((LANES, 2 * tq), jnp.float32))
        denom, pv = lax.fori_loop(0, n_chunks, chunk, init, unroll=True)
        acc_ref[:LANES, :] = pv
        acc_ref[LANES:, :] = jnp.broadcast_to(denom, (SUM_ROWS, 2 * tq))

    @pl.when(jnp.logical_not(shift_safe))
    def _():
        acc_ref[...] = jnp.zeros_like(acc_ref)
        _diff_online(qm, k_ref, vt_ref, acc_ref, s_bufs, p_bufs, tk=tk, tq=tq, n_chunks=n_chunks)

    lam = lam_ref[0, 0]
    a1 = acc_ref[:, :tq]
    a2 = acc_ref[:, tq:]
    o = a1[:LANES] * (1.0 / a1[LANES:LANES + 1]) - lam * (a2[:LANES] * (1.0 / a2[LANES:LANES + 1]))
    y = o * lax.rsqrt(jnp.mean(o * o, axis=0, keepdims=True) + NORM_EPS) * g_ref[...] * out_scale
    o_ref[...] = _bf(y.T)


def _diff_online(qm, k_ref, vt_ref, acc_ref, s_bufs, p_bufs, *, tk, tq, n_chunks):
    def scores(ci, slot):
        c0 = ci * tk if isinstance(ci, int) else pl.multiple_of(ci * tk, tk)
        k = k_ref[pl.ds(c0, tk), :]
        mx = []
        for c in range(2):
            st = _dot_nt(k, qm[c])
            s_bufs[slot][c] = st
            mx.append(jnp.max(st, axis=0, keepdims=True))
        return tuple(mx)

    def probs(slot, mx, m_run):
        m_new, alpha = [], []
        for c in range(2):
            m = jnp.maximum(m_run[c], mx[c])
            alpha.append(jnp.exp2(m_run[c] - m))
            p_bufs[slot][c] = _bf(jnp.exp2(s_bufs[slot][c] - m))
            m_new.append(m)
        return tuple(m_new), tuple(alpha)

    def accumulate(ci, slot, alpha):
        vt = vt_ref[ci]
        for c in range(2):
            cols = slice(c * tq, (c + 1) * tq)
            acc_ref[:, cols] = alpha[c] * acc_ref[:, cols] + _dot(vt, p_bufs[slot][c])

    def step(t, par, do_scores, do_probs, do_acc, state):
        m_run, mx, alpha = state
        mx_next = scores(t + 2, par) if do_scores else mx
        if do_probs:
            m_run, alpha_next = probs(1 - par, mx, m_run)
        else:
            alpha_next = alpha
        if do_acc:
            accumulate(t, par, alpha)
        return m_run, mx_next, alpha_next

    neg = jnp.full((1, tq), NEG_INF, jnp.float32)
    one = jnp.ones((1, tq), jnp.float32)
    state = ((neg, neg), (neg, neg), (one, one))
    for t in (-2, -1):
        state = step(t, t % 2, t + 2 < n_chunks, 0 <= t + 1 < n_chunks, False, state)
    n_steady = max(n_chunks - 2, 0)

    def pair(j, state):
        state = step(2 * j, 0, True, True, True, state)
        return step(2 * j + 1, 1, True, True, True, state)

    state = lax.fori_loop(0, n_steady // 2, pair, state)
    if n_steady % 2:
        state = step(n_steady - 1, 0, True, True, True, state)
    for t in range(n_steady, n_chunks):
        state = step(t, t % 2, False, t + 1 < n_chunks, True, state)


def _pick_tk(nk, cap):
    best = LANES
    for t in range(LANES, min(cap, nk) + 1, LANES):
        if nk % t == 0:
            best = t
    return best


def _diff_attention(qd, k_parts, v_parts, lam, bound, subln_g, lam_init, tq, tk_cap):
    b, n, _ = qd.shape
    nk = sum(k.shape[1] for k in k_parts)
    tk = _pick_tk(nk, tk_cap)
    smem = pl.BlockSpec(memory_space=pltpu.SMEM)
    qspec = pl.BlockSpec((None, tq, LANES), lambda bi, h, i: (bi, i, h))
    kvspecs = [pl.BlockSpec((None, k.shape[1], LANES), lambda bi, h, i: (bi, 0, h)) for k in k_parts]
    gain = jnp.broadcast_to(subln_g.astype(jnp.float32)[:, None], (LANES, tq))
    return pl.pallas_call(
        functools.partial(_diff_kernel, n_parts=len(k_parts), tk=tk, out_scale=1.0 - lam_init),
        grid=(b, DIFF_HEADS, n // tq),
        in_specs=[smem, smem, qspec] + kvspecs + kvspecs + [_const_spec((LANES, tq))],
        out_specs=qspec,
        out_shape=jax.ShapeDtypeStruct((b, n, DIFF_W), jnp.bfloat16),
        scratch_shapes=[pltpu.VMEM((nk, LANES), jnp.bfloat16),
                        pltpu.VMEM((nk // tk, LANES + SUM_ROWS, tk), jnp.bfloat16),
                        pltpu.VMEM((LANES + SUM_ROWS, 2 * tq), jnp.float32),
                        pltpu.VMEM((2, tk, tq), jnp.float32), pltpu.VMEM((2, tk, tq), jnp.float32),
                        pltpu.VMEM((2, tk, tq), jnp.bfloat16), pltpu.VMEM((2, tk, tq), jnp.bfloat16)],
        compiler_params=_params(3),
        name="diff",
    )(lam, bound, qd, *k_parts, *v_parts, gain)


def _merge_kernel(x_ref, gt_ref, ya_ref, sza_ref, yb_ref, szb_ref, uc_ref, up_ref, un_ref, cb_ref, szc_ref,
                  yd_ref, szd_ref, g_ref, cw_ref, wa_ref, wb_ref, wc_ref, wd_ref, wo_ref, o_ref, *, tm, nt):
    i = pl.program_id(1)
    f32 = jnp.float32
    u = uc_ref[...].astype(f32)
    row = lax.broadcasted_iota(jnp.int32, (tm, 1), 0)
    prev_row = jnp.where(i > 0, up_ref[7:8, :].astype(f32), 0.0)
    next_row = jnp.where(i < nt - 1, un_ref[0:1, :].astype(f32), 0.0)
    u_prev = jnp.where(row == 0, prev_row, pltpu.roll(u, 1, 0))
    u_next = jnp.where(row == tm - 1, next_row, pltpu.roll(u, tm - 1, 0))
    cw = cw_ref[...]
    conv = u_prev * cw[0:1, :] + u * cw[1:2, :] + u_next * cw[2:3, :]
    yc = cb_ref[...].astype(f32) * conv
    d = x_ref.shape[-1]
    branches = ((ya_ref[...].astype(f32), sza_ref, wa_ref), (yb_ref[...].astype(f32), szb_ref, wb_ref),
                (yc, szc_ref, wc_ref), (yd_ref[...].astype(f32), szd_ref, wd_ref))
    mixed = jnp.zeros((tm, d), f32)
    for j, (y, sz_ref, w_ref) in enumerate(branches):
        t = _dot(_bf(y * sz_ref[...].astype(f32)), w_ref[...])
        mixed = mixed + g_ref[:, j * d:(j + 1) * d].astype(f32) * t
    out = _dot(_bf(mixed), wo_ref[...])
    o_ref[...] = x_ref[...] + gt_ref[...] * out


def _merge(x, gate, ya, sza, yb, szb, uc, cb, szc, yd, szd, g, conv_w, w_a, w_b, w_c, w_d, w_out, tm):
    b, n, d = x.shape
    nt = n // tm
    tok = lambda w: pl.BlockSpec((None, tm, w), lambda bi, i: (bi, i, 0))
    hb = tm // 8
    halo_p = pl.BlockSpec((None, 8, CONV_W), lambda bi, i: (bi, jnp.maximum(i * hb - 1, 0), 0))
    halo_n = pl.BlockSpec((None, 8, CONV_W), lambda bi, i: (bi, jnp.minimum((i + 1) * hb, n // 8 - 1), 0))
    per_b = pl.BlockSpec((None, 1, d), lambda bi, i: (bi, 0, 0))
    ws = [_bf(w_a), _bf(w_b), _bf(w_c), _bf(w_d), _bf(w_out)]
    return pl.pallas_call(
        functools.partial(_merge_kernel, tm=tm, nt=nt),
        grid=(b, nt),
        in_specs=[tok(d), per_b, tok(FOURIER_W), tok(FOURIER_W), tok(SWA_W), tok(SWA_W), tok(CONV_W), halo_p, halo_n,
                  tok(CONV_W), tok(CONV_W), tok(DIFF_W), tok(DIFF_W), tok(N_BRANCH * d), _const_spec(conv_w.shape)]
                 + [_const_spec(w.shape) for w in ws],
        out_specs=tok(d),
        out_shape=jax.ShapeDtypeStruct((b, n, d), jnp.float32),
        compiler_params=_params(2),
        name="merge",
    )(x, gate, ya, sza, yb, szb, uc, uc, uc, cb, szc, yd, szd, g, conv_w.astype(jnp.float32), *ws)


def _rope_tables(n):
    f32 = np.float32
    rows = n // GRID_W
    row = np.broadcast_to(np.arange(rows, dtype=f32)[:, None], (rows, GRID_W)).reshape(-1)
    col = np.broadcast_to(np.arange(GRID_W, dtype=f32)[None, :], (rows, GRID_W)).reshape(-1)
    nf = HEAD_DIM // 4
    inv = f32(ROPE_BASE) ** (-np.arange(nf, dtype=f32) / f32(nf))
    ar = row[:, None] * inv[None, :]
    ac = col[:, None] * inv[None, :]
    cos = np.concatenate([np.cos(ar), np.cos(ar), np.cos(ac), np.cos(ac)], axis=-1)
    sin = np.concatenate([np.sin(ar), np.sin(ar), np.sin(ac), np.sin(ac)], axis=-1)
    sign = np.where((np.arange(HEAD_DIM) % 32) < 16, f32(-1.0), f32(1.0))
    return jnp.asarray(np.tile(cos, (1, 2)), jnp.float32), jnp.asarray(np.tile(sin * sign, (1, 2)), jnp.float32)


def _channel_dft():
    k = np.arange(HEAD_DIM)
    ang = 2.0 * np.pi * ((k[:, None] * k[None, :]) % HEAD_DIM) / HEAD_DIM
    eye = np.eye(FOURIER_W // HEAD_DIM)
    inv = 1.0 / math.sqrt(HEAD_DIM)
    return jnp.asarray(np.concatenate([np.kron(eye, np.cos(ang)), -np.kron(eye, np.sin(ang))], axis=1) * inv,
                       jnp.bfloat16)


def _mix(stream, scale, shift, gate, ctx_kv, lp, consts, lam, lam_init, rope, dft, tm, tq, tk, has_local):
    parts = _project(stream, scale, shift, lp["norm_g"], lp["w_ext"], consts["cs_bd"], lp["head_gains"], rope, tm)
    zr, zi, qb, kb, vb, uc, cb, qd, kd, vd, sza, szb, szc, szd, g = parts
    if ctx_kv is None:
        kx_b, vx_b, k_parts, v_parts = kb, vb, (kd,), (vd,)
    else:
        kx_b, vx_b, kx_d, vx_d = ctx_kv
        k_parts, v_parts = (kx_d, kd), (vx_d, vd)
    ya = _fourier_positions(zr, zi, dft)
    yb = _window_attention(qb, kb, vb, kx_b, vx_b, lp["sink"], lp["bound_b"], has_local)
    yd = _diff_attention(qd, k_parts, v_parts, lam, lp["bound_d"], lp["subln"], lam_init, tq, tk)
    new = _merge(stream, gate, ya, sza, yb, szb, uc, cb, szc, yd, szd, g, lp["conv_w"],
                 lp["w_o_a"], lp["w_o_b"], lp["w_o_c"], lp["w_o_d"], lp["w_out"], tm)
    return new, (kb, vb, kd, vd)


def kernel(x, c, ctx, c_ctx, norm_g, w_mod, b_mod, w_in, q_norm_b, k_norm_b, sink_b, conv_w, q_norm_d, k_norm_d,
           lam_q1, lam_k1, lam_q2, lam_k2, subln_d, w_o_a, w_o_b, w_o_c, w_o_d, w_out):
    b, n, d = x.shape
    lc = ctx.shape[1]
    depth = w_in.shape[0]
    rope = _rope_tables(n)
    consts = dict(cs_bd=_channel_dft())
    dft_x = _dft_tables(n)
    dft_c = _dft_tables(lc)
    c_rows = jnp.zeros((8, d), jnp.float32).at[:b].set(c).at[b].set(c_ctx)
    tm = min(512, n)
    tq = min(512, n)
    tk = 1280
    for l in range(depth):
        last = l == depth - 1
        lam_init = 0.8 - 0.6 * math.exp(-0.3 * l)
        lam_vecs = jnp.stack([lam_q1[l], lam_k1[l], lam_q2[l], lam_k2[l]]).astype(jnp.float32)
        mod, lam_o = _modulation(c_rows, w_mod[l], b_mod[l], lam_vecs, lam_init)
        lam = lam_o[0:1, 0:1]
        shift, scale, gate = (mod[:b, None, j * d:(j + 1) * d] for j in range(3))
        shift_c, scale_c, gate_c = (jnp.broadcast_to(mod[b:b + 1, None, j * d:(j + 1) * d], (b, 1, d))
                                    for j in range(3))
        tile2 = lambda v: jnp.tile(v.astype(jnp.float32), 2)
        lp = dict(norm_g=norm_g[l], w_ext=_bf(w_in[l]), sink=sink_b[l].astype(jnp.float32),
                  head_gains=jnp.stack([tile2(q_norm_b[l]), tile2(k_norm_b[l]), tile2(q_norm_d[l]), tile2(k_norm_d[l])]),
                  bound_b=_score_bound(q_norm_b[l], k_norm_b[l]), bound_d=_score_bound(q_norm_d[l], k_norm_d[l]),
                  subln=subln_d[l], conv_w=conv_w[l], w_o_a=w_o_a[l], w_o_b=w_o_b[l], w_o_c=w_o_c[l],
                  w_o_d=w_o_d[l], w_out=w_out[l])
        if last:
            parts = _project(ctx, scale_c, shift_c, lp["norm_g"], lp["w_ext"], consts["cs_bd"], lp["head_gains"],
                             None, min(256, lc))
            ctx_kv = (parts[3], parts[4], parts[8], parts[9])
        else:
            ctx, ctx_kv = _mix(ctx, scale_c, shift_c, gate_c, None, lp, consts, lam, lam_init, None, dft_c,
                               min(256, lc), min(256, lc), tk, False)
        x, _ = _mix(x, scale, shift, gate, ctx_kv, lp, consts, lam, lam_init, rope, dft_x, tm, tq, tk, True)
    return x
```

```python
import functools
import math

import jax
import jax.numpy as jnp
import numpy as np
from jax import lax
from jax.experimental import pallas as pl
from jax.experimental.pallas import tpu as pltpu

HEAD_DIM = 64
LANES = 128
FOURIER_W = 384
SWA_HEADS = 8
SWA_KV = 2
SWA_W = SWA_HEADS * HEAD_DIM
WINDOW = 128
BLOCK = 128
CONV_W = 384
DIFF_HEADS = 4
DIFF_W = DIFF_HEADS * 2 * HEAD_DIM
N_BRANCH = 4
ROPE_BASE = 10000.0
NORM_EPS = 1e-6
NEG_INF = -1e30
GRID_W = 64
DFT_MINOR = 128
SUM_ROWS = 16
LOG2E = math.log2(math.e)
SAFE_SHIFT = 60.0
VMEM_LIMIT = 56 * 1024 * 1024


def _bf(x):
    return x.astype(jnp.bfloat16)


def _dot(a, b):
    return jnp.dot(a, b, preferred_element_type=jnp.float32)


def _dot_nt(a, b):
    return lax.dot_general(a, b, (((1,), (1,)), ((), ())), preferred_element_type=jnp.float32)


def _params(n_axes):
    return pltpu.CompilerParams(dimension_semantics=("arbitrary",) * n_axes, vmem_limit_bytes=VMEM_LIMIT)


def _const_spec(shape):
    nd = len(shape)
    return pl.BlockSpec(shape, lambda *_: (0,) * nd, pipeline_mode=pl.Buffered(1))


def _mod_kernel(c_ref, w_ref, b_ref, lam_ref, mod_ref, lamo_ref, *, lam_init):
    c = c_ref[...]
    s = c * jax.nn.sigmoid(c)
    mod_ref[...] = _dot(_bf(s), w_ref[...]) + b_ref[...]
    lv = lam_ref[...]
    a1 = jnp.sum(lv[0:1, :] * lv[1:2, :], axis=-1, keepdims=True)
    a2 = jnp.sum(lv[2:3, :] * lv[3:4, :], axis=-1, keepdims=True)
    lam = jnp.exp(a1) - jnp.exp(a2) + lam_init
    lamo_ref[...] = jnp.broadcast_to(lam, lamo_ref.shape)


def _modulation(c_rows, w_mod, b_mod, lam_vecs, lam_init):
    r, d = c_rows.shape
    return pl.pallas_call(
        functools.partial(_mod_kernel, lam_init=lam_init),
        out_shape=(jax.ShapeDtypeStruct((r, 3 * d), jnp.float32),
                   jax.ShapeDtypeStruct((8, LANES), jnp.float32)),
        name="mod",
    )(c_rows, _bf(w_mod), b_mod.reshape(1, 3 * d), lam_vecs)


def _head_norm_rope(t, gain, cos, sin_s, lane_lo, scale):
    sq = t * t
    first = lax.broadcasted_iota(jnp.int32, t.shape, 1) < HEAD_DIM
    ms = jnp.where(first, jnp.sum(jnp.where(first, sq, 0.0), axis=-1, keepdims=True),
                   jnp.sum(jnp.where(first, 0.0, sq), axis=-1, keepdims=True)) * (1.0 / HEAD_DIM)
    y = t * lax.rsqrt(ms + NORM_EPS) * gain
    if cos is not None:
        rot = jnp.where(lane_lo, pltpu.roll(y, LANES - 16, 1), pltpu.roll(y, 16, 1))
        y = y * cos + rot * sin_s
    if scale != 1.0:
        y = y * scale
    return y


def _proj_kernel(*refs, use_rope, tm):
    if use_rope:
        (x_ref, sc_ref, sh_ref, g_ref, w_ref, cs_ref, hg_ref, cos_ref, sin_ref), outs = refs[:9], refs[9:]
    else:
        (x_ref, sc_ref, sh_ref, g_ref, w_ref, cs_ref, hg_ref), outs = refs[:7], refs[7:]
        cos_ref = sin_ref = None
    (zr_ref, zi_ref, qb_ref, kb_ref, vb_ref, uc_ref, cb_ref, qd_ref, kd_ref, vd_ref,
     sza_ref, szb_ref, szc_ref, szd_ref, gate_ref) = outs

    x = x_ref[...]
    y = x * lax.rsqrt(jnp.mean(x * x, axis=-1, keepdims=True) + NORM_EPS) * g_ref[...]
    h = _bf(y * (1.0 + sc_ref[...]) + sh_ref[...])

    def proj(c0, width):
        return _dot(h, w_ref[:, c0:c0 + width])

    col = 0
    a = _bf(proj(col, FOURIER_W))
    zz = _dot(a, cs_ref[...])
    zr_ref[...] = zz[:, :FOURIER_W].astype(zr_ref.dtype)
    zi_ref[...] = zz[:, FOURIER_W:].astype(zi_ref.dtype)
    col += FOURIER_W

    if use_rope:
        cos = cos_ref[...]
        sin_s = sin_ref[...]
    else:
        cos = sin_s = None
    lane = lax.broadcasted_iota(jnp.int32, (tm, LANES), 1)
    lane_lo = (lane & 31) < 16
    qscale = HEAD_DIM ** -0.5

    def normed(c0, width, gain_row, out_ref, scale):
        t = proj(c0, width)
        gain = hg_ref[gain_row:gain_row + 1, :]
        for s in range(width // LANES):
            ts = t[:, s * LANES:(s + 1) * LANES]
            out_ref[:, s * LANES:(s + 1) * LANES] = _bf(
                _head_norm_rope(ts, gain, cos, sin_s, lane_lo, scale))

    head_lo = lane < HEAD_DIM

    def store_duplicated(pair, out_ref):
        swapped = pltpu.roll(pair, HEAD_DIM, 1)
        out_ref[:, :LANES] = _bf(jnp.where(head_lo, pair, swapped))
        out_ref[:, LANES:] = _bf(jnp.where(head_lo, swapped, pair))

    normed(col, SWA_W, 0, qb_ref, qscale * LOG2E); col += SWA_W
    kv = proj(col, 2 * LANES); col += 2 * LANES
    store_duplicated(_head_norm_rope(kv[:, :LANES], hg_ref[1:2, :], cos, sin_s, lane_lo, 1.0), kb_ref)
    store_duplicated(kv[:, LANES:], vb_ref)
    c3 = proj(col, 3 * CONV_W); col += 3 * CONV_W
    uc_ref[...] = _bf(c3[:, 2 * CONV_W:] * c3[:, :CONV_W])
    cb_ref[...] = _bf(c3[:, CONV_W:2 * CONV_W])
    normed(col, DIFF_W, 2, qd_ref, qscale * LOG2E); col += DIFF_W
    normed(col, DIFF_W, 3, kd_ref, 1.0); col += DIFF_W
    vd_ref[...] = _bf(proj(col, DIFF_W)); col += DIFF_W
    for ref, width in ((sza_ref, FOURIER_W), (szb_ref, SWA_W), (szc_ref, CONV_W), (szd_ref, DIFF_W)):
        z = proj(col, width)
        ref[...] = _bf(z * jax.nn.sigmoid(z))
        col += width
    d = x.shape[-1]
    for j in range(N_BRANCH):
        gate_ref[:, j * d:(j + 1) * d] = _bf(jax.nn.sigmoid(proj(col, d)))
        col += d


_PROJ_OUT_W = (FOURIER_W, FOURIER_W, SWA_W, 2 * LANES, 2 * LANES, CONV_W, CONV_W, DIFF_W, DIFF_W, DIFF_W,
               FOURIER_W, SWA_W, CONV_W, DIFF_W)


def _project(x, scale, shift, norm_g, w_ext, cs_bd, head_gains, rope, tm):
    b, n, d = x.shape
    use_rope = rope is not None
    tok = lambda w: pl.BlockSpec((None, tm, w), lambda bi, i: (bi, i, 0))
    per_b = pl.BlockSpec((None, 1, d), lambda bi, i: (bi, 0, 0))
    in_specs = [tok(d), per_b, per_b, _const_spec((1, d)), _const_spec(w_ext.shape), _const_spec(cs_bd.shape),
                _const_spec(head_gains.shape)]
    args = [x, scale, shift, norm_g.reshape(1, d), w_ext, cs_bd, head_gains]
    if use_rope:
        tab = pl.BlockSpec((tm, LANES), lambda bi, i: (i, 0))
        in_specs += [tab, tab]
        args += list(rope)
    widths = _PROJ_OUT_W + (N_BRANCH * d,)
    dtypes = [jnp.float32 if (j < 2 and n > 2 * DFT_MINOR) else jnp.bfloat16 for j in range(len(widths))]
    return pl.pallas_call(
        functools.partial(_proj_kernel, use_rope=use_rope, tm=tm),
        grid=(b, n // tm),
        in_specs=in_specs,
        out_specs=[tok(w) for w in widths],
        out_shape=[jax.ShapeDtypeStruct((b, n, w), dt) for w, dt in zip(widths, dtypes)],
        compiler_params=_params(2),
        name="proj",
    )(*args)


def _dft1_kernel(zr_ref, zi_ref, f_ref, tc_ref, ts_ref, o_ref, *, r, tn2):
    for j in range(tn2):
        z = _bf(jnp.concatenate([zr_ref[:, j, :], zi_ref[:, j, :]], axis=0))
        a = _dot(f_ref[...], z)
        ar, ai = a[:r], a[r:]
        tc = jnp.concatenate([tc_ref[j]] * (FOURIER_W // LANES), axis=-1)
        ts = jnp.concatenate([ts_ref[j]] * (FOURIER_W // LANES), axis=-1)
        o_ref[0, j] = _bf(ar * tc + ai * ts)
        o_ref[1, j] = _bf(ai * tc - ar * ts)


def _left_matmul_kernel(m_ref, x_ref, o_ref):
    o_ref[...] = _bf(_dot(m_ref[...], x_ref[...]))


def _left_matmul(mat, x, tc):
    b, k, c = x.shape
    rows = mat.shape[0]
    return pl.pallas_call(
        _left_matmul_kernel,
        grid=(b, c // tc),
        in_specs=[_const_spec(mat.shape), pl.BlockSpec((None, k, tc), lambda bi, i: (bi, 0, i))],
        out_specs=pl.BlockSpec((None, rows, tc), lambda bi, i: (bi, 0, i)),
        out_shape=jax.ShapeDtypeStruct((b, rows, c), jnp.bfloat16),
        compiler_params=_params(2),
        name="dft2",
    )(mat, x)


def _dft_tables(n):
    inv = 1.0 / math.sqrt(n)
    if n <= 2 * DFT_MINOR:
        k = np.arange(n)
        ang = 2.0 * np.pi * ((k[:, None] * k[None, :]) % n) / n
        return dict(direct=jnp.asarray(np.concatenate([np.cos(ang), np.sin(ang)], axis=1) * inv, jnp.bfloat16))
    r = n // DFT_MINOR
    k1 = np.arange(r)
    a1 = 2.0 * np.pi * ((k1[:, None] * k1[None, :]) % r) / r
    c1, s1 = np.cos(a1), np.sin(a1)
    f1 = np.block([[c1, s1], [-s1, c1]])
    n2 = np.arange(DFT_MINOR)
    at = 2.0 * np.pi * (n2[:, None] * k1[None, :]) / n
    tw_c = np.repeat(np.cos(at)[:, :, None], LANES, axis=2)
    tw_s = np.repeat(np.sin(at)[:, :, None], LANES, axis=2)
    a2 = 2.0 * np.pi * ((n2[:, None] * n2[None, :]) % DFT_MINOR) / DFT_MINOR
    f2 = np.concatenate([np.cos(a2), np.sin(a2)], axis=1) * inv
    return dict(f1=jnp.asarray(f1, jnp.bfloat16), tw_c=jnp.asarray(tw_c, jnp.float32),
                tw_s=jnp.asarray(tw_s, jnp.float32), f2=jnp.asarray(f2, jnp.bfloat16))


def _fourier_positions(zr, zi, tabs):
    b, n, w = zr.shape
    if "direct" in tabs:
        return _left_matmul(tabs["direct"], jnp.concatenate([zr, zi], axis=1), w)
    r = n // DFT_MINOR
    tn2 = 8
    zin = pl.BlockSpec((None, r, tn2, w), lambda bi, i: (bi, 0, i, 0))
    tw = pl.BlockSpec((tn2, r, LANES), lambda bi, i: (i, 0, 0))
    g = pl.pallas_call(
        functools.partial(_dft1_kernel, r=r, tn2=tn2),
        grid=(b, DFT_MINOR // tn2),
        in_specs=[zin, zin, _const_spec(tabs["f1"].shape), tw, tw],
        out_specs=pl.BlockSpec((None, 2, tn2, r, w), lambda bi, i: (bi, 0, i, 0, 0)),
        out_shape=jax.ShapeDtypeStruct((b, 2, DFT_MINOR, r, w), jnp.bfloat16),
        compiler_params=_params(2),
        name="dft1",
    )(zr.reshape(b, r, DFT_MINOR, w), zi.reshape(b, r, DFT_MINOR, w), tabs["f1"], tabs["tw_c"], tabs["tw_s"])
    y = _left_matmul(tabs["f2"], g.reshape(b, 2 * DFT_MINOR, r * w), (16 if r % 16 == 0 else 4) * w)
    return y.reshape(b, n, w)


def _win_kernel(*refs, has_local, nb, nq):
    if has_local:
        bound_ref, sink_ref, q_ref, kp_ref, kc_ref, kn_ref, vp_ref, vc_ref, vn_ref, kx_ref, vx_ref, o_ref = refs
    else:
        bound_ref, sink_ref, q_ref, kx_ref, vx_ref, o_ref = refs
    i = pl.program_id(1)
    tq = BLOCK
    group = SWA_HEADS // SWA_KV
    lane = lax.broadcasted_iota(jnp.int32, (1, LANES), 1)
    halves = (lane < HEAD_DIM, lane >= HEAD_DIM)
    row_lo = lax.broadcasted_iota(jnp.int32, (LANES, 1), 0) < HEAD_DIM
    if has_local:
        kr = lax.broadcasted_iota(jnp.int32, (3 * BLOCK, tq), 0)
        qc = lax.broadcasted_iota(jnp.int32, (3 * BLOCK, tq), 1)
        in_band = jnp.abs(kr - BLOCK - qc) <= WINDOW
    bound = bound_ref[0, 0]

    def attend(s, j, use_bound):
        sl = slice(j * LANES, (j + 1) * LANES)
        rows = slice(s * BLOCK, (s + 1) * BLOCK)
        if has_local:
            def window(prev_ref, cur_ref, next_ref):
                first = prev_ref[:, sl] if s == 0 else cur_ref[(s - 1) * BLOCK:s * BLOCK, sl]
                last = next_ref[:, sl] if s == nq - 1 else cur_ref[(s + 1) * BLOCK:(s + 2) * BLOCK, sl]
                return [first, cur_ref[rows, sl], last]
            kd = jnp.concatenate(window(kp_ref, kc_ref, kn_ref) + [kx_ref[:, sl]], axis=0)
            vd = jnp.concatenate(window(vp_ref, vc_ref, vn_ref) + [vx_ref[:, sl]], axis=0)
        else:
            kd = kx_ref[:, sl]
            vd = vx_ref[:, sl]
        q_stack = []
        for t in range(group // 2):
            slab = j * (group // 2) + t
            q2 = q_ref[rows, slab * LANES:(slab + 1) * LANES]
            q_stack += [jnp.where(halves[e], q2, jnp.zeros_like(q2)) for e in range(2)]
        st = _dot_nt(kd, jnp.concatenate(q_stack, axis=0))
        if has_local:
            kpos = (i * nq + s - 1) * BLOCK + kr
            bias = jnp.where(in_band & (kpos >= 0) & (kpos < nb * BLOCK), 0.0, NEG_INF)
            st = jnp.concatenate([st[:3 * BLOCK] + jnp.concatenate([bias] * group, axis=1), st[3 * BLOCK:]], axis=0)
        sk = sink_ref[j:j + 1, :]
        m = jnp.maximum(bound, sk) if use_bound else jnp.maximum(jnp.max(st, axis=0, keepdims=True), sk)
        p = jnp.exp2(st - m)
        inv = 1.0 / (jnp.sum(p, axis=0, keepdims=True) + jnp.exp2(sk - m))
        ot = lax.dot_general(vd, _bf(p), (((0,), (0,)), ((), ())), preferred_element_type=jnp.float32) * inv
        for t in range(group // 2):
            slab = j * (group // 2) + t
            pair = jnp.where(row_lo, ot[:, (2 * t) * tq:(2 * t + 1) * tq], ot[:, (2 * t + 1) * tq:(2 * t + 2) * tq])
            o_ref[rows, slab * LANES:(slab + 1) * LANES] = _bf(pair.T)

    shift_safe = bound <= SAFE_SHIFT

    @pl.when(shift_safe)
    def _():
        for s in range(nq):
            for j in range(SWA_KV):
                attend(s, j, True)

    @pl.when(jnp.logical_not(shift_safe))
    def _():
        for s in range(nq):
            for j in range(SWA_KV):
                attend(s, j, False)


def _score_bound(q_gain, k_gain):
    g = jnp.max(jnp.abs(q_gain.astype(jnp.float32))) * jnp.max(jnp.abs(k_gain.astype(jnp.float32)))
    return (1.02 * HEAD_DIM ** 0.5 * LOG2E * g).reshape(1, 1)


def _window_attention(qb, kb, vb, kx, vx, sink, bound, has_local):
    b, n, _ = qb.shape
    nb = n // BLOCK
    nq = 4 if nb % 4 == 0 else 1
    lx = kx.shape[1]
    qspec = pl.BlockSpec((None, nq * BLOCK, SWA_W), lambda bi, i: (bi, i, 0))
    ctx = pl.BlockSpec((None, lx, 2 * LANES), lambda bi, i: (bi, 0, 0))
    sink_rows = jnp.repeat(sink.astype(jnp.float32) * LOG2E, BLOCK).reshape(SWA_KV, -1)
    args = [bound, sink_rows, qb]
    specs = [pl.BlockSpec(memory_space=pltpu.SMEM), _const_spec(sink_rows.shape), qspec]
    if has_local:
        prv = pl.BlockSpec((None, BLOCK, 2 * LANES), lambda bi, i: (bi, jnp.maximum(i * nq - 1, 0), 0))
        cur = pl.BlockSpec((None, nq * BLOCK, 2 * LANES), lambda bi, i: (bi, i, 0))
        nxt = pl.BlockSpec((None, BLOCK, 2 * LANES), lambda bi, i: (bi, jnp.minimum((i + 1) * nq, nb - 1), 0))
        args += [kb, kb, kb, vb, vb, vb]
        specs += [prv, cur, nxt, prv, cur, nxt]
    args += [kx, vx]
    specs += [ctx, ctx]
    return pl.pallas_call(
        functools.partial(_win_kernel, has_local=has_local, nb=nb, nq=nq),
        grid=(b, nb // nq),
        in_specs=specs,
        out_specs=qspec,
        out_shape=jax.ShapeDtypeStruct((b, n, SWA_W), jnp.bfloat16),
        compiler_params=_params(2),
        name="win",
    )(*args)


def _diff_kernel(lam_ref, bound_ref, q_ref, *refs, n_parts, tk, out_scale):
    k_parts, v_parts = refs[:n_parts], refs[n_parts:2 * n_parts]
    g_ref, o_ref, k_ref, vt_ref, acc_ref, s0_ref, s1_ref, p0_ref, p1_ref = refs[2 * n_parts:]
    qi = pl.program_id(2)
    tq = g_ref.shape[1]
    n_chunks = vt_ref.shape[0]

    @pl.when(qi == 0)
    def _():
        row = 0
        for kp, vp in zip(k_parts, v_parts):
            rows = kp.shape[0]
            k_ref[row:row + rows, :] = kp[...]
            done = 0
            while done < rows:
                ci, off = divmod(row + done, tk)
                take = min(tk - off, rows - done)
                vt_ref[ci, :LANES, off:off + take] = _bf(vp[done:done + take, :].astype(jnp.float32).T)
                done += take
            row += rows

        vt_ref[:, LANES:, :] = jnp.ones((n_chunks, SUM_ROWS, tk), jnp.bfloat16)

    n_sub = acc_ref.shape[0]
    lane = lax.broadcasted_iota(jnp.int32, (1, LANES), 1)
    qms = []
    for sub in range(n_sub):
        q = q_ref[sub * tq:(sub + 1) * tq, :]
        qms.append((jnp.where(lane < HEAD_DIM, q, jnp.zeros_like(q)), jnp.where(lane >= HEAD_DIM, q, jnp.zeros_like(q))))
    s_bufs = (s0_ref, s1_ref)
    p_bufs = (p0_ref, p1_ref)

    bound = bound_ref[0, 0]
    shift_safe = bound <= SAFE_SHIFT

    @pl.when(shift_safe)
    def _():
        for sub in range(n_sub):
            q_both = jnp.concatenate(qms[sub], axis=0)

            def chunk(ci, carry, q_both=q_both):
                denom, pv = carry
                c0 = pl.multiple_of(ci * tk, tk)
                p = jnp.exp2(_dot_nt(k_ref[pl.ds(c0, tk), :], q_both) - bound)
                return denom + jnp.sum(p, axis=0, keepdims=True), pv + _dot(vt_ref[ci, :LANES, :], _bf(p))
            init = (jnp.zeros((1, 2 * tq), jnp.float32), jnp.zeros((LANES, 2 * tq), jnp.float32))
            denom, pv = lax.fori_loop(0, n_chunks, chunk, init, unroll=True)
            acc_ref[sub, :LANES, :] = pv
            acc_ref[sub, LANES:, :] = jnp.broadcast_to(denom, (SUM_ROWS, 2 * tq))

    @pl.when(jnp.logical_not(shift_safe))
    def _():
        acc_ref[...] = jnp.zeros_like(acc_ref)
        for sub in range(n_sub):
            _diff_online(qms[sub], k_ref, vt_ref, acc_ref.at[sub], s_bufs, p_bufs, tk=tk, tq=tq, n_chunks=n_chunks)

    lam = lam_ref[0, 0]
    for sub in range(n_sub):
        a1 = acc_ref[sub, :, :tq]
        a2 = acc_ref[sub, :, tq:]
        o = a1[:LANES] * (1.0 / a1[LANES:LANES + 1]) - lam * (a2[:LANES] * (1.0 / a2[LANES:LANES + 1]))
        y = o * lax.rsqrt(jnp.mean(o * o, axis=0, keepdims=True) + NORM_EPS) * g_ref[...] * out_scale
        o_ref[sub * tq:(sub + 1) * tq, :] = _bf(y.T)


def _diff_online(qm, k_ref, vt_ref, acc_ref, s_bufs, p_bufs, *, tk, tq, n_chunks):
    def scores(ci, slot):
        c0 = ci * tk if isinstance(ci, int) else pl.multiple_of(ci * tk, tk)
        k = k_ref[pl.ds(c0, tk), :]
        mx = []
        for c in range(2):
            st = _dot_nt(k, qm[c])
            s_bufs[slot][c] = st
            mx.append(jnp.max(st, axis=0, keepdims=True))
        return tuple(mx)

    def probs(slot, mx, m_run):
        m_new, alpha = [], []
        for c in range(2):
            m = jnp.maximum(m_run[c], mx[c])
            alpha.append(jnp.exp2(m_run[c] - m))
            p_bufs[slot][c] = _bf(jnp.exp2(s_bufs[slot][c] - m))
            m_new.append(m)
        return tuple(m_new), tuple(alpha)

    def accumulate(ci, slot, alpha):
        vt = vt_ref[ci]
        for c in range(2):
            cols = slice(c * tq, (c + 1) * tq)
            acc_ref[:, cols] = alpha[c] * acc_ref[:, cols] + _dot(vt, p_bufs[slot][c])

    def step(t, par, do_scores, do_probs, do_acc, state):
        m_run, mx, alpha = state
        mx_next = scores(t + 2, par) if do_scores else mx
        if do_probs:
            m_run, alpha_next = probs(1 - par, mx, m_run)
        else:
            alpha_next = alpha
        if do_acc:
            accumulate(t, par, alpha)
        return m_run, mx_next, alpha_next

    neg = jnp.full((1, tq), NEG_INF, jnp.float32)
    one = jnp.ones((1, tq), jnp.float32)
    state = ((neg, neg), (neg, neg), (one, one))
    for t in (-2, -1):
        state = step(t, t % 2, t + 2 < n_chunks, 0 <= t + 1 < n_chunks, False, state)
    n_steady = max(n_chunks - 2, 0)

    def pair(j, state):
        state = step(2 * j, 0, True, True, True, state)
        return step(2 * j + 1, 1, True, True, True, state)

    state = lax.fori_loop(0, n_steady // 2, pair, state)
    if n_steady % 2:
        state = step(n_steady - 1, 0, True, True, True, state)
    for t in range(n_steady, n_chunks):
        state = step(t, t % 2, False, t + 1 < n_chunks, True, state)


def _pick_tk(nk, cap):
    best = LANES
    for t in range(LANES, min(cap, nk) + 1, LANES):
        if nk % t == 0:
            best = t
    return best


def _diff_attention(qd, k_parts, v_parts, lam, bound, subln_g, lam_init, tq, tk_cap):
    b, n, _ = qd.shape
    nk = sum(k.shape[1] for k in k_parts)
    tk = _pick_tk(nk, tk_cap)
    smem = pl.BlockSpec(memory_space=pltpu.SMEM)
    n_sub = 2 if n % (2 * tq) == 0 else 1
    qspec = pl.BlockSpec((None, n_sub * tq, LANES), lambda bi, h, i: (bi, i, h))
    kvspecs = [pl.BlockSpec((None, k.shape[1], LANES), lambda bi, h, i: (bi, 0, h)) for k in k_parts]
    gain = jnp.broadcast_to(subln_g.astype(jnp.float32)[:, None], (LANES, tq))
    return pl.pallas_call(
        functools.partial(_diff_kernel, n_parts=len(k_parts), tk=tk, out_scale=1.0 - lam_init),
        grid=(b, DIFF_HEADS, n // (n_sub * tq)),
        in_specs=[smem, smem, qspec] + kvspecs + kvspecs + [_const_spec((LANES, tq))],
        out_specs=qspec,
        out_shape=jax.ShapeDtypeStruct((b, n, DIFF_W), jnp.bfloat16),
        scratch_shapes=[pltpu.VMEM((nk, LANES), jnp.bfloat16),
                        pltpu.VMEM((nk // tk, LANES + SUM_ROWS, tk), jnp.bfloat16),
                        pltpu.VMEM((n_sub, LANES + SUM_ROWS, 2 * tq), jnp.float32),
                        pltpu.VMEM((2, tk, tq), jnp.float32), pltpu.VMEM((2, tk, tq), jnp.float32),
                        pltpu.VMEM((2, tk, tq), jnp.bfloat16), pltpu.VMEM((2, tk, tq), jnp.bfloat16)],
        compiler_params=_params(3),
        name="diff",
    )(lam, bound, qd, *k_parts, *v_parts, gain)


def _merge_kernel(x_ref, gt_ref, ya_ref, sza_ref, yb_ref, szb_ref, uc_ref, up_ref, un_ref, cb_ref, szc_ref,
                  yd_ref, szd_ref, g_ref, cw_ref, wa_ref, wb_ref, wc_ref, wd_ref, wo_ref, o_ref, *, tm, nt):
    i = pl.program_id(1)
    f32 = jnp.float32
    u = uc_ref[...].astype(f32)
    row = lax.broadcasted_iota(jnp.int32, (tm, 1), 0)
    prev_row = jnp.where(i > 0, up_ref[7:8, :].astype(f32), 0.0)
    next_row = jnp.where(i < nt - 1, un_ref[0:1, :].astype(f32), 0.0)
    u_prev = jnp.where(row == 0, prev_row, pltpu.roll(u, 1, 0))
    u_next = jnp.where(row == tm - 1, next_row, pltpu.roll(u, tm - 1, 0))
    cw = cw_ref[...]
    conv = u_prev * cw[0:1, :] + u * cw[1:2, :] + u_next * cw[2:3, :]
    yc = cb_ref[...].astype(f32) * conv
    d = x_ref.shape[-1]
    branches = ((ya_ref[...].astype(f32), sza_ref, wa_ref), (yb_ref[...].astype(f32), szb_ref, wb_ref),
                (yc, szc_ref, wc_ref), (yd_ref[...].astype(f32), szd_ref, wd_ref))
    mixed = jnp.zeros((tm, d), f32)
    for j, (y, sz_ref, w_ref) in enumerate(branches):
        t = _dot(_bf(y * sz_ref[...].astype(f32)), w_ref[...])
        mixed = mixed + g_ref[:, j * d:(j + 1) * d].astype(f32) * t
    out = _dot(_bf(mixed), wo_ref[...])
    o_ref[...] = x_ref[...] + gt_ref[...] * out


def _merge(x, gate, ya, sza, yb, szb, uc, cb, szc, yd, szd, g, conv_w, w_a, w_b, w_c, w_d, w_out, tm):
    b, n, d = x.shape
    nt = n // tm
    tok = lambda w: pl.BlockSpec((None, tm, w), lambda bi, i: (bi, i, 0))
    hb = tm // 8
    halo_p = pl.BlockSpec((None, 8, CONV_W), lambda bi, i: (bi, jnp.maximum(i * hb - 1, 0), 0))
    halo_n = pl.BlockSpec((None, 8, CONV_W), lambda bi, i: (bi, jnp.minimum((i + 1) * hb, n // 8 - 1), 0))
    per_b = pl.BlockSpec((None, 1, d), lambda bi, i: (bi, 0, 0))
    ws = [_bf(w_a), _bf(w_b), _bf(w_c), _bf(w_d), _bf(w_out)]
    return pl.pallas_call(
        functools.partial(_merge_kernel, tm=tm, nt=nt),
        grid=(b, nt),
        in_specs=[tok(d), per_b, tok(FOURIER_W), tok(FOURIER_W), tok(SWA_W), tok(SWA_W), tok(CONV_W), halo_p, halo_n,
                  tok(CONV_W), tok(CONV_W), tok(DIFF_W), tok(DIFF_W), tok(N_BRANCH * d), _const_spec(conv_w.shape)]
                 + [_const_spec(w.shape) for w in ws],
        out_specs=tok(d),
        out_shape=jax.ShapeDtypeStruct((b, n, d), jnp.float32),
        compiler_params=_params(2),
        name="merge",
    )(x, gate, ya, sza, yb, szb, uc, uc, uc, cb, szc, yd, szd, g, conv_w.astype(jnp.float32), *ws)


def _rope_tables(n):
    f32 = np.float32
    rows = n // GRID_W
    row = np.broadcast_to(np.arange(rows, dtype=f32)[:, None], (rows, GRID_W)).reshape(-1)
    col = np.broadcast_to(np.arange(GRID_W, dtype=f32)[None, :], (rows, GRID_W)).reshape(-1)
    nf = HEAD_DIM // 4
    inv = f32(ROPE_BASE) ** (-np.arange(nf, dtype=f32) / f32(nf))
    ar = row[:, None] * inv[None, :]
    ac = col[:, None] * inv[None, :]
    cos = np.concatenate([np.cos(ar), np.cos(ar), np.cos(ac), np.cos(ac)], axis=-1)
    sin = np.concatenate([np.sin(ar), np.sin(ar), np.sin(ac), np.sin(ac)], axis=-1)
    sign = np.where((np.arange(HEAD_DIM) % 32) < 16, f32(-1.0), f32(1.0))
    return jnp.asarray(np.tile(cos, (1, 2)), jnp.float32), jnp.asarray(np.tile(sin * sign, (1, 2)), jnp.float32)


def _channel_dft():
    k = np.arange(HEAD_DIM)
    ang = 2.0 * np.pi * ((k[:, None] * k[None, :]) % HEAD_DIM) / HEAD_DIM
    eye = np.eye(FOURIER_W // HEAD_DIM)
    inv = 1.0 / math.sqrt(HEAD_DIM)
    return jnp.asarray(np.concatenate([np.kron(eye, np.cos(ang)), -np.kron(eye, np.sin(ang))], axis=1) * inv,
                       jnp.bfloat16)


def _mix(stream, scale, shift, gate, ctx_kv, lp, consts, lam, lam_init, rope, dft, tm, tq, tk, has_local):
    parts = _project(stream, scale, shift, lp["norm_g"], lp["w_ext"], consts["cs_bd"], lp["head_gains"], rope, tm)
    zr, zi, qb, kb, vb, uc, cb, qd, kd, vd, sza, szb, szc, szd, g = parts
    if ctx_kv is None:
        kx_b, vx_b, k_parts, v_parts = kb, vb, (kd,), (vd,)
    else:
        kx_b, vx_b, kx_d, vx_d = ctx_kv
        k_parts, v_parts = (kx_d, kd), (vx_d, vd)
    ya = _fourier_positions(zr, zi, dft)
    yb = _window_attention(qb, kb, vb, kx_b, vx_b, lp["sink"], lp["bound_b"], has_local)
    yd = _diff_attention(qd, k_parts, v_parts, lam, lp["bound_d"], lp["subln"], lam_init, tq, tk)
    new = _merge(stream, gate, ya, sza, yb, szb, uc, cb, szc, yd, szd, g, lp["conv_w"],
                 lp["w_o_a"], lp["w_o_b"], lp["w_o_c"], lp["w_o_d"], lp["w_out"], tm)
    return new, (kb, vb, kd, vd)


def kernel(x, c, ctx, c_ctx, norm_g, w_mod, b_mod, w_in, q_norm_b, k_norm_b, sink_b, conv_w, q_norm_d, k_norm_d,
           lam_q1, lam_k1, lam_q2, lam_k2, subln_d, w_o_a, w_o_b, w_o_c, w_o_d, w_out):
    b, n, d = x.shape
    lc = ctx.shape[1]
    depth = w_in.shape[0]
    rope = _rope_tables(n)
    consts = dict(cs_bd=_channel_dft())
    dft_x = _dft_tables(n)
    dft_c = _dft_tables(lc)
    c_rows = jnp.zeros((8, d), jnp.float32).at[:b].set(c).at[b].set(c_ctx)
    tm = min(512, n)
    tq = min(512, n)
    tk = 1280
    for l in range(depth):
        last = l == depth - 1
        lam_init = 0.8 - 0.6 * math.exp(-0.3 * l)
        lam_vecs = jnp.stack([lam_q1[l], lam_k1[l], lam_q2[l], lam_k2[l]]).astype(jnp.float32)
        mod, lam_o = _modulation(c_rows, w_mod[l], b_mod[l], lam_vecs, lam_init)
        lam = lam_o[0:1, 0:1]
        shift, scale, gate = (mod[:b, None, j * d:(j + 1) * d] for j in range(3))
        shift_c, scale_c, gate_c = (jnp.broadcast_to(mod[b:b + 1, None, j * d:(j + 1) * d], (b, 1, d))
                                    for j in range(3))
        tile2 = lambda v: jnp.tile(v.astype(jnp.float32), 2)
        lp = dict(norm_g=norm_g[l], w_ext=_bf(w_in[l]), sink=sink_b[l].astype(jnp.float32),
                  head_gains=jnp.stack([tile2(q_norm_b[l]), tile2(k_norm_b[l]), tile2(q_norm_d[l]), tile2(k_norm_d[l])]),
                  bound_b=_score_bound(q_norm_b[l], k_norm_b[l]), bound_d=_score_bound(q_norm_d[l], k_norm_d[l]),
                  subln=subln_d[l], conv_w=conv_w[l], w_o_a=w_o_a[l], w_o_b=w_o_b[l], w_o_c=w_o_c[l],
                  w_o_d=w_o_d[l], w_out=w_out[l])
        if last:
            parts = _project(ctx, scale_c, shift_c, lp["norm_g"], lp["w_ext"], consts["cs_bd"], lp["head_gains"],
                             None, min(256, lc))
            ctx_kv = (parts[3], parts[4], parts[8], parts[9])
        else:
            ctx, ctx_kv = _mix(ctx, scale_c, shift_c, gate_c, None, lp, consts, lam, lam_init, None, dft_c,
                               min(256, lc), min(256, lc), tk, False)
        x, _ = _mix(x, scale, shift, gate, ctx_kv, lp, consts, lam, lam_init, rope, dft_x, tm, tq, tk, True)
    return x
```

```python
import functools
import math

import jax
import jax.numpy as jnp
import numpy as np
from jax import lax
from jax.experimental import pallas as pl
from jax.experimental.pallas import tpu as pltpu

HEAD_DIM = 64
LANES = 128
FOURIER_W = 384
SWA_HEADS = 8
SWA_KV = 2
SWA_W = SWA_HEADS * HEAD_DIM
WINDOW = 128
BLOCK = 128
CONV_W = 384
DIFF_HEADS = 4
DIFF_W = DIFF_HEADS * 2 * HEAD_DIM
N_BRANCH = 4
ROPE_BASE = 10000.0
NORM_EPS = 1e-6
NEG_INF = -1e30
GRID_W = 64
DFT_MINOR = 128
SUM_ROWS = 16
LOG2E = math.log2(math.e)
SAFE_SHIFT = 60.0
VMEM_LIMIT = 56 * 1024 * 1024


def _bf(x):
    return x.astype(jnp.bfloat16)


def _dot(a, b):
    return jnp.dot(a, b, preferred_element_type=jnp.float32)


def _dot_nt(a, b):
    return lax.dot_general(a, b, (((1,), (1,)), ((), ())), preferred_element_type=jnp.float32)


def _params(n_axes):
    return pltpu.CompilerParams(dimension_semantics=("arbitrary",) * n_axes, vmem_limit_bytes=VMEM_LIMIT)


def _const_spec(shape):
    nd = len(shape)
    return pl.BlockSpec(shape, lambda *_: (0,) * nd, pipeline_mode=pl.Buffered(1))


def _mod_kernel(c_ref, w_ref, b_ref, lam_ref, mod_ref, lamo_ref, *, lam_init):
    c = c_ref[...]
    s = c * jax.nn.sigmoid(c)
    mod_ref[...] = _dot(_bf(s), w_ref[...]) + b_ref[...]
    lv = lam_ref[...]
    a1 = jnp.sum(lv[0:1, :] * lv[1:2, :], axis=-1, keepdims=True)
    a2 = jnp.sum(lv[2:3, :] * lv[3:4, :], axis=-1, keepdims=True)
    lam = jnp.exp(a1) - jnp.exp(a2) + lam_init
    lamo_ref[...] = jnp.broadcast_to(lam, lamo_ref.shape)


def _modulation(c_rows, w_mod, b_mod, lam_vecs, lam_init):
    r, d = c_rows.shape
    return pl.pallas_call(
        functools.partial(_mod_kernel, lam_init=lam_init),
        out_shape=(jax.ShapeDtypeStruct((r, 3 * d), jnp.float32),
                   jax.ShapeDtypeStruct((8, LANES), jnp.float32)),
        name="mod",
    )(c_rows, _bf(w_mod), b_mod.reshape(1, 3 * d), lam_vecs)


def _head_norm_rope(t, gain, cos, sin_s, lane_lo, scale):
    sq = t * t
    first = lax.broadcasted_iota(jnp.int32, t.shape, 1) < HEAD_DIM
    ms = jnp.where(first, jnp.sum(jnp.where(first, sq, 0.0), axis=-1, keepdims=True),
                   jnp.sum(jnp.where(first, 0.0, sq), axis=-1, keepdims=True)) * (1.0 / HEAD_DIM)
    y = t * lax.rsqrt(ms + NORM_EPS) * gain
    if cos is not None:
        rot = jnp.where(lane_lo, pltpu.roll(y, LANES - 16, 1), pltpu.roll(y, 16, 1))
        y = y * cos + rot * sin_s
    if scale != 1.0:
        y = y * scale
    return y


def _proj_kernel(*refs, use_rope, tm):
    if use_rope:
        (x_ref, sc_ref, sh_ref, g_ref, w_ref, cs_ref, hg_ref, cos_ref, sin_ref), outs = refs[:9], refs[9:]
    else:
        (x_ref, sc_ref, sh_ref, g_ref, w_ref, cs_ref, hg_ref), outs = refs[:7], refs[7:]
        cos_ref = sin_ref = None
    (zr_ref, zi_ref, qb_ref, kb_ref, vb_ref, uc_ref, cb_ref, qd_ref, kd_ref, vd_ref,
     sza_ref, szb_ref, szc_ref, szd_ref, gate_ref) = outs

    x = x_ref[...]
    y = x * lax.rsqrt(jnp.mean(x * x, axis=-1, keepdims=True) + NORM_EPS) * g_ref[...]
    h = _bf(y * (1.0 + sc_ref[...]) + sh_ref[...])

    def proj(c0, width):
        return _dot(h, w_ref[:, c0:c0 + width])

    col = 0
    a = _bf(proj(col, FOURIER_W))
    zz = _dot(a, cs_ref[...])
    zr_ref[...] = zz[:, :FOURIER_W].astype(zr_ref.dtype)
    zi_ref[...] = zz[:, FOURIER_W:].astype(zi_ref.dtype)
    col += FOURIER_W

    if use_rope:
        cos = cos_ref[...]
        sin_s = sin_ref[...]
    else:
        cos = sin_s = None
    lane = lax.broadcasted_iota(jnp.int32, (tm, LANES), 1)
    lane_lo = (lane & 31) < 16
    qscale = HEAD_DIM ** -0.5

    def normed(c0, width, gain_row, out_ref, scale):
        t = proj(c0, width)
        gain = hg_ref[gain_row:gain_row + 1, :]
        for s in range(width // LANES):
            ts = t[:, s * LANES:(s + 1) * LANES]
            out_ref[:, s * LANES:(s + 1) * LANES] = _bf(
                _head_norm_rope(ts, gain, cos, sin_s, lane_lo, scale))

    head_lo = lane < HEAD_DIM

    def store_duplicated(pair, out_ref):
        swapped = pltpu.roll(pair, HEAD_DIM, 1)
        out_ref[:, :LANES] = _bf(jnp.where(head_lo, pair, swapped))
        out_ref[:, LANES:] = _bf(jnp.where(head_lo, swapped, pair))

    normed(col, SWA_W, 0, qb_ref, qscale * LOG2E); col += SWA_W
    kv = proj(col, 2 * LANES); col += 2 * LANES
    store_duplicated(_head_norm_rope(kv[:, :LANES], hg_ref[1:2, :], cos, sin_s, lane_lo, 1.0), kb_ref)
    store_duplicated(kv[:, LANES:], vb_ref)
    c3 = proj(col, 3 * CONV_W); col += 3 * CONV_W
    uc_ref[...] = _bf(c3[:, 2 * CONV_W:] * c3[:, :CONV_W])
    cb_ref[...] = _bf(c3[:, CONV_W:2 * CONV_W])
    normed(col, DIFF_W, 2, qd_ref, qscale * LOG2E); col += DIFF_W
    normed(col, DIFF_W, 3, kd_ref, 1.0); col += DIFF_W
    vd_ref[...] = _bf(proj(col, DIFF_W)); col += DIFF_W
    z3 = proj(col, FOURIER_W + SWA_W + CONV_W)
    off = 0
    for ref, width in ((sza_ref, FOURIER_W), (szb_ref, SWA_W), (szc_ref, CONV_W)):
        z = z3[:, off:off + width]
        ref[...] = _bf(z * jax.nn.sigmoid(z))
        off += width
    col += off
    z = proj(col, DIFF_W); col += DIFF_W
    szd_ref[...] = _bf(z * jax.nn.sigmoid(z))
    d = x.shape[-1]
    for j in range(N_BRANCH):
        gate_ref[:, j * d:(j + 1) * d] = _bf(jax.nn.sigmoid(proj(col, d)))
        col += d


_PROJ_OUT_W = (FOURIER_W, FOURIER_W, SWA_W, 2 * LANES, 2 * LANES, CONV_W, CONV_W, DIFF_W, DIFF_W, DIFF_W,
               FOURIER_W, SWA_W, CONV_W, DIFF_W)


def _project(x, scale, shift, norm_g, w_ext, cs_bd, head_gains, rope, tm):
    b, n, d = x.shape
    use_rope = rope is not None
    tok = lambda w: pl.BlockSpec((None, tm, w), lambda bi, i: (bi, i, 0))
    per_b = pl.BlockSpec((None, 1, d), lambda bi, i: (bi, 0, 0))
    in_specs = [tok(d), per_b, per_b, _const_spec((1, d)), _const_spec(w_ext.shape), _const_spec(cs_bd.shape),
                _const_spec(head_gains.shape)]
    args = [x, scale, shift, norm_g.reshape(1, d), w_ext, cs_bd, head_gains]
    if use_rope:
        tab = pl.BlockSpec((tm, LANES), lambda bi, i: (i, 0))
        in_specs += [tab, tab]
        args += list(rope)
    widths = _PROJ_OUT_W + (N_BRANCH * d,)
    dtypes = [jnp.float32 if (j < 2 and n > 2 * DFT_MINOR) else jnp.bfloat16 for j in range(len(widths))]
    return pl.pallas_call(
        functools.partial(_proj_kernel, use_rope=use_rope, tm=tm),
        grid=(b, n // tm),
        in_specs=in_specs,
        out_specs=[tok(w) for w in widths],
        out_shape=[jax.ShapeDtypeStruct((b, n, w), dt) for w, dt in zip(widths, dtypes)],
        compiler_params=_params(2),
        name="proj",
    )(*args)


def _dft1_kernel(zr_ref, zi_ref, f_ref, tc_ref, ts_ref, o_ref, *, r, tn2):
    for j in range(tn2):
        z = _bf(jnp.concatenate([zr_ref[:, j, :], zi_ref[:, j, :]], axis=0))
        a = _dot(f_ref[...], z)
        ar, ai = a[:r], a[r:]
        tc = jnp.concatenate([tc_ref[j]] * (FOURIER_W // LANES), axis=-1)
        ts = jnp.concatenate([ts_ref[j]] * (FOURIER_W // LANES), axis=-1)
        o_ref[0, j] = _bf(ar * tc + ai * ts)
        o_ref[1, j] = _bf(ai * tc - ar * ts)


def _left_matmul_kernel(m_ref, x_ref, o_ref):
    o_ref[...] = _bf(_dot(m_ref[...], x_ref[...]))


def _left_matmul(mat, x, tc):
    b, k, c = x.shape
    rows = mat.shape[0]
    return pl.pallas_call(
        _left_matmul_kernel,
        grid=(b, c // tc),
        in_specs=[_const_spec(mat.shape), pl.BlockSpec((None, k, tc), lambda bi, i: (bi, 0, i))],
        out_specs=pl.BlockSpec((None, rows, tc), lambda bi, i: (bi, 0, i)),
        out_shape=jax.ShapeDtypeStruct((b, rows, c), jnp.bfloat16),
        compiler_params=_params(2),
        name="dft2",
    )(mat, x)


def _dft_tables(n):
    inv = 1.0 / math.sqrt(n)
    if n <= 2 * DFT_MINOR:
        k = np.arange(n)
        ang = 2.0 * np.pi * ((k[:, None] * k[None, :]) % n) / n
        return dict(direct=jnp.asarray(np.concatenate([np.cos(ang), np.sin(ang)], axis=1) * inv, jnp.bfloat16))
    r = n // DFT_MINOR
    k1 = np.arange(r)
    a1 = 2.0 * np.pi * ((k1[:, None] * k1[None, :]) % r) / r
    c1, s1 = np.cos(a1), np.sin(a1)
    f1 = np.block([[c1, s1], [-s1, c1]])
    n2 = np.arange(DFT_MINOR)
    at = 2.0 * np.pi * (n2[:, None] * k1[None, :]) / n
    tw_c = np.repeat(np.cos(at)[:, :, None], LANES, axis=2)
    tw_s = np.repeat(np.sin(at)[:, :, None], LANES, axis=2)
    a2 = 2.0 * np.pi * ((n2[:, None] * n2[None, :]) % DFT_MINOR) / DFT_MINOR
    f2 = np.concatenate([np.cos(a2), np.sin(a2)], axis=1) * inv
    return dict(f1=jnp.asarray(f1, jnp.bfloat16), tw_c=jnp.asarray(tw_c, jnp.float32),
                tw_s=jnp.asarray(tw_s, jnp.float32), f2=jnp.asarray(f2, jnp.bfloat16))


def _fourier_positions(zr, zi, tabs):
    b, n, w = zr.shape
    if "direct" in tabs:
        return _left_matmul(tabs["direct"], jnp.concatenate([zr, zi], axis=1), w)
    r = n // DFT_MINOR
    tn2 = 8
    zin = pl.BlockSpec((None, r, tn2, w), lambda bi, i: (bi, 0, i, 0))
    tw = pl.BlockSpec((tn2, r, LANES), lambda bi, i: (i, 0, 0))
    g = pl.pallas_call(
        functools.partial(_dft1_kernel, r=r, tn2=tn2),
        grid=(b, DFT_MINOR // tn2),
        in_specs=[zin, zin, _const_spec(tabs["f1"].shape), tw, tw],
        out_specs=pl.BlockSpec((None, 2, tn2, r, w), lambda bi, i: (bi, 0, i, 0, 0)),
        out_shape=jax.ShapeDtypeStruct((b, 2, DFT_MINOR, r, w), jnp.bfloat16),
        compiler_params=_params(2),
        name="dft1",
    )(zr.reshape(b, r, DFT_MINOR, w), zi.reshape(b, r, DFT_MINOR, w), tabs["f1"], tabs["tw_c"], tabs["tw_s"])
    y = _left_matmul(tabs["f2"], g.reshape(b, 2 * DFT_MINOR, r * w), (16 if r % 16 == 0 else 4) * w)
    return y.reshape(b, n, w)


def _win_kernel(*refs, has_local, nb, nq):
    if has_local:
        bound_ref, sink_ref, q_ref, kp_ref, kc_ref, kn_ref, vp_ref, vc_ref, vn_ref, kx_ref, vx_ref, o_ref = refs
    else:
        bound_ref, sink_ref, q_ref, kx_ref, vx_ref, o_ref = refs
    i = pl.program_id(1)
    tq = BLOCK
    group = SWA_HEADS // SWA_KV
    lane = lax.broadcasted_iota(jnp.int32, (1, LANES), 1)
    halves = (lane < HEAD_DIM, lane >= HEAD_DIM)
    row_lo = lax.broadcasted_iota(jnp.int32, (LANES, 1), 0) < HEAD_DIM
    if has_local:
        kr = lax.broadcasted_iota(jnp.int32, (3 * BLOCK, tq), 0)
        qc = lax.broadcasted_iota(jnp.int32, (3 * BLOCK, tq), 1)
        in_band = jnp.abs(kr - BLOCK - qc) <= WINDOW
    bound = bound_ref[0, 0]

    def attend(s, j, use_bound):
        sl = slice(j * LANES, (j + 1) * LANES)
        rows = slice(s * BLOCK, (s + 1) * BLOCK)
        if has_local:
            def window(prev_ref, cur_ref, next_ref):
                first = prev_ref[:, sl] if s == 0 else cur_ref[(s - 1) * BLOCK:s * BLOCK, sl]
                last = next_ref[:, sl] if s == nq - 1 else cur_ref[(s + 1) * BLOCK:(s + 2) * BLOCK, sl]
                return [first, cur_ref[rows, sl], last]
            kd = jnp.concatenate(window(kp_ref, kc_ref, kn_ref) + [kx_ref[:, sl]], axis=0)
            vd = jnp.concatenate(window(vp_ref, vc_ref, vn_ref) + [vx_ref[:, sl]], axis=0)
        else:
            kd = kx_ref[:, sl]
            vd = vx_ref[:, sl]
        q_stack = []
        for t in range(group // 2):
            slab = j * (group // 2) + t
            q2 = q_ref[rows, slab * LANES:(slab + 1) * LANES]
            q_stack += [jnp.where(halves[e], q2, jnp.zeros_like(q2)) for e in range(2)]
        st = _dot_nt(kd, jnp.concatenate(q_stack, axis=0))
        if has_local:
            kpos = (i * nq + s - 1) * BLOCK + kr
            bias = jnp.where(in_band & (kpos >= 0) & (kpos < nb * BLOCK), 0.0, NEG_INF)
            st = jnp.concatenate([st[:3 * BLOCK] + jnp.concatenate([bias] * group, axis=1), st[3 * BLOCK:]], axis=0)
        sk = sink_ref[j:j + 1, :]
        m = jnp.maximum(bound, sk) if use_bound else jnp.maximum(jnp.max(st, axis=0, keepdims=True), sk)
        p = jnp.exp2(st - m)
        inv = 1.0 / (jnp.sum(p, axis=0, keepdims=True) + jnp.exp2(sk - m))
        ot = lax.dot_general(vd, _bf(p), (((0,), (0,)), ((), ())), preferred_element_type=jnp.float32) * inv
        for t in range(group // 2):
            slab = j * (group // 2) + t
            pair = jnp.where(row_lo, ot[:, (2 * t) * tq:(2 * t + 1) * tq], ot[:, (2 * t + 1) * tq:(2 * t + 2) * tq])
            o_ref[rows, slab * LANES:(slab + 1) * LANES] = _bf(pair.T)

    shift_safe = bound <= SAFE_SHIFT

    @pl.when(shift_safe)
    def _():
        for s in range(nq):
            for j in range(SWA_KV):
                attend(s, j, True)

    @pl.when(jnp.logical_not(shift_safe))
    def _():
        for s in range(nq):
            for j in range(SWA_KV):
                attend(s, j, False)


def _score_bound(q_gain, k_gain):
    g = jnp.max(jnp.abs(q_gain.astype(jnp.float32))) * jnp.max(jnp.abs(k_gain.astype(jnp.float32)))
    return (1.02 * HEAD_DIM ** 0.5 * LOG2E * g).reshape(1, 1)


def _window_attention(qb, kb, vb, kx, vx, sink, bound, has_local):
    b, n, _ = qb.shape
    nb = n // BLOCK
    nq = 8 if nb % 8 == 0 else 1
    lx = kx.shape[1]
    qspec = pl.BlockSpec((None, nq * BLOCK, SWA_W), lambda bi, i: (bi, i, 0))
    ctx = pl.BlockSpec((None, lx, 2 * LANES), lambda bi, i: (bi, 0, 0))
    sink_rows = jnp.repeat(sink.astype(jnp.float32) * LOG2E, BLOCK).reshape(SWA_KV, -1)
    args = [bound, sink_rows, qb]
    specs = [pl.BlockSpec(memory_space=pltpu.SMEM), _const_spec(sink_rows.shape), qspec]
    if has_local:
        prv = pl.BlockSpec((None, BLOCK, 2 * LANES), lambda bi, i: (bi, jnp.maximum(i * nq - 1, 0), 0))
        cur = pl.BlockSpec((None, nq * BLOCK, 2 * LANES), lambda bi, i: (bi, i, 0))
        nxt = pl.BlockSpec((None, BLOCK, 2 * LANES), lambda bi, i: (bi, jnp.minimum((i + 1) * nq, nb - 1), 0))
        args += [kb, kb, kb, vb, vb, vb]
        specs += [prv, cur, nxt, prv, cur, nxt]
    args += [kx, vx]
    specs += [ctx, ctx]
    return pl.pallas_call(
        functools.partial(_win_kernel, has_local=has_local, nb=nb, nq=nq),
        grid=(b, nb // nq),
        in_specs=specs,
        out_specs=qspec,
        out_shape=jax.ShapeDtypeStruct((b, n, SWA_W), jnp.bfloat16),
        compiler_params=_params(2),
        name="win",
    )(*args)


def _diff_kernel(lam_ref, bound_ref, q_ref, *refs, n_parts, tk, out_scale):
    k_parts, v_parts = refs[:n_parts], refs[n_parts:2 * n_parts]
    g_ref, o_ref, k_ref, vt_ref, acc_ref, s0_ref, s1_ref, p0_ref, p1_ref = refs[2 * n_parts:]
    qi = pl.program_id(2)
    tq = q_ref.shape[0]
    n_chunks = vt_ref.shape[0]

    @pl.when(qi == 0)
    def _():
        row = 0
        for kp, vp in zip(k_parts, v_parts):
            rows = kp.shape[0]
            k_ref[row:row + rows, :] = kp[...]
            done = 0
            while done < rows:
                ci, off = divmod(row + done, tk)
                take = min(tk - off, rows - done)
                vt_ref[ci, :LANES, off:off + take] = _bf(vp[done:done + take, :].astype(jnp.float32).T)
                done += take
            row += rows

        vt_ref[:, LANES:, :] = jnp.ones((n_chunks, SUM_ROWS, tk), jnp.bfloat16)

    q = q_ref[...]
    lane = lax.broadcasted_iota(jnp.int32, (1, LANES), 1)
    qm = (jnp.where(lane < HEAD_DIM, q, jnp.zeros_like(q)), jnp.where(lane >= HEAD_DIM, q, jnp.zeros_like(q)))
    s_bufs = (s0_ref, s1_ref)
    p_bufs = (p0_ref, p1_ref)

    bound = bound_ref[0, 0]
    shift_safe = bound <= SAFE_SHIFT

    @pl.when(shift_safe)
    def _():
        q_both = jnp.concatenate(qm, axis=0)

        def chunk(ci, carry):
            denom, pv = carry
            c0 = pl.multiple_of(ci * tk, tk)
            p = jnp.exp2(_dot_nt(k_ref[pl.ds(c0, tk), :], q_both) - bound)
            return denom + jnp.sum(p, axis=0, keepdims=True), pv + _dot(vt_ref[ci, :LANES, :], _bf(p))
        init = (jnp.zeros((1, 2 * tq), jnp.float32), jnp.zeros((LANES, 2 * tq), jnp.float32))
        denom, pv = lax.fori_loop(0, n_chunks, chunk, init, unroll=True)
        acc_ref[:LANES, :] = pv
        acc_ref[LANES:, :] = jnp.broadcast_to(denom, (SUM_ROWS, 2 * tq))

    @pl.when(jnp.logical_not(shift_safe))
    def _():
        acc_ref[...] = jnp.zeros_like(acc_ref)
        _diff_online(qm, k_ref, vt_ref, acc_ref, s_bufs, p_bufs, tk=tk, tq=tq, n_chunks=n_chunks)

    lam = lam_ref[0, 0]
    a1 = acc_ref[:, :tq]
    a2 = acc_ref[:, tq:]
    o = a1[:LANES] * (1.0 / a1[LANES:LANES + 1]) - lam * (a2[:LANES] * (1.0 / a2[LANES:LANES + 1]))
    y = o * lax.rsqrt(jnp.mean(o * o, axis=0, keepdims=True) + NORM_EPS) * g_ref[...] * out_scale
    o_ref[...] = _bf(y.T)


def _diff_online(qm, k_ref, vt_ref, acc_ref, s_bufs, p_bufs, *, tk, tq, n_chunks):
    def scores(ci, slot):
        c0 = ci * tk if isinstance(ci, int) else pl.multiple_of(ci * tk, tk)
        k = k_ref[pl.ds(c0, tk), :]
        mx = []
        for c in range(2):
            st = _dot_nt(k, qm[c])
            s_bufs[slot][c] = st
            mx.append(jnp.max(st, axis=0, keepdims=True))
        return tuple(mx)

    def probs(slot, mx, m_run):
        m_new, alpha = [], []
        for c in range(2):
            m = jnp.maximum(m_run[c], mx[c])
            alpha.append(jnp.exp2(m_run[c] - m))
            p_bufs[slot][c] = _bf(jnp.exp2(s_bufs[slot][c] - m))
            m_new.append(m)
        return tuple(m_new), tuple(alpha)

    def accumulate(ci, slot, alpha):
        vt = vt_ref[ci]
        for c in range(2):
            cols = slice(c * tq, (c + 1) * tq)
            acc_ref[:, cols] = alpha[c] * acc_ref[:, cols] + _dot(vt, p_bufs[slot][c])

    def step(t, par, do_scores, do_probs, do_acc, state):
        m_run, mx, alpha = state
        mx_next = scores(t + 2, par) if do_scores else mx
        if do_probs:
            m_run, alpha_next = probs(1 - par, mx, m_run)
        else:
            alpha_next = alpha
        if do_acc:
            accumulate(t, par, alpha)
        return m_run, mx_next, alpha_next

    neg = jnp.full((1, tq), NEG_INF, jnp.float32)
    one = jnp.ones((1, tq), jnp.float32)
    state = ((neg, neg), (neg, neg), (one, one))
    for t in (-2, -1):
        state = step(t, t % 2, t + 2 < n_chunks, 0 <= t + 1 < n_chunks, False, state)
    n_steady = max(n_chunks - 2, 0)

    def pair(j, state):
        state = step(2 * j, 0, True, True, True, state)
        return step(2 * j + 1, 1, True, True, True, state)

    state = lax.fori_loop(0, n_steady // 2, pair, state)
    if n_steady % 2:
        state = step(n_steady - 1, 0, True, True, True, state)
    for t in range(n_steady, n_chunks):
        state = step(t, t % 2, False, t + 1 < n_chunks, True, state)


def _pick_tk(nk, cap):
    best = LANES
    for t in range(LANES, min(cap, nk) + 1, LANES):
        if nk % t == 0:
            best = t
    return best


def _diff_attention(qd, k_parts, v_parts, lam, bound, subln_g, lam_init, tq, tk_cap):
    b, n, _ = qd.shape
    nk = sum(k.shape[1] for k in k_parts)
    tk = _pick_tk(nk, tk_cap)
    smem = pl.BlockSpec(memory_space=pltpu.SMEM)
    qspec = pl.BlockSpec((None, tq, LANES), lambda bi, h, i: (bi, i, h))
    kvspecs = [pl.BlockSpec((None, k.shape[1], LANES), lambda bi, h, i: (bi, 0, h)) for k in k_parts]
    gain = jnp.broadcast_to(subln_g.astype(jnp.float32)[:, None], (LANES, tq))
    return pl.pallas_call(
        functools.partial(_diff_kernel, n_parts=len(k_parts), tk=tk, out_scale=1.0 - lam_init),
        grid=(b, DIFF_HEADS, n // tq),
        in_specs=[smem, smem, qspec] + kvspecs + kvspecs + [_const_spec((LANES, tq))],
        out_specs=qspec,
        out_shape=jax.ShapeDtypeStruct((b, n, DIFF_W), jnp.bfloat16),
        scratch_shapes=[pltpu.VMEM((nk, LANES), jnp.bfloat16),
                        pltpu.VMEM((nk // tk, LANES + SUM_ROWS, tk), jnp.bfloat16),
                        pltpu.VMEM((LANES + SUM_ROWS, 2 * tq), jnp.float32),
                        pltpu.VMEM((2, tk, tq), jnp.float32), pltpu.VMEM((2, tk, tq), jnp.float32),
                        pltpu.VMEM((2, tk, tq), jnp.bfloat16), pltpu.VMEM((2, tk, tq), jnp.bfloat16)],
        compiler_params=_params(3),
        name="diff",
    )(lam, bound, qd, *k_parts, *v_parts, gain)


def _merge_kernel(x_ref, gt_ref, ya_ref, sza_ref, yb_ref, szb_ref, uc_ref, up_ref, un_ref, cb_ref, szc_ref,
                  yd_ref, szd_ref, g_ref, cw_ref, wa_ref, wb_ref, wc_ref, wd_ref, wo_ref, o_ref, *, tm, nt):
    i = pl.program_id(1)
    f32 = jnp.float32
    u = uc_ref[...].astype(f32)
    row = lax.broadcasted_iota(jnp.int32, (tm, 1), 0)
    prev_row = jnp.where(i > 0, up_ref[7:8, :].astype(f32), 0.0)
    next_row = jnp.where(i < nt - 1, un_ref[0:1, :].astype(f32), 0.0)
    u_prev = jnp.where(row == 0, prev_row, pltpu.roll(u, 1, 0))
    u_next = jnp.where(row == tm - 1, next_row, pltpu.roll(u, tm - 1, 0))
    cw = cw_ref[...]
    conv = u_prev * cw[0:1, :] + u * cw[1:2, :] + u_next * cw[2:3, :]
    yc = cb_ref[...].astype(f32) * conv
    d = x_ref.shape[-1]
    branches = ((ya_ref[...].astype(f32), sza_ref, wa_ref), (yb_ref[...].astype(f32), szb_ref, wb_ref),
                (yc, szc_ref, wc_ref), (yd_ref[...].astype(f32), szd_ref, wd_ref))
    mixed = jnp.zeros((tm, d), f32)
    for j, (y, sz_ref, w_ref) in enumerate(branches):
        t = _dot(_bf(y * sz_ref[...].astype(f32)), w_ref[...])
        mixed = mixed + g_ref[:, j * d:(j + 1) * d].astype(f32) * t
    out = _dot(_bf(mixed), wo_ref[...])
    o_ref[...] = x_ref[...] + gt_ref[...] * out


def _merge(x, gate, ya, sza, yb, szb, uc, cb, szc, yd, szd, g, conv_w, w_a, w_b, w_c, w_d, w_out, tm):
    b, n, d = x.shape
    nt = n // tm
    tok = lambda w: pl.BlockSpec((None, tm, w), lambda bi, i: (bi, i, 0))
    hb = tm // 8
    halo_p = pl.BlockSpec((None, 8, CONV_W), lambda bi, i: (bi, jnp.maximum(i * hb - 1, 0), 0))
    halo_n = pl.BlockSpec((None, 8, CONV_W), lambda bi, i: (bi, jnp.minimum((i + 1) * hb, n // 8 - 1), 0))
    per_b = pl.BlockSpec((None, 1, d), lambda bi, i: (bi, 0, 0))
    ws = [_bf(w_a), _bf(w_b), _bf(w_c), _bf(w_d), _bf(w_out)]
    return pl.pallas_call(
        functools.partial(_merge_kernel, tm=tm, nt=nt),
        grid=(b, nt),
        in_specs=[tok(d), per_b, tok(FOURIER_W), tok(FOURIER_W), tok(SWA_W), tok(SWA_W), tok(CONV_W), halo_p, halo_n,
                  tok(CONV_W), tok(CONV_W), tok(DIFF_W), tok(DIFF_W), tok(N_BRANCH * d), _const_spec(conv_w.shape)]
                 + [_const_spec(w.shape) for w in ws],
        out_specs=tok(d),
        out_shape=jax.ShapeDtypeStruct((b, n, d), jnp.float32),
        compiler_params=_params(2),
        name="merge",
    )(x, gate, ya, sza, yb, szb, uc, uc, uc, cb, szc, yd, szd, g, conv_w.astype(jnp.float32), *ws)


def _rope_tables(n):
    f32 = np.float32
    rows = n // GRID_W
    row = np.broadcast_to(np.arange(rows, dtype=f32)[:, None], (rows, GRID_W)).reshape(-1)
    col = np.broadcast_to(np.arange(GRID_W, dtype=f32)[None, :], (rows, GRID_W)).reshape(-1)
    nf = HEAD_DIM // 4
    inv = f32(ROPE_BASE) ** (-np.arange(nf, dtype=f32) / f32(nf))
    ar = row[:, None] * inv[None, :]
    ac = col[:, None] * inv[None, :]
    cos = np.concatenate([np.cos(ar), np.cos(ar), np.cos(ac), np.cos(ac)], axis=-1)
    sin = np.concatenate([np.sin(ar), np.sin(ar), np.sin(ac), np.sin(ac)], axis=-1)
    sign = np.where((np.arange(HEAD_DIM) % 32) < 16, f32(-1.0), f32(1.0))
    return jnp.asarray(np.tile(cos, (1, 2)), jnp.float32), jnp.asarray(np.tile(sin * sign, (1, 2)), jnp.float32)


def _channel_dft():
    k = np.arange(HEAD_DIM)
    ang = 2.0 * np.pi * ((k[:, None] * k[None, :]) % HEAD_DIM) / HEAD_DIM
    eye = np.eye(FOURIER_W // HEAD_DIM)
    inv = 1.0 / math.sqrt(HEAD_DIM)
    return jnp.asarray(np.concatenate([np.kron(eye, np.cos(ang)), -np.kron(eye, np.sin(ang))], axis=1) * inv,
                       jnp.bfloat16)


def _mix(stream, scale, shift, gate, ctx_kv, lp, consts, lam, lam_init, rope, dft, tm, tq, tk, has_local):
    parts = _project(stream, scale, shift, lp["norm_g"], lp["w_ext"], consts["cs_bd"], lp["head_gains"], rope, tm)
    zr, zi, qb, kb, vb, uc, cb, qd, kd, vd, sza, szb, szc, szd, g = parts
    if ctx_kv is None:
        kx_b, vx_b, k_parts, v_parts = kb, vb, (kd,), (vd,)
    else:
        kx_b, vx_b, kx_d, vx_d = ctx_kv
        k_parts, v_parts = (kx_d, kd), (vx_d, vd)
    ya = _fourier_positions(zr, zi, dft)
    yb = _window_attention(qb, kb, vb, kx_b, vx_b, lp["sink"], lp["bound_b"], has_local)
    yd = _diff_attention(qd, k_parts, v_parts, lam, lp["bound_d"], lp["subln"], lam_init, tq, tk)
    new = _merge(stream, gate, ya, sza, yb, szb, uc, cb, szc, yd, szd, g, lp["conv_w"],
                 lp["w_o_a"], lp["w_o_b"], lp["w_o_c"], lp["w_o_d"], lp["w_out"], tm)
    return new, (kb, vb, kd, vd)


def kernel(x, c, ctx, c_ctx, norm_g, w_mod, b_mod, w_in, q_norm_b, k_norm_b, sink_b, conv_w, q_norm_d, k_norm_d,
           lam_q1, lam_k1, lam_q2, lam_k2, subln_d, w_o_a, w_o_b, w_o_c, w_o_d, w_out):
    b, n, d = x.shape
    lc = ctx.shape[1]
    depth = w_in.shape[0]
    rope = _rope_tables(n)
    consts = dict(cs_bd=_channel_dft())
    dft_x = _dft_tables(n)
    dft_c = _dft_tables(lc)
    c_rows = jnp.zeros((8, d), jnp.float32).at[:b].set(c).at[b].set(c_ctx)
    tm = min(512, n)
    tq = min(512, n)
    tk = 1280
    for l in range(depth):
        last = l == depth - 1
        lam_init = 0.8 - 0.6 * math.exp(-0.3 * l)
        lam_vecs = jnp.stack([lam_q1[l], lam_k1[l], lam_q2[l], lam_k2[l]]).astype(jnp.float32)
        mod, lam_o = _modulation(c_rows, w_mod[l], b_mod[l], lam_vecs, lam_init)
        lam = lam_o[0:1, 0:1]
        shift, scale, gate = (mod[:b, None, j * d:(j + 1) * d] for j in range(3))
        shift_c, scale_c, gate_c = (jnp.broadcast_to(mod[b:b + 1, None, j * d:(j + 1) * d], (b, 1, d))
                                    for j in range(3))
        tile2 = lambda v: jnp.tile(v.astype(jnp.float32), 2)
        lp = dict(norm_g=norm_g[l], w_ext=_bf(w_in[l]), sink=sink_b[l].astype(jnp.float32),
                  head_gains=jnp.stack([tile2(q_norm_b[l]), tile2(k_norm_b[l]), tile2(q_norm_d[l]), tile2(k_norm_d[l])]),
                  bound_b=_score_bound(q_norm_b[l], k_norm_b[l]), bound_d=_score_bound(q_norm_d[l], k_norm_d[l]),
                  subln=subln_d[l], conv_w=conv_w[l], w_o_a=w_o_a[l], w_o_b=w_o_b[l], w_o_c=w_o_c[l],
                  w_o_d=w_o_d[l], w_out=w_out[l])
        if last:
            parts = _project(ctx, scale_c, shift_c, lp["norm_g"], lp["w_ext"], consts["cs_bd"], lp["head_gains"],
                             None, min(256, lc))
            ctx_kv = (parts[3], parts[4], parts[8], parts[9])
        else:
            ctx, ctx_kv = _mix(ctx, scale_c, shift_c, gate_c, None, lp, consts, lam, lam_init, None, dft_c,
                               min(256, lc), min(256, lc), tk, False)
        x, _ = _mix(x, scale, shift, gate, ctx_kv, lp, consts, lam, lam_init, rope, dft_x, tm, tq, tk, True)
    return x
```

```python
import functools
import math

import jax
import jax.numpy as jnp
import numpy as np
from jax import lax
from jax.experimental import pallas as pl
from jax.experimental.pallas import tpu as pltpu

HEAD_DIM = 64
LANES = 128
FOURIER_W = 384
SWA_HEADS = 8
SWA_KV = 2
SWA_W = SWA_HEADS * HEAD_DIM
WINDOW = 128
BLOCK = 128
CONV_W = 384
DIFF_HEADS = 4
DIFF_W = DIFF_HEADS * 2 * HEAD_DIM
N_BRANCH = 4
ROPE_BASE = 10000.0
NORM_EPS = 1e-6
NEG_INF = -1e30
GRID_W = 64
DFT_MINOR = 128
SUM_ROWS = 16
LOG2E = math.log2(math.e)
SAFE_SHIFT = 60.0
VMEM_LIMIT = 56 * 1024 * 1024


def _bf(x):
    return x.astype(jnp.bfloat16)


def _dot(a, b):
    return jnp.dot(a, b, preferred_element_type=jnp.float32)


def _dot_nt(a, b):
    return lax.dot_general(a, b, (((1,), (1,)), ((), ())), preferred_element_type=jnp.float32)


def _params(n_axes):
    return pltpu.CompilerParams(dimension_semantics=("arbitrary",) * n_axes, vmem_limit_bytes=VMEM_LIMIT)


def _const_spec(shape):
    nd = len(shape)
    return pl.BlockSpec(shape, lambda *_: (0,) * nd, pipeline_mode=pl.Buffered(1))


def _mod_kernel(c_ref, w_ref, b_ref, lam_ref, mod_ref, lamo_ref, *, lam_init):
    c = c_ref[...]
    s = c * jax.nn.sigmoid(c)
    mod_ref[...] = _dot(_bf(s), w_ref[...]) + b_ref[...]
    lv = lam_ref[...]
    a1 = jnp.sum(lv[0:1, :] * lv[1:2, :], axis=-1, keepdims=True)
    a2 = jnp.sum(lv[2:3, :] * lv[3:4, :], axis=-1, keepdims=True)
    lam = jnp.exp(a1) - jnp.exp(a2) + lam_init
    lamo_ref[...] = jnp.broadcast_to(lam, lamo_ref.shape)


def _modulation(c_rows, w_mod, b_mod, lam_vecs, lam_init):
    r, d = c_rows.shape
    return pl.pallas_call(
        functools.partial(_mod_kernel, lam_init=lam_init),
        out_shape=(jax.ShapeDtypeStruct((r, 3 * d), jnp.float32),
                   jax.ShapeDtypeStruct((8, LANES), jnp.float32)),
        name="mod",
    )(c_rows, _bf(w_mod), b_mod.reshape(1, 3 * d), lam_vecs)


def _head_norm_rope(t, gain, cos, sin_s, lane_lo, scale):
    sq = t * t
    first = lax.broadcasted_iota(jnp.int32, t.shape, 1) < HEAD_DIM
    ms = jnp.where(first, jnp.sum(jnp.where(first, sq, 0.0), axis=-1, keepdims=True),
                   jnp.sum(jnp.where(first, 0.0, sq), axis=-1, keepdims=True)) * (1.0 / HEAD_DIM)
    y = t * lax.rsqrt(ms + NORM_EPS) * gain
    if cos is not None:
        rot = jnp.where(lane_lo, pltpu.roll(y, LANES - 16, 1), pltpu.roll(y, 16, 1))
        y = y * cos + rot * sin_s
    if scale != 1.0:
        y = y * scale
    return y


def _proj_kernel(*refs, use_rope, tm):
    if use_rope:
        (x_ref, sc_ref, sh_ref, g_ref, w_ref, cs_ref, hg_ref, cos_ref, sin_ref), outs = refs[:9], refs[9:]
    else:
        (x_ref, sc_ref, sh_ref, g_ref, w_ref, cs_ref, hg_ref), outs = refs[:7], refs[7:]
        cos_ref = sin_ref = None
    (zr_ref, zi_ref, qb_ref, kb_ref, vb_ref, uc_ref, cb_ref, qd_ref, kd_ref, vd_ref,
     sza_ref, szb_ref, szc_ref, szd_ref, gate_ref) = outs

    x = x_ref[...]
    y = x * lax.rsqrt(jnp.mean(x * x, axis=-1, keepdims=True) + NORM_EPS) * g_ref[...]
    h = _bf(y * (1.0 + sc_ref[...]) + sh_ref[...])

    def proj(c0, width):
        return _dot(h, w_ref[:, c0:c0 + width])

    col = 0
    a = _bf(proj(col, FOURIER_W))
    zz = _dot(a, cs_ref[...])
    zr_ref[...] = zz[:, :FOURIER_W].astype(zr_ref.dtype)
    zi_ref[...] = zz[:, FOURIER_W:].astype(zi_ref.dtype)
    col += FOURIER_W

    if use_rope:
        cos = cos_ref[...]
        sin_s = sin_ref[...]
    else:
        cos = sin_s = None
    lane = lax.broadcasted_iota(jnp.int32, (tm, LANES), 1)
    lane_lo = (lane & 31) < 16
    qscale = HEAD_DIM ** -0.5

    def normed(c0, width, gain_row, out_ref, scale):
        t = proj(c0, width)
        gain = hg_ref[gain_row:gain_row + 1, :]
        for s in range(width // LANES):
            ts = t[:, s * LANES:(s + 1) * LANES]
            out_ref[:, s * LANES:(s + 1) * LANES] = _bf(
                _head_norm_rope(ts, gain, cos, sin_s, lane_lo, scale))

    head_lo = lane < HEAD_DIM

    def store_duplicated(pair, out_ref):
        swapped = pltpu.roll(pair, HEAD_DIM, 1)
        out_ref[:, :LANES] = _bf(jnp.where(head_lo, pair, swapped))
        out_ref[:, LANES:] = _bf(jnp.where(head_lo, swapped, pair))

    normed(col, SWA_W, 0, qb_ref, qscale * LOG2E); col += SWA_W
    kv = proj(col, 2 * LANES); col += 2 * LANES
    store_duplicated(_head_norm_rope(kv[:, :LANES], hg_ref[1:2, :], cos, sin_s, lane_lo, 1.0), kb_ref)
    store_duplicated(kv[:, LANES:], vb_ref)
    c3 = proj(col, 3 * CONV_W); col += 3 * CONV_W
    uc_ref[...] = _bf(c3[:, 2 * CONV_W:] * c3[:, :CONV_W])
    cb_ref[...] = _bf(c3[:, CONV_W:2 * CONV_W])
    normed(col, DIFF_W, 2, qd_ref, qscale * LOG2E); col += DIFF_W
    normed(col, DIFF_W, 3, kd_ref, 1.0); col += DIFF_W
    vd_ref[...] = _bf(proj(col, DIFF_W)); col += DIFF_W
    z3 = proj(col, FOURIER_W + SWA_W + CONV_W)
    off = 0
    for ref, width in ((sza_ref, FOURIER_W), (szb_ref, SWA_W), (szc_ref, CONV_W)):
        z = z3[:, off:off + width]
        ref[...] = _bf(z * jax.nn.sigmoid(z))
        off += width
    col += off
    z = proj(col, DIFF_W); col += DIFF_W
    szd_ref[...] = _bf(z * jax.nn.sigmoid(z))
    d = x.shape[-1]
    for j in range(N_BRANCH):
        gate_ref[:, j * d:(j + 1) * d] = _bf(jax.nn.sigmoid(proj(col, d)))
        col += d


_PROJ_OUT_W = (FOURIER_W, FOURIER_W, SWA_W, 2 * LANES, 2 * LANES, CONV_W, CONV_W, DIFF_W, DIFF_W, DIFF_W,
               FOURIER_W, SWA_W, CONV_W, DIFF_W)


def _project(x, scale, shift, norm_g, w_ext, cs_bd, head_gains, rope, tm):
    b, n, d = x.shape
    use_rope = rope is not None
    tok = lambda w: pl.BlockSpec((None, tm, w), lambda bi, i: (bi, i, 0))
    per_b = pl.BlockSpec((None, 1, d), lambda bi, i: (bi, 0, 0))
    in_specs = [tok(d), per_b, per_b, _const_spec((1, d)), _const_spec(w_ext.shape), _const_spec(cs_bd.shape),
                _const_spec(head_gains.shape)]
    args = [x, scale, shift, norm_g.reshape(1, d), w_ext, cs_bd, head_gains]
    if use_rope:
        tab = pl.BlockSpec((tm, LANES), lambda bi, i: (i, 0))
        in_specs += [tab, tab]
        args += list(rope)
    widths = _PROJ_OUT_W + (N_BRANCH * d,)
    dtypes = [jnp.float32 if (j < 2 and n > 2 * DFT_MINOR) else jnp.bfloat16 for j in range(len(widths))]
    return pl.pallas_call(
        functools.partial(_proj_kernel, use_rope=use_rope, tm=tm),
        grid=(b, n // tm),
        in_specs=in_specs,
        out_specs=[tok(w) for w in widths],
        out_shape=[jax.ShapeDtypeStruct((b, n, w), dt) for w, dt in zip(widths, dtypes)],
        compiler_params=_params(2),
        name="proj",
    )(*args)


def _dft1_kernel(zr_ref, zi_ref, f_ref, tc_ref, ts_ref, o_ref, *, r, tn2):
    for j in range(tn2):
        z = _bf(jnp.concatenate([zr_ref[:, j, :], zi_ref[:, j, :]], axis=0))
        a = _dot(f_ref[...], z)
        ar, ai = a[:r], a[r:]
        tc = jnp.concatenate([tc_ref[j]] * (FOURIER_W // LANES), axis=-1)
        ts = jnp.concatenate([ts_ref[j]] * (FOURIER_W // LANES), axis=-1)
        o_ref[0, j] = _bf(ar * tc + ai * ts)
        o_ref[1, j] = _bf(ai * tc - ar * ts)


def _left_matmul_kernel(m_ref, x_ref, o_ref):
    o_ref[...] = _bf(_dot(m_ref[...], x_ref[...]))


def _left_matmul(mat, x, tc):
    b, k, c = x.shape
    rows = mat.shape[0]
    return pl.pallas_call(
        _left_matmul_kernel,
        grid=(b, c // tc),
        in_specs=[_const_spec(mat.shape), pl.BlockSpec((None, k, tc), lambda bi, i: (bi, 0, i))],
        out_specs=pl.BlockSpec((None, rows, tc), lambda bi, i: (bi, 0, i)),
        out_shape=jax.ShapeDtypeStruct((b, rows, c), jnp.bfloat16),
        compiler_params=_params(2),
        name="dft2",
    )(mat, x)


def _dft_tables(n):
    inv = 1.0 / math.sqrt(n)
    if n <= 2 * DFT_MINOR:
        k = np.arange(n)
        ang = 2.0 * np.pi * ((k[:, None] * k[None, :]) % n) / n
        return dict(direct=jnp.asarray(np.concatenate([np.cos(ang), np.sin(ang)], axis=1) * inv, jnp.bfloat16))
    r = n // DFT_MINOR
    k1 = np.arange(r)
    a1 = 2.0 * np.pi * ((k1[:, None] * k1[None, :]) % r) / r
    c1, s1 = np.cos(a1), np.sin(a1)
    f1 = np.block([[c1, s1], [-s1, c1]])
    n2 = np.arange(DFT_MINOR)
    at = 2.0 * np.pi * (n2[:, None] * k1[None, :]) / n
    tw_c = np.repeat(np.cos(at)[:, :, None], LANES, axis=2)
    tw_s = np.repeat(np.sin(at)[:, :, None], LANES, axis=2)
    a2 = 2.0 * np.pi * ((n2[:, None] * n2[None, :]) % DFT_MINOR) / DFT_MINOR
    f2 = np.concatenate([np.cos(a2), np.sin(a2)], axis=1) * inv
    return dict(f1=jnp.asarray(f1, jnp.bfloat16), tw_c=jnp.asarray(tw_c, jnp.float32),
                tw_s=jnp.asarray(tw_s, jnp.float32), f2=jnp.asarray(f2, jnp.bfloat16))


def _fourier_positions(zr, zi, tabs):
    b, n, w = zr.shape
    if "direct" in tabs:
        return _left_matmul(tabs["direct"], jnp.concatenate([zr, zi], axis=1), w)
    r = n // DFT_MINOR
    tn2 = 8
    zin = pl.BlockSpec((None, r, tn2, w), lambda bi, i: (bi, 0, i, 0))
    tw = pl.BlockSpec((tn2, r, LANES), lambda bi, i: (i, 0, 0))
    g = pl.pallas_call(
        functools.partial(_dft1_kernel, r=r, tn2=tn2),
        grid=(b, DFT_MINOR // tn2),
        in_specs=[zin, zin, _const_spec(tabs["f1"].shape), tw, tw],
        out_specs=pl.BlockSpec((None, 2, tn2, r, w), lambda bi, i: (bi, 0, i, 0, 0)),
        out_shape=jax.ShapeDtypeStruct((b, 2, DFT_MINOR, r, w), jnp.bfloat16),
        compiler_params=_params(2),
        name="dft1",
    )(zr.reshape(b, r, DFT_MINOR, w), zi.reshape(b, r, DFT_MINOR, w), tabs["f1"], tabs["tw_c"], tabs["tw_s"])
    y = _left_matmul(tabs["f2"], g.reshape(b, 2 * DFT_MINOR, r * w), (16 if r % 16 == 0 else 4) * w)
    return y.reshape(b, n, w)


def _win_kernel(*refs, has_local, nb, nq):
    if has_local:
        bound_ref, sink_ref, q_ref, sz_ref, kp_ref, kc_ref, kn_ref, vp_ref, vc_ref, vn_ref, kx_ref, vx_ref, o_ref = refs
    else:
        bound_ref, sink_ref, q_ref, sz_ref, kx_ref, vx_ref, o_ref = refs
    i = pl.program_id(1)
    tq = BLOCK
    group = SWA_HEADS // SWA_KV
    lane = lax.broadcasted_iota(jnp.int32, (1, LANES), 1)
    halves = (lane < HEAD_DIM, lane >= HEAD_DIM)
    row_lo = lax.broadcasted_iota(jnp.int32, (LANES, 1), 0) < HEAD_DIM
    if has_local:
        kr = lax.broadcasted_iota(jnp.int32, (3 * BLOCK, tq), 0)
        qc = lax.broadcasted_iota(jnp.int32, (3 * BLOCK, tq), 1)
        in_band = jnp.abs(kr - BLOCK - qc) <= WINDOW
    bound = bound_ref[0, 0]

    def attend(s, j, use_bound):
        sl = slice(j * LANES, (j + 1) * LANES)
        rows = slice(s * BLOCK, (s + 1) * BLOCK)
        if has_local:
            def window(prev_ref, cur_ref, next_ref):
                first = prev_ref[:, sl] if s == 0 else cur_ref[(s - 1) * BLOCK:s * BLOCK, sl]
                last = next_ref[:, sl] if s == nq - 1 else cur_ref[(s + 1) * BLOCK:(s + 2) * BLOCK, sl]
                return [first, cur_ref[rows, sl], last]
            kd = jnp.concatenate(window(kp_ref, kc_ref, kn_ref) + [kx_ref[:, sl]], axis=0)
            vd = jnp.concatenate(window(vp_ref, vc_ref, vn_ref) + [vx_ref[:, sl]], axis=0)
        else:
            kd = kx_ref[:, sl]
            vd = vx_ref[:, sl]
        q_stack = []
        for t in range(group // 2):
            slab = j * (group // 2) + t
            q2 = q_ref[rows, slab * LANES:(slab + 1) * LANES]
            q_stack += [jnp.where(halves[e], q2, jnp.zeros_like(q2)) for e in range(2)]
        st = _dot_nt(kd, jnp.concatenate(q_stack, axis=0))
        if has_local:
            kpos = (i * nq + s - 1) * BLOCK + kr
            bias = jnp.where(in_band & (kpos >= 0) & (kpos < nb * BLOCK), 0.0, NEG_INF)
            st = jnp.concatenate([st[:3 * BLOCK] + jnp.concatenate([bias] * group, axis=1), st[3 * BLOCK:]], axis=0)
        sk = sink_ref[j:j + 1, :]
        m = jnp.maximum(bound, sk) if use_bound else jnp.maximum(jnp.max(st, axis=0, keepdims=True), sk)
        p = jnp.exp2(st - m)
        inv = 1.0 / (jnp.sum(p, axis=0, keepdims=True) + jnp.exp2(sk - m))
        ot = lax.dot_general(vd, _bf(p), (((0,), (0,)), ((), ())), preferred_element_type=jnp.float32) * inv
        for t in range(group // 2):
            slab = j * (group // 2) + t
            pair = jnp.where(row_lo, ot[:, (2 * t) * tq:(2 * t + 1) * tq], ot[:, (2 * t + 1) * tq:(2 * t + 2) * tq])
            cols = slice(slab * LANES, (slab + 1) * LANES)
            o_ref[rows, cols] = _bf(pair.T * sz_ref[rows, cols].astype(jnp.float32))

    shift_safe = bound <= SAFE_SHIFT

    @pl.when(shift_safe)
    def _():
        for s in range(nq):
            for j in range(SWA_KV):
                attend(s, j, True)

    @pl.when(jnp.logical_not(shift_safe))
    def _():
        for s in range(nq):
            for j in range(SWA_KV):
                attend(s, j, False)


def _score_bound(q_gain, k_gain):
    g = jnp.max(jnp.abs(q_gain.astype(jnp.float32))) * jnp.max(jnp.abs(k_gain.astype(jnp.float32)))
    return (1.02 * HEAD_DIM ** 0.5 * LOG2E * g).reshape(1, 1)


def _window_attention(qb, sz, kb, vb, kx, vx, sink, bound, has_local):
    b, n, _ = qb.shape
    nb = n // BLOCK
    nq = 8 if nb % 8 == 0 else 1
    lx = kx.shape[1]
    qspec = pl.BlockSpec((None, nq * BLOCK, SWA_W), lambda bi, i: (bi, i, 0))
    ctx = pl.BlockSpec((None, lx, 2 * LANES), lambda bi, i: (bi, 0, 0))
    sink_rows = jnp.repeat(sink.astype(jnp.float32) * LOG2E, BLOCK).reshape(SWA_KV, -1)
    args = [bound, sink_rows, qb, sz]
    specs = [pl.BlockSpec(memory_space=pltpu.SMEM), _const_spec(sink_rows.shape), qspec, qspec]
    if has_local:
        prv = pl.BlockSpec((None, BLOCK, 2 * LANES), lambda bi, i: (bi, jnp.maximum(i * nq - 1, 0), 0))
        cur = pl.BlockSpec((None, nq * BLOCK, 2 * LANES), lambda bi, i: (bi, i, 0))
        nxt = pl.BlockSpec((None, BLOCK, 2 * LANES), lambda bi, i: (bi, jnp.minimum((i + 1) * nq, nb - 1), 0))
        args += [kb, kb, kb, vb, vb, vb]
        specs += [prv, cur, nxt, prv, cur, nxt]
    args += [kx, vx]
    specs += [ctx, ctx]
    return pl.pallas_call(
        functools.partial(_win_kernel, has_local=has_local, nb=nb, nq=nq),
        grid=(b, nb // nq),
        in_specs=specs,
        out_specs=qspec,
        out_shape=jax.ShapeDtypeStruct((b, n, SWA_W), jnp.bfloat16),
        compiler_params=_params(2),
        name="win",
    )(*args)


def _diff_kernel(lam_ref, bound_ref, q_ref, sz_ref, *refs, n_parts, tk, out_scale):
    k_parts, v_parts = refs[:n_parts], refs[n_parts:2 * n_parts]
    g_ref, o_ref, k_ref, vt_ref, acc_ref, s0_ref, s1_ref, p0_ref, p1_ref = refs[2 * n_parts:]
    qi = pl.program_id(2)
    tq = q_ref.shape[0]
    n_chunks = vt_ref.shape[0]

    @pl.when(qi == 0)
    def _():
        row = 0
        for kp, vp in zip(k_parts, v_parts):
            rows = kp.shape[0]
            k_ref[row:row + rows, :] = kp[...]
            done = 0
            while done < rows:
                ci, off = divmod(row + done, tk)
                take = min(tk - off, rows - done)
                vt_ref[ci, :LANES, off:off + take] = _bf(vp[done:done + take, :].astype(jnp.float32).T)
                done += take
            row += rows

        vt_ref[:, LANES:, :] = jnp.ones((n_chunks, SUM_ROWS, tk), jnp.bfloat16)

    q = q_ref[...]
    lane = lax.broadcasted_iota(jnp.int32, (1, LANES), 1)
    qm = (jnp.where(lane < HEAD_DIM, q, jnp.zeros_like(q)), jnp.where(lane >= HEAD_DIM, q, jnp.zeros_like(q)))
    s_bufs = (s0_ref, s1_ref)
    p_bufs = (p0_ref, p1_ref)

    bound = bound_ref[0, 0]
    shift_safe = bound <= SAFE_SHIFT

    @pl.when(shift_safe)
    def _():
        q_both = jnp.concatenate(qm, axis=0)

        def chunk(ci, carry):
            denom, pv = carry
            c0 = pl.multiple_of(ci * tk, tk)
            p = jnp.exp2(_dot_nt(k_ref[pl.ds(c0, tk), :], q_both) - bound)
            return denom + jnp.sum(p, axis=0, keepdims=True), pv + _dot(vt_ref[ci, :LANES, :], _bf(p))
        init = (jnp.zeros((1, 2 * tq), jnp.float32), jnp.zeros((LANES, 2 * tq), jnp.float32))
        denom, pv = lax.fori_loop(0, n_chunks, chunk, init, unroll=True)
        acc_ref[:LANES, :] = pv
        acc_ref[LANES:, :] = jnp.broadcast_to(denom, (SUM_ROWS, 2 * tq))

    @pl.when(jnp.logical_not(shift_safe))
    def _():
        acc_ref[...] = jnp.zeros_like(acc_ref)
        _diff_online(qm, k_ref, vt_ref, acc_ref, s_bufs, p_bufs, tk=tk, tq=tq, n_chunks=n_chunks)

    lam = lam_ref[0, 0]
    a1 = acc_ref[:, :tq]
    a2 = acc_ref[:, tq:]
    o = a1[:LANES] * (1.0 / a1[LANES:LANES + 1]) - lam * (a2[:LANES] * (1.0 / a2[LANES:LANES + 1]))
    y = o * lax.rsqrt(jnp.mean(o * o, axis=0, keepdims=True) + NORM_EPS) * g_ref[...] * out_scale
    o_ref[...] = _bf(y.T * sz_ref[...].astype(jnp.float32))


def _diff_online(qm, k_ref, vt_ref, acc_ref, s_bufs, p_bufs, *, tk, tq, n_chunks):
    def scores(ci, slot):
        c0 = ci * tk if isinstance(ci, int) else pl.multiple_of(ci * tk, tk)
        k = k_ref[pl.ds(c0, tk), :]
        mx = []
        for c in range(2):
            st = _dot_nt(k, qm[c])
            s_bufs[slot][c] = st
            mx.append(jnp.max(st, axis=0, keepdims=True))
        return tuple(mx)

    def probs(slot, mx, m_run):
        m_new, alpha = [], []
        for c in range(2):
            m = jnp.maximum(m_run[c], mx[c])
            alpha.append(jnp.exp2(m_run[c] - m))
            p_bufs[slot][c] = _bf(jnp.exp2(s_bufs[slot][c] - m))
            m_new.append(m)
        return tuple(m_new), tuple(alpha)

    def accumulate(ci, slot, alpha):
        vt = vt_ref[ci]
        for c in range(2):
            cols = slice(c * tq, (c + 1) * tq)
            acc_ref[:, cols] = alpha[c] * acc_ref[:, cols] + _dot(vt, p_bufs[slot][c])

    def step(t, par, do_scores, do_probs, do_acc, state):
        m_run, mx, alpha = state
        mx_next = scores(t + 2, par) if do_scores else mx
        if do_probs:
            m_run, alpha_next = probs(1 - par, mx, m_run)
        else:
            alpha_next = alpha
        if do_acc:
            accumulate(t, par, alpha)
        return m_run, mx_next, alpha_next

    neg = jnp.full((1, tq), NEG_INF, jnp.float32)
    one = jnp.ones((1, tq), jnp.float32)
    state = ((neg, neg), (neg, neg), (one, one))
    for t in (-2, -1):
        state = step(t, t % 2, t + 2 < n_chunks, 0 <= t + 1 < n_chunks, False, state)
    n_steady = max(n_chunks - 2, 0)

    def pair(j, state):
        state = step(2 * j, 0, True, True, True, state)
        return step(2 * j + 1, 1, True, True, True, state)

    state = lax.fori_loop(0, n_steady // 2, pair, state)
    if n_steady % 2:
        state = step(n_steady - 1, 0, True, True, True, state)
    for t in range(n_steady, n_chunks):
        state = step(t, t % 2, False, t + 1 < n_chunks, True, state)


def _pick_tk(nk, cap):
    best = LANES
    for t in range(LANES, min(cap, nk) + 1, LANES):
        if nk % t == 0:
            best = t
    return best


def _diff_attention(qd, sz, k_parts, v_parts, lam, bound, subln_g, lam_init, tq, tk_cap):
    b, n, _ = qd.shape
    nk = sum(k.shape[1] for k in k_parts)
    tk = _pick_tk(nk, tk_cap)
    smem = pl.BlockSpec(memory_space=pltpu.SMEM)
    qspec = pl.BlockSpec((None, tq, LANES), lambda bi, h, i: (bi, i, h))
    kvspecs = [pl.BlockSpec((None, k.shape[1], LANES), lambda bi, h, i: (bi, 0, h)) for k in k_parts]
    gain = jnp.broadcast_to(subln_g.astype(jnp.float32)[:, None], (LANES, tq))
    return pl.pallas_call(
        functools.partial(_diff_kernel, n_parts=len(k_parts), tk=tk, out_scale=1.0 - lam_init),
        grid=(b, DIFF_HEADS, n // tq),
        in_specs=[smem, smem, qspec, qspec] + kvspecs + kvspecs + [_const_spec((LANES, tq))],
        out_specs=qspec,
        out_shape=jax.ShapeDtypeStruct((b, n, DIFF_W), jnp.bfloat16),
        scratch_shapes=[pltpu.VMEM((nk, LANES), jnp.bfloat16),
                        pltpu.VMEM((nk // tk, LANES + SUM_ROWS, tk), jnp.bfloat16),
                        pltpu.VMEM((LANES + SUM_ROWS, 2 * tq), jnp.float32),
                        pltpu.VMEM((2, tk, tq), jnp.float32), pltpu.VMEM((2, tk, tq), jnp.float32),
                        pltpu.VMEM((2, tk, tq), jnp.bfloat16), pltpu.VMEM((2, tk, tq), jnp.bfloat16)],
        compiler_params=_params(3),
        name="diff",
    )(lam, bound, qd, sz, *k_parts, *v_parts, gain)


def _merge_kernel(x_ref, gt_ref, ya_ref, sza_ref, yb_ref, uc_ref, up_ref, un_ref, cb_ref, szc_ref,
                  yd_ref, g_ref, cw_ref, wa_ref, wb_ref, wc_ref, wd_ref, wo_ref, o_ref, *, tm, nt):
    i = pl.program_id(1)
    f32 = jnp.float32
    u = uc_ref[...].astype(f32)
    row = lax.broadcasted_iota(jnp.int32, (tm, 1), 0)
    prev_row = jnp.where(i > 0, up_ref[7:8, :].astype(f32), 0.0)
    next_row = jnp.where(i < nt - 1, un_ref[0:1, :].astype(f32), 0.0)
    u_prev = jnp.where(row == 0, prev_row, pltpu.roll(u, 1, 0))
    u_next = jnp.where(row == tm - 1, next_row, pltpu.roll(u, tm - 1, 0))
    cw = cw_ref[...]
    conv = u_prev * cw[0:1, :] + u * cw[1:2, :] + u_next * cw[2:3, :]
    yc = cb_ref[...].astype(f32) * conv
    d = x_ref.shape[-1]
    branches = ((_bf(ya_ref[...].astype(f32) * sza_ref[...].astype(f32)), wa_ref), (yb_ref[...], wb_ref),
                (_bf(yc * szc_ref[...].astype(f32)), wc_ref), (yd_ref[...], wd_ref))
    mixed = jnp.zeros((tm, d), f32)
    for j, (u_j, w_ref) in enumerate(branches):
        t = _dot(u_j, w_ref[...])
        mixed = mixed + g_ref[:, j * d:(j + 1) * d].astype(f32) * t
    out = _dot(_bf(mixed), wo_ref[...])
    o_ref[...] = x_ref[...] + gt_ref[...] * out


def _merge(x, gate, ya, sza, yb, uc, cb, szc, yd, g, conv_w, w_a, w_b, w_c, w_d, w_out, tm):
    b, n, d = x.shape
    nt = n // tm
    tok = lambda w: pl.BlockSpec((None, tm, w), lambda bi, i: (bi, i, 0))
    hb = tm // 8
    halo_p = pl.BlockSpec((None, 8, CONV_W), lambda bi, i: (bi, jnp.maximum(i * hb - 1, 0), 0))
    halo_n = pl.BlockSpec((None, 8, CONV_W), lambda bi, i: (bi, jnp.minimum((i + 1) * hb, n // 8 - 1), 0))
    per_b = pl.BlockSpec((None, 1, d), lambda bi, i: (bi, 0, 0))
    ws = [_bf(w_a), _bf(w_b), _bf(w_c), _bf(w_d), _bf(w_out)]
    return pl.pallas_call(
        functools.partial(_merge_kernel, tm=tm, nt=nt),
        grid=(b, nt),
        in_specs=[tok(d), per_b, tok(FOURIER_W), tok(FOURIER_W), tok(SWA_W), tok(CONV_W), halo_p, halo_n,
                  tok(CONV_W), tok(CONV_W), tok(DIFF_W), tok(N_BRANCH * d), _const_spec(conv_w.shape)]
                 + [_const_spec(w.shape) for w in ws],
        out_specs=tok(d),
        out_shape=jax.ShapeDtypeStruct((b, n, d), jnp.float32),
        compiler_params=_params(2),
        name="merge",
    )(x, gate, ya, sza, yb, uc, uc, uc, cb, szc, yd, g, conv_w.astype(jnp.float32), *ws)


def _rope_tables(n):
    f32 = np.float32
    rows = n // GRID_W
    row = np.broadcast_to(np.arange(rows, dtype=f32)[:, None], (rows, GRID_W)).reshape(-1)
    col = np.broadcast_to(np.arange(GRID_W, dtype=f32)[None, :], (rows, GRID_W)).reshape(-1)
    nf = HEAD_DIM // 4
    inv = f32(ROPE_BASE) ** (-np.arange(nf, dtype=f32) / f32(nf))
    ar = row[:, None] * inv[None, :]
    ac = col[:, None] * inv[None, :]
    cos = np.concatenate([np.cos(ar), np.cos(ar), np.cos(ac), np.cos(ac)], axis=-1)
    sin = np.concatenate([np.sin(ar), np.sin(ar), np.sin(ac), np.sin(ac)], axis=-1)
    sign = np.where((np.arange(HEAD_DIM) % 32) < 16, f32(-1.0), f32(1.0))
    return jnp.asarray(np.tile(cos, (1, 2)), jnp.float32), jnp.asarray(np.tile(sin * sign, (1, 2)), jnp.float32)


def _channel_dft():
    k = np.arange(HEAD_DIM)
    ang = 2.0 * np.pi * ((k[:, None] * k[None, :]) % HEAD_DIM) / HEAD_DIM
    eye = np.eye(FOURIER_W // HEAD_DIM)
    inv = 1.0 / math.sqrt(HEAD_DIM)
    return jnp.asarray(np.concatenate([np.kron(eye, np.cos(ang)), -np.kron(eye, np.sin(ang))], axis=1) * inv,
                       jnp.bfloat16)


def _mix(stream, scale, shift, gate, ctx_kv, lp, consts, lam, lam_init, rope, dft, tm, tq, tk, has_local):
    parts = _project(stream, scale, shift, lp["norm_g"], lp["w_ext"], consts["cs_bd"], lp["head_gains"], rope, tm)
    zr, zi, qb, kb, vb, uc, cb, qd, kd, vd, sza, szb, szc, szd, g = parts
    if ctx_kv is None:
        kx_b, vx_b, k_parts, v_parts = kb, vb, (kd,), (vd,)
    else:
        kx_b, vx_b, kx_d, vx_d = ctx_kv
        k_parts, v_parts = (kx_d, kd), (vx_d, vd)
    ya = _fourier_positions(zr, zi, dft)
    yb = _window_attention(qb, szb, kb, vb, kx_b, vx_b, lp["sink"], lp["bound_b"], has_local)
    yd = _diff_attention(qd, szd, k_parts, v_parts, lam, lp["bound_d"], lp["subln"], lam_init, tq, tk)
    new = _merge(stream, gate, ya, sza, yb, uc, cb, szc, yd, g, lp["conv_w"],
                 lp["w_o_a"], lp["w_o_b"], lp["w_o_c"], lp["w_o_d"], lp["w_out"], tm)
    return new, (kb, vb, kd, vd)


def kernel(x, c, ctx, c_ctx, norm_g, w_mod, b_mod, w_in, q_norm_b, k_norm_b, sink_b, conv_w, q_norm_d, k_norm_d,
           lam_q1, lam_k1, lam_q2, lam_k2, subln_d, w_o_a, w_o_b, w_o_c, w_o_d, w_out):
    b, n, d = x.shape
    lc = ctx.shape[1]
    depth = w_in.shape[0]
    rope = _rope_tables(n)
    consts = dict(cs_bd=_channel_dft())
    dft_x = _dft_tables(n)
    dft_c = _dft_tables(lc)
    c_rows = jnp.zeros((8, d), jnp.float32).at[:b].set(c).at[b].set(c_ctx)
    tm = min(512, n)
    tq = min(512, n)
    tk = 1280
    for l in range(depth):
        last = l == depth - 1
        lam_init = 0.8 - 0.6 * math.exp(-0.3 * l)
        lam_vecs = jnp.stack([lam_q1[l], lam_k1[l], lam_q2[l], lam_k2[l]]).astype(jnp.float32)
        mod, lam_o = _modulation(c_rows, w_mod[l], b_mod[l], lam_vecs, lam_init)
        lam = lam_o[0:1, 0:1]
        shift, scale, gate = (mod[:b, None, j * d:(j + 1) * d] for j in range(3))
        shift_c, scale_c, gate_c = (jnp.broadcast_to(mod[b:b + 1, None, j * d:(j + 1) * d], (b, 1, d))
                                    for j in range(3))
        tile2 = lambda v: jnp.tile(v.astype(jnp.float32), 2)
        lp = dict(norm_g=norm_g[l], w_ext=_bf(w_in[l]), sink=sink_b[l].astype(jnp.float32),
                  head_gains=jnp.stack([tile2(q_norm_b[l]), tile2(k_norm_b[l]), tile2(q_norm_d[l]), tile2(k_norm_d[l])]),
                  bound_b=_score_bound(q_norm_b[l], k_norm_b[l]), bound_d=_score_bound(q_norm_d[l], k_norm_d[l]),
                  subln=subln_d[l], conv_w=conv_w[l], w_o_a=w_o_a[l], w_o_b=w_o_b[l], w_o_c=w_o_c[l],
                  w_o_d=w_o_d[l], w_out=w_out[l])
        if last:
            parts = _project(ctx, scale_c, shift_c, lp["norm_g"], lp["w_ext"], consts["cs_bd"], lp["head_gains"],
                             None, min(256, lc))
            ctx_kv = (parts[3], parts[4], parts[8], parts[9])
        else:
            ctx, ctx_kv = _mix(ctx, scale_c, shift_c, gate_c, None, lp, consts, lam, lam_init, None, dft_c,
                               min(256, lc), min(256, lc), tk, False)
        x, _ = _mix(x, scale, shift, gate, ctx_kv, lp, consts, lam, lam_init, rope, dft_x, tm, tq, tk, True)
    return x
```
